```python
import math
import jax, jax.numpy as jnp
from jax import lax
import numpy as np

D_MODEL = 1024
BATCH = 16
SEQ = 256
DEPTH = 1
DEC_BATCH = 8
DEC_SEQ = 1024
PAST_LEN = 256

GRID_W = 64
N_HEADS = 8
HEAD_DIM = 64
V_DIM = 2 * HEAD_DIM
ATTN_WIDTH = N_HEADS * V_DIM
D_HYENA = D_MODEL // 2
HYENA_ORDER = 2
FILTER_EMB = 33
FILTER_HIDDEN = 64
D_FF = 4 * D_MODEL
ROPE_THETA = 10000.0
EPS = 1e-6
Q_BLOCK = 128
DECAY_TARGET = 1e-2
FAST_DECAY_PCT = 0.3
SLOW_DECAY_PCT = 1.5

Q_COLS = N_HEADS * 2 * HEAD_DIM
K_COLS = N_HEADS * 2 * HEAD_DIM
V_COLS = N_HEADS * V_DIM
HY_COLS = 3 * D_HYENA
GATE_COLS = 2 * D_MODEL
IN_COLS = Q_COLS + K_COLS + V_COLS + HY_COLS + GATE_COLS
SPLITS = (Q_COLS, Q_COLS + K_COLS, Q_COLS + K_COLS + V_COLS, Q_COLS + K_COLS + V_COLS + HY_COLS)

kernel_name = "diff_hyena_prefix_dit_step"


def _rmsnorm(x, g):
    xf = x.astype(jnp.float32)
    y = xf * lax.rsqrt(jnp.mean(xf * xf, axis=-1, keepdims=True) + EPS)
    return (y * g.astype(jnp.float32)).astype(x.dtype)


def _rope_half(x, pos):
    n = x.shape[-1] // 2
    inv = ROPE_THETA ** (-jnp.arange(n, dtype=jnp.float32) / n)
    ang = pos.astype(jnp.float32)[:, None] * inv[None, :]
    cos = jnp.cos(ang)[None, :, None, None, :]
    sin = jnp.sin(ang)[None, :, None, None, :]
    x1 = x[..., :n].astype(jnp.float32)
    x2 = x[..., n:].astype(jnp.float32)
    return jnp.concatenate([x1 * cos - x2 * sin, x2 * cos + x1 * sin], axis=-1).astype(x.dtype)


def _rope2d(x, n_tokens):
    rows = n_tokens // GRID_W
    row = jnp.repeat(jnp.arange(rows), GRID_W)
    col = jnp.tile(jnp.arange(GRID_W), rows)
    half = HEAD_DIM // 2
    return jnp.concatenate([_rope_half(x[..., :half], row), _rope_half(x[..., half:], col)], axis=-1)


def _diff_attention(q, k, v, lam, lam_init, subln_g):
    B, Lq = q.shape[0], q.shape[1]
    nblk = Lq // Q_BLOCK
    qb = jnp.moveaxis(q.reshape(B, nblk, Q_BLOCK, N_HEADS, 2, HEAD_DIM), 1, 0)

    def one_block(qi):
        s = jnp.einsum('bqhmd,bkhmd->bhmqk', qi, k, preferred_element_type=jnp.float32) * (HEAD_DIM ** -0.5)
        p = jax.nn.softmax(s, axis=-1)
        a = p[:, :, 0] - lam * p[:, :, 1]
        return jnp.einsum('bhqk,bkhe->bqhe', a.astype(v.dtype), v)

    o = lax.map(one_block, qb)
    o = jnp.moveaxis(o, 0, 1).reshape(B, Lq, N_HEADS, V_DIM)
    o = _rmsnorm(o, subln_g) * (1.0 - lam_init)
    return o.reshape(B, Lq, ATTN_WIDTH)


def _short_conv(x, w, b):
    xp = jnp.pad(x, ((0, 0), (1, 1), (0, 0)))
    return xp[:, :-2] * w[0] + xp[:, 1:-1] * w[1] + xp[:, 2:] * w[2] + b


def _hyena_filters(L, w1, b1, w2, b2, w3, freq):
    bands = (FILTER_EMB - 1) // 2
    t = jnp.linspace(0.0, 1.0, L, dtype=jnp.float32)[:, None]
    wpos = 2.0 * math.pi * jnp.arange(L, dtype=jnp.float32)[:, None] / L
    f = jnp.linspace(1e-4, bands - 1, bands, dtype=jnp.float32)[None, :]
    emb = jnp.concatenate([t, jnp.cos(f * wpos), -jnp.sin(f * wpos)], axis=-1)
    fr = freq.astype(jnp.float32)
    h = jnp.sin(fr * (emb @ w1.astype(jnp.float32) + b1.astype(jnp.float32)))
    h = jnp.sin(fr * (h @ w2.astype(jnp.float32) + b2.astype(jnp.float32)))
    h = (h @ w3.astype(jnp.float32)).reshape(L, 2, HYENA_ORDER, D_HYENA)
    deltas = jnp.linspace(math.log(DECAY_TARGET) / FAST_DECAY_PCT,
                          math.log(DECAY_TARGET) / SLOW_DECAY_PCT, D_HYENA, dtype=jnp.float32)
    h = h * jnp.exp(-t * jnp.abs(deltas))[:, None, None, :]
    fwd, bwd = h[:, 0], h[:, 1]
    two_sided = jnp.concatenate([fwd[:1] + bwd[:1], fwd[1:],
                                 jnp.zeros((1, HYENA_ORDER, D_HYENA), jnp.float32),
                                 bwd[1:][::-1]], axis=0)
    return jnp.fft.rfft(two_sided, axis=0)


def _fftconv(u, hf, bias):
    L = u.shape[1]
    uf = u.astype(jnp.float32)
    U = jnp.fft.rfft(uf, n=2 * L, axis=1)
    y = jnp.fft.irfft(U * hf[None], n=2 * L, axis=1)[:, :L]
    return (y + uf * bias.astype(jnp.float32)).astype(u.dtype)


def _hyena(u, conv_w, conv_b, w1, b1, w2, b2, w3, freq, hy_bias):
    L = u.shape[1]
    u = _short_conv(u, conv_w, conv_b)
    v, x1, x2 = jnp.split(u, 3, axis=-1)
    hf = _hyena_filters(L, w1, b1, w2, b2, w3, freq)
    z = x1 * _fftconv(v, hf[:, 0], hy_bias[0])
    return x2 * _fftconv(z, hf[:, 1], hy_bias[1])


def _layer(x, mod, ctx_kv, lam_init, p):
    (norm1_g, norm2_g, w_in, lam_q1, lam_k1, lam_q2, lam_k2, attn_subln_g, conv_w, conv_b,
     filt_w1, filt_b1, filt_w2, filt_b2, filt_w3, filt_freq, hy_bias,
     w_br_attn, w_br_hy, w_out, w_up, w_down) = p
    B, L, _ = x.shape
    shift1, scale1, gate1, shift2, scale2, gate2 = [m[:, None, :] for m in jnp.split(mod, 6, axis=-1)]
    h = _rmsnorm(x, norm1_g) * (1.0 + scale1) + shift1
    proj = h @ w_in
    q, k, v, u_hy, gates = jnp.split(proj, SPLITS, axis=-1)
    q = q.reshape(B, L, N_HEADS, 2, HEAD_DIM)
    k = k.reshape(B, L, N_HEADS, 2, HEAD_DIM)
    v = v.reshape(B, L, N_HEADS, V_DIM)
    if ctx_kv is None:
        q_att, k_all, v_all = q, k, v
    else:
        q_att = _rope2d(q, L)
        k_all = jnp.concatenate([ctx_kv[0], _rope2d(k, L)], axis=1)
        v_all = jnp.concatenate([ctx_kv[1], v], axis=1)
    lam = (jnp.exp(jnp.sum(lam_q1.astype(jnp.float32) * lam_k1.astype(jnp.float32)))
           - jnp.exp(jnp.sum(lam_q2.astype(jnp.float32) * lam_k2.astype(jnp.float32))) + lam_init)
    o_attn = _diff_attention(q_att, k_all, v_all, lam, lam_init, attn_subln_g)
    o_hy = _hyena(u_hy, conv_w, conv_b, filt_w1, filt_b1, filt_w2, filt_b2, filt_w3, filt_freq, hy_bias)
    g_attn, g_hy = jnp.split(jax.nn.sigmoid(gates), 2, axis=-1)
    merged = g_attn * (o_attn @ w_br_attn) + g_hy * (o_hy @ w_br_hy)
    x = x + gate1 * (merged @ w_out)
    h2 = _rmsnorm(x, norm2_g) * (1.0 + scale2) + shift2
    x = x + gate2 * (jnp.square(jax.nn.relu(h2 @ w_up)) @ w_down)
    return x, k, v


def setup_inputs(seed: int = 0) -> dict:
    key = jax.random.key(seed)
    ks = jax.random.split(key, 31)
    f32 = jnp.float32

    def nrm(k, shape, scale):
        return jax.random.normal(k, shape, f32) * scale

    return {
        "x_prompt": nrm(ks[0], (BATCH, SEQ, D_MODEL), 1.0),
        "x_sample": nrm(ks[1], (DEC_BATCH, DEC_SEQ, D_MODEL), 1.0),
        "cache_k": nrm(ks[2], (DEC_BATCH, DEPTH, PAST_LEN, N_HEADS, 2, HEAD_DIM), 1.0),
        "cache_v": nrm(ks[3], (DEC_BATCH, DEPTH, PAST_LEN, N_HEADS, V_DIM), 1.0),
        "c": nrm(ks[4], (DEC_BATCH, D_MODEL), 1.0),
        "c_ctx": nrm(ks[5], (D_MODEL,), 1.0),
        "w_ada": nrm(ks[6], (DEPTH, D_MODEL, 6 * D_MODEL), 0.5 * D_MODEL ** -0.5),
        "b_ada": nrm(ks[7], (DEPTH, 6 * D_MODEL), 0.01),
        "norm1_g": 1.0 + nrm(ks[8], (DEPTH, D_MODEL), 0.02),
        "norm2_g": 1.0 + nrm(ks[9], (DEPTH, D_MODEL), 0.02),
        "w_in": nrm(ks[10], (DEPTH, D_MODEL, IN_COLS), D_MODEL ** -0.5),
        "lam_q1": nrm(ks[11], (DEPTH, HEAD_DIM), 0.1),
        "lam_k1": nrm(ks[12], (DEPTH, HEAD_DIM), 0.1),
        "lam_q2": nrm(ks[13], (DEPTH, HEAD_DIM), 0.1),
        "lam_k2": nrm(ks[14], (DEPTH, HEAD_DIM), 0.1),
        "attn_subln_g": 1.0 + nrm(ks[15], (DEPTH, V_DIM), 0.02),
        "conv_w": nrm(ks[16], (DEPTH, 3, HY_COLS), 0.5),
        "conv_b": nrm(ks[17], (DEPTH, HY_COLS), 0.01),
        "filt_w1": nrm(ks[18], (DEPTH, FILTER_EMB, FILTER_HIDDEN), FILTER_EMB ** -0.5),
        "filt_b1": nrm(ks[19], (DEPTH, FILTER_HIDDEN), 0.1),
        "filt_w2": nrm(ks[20], (DEPTH, FILTER_HIDDEN, FILTER_HIDDEN), FILTER_HIDDEN ** -0.5),
        "filt_b2": nrm(ks[21], (DEPTH, FILTER_HIDDEN), 0.1),
        "filt_w3": nrm(ks[22], (DEPTH, FILTER_HIDDEN, 2 * HYENA_ORDER * D_HYENA), 0.1 * FILTER_HIDDEN ** -0.5),
        "filt_freq": 1.0 + nrm(ks[23], (DEPTH, FILTER_HIDDEN), 0.1),
        "hy_bias": nrm(ks[24], (DEPTH, HYENA_ORDER, D_HYENA), 0.5),
        "w_br_attn": nrm(ks[25], (DEPTH, ATTN_WIDTH, D_MODEL), ATTN_WIDTH ** -0.5),
        "w_br_hy": nrm(ks[26], (DEPTH, D_HYENA, D_MODEL), D_HYENA ** -0.5),
        "w_out": nrm(ks[27], (DEPTH, D_MODEL, D_MODEL), D_MODEL ** -0.5),
        "w_up": nrm(ks[28], (DEPTH, D_MODEL, D_FF), D_MODEL ** -0.5),
        "w_down": nrm(ks[29], (DEPTH, D_FF, D_MODEL), D_FF ** -0.5),
        "final_g": 1.0 + nrm(ks[30], (D_MODEL,), 0.02),
    }


def reference(x_prompt, x_sample, cache_k, cache_v, c, c_ctx, w_ada, b_ada, norm1_g, norm2_g, w_in,
              lam_q1, lam_k1, lam_q2, lam_k2, attn_subln_g, conv_w, conv_b, filt_w1, filt_b1,
              filt_w2, filt_b2, filt_w3, filt_freq, hy_bias, w_br_attn, w_br_hy, w_out, w_up,
              w_down, final_g):
    y_p = x_prompt
    y_s = x_sample
    new_k = []
    new_v = []
    for l in range(DEPTH):
        p = (norm1_g[l], norm2_g[l], w_in[l], lam_q1[l], lam_k1[l], lam_q2[l], lam_k2[l],
             attn_subln_g[l], conv_w[l], conv_b[l], filt_w1[l], filt_b1[l], filt_w2[l],
             filt_b2[l], filt_w3[l], filt_freq[l], hy_bias[l], w_br_attn[l], w_br_hy[l],
             w_out[l], w_up[l], w_down[l])
        lam_init = 0.8 - 0.6 * math.exp(-0.3 * l)
        mod_ctx = (jax.nn.silu(c_ctx) @ w_ada[l] + b_ada[l])[None, :]
        mod_lat = jax.nn.silu(c) @ w_ada[l] + b_ada[l]
        y_p, k_ctx, v_ctx = _layer(y_p, mod_ctx, None, lam_init, p)
        new_k.append(k_ctx)
        new_v.append(v_ctx)
        y_s, _, _ = _layer(y_s, mod_lat, (cache_k[:, l], cache_v[:, l]), lam_init, p)
    y_prompt = _rmsnorm(y_p, final_g)
    y_sample = _rmsnorm(y_s, final_g)
    new_cache_k = jnp.stack(new_k, axis=1)
    new_cache_v = jnp.stack(new_v, axis=1)
    return (y_prompt, y_sample, new_cache_k, new_cache_v)
```

```python
import functools
import math

import numpy as np
import jax
import jax.numpy as jnp
from jax import lax
from jax.experimental import pallas as pl
from jax.experimental.pallas import tpu as pltpu

D_MODEL = 1024
GRID_W = 64
N_HEADS = 8
HEAD_DIM = 64
V_DIM = 2 * HEAD_DIM
D_HYENA = D_MODEL // 2
HYENA_ORDER = 2
FILTER_EMB = 33
FILTER_HIDDEN = 64
D_FF = 4 * D_MODEL
ROPE_THETA = 10000.0
EPS = 1e-6
DECAY_TARGET = 1e-2
FAST_DECAY_PCT = 0.3
SLOW_DECAY_PCT = 1.5

Q_COLS = N_HEADS * 2 * HEAD_DIM
K_COLS = N_HEADS * 2 * HEAD_DIM
V_COLS = N_HEADS * V_DIM
HY_COLS = 3 * D_HYENA
GATE_COLS = 2 * D_MODEL
Q_OFF = 0
K_OFF = Q_OFF + Q_COLS
V_OFF = K_OFF + K_COLS
HY_OFF = V_OFF + V_COLS
GATE_OFF = HY_OFF + HY_COLS
IN_COLS = GATE_OFF + GATE_COLS

LANES = 128
MOD_ROWS = 16
FILT_PAD = 128
VMEM_LIMIT = 56 * 1024 * 1024

BF16 = jnp.bfloat16
F32 = jnp.float32


def _resident(shape):
    nd = len(shape)
    return pl.BlockSpec(shape, lambda *_: (0,) * nd, pipeline_mode=pl.Buffered(1))


def _params(*sem):
    return pltpu.CompilerParams(dimension_semantics=sem, vmem_limit_bytes=VMEM_LIMIT)


def _rms(x, g):
    return x * lax.rsqrt(jnp.mean(x * x, axis=-1, keepdims=True) + EPS) * g


def _mod_kernel(c_ref, w_ref, b_ref, o_ref):
    c = c_ref[...]
    s = (c * jax.nn.sigmoid(c)).astype(BF16)
    o_ref[...] = jnp.dot(s, w_ref[...].astype(BF16), preferred_element_type=F32) + b_ref[...]


def _mod_call(cvec, w_ada, b_ada):
    tn = 1024
    n = w_ada.shape[1]
    return pl.pallas_call(
        _mod_kernel,
        grid=(n // tn,),
        in_specs=[pl.BlockSpec((MOD_ROWS, D_MODEL), lambda j: (0, 0)),
                  pl.BlockSpec((D_MODEL, tn), lambda j: (0, j)),
                  pl.BlockSpec((1, tn), lambda j: (0, j))],
        out_specs=pl.BlockSpec((MOD_ROWS, tn), lambda j: (0, j)),
        out_shape=jax.ShapeDtypeStruct((MOD_ROWS, n), F32),
        compiler_params=_params("arbitrary"),
        name="mod",
    )(cvec, w_ada, b_ada)


def _rope(x, cos, sin_lo, sin_hi):
    return (x * cos + pltpu.roll(x, 16, axis=1) * sin_hi
            + pltpu.roll(x, LANES - 16, axis=1) * sin_lo)


def _in_proj_kernel(*refs, latent):
    if latent:
        (x_ref, mod_ref, g_ref, w_ref, cos_ref, slo_ref, shi_ref,
         q_ref, k_ref, v_ref, u_ref, gt_ref) = refs
    else:
        (x_ref, mod_ref, g_ref, w_ref,
         q_ref, k_ref, v_ref, kf_ref, vf_ref, u_ref, gt_ref) = refs
    x = x_ref[...]
    h = _rms(x, g_ref[...]) * (1.0 + mod_ref[1:2, :]) + mod_ref[0:1, :]
    hb = h.astype(BF16)

    def proj(off, width):
        return jnp.dot(hb, w_ref[:, off:off + width], preferred_element_type=F32)

    q = proj(Q_OFF, Q_COLS) * (HEAD_DIM ** -0.5)
    k = proj(K_OFF, K_COLS)
    if latent:
        cos, slo, shi = cos_ref[...], slo_ref[...], shi_ref[...]
        for hd in range(N_HEADS):
            sl = slice(hd * LANES, (hd + 1) * LANES)
            q_ref[:, sl] = _rope(q[:, sl], cos, slo, shi).astype(BF16)
            k_ref[:, sl] = _rope(k[:, sl], cos, slo, shi).astype(BF16)
    else:
        q_ref[...] = q.astype(BF16)
        k_ref[...] = k.astype(BF16)
        kf_ref[...] = k
    v = proj(V_OFF, V_COLS)
    v_ref[...] = v.astype(BF16)
    if not latent:
        vf_ref[...] = v
    u_ref[...] = proj(HY_OFF, HY_COLS)
    gt_ref[...] = jax.nn.sigmoid(proj(GATE_OFF, GATE_COLS))


def _in_proj_call(x, mod, g1, w_in, rope, *, seq, latent, tm):
    t = x.shape[0]
    per_seq = seq // tm if latent else 1

    def mod_idx(i):
        return (1 + i // per_seq if latent else 0, 0, 0)

    row = lambda i: (i, 0)
    in_specs = [pl.BlockSpec((tm, D_MODEL), row),
                pl.BlockSpec((None, 6, D_MODEL), mod_idx),
                _resident((1, D_MODEL)),
                _resident((D_MODEL, IN_COLS))]
    args = [x, mod, g1, w_in]
    if latent:
        in_specs += [pl.BlockSpec((tm, LANES), lambda i: (i % per_seq, 0))] * 3
        args += list(rope)
    outs = [((t, Q_COLS), BF16), ((t, K_COLS), BF16), ((t, V_COLS), BF16)]
    if not latent:
        outs += [((t, K_COLS), F32), ((t, V_COLS), F32)]
    outs += [((t, HY_COLS), F32), ((t, GATE_COLS), F32)]
    return pl.pallas_call(
        functools.partial(_in_proj_kernel, latent=latent),
        grid=(t // tm,),
        in_specs=in_specs,
        out_specs=[pl.BlockSpec((tm, s[1]), row) for s, _ in outs],
        out_shape=[jax.ShapeDtypeStruct(s, d) for s, d in outs],
        compiler_params=_params("arbitrary"),
        name="in_proj_lat" if latent else "in_proj_ctx",
    )(*args)


def _attn_kernel(*refs, n_cache, lam_init):
    if n_cache:
        (q_ref, k_ref, v_ref, ck_ref, cv_ref, lam_ref, g_ref, o_ref, kbuf, vbuf) = refs

        @pl.when(pl.program_id(2) == 0)
        def _():
            kbuf[0:n_cache, :] = ck_ref[...].astype(BF16)
            kbuf[n_cache:, :] = k_ref[...]
            vbuf[0:n_cache, :] = cv_ref[...].astype(BF16)
            vbuf[n_cache:, :] = v_ref[...]

        k_all, v_all = kbuf[...], vbuf[...]
    else:
        (q_ref, k_ref, v_ref, lam_ref, g_ref, o_ref) = refs
        k_all, v_all = k_ref[...], v_ref[...]
    lv = lam_ref[...]
    lam = (jnp.exp(jnp.sum(lv[0:1] * lv[1:2], axis=-1, keepdims=True))
           - jnp.exp(jnp.sum(lv[2:3] * lv[3:4], axis=-1, keepdims=True)) + lam_init)
    q = q_ref[...]
    tq = q.shape[0]
    first_map = lax.broadcasted_iota(jnp.int32, (1, LANES), 1) < HEAD_DIM
    zero = jnp.zeros_like(q)
    q2 = jnp.concatenate([jnp.where(first_map, q, zero), jnp.where(first_map, zero, q)], axis=0)
    s = lax.dot_general(q2, k_all, (((1,), (1,)), ((), ())), preferred_element_type=F32)
    p = jnp.exp(s - jnp.max(s, axis=-1, keepdims=True))
    inv_l = 1.0 / jnp.sum(p, axis=-1, keepdims=True)
    pv = jnp.dot(p.astype(BF16), v_all, preferred_element_type=F32) * inv_l
    o = pv[:tq] - lam * pv[tq:]
    o_ref[...] = (_rms(o, g_ref[...]) * (1.0 - lam_init)).astype(BF16)


def _attn_call(q, k, v, cache, lam_vecs, subln_g, *, batch, seq, lam_init, tq):
    nq = seq // tq
    qmap = lambda b, h, i: (b * nq + i, h)
    kvmap = lambda b, h, i: (b, h)
    in_specs = [pl.BlockSpec((tq, LANES), qmap),
                pl.BlockSpec((seq, LANES), kvmap),
                pl.BlockSpec((seq, LANES), kvmap)]
    args = [q, k, v]
    scratch = []
    n_cache = 0
    if cache is not None:
        ck, cv = cache
        n_cache = ck.shape[0] // batch
        in_specs += [pl.BlockSpec((n_cache, LANES), kvmap)] * 2
        args += [ck, cv]
        scratch = [pltpu.VMEM((n_cache + seq, LANES), BF16)] * 2
    in_specs += [pl.BlockSpec(lam_vecs.shape, lambda b, h, i: (0, 0)),
                 pl.BlockSpec((1, LANES), lambda b, h, i: (0, 0))]
    args += [lam_vecs, subln_g]
    return pl.pallas_call(
        functools.partial(_attn_kernel, n_cache=n_cache, lam_init=lam_init),
        grid=(batch, N_HEADS, nq),
        in_specs=in_specs,
        out_specs=pl.BlockSpec((tq, LANES), qmap),
        out_shape=jax.ShapeDtypeStruct((batch * seq, N_HEADS * V_DIM), BF16),
        scratch_shapes=scratch,
        compiler_params=_params("arbitrary", "arbitrary", "arbitrary"),
        name="attn_lat" if cache is not None else "attn_ctx",
    )(*args)


def _dft_tables(seq):
    n = 2 * seq
    f = np.arange(seq, dtype=np.int64)[:, None]
    t = np.arange(seq, dtype=np.int64)[None, :]
    ang = 2.0 * np.pi * ((f * t) % n).astype(np.float64) / n
    alt = np.where(np.arange(seq) % 2 == 0, 1.0, -1.0)
    cos, msin = np.cos(ang), -np.sin(ang)
    msin[0, :] = alt
    fwd = np.concatenate([cos, msin], axis=0)
    wgt = np.where(f == 0, 1.0, 2.0) / n
    inv_c = (cos * wgt).T
    inv_s = (msin * wgt).T
    inv = np.concatenate([inv_c, inv_s], axis=1)
    return fwd.astype(np.float32), inv.astype(np.float32)


def _split_bf16(x):
    hi = x.astype(BF16)
    lo = (x - hi.astype(F32)).astype(BF16)
    return hi, lo


def _dot3(a_hi, a_lo, b):
    b_hi, b_lo = _split_bf16(b)
    d = functools.partial(jnp.dot, preferred_element_type=F32)
    return d(a_hi, b_hi) + (d(a_hi, b_lo) + d(a_lo, b_hi))


def _filt_kernel(emb_ref, w1_ref, b1_ref, w2_ref, b2_ref, fr_ref, w3f_ref, w3b_ref,
                 t_ref, dl_ref, fhi_ref, flo_ref, o_ref):
    seq = emb_ref.shape[0]
    hp = functools.partial(jnp.dot, preferred_element_type=F32, precision=lax.Precision.HIGHEST)
    fr = fr_ref[...]
    h = jnp.sin(fr * (hp(emb_ref[...], w1_ref[...]) + b1_ref[...]))
    h = jnp.sin(fr * (hp(h, w2_ref[...]) + b2_ref[...]))
    decay = jnp.exp(-t_ref[...] * jnp.abs(dl_ref[...]))
    fwd = hp(h, w3f_ref[...]) * decay
    bwd = hp(h, w3b_ref[...]) * decay
    hsum, hdif = fwd + bwd, fwd - bwd
    re = _dot3(fhi_ref[0:seq, :], flo_ref[0:seq, :], hsum)
    im = _dot3(fhi_ref[seq:, :], flo_ref[seq:, :], hdif)
    row = lax.broadcasted_iota(jnp.int32, (seq, 1), 0)
    alt = jnp.where(row % 2 == 0, 1.0, -1.0)
    nyq = jnp.sum(hsum * alt, axis=0, keepdims=True)
    o_ref[0] = re
    o_ref[1] = jnp.where(row == 0, nyq, im)


def _filt_call(emb, w1, b1, w2, b2, fr, w3, tcol, deltas, fwd_hi, fwd_lo, *, seq, ct):
    n_out = HYENA_ORDER * D_HYENA
    nct = n_out // ct
    per_order = D_HYENA // ct
    small = lambda shape: pl.BlockSpec(shape, lambda j: (0, 0))
    return pl.pallas_call(
        _filt_kernel,
        grid=(nct,),
        in_specs=[small((seq, FILT_PAD)), small((FILT_PAD, FILT_PAD)), small((1, FILT_PAD)),
                  small((FILT_PAD, FILT_PAD)), small((1, FILT_PAD)), small((1, FILT_PAD)),
                  pl.BlockSpec((FILT_PAD, ct), lambda j: (0, j)),
                  pl.BlockSpec((FILT_PAD, ct), lambda j: (0, nct + j)),
                  small((seq, 1)),
                  pl.BlockSpec((1, ct), lambda j: (0, j % per_order)),
                  _resident((2 * seq, seq)), _resident((2 * seq, seq))],
        out_specs=pl.BlockSpec((2, seq, ct), lambda j: (0, 0, j)),
        out_shape=jax.ShapeDtypeStruct((2, seq, n_out), F32),
        compiler_params=_params("arbitrary"),
        name=f"filt_{seq}",
    )(emb, w1, b1, w2, b2, fr, w3, w3, tcol, deltas, fwd_hi, fwd_lo)


def _hyena_kernel(uv_ref, u1_ref, u2_ref, wv_ref, w1_ref, w2_ref, bv_ref, b1_ref, b2_ref,
                  h0_ref, h1_ref, hb_ref, fwd_ref, inv_ref, o_ref):
    seq = uv_ref.shape[0]
    row = lax.broadcasted_iota(jnp.int32, (seq, 1), 0)
    first, last = row == 0, row == seq - 1

    def short_conv(u_ref, w_ref, b_ref):
        u = u_ref[...]
        prev = jnp.where(first, 0.0, pltpu.roll(u, 1, axis=0))
        nxt = jnp.where(last, 0.0, pltpu.roll(u, seq - 1, axis=0))
        return prev * w_ref[0:1, :] + u * w_ref[1:2, :] + nxt * w_ref[2:3, :] + b_ref[...]

    def long_conv(u, h_ref, bias):
        spec = jnp.dot(fwd_ref[...], u.astype(BF16), preferred_element_type=F32)
        ure, uim = spec[:seq], spec[seq:]
        hre, him = h_ref[0], h_ref[1]
        cross = uim * him
        yre = ure * hre - jnp.where(first, 0.0, cross)
        yim = jnp.where(first, cross, ure * him + uim * hre)
        y = jnp.concatenate([yre, yim], axis=0).astype(BF16)
        return jnp.dot(inv_ref[...], y, preferred_element_type=F32) + u * bias

    v = short_conv(uv_ref, wv_ref, bv_ref)
    x1 = short_conv(u1_ref, w1_ref, b1_ref)
    z = x1 * long_conv(v, h0_ref, hb_ref[0:1, :])
    x2 = short_conv(u2_ref, w2_ref, b2_ref)
    o_ref[...] = (x2 * long_conv(z, h1_ref, hb_ref[1:2, :])).astype(BF16)


def _hyena_call(u, conv_w, conv_b, hf, hy_bias, fwd, inv, *, batch, seq, ct):
    nct = D_HYENA // ct
    col = lambda part: (lambda c, b: (b, part * nct + c))
    wcol = lambda part: (lambda c, b: (0, part * nct + c))
    in_specs = ([pl.BlockSpec((seq, ct), col(p)) for p in range(3)]
                + [pl.BlockSpec((3, ct), wcol(p)) for p in range(3)]
                + [pl.BlockSpec((1, ct), wcol(p)) for p in range(3)]
                + [pl.BlockSpec((2, seq, ct), lambda c, b, o=o: (0, 0, o * nct + c))
                   for o in range(HYENA_ORDER)]
                + [pl.BlockSpec((HYENA_ORDER, ct), lambda c, b: (0, c)),
                   _resident((2 * seq, seq)), _resident((seq, 2 * seq))])
    return pl.pallas_call(
        _hyena_kernel,
        grid=(nct, batch),
        in_specs=in_specs,
        out_specs=pl.BlockSpec((seq, ct), lambda c, b: (b, c)),
        out_shape=jax.ShapeDtypeStruct((batch * seq, D_HYENA), BF16),
        compiler_params=_params("arbitrary", "arbitrary"),
        name=f"hyena_{seq}",
    )(u, u, u, conv_w, conv_w, conv_w, conv_b, conv_b, conv_b, hf, hf, hy_bias, fwd, inv)


def _post_kernel(x_ref, oa_ref, oh_ref, gt_ref, mod_ref, g2_ref, gf_ref,
                 wa_ref, wh_ref, wo_ref, wu_ref, wd_ref, y_ref):
    d = functools.partial(jnp.dot, preferred_element_type=F32)
    merged = (gt_ref[:, :D_MODEL] * d(oa_ref[...], wa_ref[...])
              + gt_ref[:, D_MODEL:] * d(oh_ref[...], wh_ref[...]))
    x = x_ref[...] + mod_ref[2:3, :] * d(merged.astype(BF16), wo_ref[...])
    h2 = (_rms(x, g2_ref[...]) * (1.0 + mod_ref[4:5, :]) + mod_ref[3:4, :]).astype(BF16)
    ff_chunk = D_MODEL
    acc = jnp.zeros_like(x)
    for c in range(D_FF // ff_chunk):
        sl = slice(c * ff_chunk, (c + 1) * ff_chunk)
        up = jnp.maximum(d(h2, wu_ref[:, sl]), 0.0)
        acc = acc + d((up * up).astype(BF16), wd_ref[sl, :])
    x = x + mod_ref[5:6, :] * acc
    y_ref[...] = _rms(x, gf_ref[...])


def _post_call(x, o_attn, o_hy, gates, mod, g2, gf, wa, wh, wo, wu, wd, *, seq, latent, tm):
    t = x.shape[0]
    per_seq = seq // tm if latent else 1
    row = lambda i: (i, 0)

    def mod_idx(i):
        return (1 + i // per_seq if latent else 0, 0, 0)

    return pl.pallas_call(
        _post_kernel,
        grid=(t // tm,),
        in_specs=[pl.BlockSpec((tm, D_MODEL), row),
                  pl.BlockSpec((tm, N_HEADS * V_DIM), row),
                  pl.BlockSpec((tm, D_HYENA), row),
                  pl.BlockSpec((tm, GATE_COLS), row),
                  pl.BlockSpec((None, 6, D_MODEL), mod_idx),
                  _resident((1, D_MODEL)), _resident((1, D_MODEL)),
                  _resident(wa.shape), _resident(wh.shape), _resident(wo.shape),
                  _resident(wu.shape), _resident(wd.shape)],
        out_specs=pl.BlockSpec((tm, D_MODEL), row),
        out_shape=jax.ShapeDtypeStruct((t, D_MODEL), F32),
        compiler_params=_params("arbitrary"),
        name="post_lat" if latent else "post_ctx",
    )(x, o_attn, o_hy, gates, mod, g2, gf, wa, wh, wo, wu, wd)


def _rope_tables(seq):
    half = HEAD_DIM // 2
    n = half // 2
    inv = ROPE_THETA ** (-jnp.arange(n, dtype=F32) / n)
    pos = jnp.arange(seq)
    ang_row = (pos // GRID_W).astype(F32)[:, None] * inv[None, :]
    ang_col = (pos % GRID_W).astype(F32)[:, None] * inv[None, :]
    zeros = jnp.zeros_like(ang_row)

    def per_map(a_row, a_col, lo, hi):
        return jnp.concatenate([lo(a_row), hi(a_row), lo(a_col), hi(a_col)], axis=-1)

    cos = per_map(ang_row, ang_col, jnp.cos, jnp.cos)
    sin_lo = per_map(ang_row, ang_col, lambda a: -jnp.sin(a), lambda a: zeros)
    sin_hi = per_map(ang_row, ang_col, lambda a: zeros, jnp.sin)
    return tuple(jnp.concatenate([tab, tab], axis=-1) for tab in (cos, sin_lo, sin_hi))


def _filter_embedding(seq):
    bands = (FILTER_EMB - 1) // 2
    t = jnp.linspace(0.0, 1.0, seq, dtype=F32)[:, None]
    wpos = 2.0 * math.pi * jnp.arange(seq, dtype=F32)[:, None] / seq
    f = jnp.linspace(1e-4, bands - 1, bands, dtype=F32)[None, :]
    emb = jnp.concatenate([t, jnp.cos(f * wpos), -jnp.sin(f * wpos)], axis=-1)
    return jnp.pad(emb, ((0, 0), (0, FILT_PAD - FILTER_EMB))), t


def _pad_to(x, rows, cols):
    return jnp.pad(x, ((0, rows - x.shape[0]), (0, cols - x.shape[1])))


def kernel(x_prompt, x_sample, cache_k, cache_v, c, c_ctx, w_ada, b_ada, norm1_g, norm2_g, w_in,
           lam_q1, lam_k1, lam_q2, lam_k2, attn_subln_g, conv_w, conv_b, filt_w1, filt_b1,
           filt_w2, filt_b2, filt_w3, filt_freq, hy_bias, w_br_attn, w_br_hy, w_out, w_up,
           w_down, final_g):
    depth = w_in.shape[0]
    assert depth == 1, "single trunk layer"
    layer = 0
    lam_init = 0.8 - 0.6 * math.exp(-0.3 * layer)
    n_ctx, ctx_len, _ = x_prompt.shape
    n_lat, lat_len, _ = x_sample.shape
    past = cache_k.shape[2]

    cvec = jnp.concatenate([c_ctx[None, :], c], axis=0)
    cvec = jnp.pad(cvec, ((0, MOD_ROWS - cvec.shape[0]), (0, 0)))
    mod = _mod_call(cvec, w_ada[layer], b_ada[layer][None, :]).reshape(MOD_ROWS, 6, D_MODEL)

    w_in_b = w_in[layer].astype(BF16)
    wa, wh, wo = (w[layer].astype(BF16) for w in (w_br_attn, w_br_hy, w_out))
    wu, wd = w_up[layer].astype(BF16), w_down[layer].astype(BF16)
    g1, g2, gf = norm1_g[layer][None, :], norm2_g[layer][None, :], final_g[None, :]
    lam_vecs = jnp.stack([lam_q1[layer], lam_k1[layer], lam_q2[layer], lam_k2[layer]])
    subln_g = attn_subln_g[layer][None, :]

    w1 = _pad_to(filt_w1[layer], FILT_PAD, FILT_PAD)
    w2 = _pad_to(filt_w2[layer], FILT_PAD, FILT_PAD)
    w3 = _pad_to(filt_w3[layer], FILT_PAD, filt_w3.shape[2])
    b1 = _pad_to(filt_b1[layer][None, :], 1, FILT_PAD)
    b2 = _pad_to(filt_b2[layer][None, :], 1, FILT_PAD)
    fr = _pad_to(filt_freq[layer][None, :], 1, FILT_PAD)
    deltas = jnp.linspace(math.log(DECAY_TARGET) / FAST_DECAY_PCT,
                          math.log(DECAY_TARGET) / SLOW_DECAY_PCT, D_HYENA, dtype=F32)[None, :]

    def branch(x3, batch, seq, latent):
        x = x3.reshape(batch * seq, D_MODEL)
        tm = 512
        rope = _rope_tables(seq) if latent else None
        outs = _in_proj_call(x, mod, g1, w_in_b, rope, seq=seq, latent=latent, tm=tm)
        if latent:
            q, k, v, u, gates = outs
            cache = (cache_k[:, layer].reshape(batch * past, K_COLS),
                     cache_v[:, layer].reshape(batch * past, V_COLS))
            kf = vf = None
        else:
            q, k, v, kf, vf, u, gates = outs
            cache = None
        o_attn = _attn_call(q, k, v, cache, lam_vecs, subln_g, batch=batch, seq=seq,
                            lam_init=lam_init, tq=256)

        fwd_np, inv_np = _dft_tables(seq)
        fwd_f32 = jnp.asarray(fwd_np)
        fwd_hi, fwd_lo = _split_bf16(fwd_f32)
        inv_b = jnp.asarray(inv_np).astype(BF16)
        emb, tcol = _filter_embedding(seq)
        hf = _filt_call(emb, w1, b1, w2, b2, fr, w3, tcol, deltas, fwd_hi, fwd_lo, seq=seq, ct=256)
        o_hy = _hyena_call(u, conv_w[layer], conv_b[layer][None, :], hf, hy_bias[layer],
                           fwd_hi, inv_b, batch=batch, seq=seq, ct=256)

        y = _post_call(x, o_attn, o_hy, gates, mod, g2, gf, wa, wh, wo, wu, wd,
                       seq=seq, latent=latent, tm=tm)
        return y.reshape(batch, seq, D_MODEL), kf, vf

    y_prompt, kf, vf = branch(x_prompt, n_ctx, ctx_len, False)
    y_sample, _, _ = branch(x_sample, n_lat, lat_len, True)
    new_cache_k = kf.reshape(n_ctx, depth, ctx_len, N_HEADS, 2, HEAD_DIM)
    new_cache_v = vf.reshape(n_ctx, depth, ctx_len, N_HEADS, V_DIM)
    return (y_prompt, y_sample, new_cache_k, new_cache_v)
```

```python
import functools
import math

import numpy as np
import jax
import jax.numpy as jnp
from jax import lax
from jax.experimental import pallas as pl
from jax.experimental.pallas import tpu as pltpu

D_MODEL = 1024
GRID_W = 64
N_HEADS = 8
HEAD_DIM = 64
V_DIM = 2 * HEAD_DIM
D_HYENA = D_MODEL // 2
HYENA_ORDER = 2
FILTER_EMB = 33
FILTER_HIDDEN = 64
D_FF = 4 * D_MODEL
ROPE_THETA = 10000.0
EPS = 1e-6
LOG2_E = math.log2(math.e)
DECAY_TARGET = 1e-2
FAST_DECAY_PCT = 0.3
SLOW_DECAY_PCT = 1.5

Q_COLS = N_HEADS * 2 * HEAD_DIM
K_COLS = N_HEADS * 2 * HEAD_DIM
V_COLS = N_HEADS * V_DIM
HY_COLS = 3 * D_HYENA
GATE_COLS = 2 * D_MODEL
Q_OFF = 0
K_OFF = Q_OFF + Q_COLS
V_OFF = K_OFF + K_COLS
HY_OFF = V_OFF + V_COLS
GATE_OFF = HY_OFF + HY_COLS
IN_COLS = GATE_OFF + GATE_COLS

LANES = 128
MOD_ROWS = 16
FILT_PAD = 128
VMEM_LIMIT = 56 * 1024 * 1024

BF16 = jnp.bfloat16
F32 = jnp.float32


def _resident(shape):
    nd = len(shape)
    return pl.BlockSpec(shape, lambda *_: (0,) * nd, pipeline_mode=pl.Buffered(1))


def _params(*sem):
    return pltpu.CompilerParams(dimension_semantics=sem, vmem_limit_bytes=VMEM_LIMIT)


def _rms(x, g):
    return x * lax.rsqrt(jnp.mean(x * x, axis=-1, keepdims=True) + EPS) * g


def _mod_kernel(c_ref, w_ref, b_ref, o_ref):
    c = c_ref[...]
    s = (c * jax.nn.sigmoid(c)).astype(BF16)
    o_ref[...] = jnp.dot(s, w_ref[...].astype(BF16), preferred_element_type=F32) + b_ref[...]


def _mod_call(cvec, w_ada, b_ada):
    tn = 1024
    n = w_ada.shape[1]
    return pl.pallas_call(
        _mod_kernel,
        grid=(n // tn,),
        in_specs=[pl.BlockSpec((MOD_ROWS, D_MODEL), lambda j: (0, 0)),
                  pl.BlockSpec((D_MODEL, tn), lambda j: (0, j)),
                  pl.BlockSpec((1, tn), lambda j: (0, j))],
        out_specs=pl.BlockSpec((MOD_ROWS, tn), lambda j: (0, j)),
        out_shape=jax.ShapeDtypeStruct((MOD_ROWS, n), F32),
        compiler_params=_params("arbitrary"),
        name="mod",
    )(cvec, w_ada, b_ada)


def _rope(x, cos, sin_lo, sin_hi):
    return (x * cos + pltpu.roll(x, 16, axis=1) * sin_hi
            + pltpu.roll(x, LANES - 16, axis=1) * sin_lo)


def _in_proj_kernel(*refs, latent):
    if latent:
        (x_ref, mod_ref, g_ref, w_ref, cos_ref, slo_ref, shi_ref,
         q_ref, kt_ref, v_ref, u_ref, gt_ref) = refs
    else:
        (x_ref, mod_ref, g_ref, w_ref,
         q_ref, kt_ref, v_ref, ktf_ref, vf_ref, u_ref, gt_ref) = refs
    tm = x_ref.shape[0]
    x = x_ref[...]
    h = _rms(x, g_ref[...]) * (1.0 + mod_ref[1:2, :]) + mod_ref[0:1, :]
    hb = h.astype(BF16)

    def proj(off, width):
        return jnp.dot(hb, w_ref[:, off:off + width], preferred_element_type=F32)

    q = proj(Q_OFF, Q_COLS) * (HEAD_DIM ** -0.5 * LOG2_E)
    k = proj(K_OFF, K_COLS)
    if latent:
        cos, slo, shi = cos_ref[...], slo_ref[...], shi_ref[...]
        k_heads = []
        for hd in range(N_HEADS):
            sl = slice(hd * LANES, (hd + 1) * LANES)
            q_ref[:, sl] = _rope(q[:, sl], cos, slo, shi).astype(BF16)
            k_heads.append(_rope(k[:, sl], cos, slo, shi))
        kt_ref[...] = jnp.concatenate(k_heads, axis=1).T.astype(BF16)
    else:
        q_ref[...] = q.astype(BF16)
        seq = kt_ref.shape[1]
        for s in range(tm // seq):
            kt = k[s * seq:(s + 1) * seq, :].T
            rows = slice(s * K_COLS, (s + 1) * K_COLS)
            ktf_ref[rows, :] = kt
            kt_ref[rows, :] = kt.astype(BF16)
    v = proj(V_OFF, V_COLS)
    v_ref[...] = v.astype(BF16)
    if not latent:
        for hd in range(N_HEADS):
            vf_ref[pl.ds(hd, tm, stride=N_HEADS), :] = v[:, hd * LANES:(hd + 1) * LANES]
    u_ref[...] = proj(HY_OFF, HY_COLS)
    gt_ref[...] = jax.nn.sigmoid(proj(GATE_OFF, GATE_COLS))


def _in_proj_call(x, mod, g1, w_in, rope, *, seq, latent, tm):
    t = x.shape[0]
    per_seq = seq // tm if latent else 1

    def mod_idx(i):
        return (1 + i // per_seq if latent else 0, 0, 0)

    row = lambda i: (i, 0)
    in_specs = [pl.BlockSpec((tm, D_MODEL), row),
                pl.BlockSpec((None, 6, D_MODEL), mod_idx),
                _resident((1, D_MODEL)),
                _resident((D_MODEL, IN_COLS))]
    args = [x, mod, g1, w_in]
    tok = lambda width, dtype: ((t, width), dtype, pl.BlockSpec((tm, width), row))
    if latent:
        in_specs += [pl.BlockSpec((tm, LANES), lambda i: (i % per_seq, 0))] * 3
        args += list(rope)
        kt_spec = pl.BlockSpec((K_COLS, tm), lambda i: (i // per_seq, i % per_seq))
    else:
        kt_spec = pl.BlockSpec((tm // seq * K_COLS, seq), row)
    kt_shape = (t // seq * K_COLS, seq)
    outs = [tok(Q_COLS, BF16), (kt_shape, BF16, kt_spec), tok(V_COLS, BF16)]
    if not latent:
        outs += [(kt_shape, F32, kt_spec),
                 ((t * N_HEADS, LANES), F32, pl.BlockSpec((tm * N_HEADS, LANES), row))]
    outs += [tok(HY_COLS, F32), tok(GATE_COLS, F32)]
    return pl.pallas_call(
        functools.partial(_in_proj_kernel, latent=latent),
        grid=(t // tm,),
        in_specs=in_specs,
        out_specs=[spec for _, _, spec in outs],
        out_shape=[jax.ShapeDtypeStruct(s, d) for s, d, _ in outs],
        compiler_params=_params("arbitrary"),
        name="in_proj_lat" if latent else "in_proj_ctx",
    )(*args)


def _attn_kernel(*refs, n_cache, lam_init, tq):
    if n_cache:
        (q_ref, kt_ref, v_ref, ckt_ref, cv_ref, lam_ref, g_ref, o_ref, kt_s, v2_s) = refs
        head = pl.program_id(1)
        kt_s[:, 0:n_cache] = ckt_ref[...].astype(BF16)
        kt_s[:, n_cache:] = kt_ref[...]
        v2_s[0:n_cache, 0:LANES] = cv_ref[pl.ds(head, n_cache, stride=N_HEADS), :].astype(BF16)
        v2_s[n_cache:, 0:LANES] = v_ref[...]
        kt = kt_s[...]
    else:
        (q_ref, kt_ref, v_ref, lam_ref, g_ref, o_ref, v2_s) = refs
        v2_s[:, 0:LANES] = v_ref[...]
        kt = kt_ref[...]
    n_keys = v2_s.shape[0]
    v2_s[:, LANES:] = jnp.ones((n_keys, LANES), BF16)
    v2 = v2_s[...]
    lv = lam_ref[...]
    lam = (jnp.exp(jnp.sum(lv[0:1] * lv[1:2], axis=-1, keepdims=True))
           - jnp.exp(jnp.sum(lv[2:3] * lv[3:4], axis=-1, keepdims=True)) + lam_init)
    gain = g_ref[...] * (1.0 - lam_init)
    first_map = lax.broadcasted_iota(jnp.int32, (1, LANES), 1) < HEAD_DIM
    zero = jnp.zeros((tq, LANES), BF16)

    def scores(i):
        q = q_ref[i * tq:(i + 1) * tq, :]
        q2 = jnp.concatenate([jnp.where(first_map, q, zero), jnp.where(first_map, zero, q)], axis=0)
        s = jnp.dot(q2, kt, preferred_element_type=F32)
        return s, jnp.max(s, axis=-1, keepdims=True)

    def finish(i, s, m):
        p = jnp.exp2(s - m).astype(BF16)
        pv = jnp.dot(p, v2, preferred_element_type=F32)
        pv = pv[:, :LANES] / pv[:, LANES:]
        o = pv[:tq] - lam * pv[tq:]
        o_ref[i * tq:(i + 1) * tq, :] = (_rms(o, 1.0) * gain).astype(BF16)

    n_tiles = q_ref.shape[0] // tq
    pending = scores(0)
    for i in range(n_tiles):
        nxt = scores(i + 1) if i + 1 < n_tiles else None
        finish(i, *pending)
        pending = nxt


def _attn_call(q, kt, v, cache, lam_vecs, subln_g, *, batch, seq, lam_init, tq):
    const = lambda b, h: (0, 0)
    in_specs = [pl.BlockSpec((seq, LANES), lambda b, h: (b, h)),
                pl.BlockSpec((2 * HEAD_DIM, seq), lambda b, h: (b * N_HEADS + h, 0)),
                pl.BlockSpec((seq, LANES), lambda b, h: (b, h))]
    args = [q, kt, v]
    n_cache = 0
    if cache is not None:
        ckt, cv = cache
        n_cache = ckt.shape[1]
        in_specs += [pl.BlockSpec((2 * HEAD_DIM, n_cache), lambda b, h: (b * N_HEADS + h, 0)),
                     pl.BlockSpec((n_cache * N_HEADS, LANES), lambda b, h: (b, 0))]
        args += [ckt, cv]
    n_keys = n_cache + seq
    scratch = [pltpu.VMEM((n_keys, 2 * LANES), BF16)]
    if n_cache:
        scratch = [pltpu.VMEM((2 * HEAD_DIM, n_keys), BF16)] + scratch
    in_specs += [pl.BlockSpec(lam_vecs.shape, const), pl.BlockSpec((1, LANES), const)]
    args += [lam_vecs, subln_g]
    return pl.pallas_call(
        functools.partial(_attn_kernel, n_cache=n_cache, lam_init=lam_init, tq=tq),
        grid=(batch, N_HEADS),
        in_specs=in_specs,
        out_specs=pl.BlockSpec((seq, LANES), lambda b, h: (b, h)),
        out_shape=jax.ShapeDtypeStruct((batch * seq, N_HEADS * V_DIM), BF16),
        scratch_shapes=scratch,
        compiler_params=_params("arbitrary", "arbitrary"),
        name="attn_lat" if cache is not None else "attn_ctx",
    )(*args)


def _dft_tables(seq):
    n = 2 * seq
    f = np.arange(seq, dtype=np.int64)[:, None]
    t = np.arange(seq, dtype=np.int64)[None, :]
    ang = 2.0 * np.pi * ((f * t) % n).astype(np.float64) / n
    alt = np.where(np.arange(seq) % 2 == 0, 1.0, -1.0)
    cos, msin = np.cos(ang), -np.sin(ang)
    msin[0, :] = alt
    fwd = np.concatenate([cos, msin], axis=0)
    wgt = np.where(f == 0, 1.0, 2.0) / n
    inv_c = (cos * wgt).T
    inv_s = (msin * wgt).T
    inv = np.concatenate([inv_c, inv_s], axis=1)
    return fwd.astype(np.float32), inv.astype(np.float32)


def _split_bf16(x):
    hi = x.astype(BF16)
    lo = (x - hi.astype(F32)).astype(BF16)
    return hi, lo


def _dot3(a_hi, a_lo, b):
    b_hi, b_lo = _split_bf16(b)
    d = functools.partial(jnp.dot, preferred_element_type=F32)
    return d(a_hi, b_hi) + (d(a_hi, b_lo) + d(a_lo, b_hi))


def _filt_kernel(emb_ref, w1_ref, b1_ref, w2_ref, b2_ref, fr_ref, w3f_ref, w3b_ref,
                 t_ref, dl_ref, fhi_ref, flo_ref, o_ref):
    seq = emb_ref.shape[0]
    hp = functools.partial(jnp.dot, preferred_element_type=F32, precision=lax.Precision.HIGHEST)
    fr = fr_ref[...]
    h = jnp.sin(fr * (hp(emb_ref[...], w1_ref[...]) + b1_ref[...]))
    h = jnp.sin(fr * (hp(h, w2_ref[...]) + b2_ref[...]))
    decay = jnp.exp(-t_ref[...] * jnp.abs(dl_ref[...]))
    fwd = hp(h, w3f_ref[...]) * decay
    bwd = hp(h, w3b_ref[...]) * decay
    hsum, hdif = fwd + bwd, fwd - bwd
    re = _dot3(fhi_ref[0:seq, :], flo_ref[0:seq, :], hsum)
    im = _dot3(fhi_ref[seq:, :], flo_ref[seq:, :], hdif)
    row = lax.broadcasted_iota(jnp.int32, (seq, 1), 0)
    alt = jnp.where(row % 2 == 0, 1.0, -1.0)
    nyq = jnp.sum(hsum * alt, axis=0, keepdims=True)
    o_ref[0] = re
    o_ref[1] = jnp.where(row == 0, nyq, im)


def _filt_call(emb, w1, b1, w2, b2, fr, w3, tcol, deltas, fwd_hi, fwd_lo, *, seq, ct):
    n_out = HYENA_ORDER * D_HYENA
    nct = n_out // ct
    per_order = D_HYENA // ct
    small = lambda shape: pl.BlockSpec(shape, lambda j: (0, 0))
    return pl.pallas_call(
        _filt_kernel,
        grid=(nct,),
        in_specs=[small((seq, FILT_PAD)), small((FILT_PAD, FILT_PAD)), small((1, FILT_PAD)),
                  small((FILT_PAD, FILT_PAD)), small((1, FILT_PAD)), small((1, FILT_PAD)),
                  pl.BlockSpec((FILT_PAD, ct), lambda j: (0, j)),
                  pl.BlockSpec((FILT_PAD, ct), lambda j: (0, nct + j)),
                  small((seq, 1)),
                  pl.BlockSpec((1, ct), lambda j: (0, j % per_order)),
                  _resident((2 * seq, seq)), _resident((2 * seq, seq))],
        out_specs=pl.BlockSpec((2, seq, ct), lambda j: (0, 0, j)),
        out_shape=jax.ShapeDtypeStruct((2, seq, n_out), F32),
        compiler_params=_params("arbitrary"),
        name=f"filt_{seq}",
    )(emb, w1, b1, w2, b2, fr, w3, w3, tcol, deltas, fwd_hi, fwd_lo)


def _hyena_kernel(uv_ref, u1_ref, u2_ref, wv_ref, w1_ref, w2_ref, bv_ref, b1_ref, b2_ref,
                  h0_ref, h1_ref, hb_ref, fwd_ref, inv_ref, o_ref):
    seq = uv_ref.shape[0]
    row = lax.broadcasted_iota(jnp.int32, (seq, 1), 0)
    first, last = row == 0, row == seq - 1

    def short_conv(u_ref, w_ref, b_ref):
        u = u_ref[...]
        prev = jnp.where(first, 0.0, pltpu.roll(u, 1, axis=0))
        nxt = jnp.where(last, 0.0, pltpu.roll(u, seq - 1, axis=0))
        return prev * w_ref[0:1, :] + u * w_ref[1:2, :] + nxt * w_ref[2:3, :] + b_ref[...]

    def long_conv(u, h_ref, bias):
        spec = jnp.dot(fwd_ref[...], u.astype(BF16), preferred_element_type=F32)
        ure, uim = spec[:seq], spec[seq:]
        hre, him = h_ref[0], h_ref[1]
        cross = uim * him
        yre = ure * hre - jnp.where(first, 0.0, cross)
        yim = jnp.where(first, cross, ure * him + uim * hre)
        y = jnp.concatenate([yre, yim], axis=0).astype(BF16)
        return jnp.dot(inv_ref[...], y, preferred_element_type=F32) + u * bias

    v = short_conv(uv_ref, wv_ref, bv_ref)
    x1 = short_conv(u1_ref, w1_ref, b1_ref)
    z = x1 * long_conv(v, h0_ref, hb_ref[0:1, :])
    x2 = short_conv(u2_ref, w2_ref, b2_ref)
    o_ref[...] = (x2 * long_conv(z, h1_ref, hb_ref[1:2, :])).astype(BF16)


def _hyena_call(u, conv_w, conv_b, hf, hy_bias, fwd, inv, *, batch, seq, ct):
    nct = D_HYENA // ct
    col = lambda part: (lambda c, b: (b, part * nct + c))
    wcol = lambda part: (lambda c, b: (0, part * nct + c))
    in_specs = ([pl.BlockSpec((seq, ct), col(p)) for p in range(3)]
                + [pl.BlockSpec((3, ct), wcol(p)) for p in range(3)]
                + [pl.BlockSpec((1, ct), wcol(p)) for p in range(3)]
                + [pl.BlockSpec((2, seq, ct), lambda c, b, o=o: (0, 0, o * nct + c))
                   for o in range(HYENA_ORDER)]
                + [pl.BlockSpec((HYENA_ORDER, ct), lambda c, b: (0, c)),
                   _resident((2 * seq, seq)), _resident((seq, 2 * seq))])
    return pl.pallas_call(
        _hyena_kernel,
        grid=(nct, batch),
        in_specs=in_specs,
        out_specs=pl.BlockSpec((seq, ct), lambda c, b: (b, c)),
        out_shape=jax.ShapeDtypeStruct((batch * seq, D_HYENA), BF16),
        compiler_params=_params("arbitrary", "arbitrary"),
        name=f"hyena_{seq}",
    )(u, u, u, conv_w, conv_w, conv_w, conv_b, conv_b, conv_b, hf, hf, hy_bias, fwd, inv)


def _post_kernel(x_ref, oa_ref, oh_ref, gt_ref, mod_ref, g2_ref, gf_ref,
                 wa_ref, wh_ref, wo_ref, wu_ref, wd_ref, y_ref):
    d = functools.partial(jnp.dot, preferred_element_type=F32)
    merged = (gt_ref[:, :D_MODEL] * d(oa_ref[...], wa_ref[...])
              + gt_ref[:, D_MODEL:] * d(oh_ref[...], wh_ref[...]))
    x = x_ref[...] + mod_ref[2:3, :] * d(merged.astype(BF16), wo_ref[...])
    h2 = (_rms(x, g2_ref[...]) * (1.0 + mod_ref[4:5, :]) + mod_ref[3:4, :]).astype(BF16)
    ff_chunk = D_MODEL
    acc = jnp.zeros_like(x)
    for c in range(D_FF // ff_chunk):
        sl = slice(c * ff_chunk, (c + 1) * ff_chunk)
        up = jnp.maximum(d(h2, wu_ref[:, sl]), 0.0)
        acc = acc + d((up * up).astype(BF16), wd_ref[sl, :])
    x = x + mod_ref[5:6, :] * acc
    y_ref[...] = _rms(x, gf_ref[...])


def _post_call(x, o_attn, o_hy, gates, mod, g2, gf, wa, wh, wo, wu, wd, *, seq, latent, tm):
    t = x.shape[0]
    per_seq = seq // tm if latent else 1
    row = lambda i: (i, 0)

    def mod_idx(i):
        return (1 + i // per_seq if latent else 0, 0, 0)

    return pl.pallas_call(
        _post_kernel,
        grid=(t // tm,),
        in_specs=[pl.BlockSpec((tm, D_MODEL), row),
                  pl.BlockSpec((tm, N_HEADS * V_DIM), row),
                  pl.BlockSpec((tm, D_HYENA), row),
                  pl.BlockSpec((tm, GATE_COLS), row),
                  pl.BlockSpec((None, 6, D_MODEL), mod_idx),
                  _resident((1, D_MODEL)), _resident((1, D_MODEL)),
                  _resident(wa.shape), _resident(wh.shape), _resident(wo.shape),
                  _resident(wu.shape), _resident(wd.shape)],
        out_specs=pl.BlockSpec((tm, D_MODEL), row),
        out_shape=jax.ShapeDtypeStruct((t, D_MODEL), F32),
        compiler_params=_params("arbitrary"),
        name="post_lat" if latent else "post_ctx",
    )(x, o_attn, o_hy, gates, mod, g2, gf, wa, wh, wo, wu, wd)


def _rope_tables(seq):
    half = HEAD_DIM // 2
    n = half // 2
    inv = ROPE_THETA ** (-jnp.arange(n, dtype=F32) / n)
    pos = jnp.arange(seq)
    ang_row = (pos // GRID_W).astype(F32)[:, None] * inv[None, :]
    ang_col = (pos % GRID_W).astype(F32)[:, None] * inv[None, :]
    zeros = jnp.zeros_like(ang_row)

    def per_map(a_row, a_col, lo, hi):
        return jnp.concatenate([lo(a_row), hi(a_row), lo(a_col), hi(a_col)], axis=-1)

    cos = per_map(ang_row, ang_col, jnp.cos, jnp.cos)
    sin_lo = per_map(ang_row, ang_col, lambda a: -jnp.sin(a), lambda a: zeros)
    sin_hi = per_map(ang_row, ang_col, lambda a: zeros, jnp.sin)
    return tuple(jnp.concatenate([tab, tab], axis=-1) for tab in (cos, sin_lo, sin_hi))


def _filter_embedding(seq):
    bands = (FILTER_EMB - 1) // 2
    t = jnp.linspace(0.0, 1.0, seq, dtype=F32)[:, None]
    wpos = 2.0 * math.pi * jnp.arange(seq, dtype=F32)[:, None] / seq
    f = jnp.linspace(1e-4, bands - 1, bands, dtype=F32)[None, :]
    emb = jnp.concatenate([t, jnp.cos(f * wpos), -jnp.sin(f * wpos)], axis=-1)
    return jnp.pad(emb, ((0, 0), (0, FILT_PAD - FILTER_EMB))), t


def _pad_to(x, rows, cols):
    return jnp.pad(x, ((0, rows - x.shape[0]), (0, cols - x.shape[1])))


def kernel(x_prompt, x_sample, cache_k, cache_v, c, c_ctx, w_ada, b_ada, norm1_g, norm2_g, w_in,
           lam_q1, lam_k1, lam_q2, lam_k2, attn_subln_g, conv_w, conv_b, filt_w1, filt_b1,
           filt_w2, filt_b2, filt_w3, filt_freq, hy_bias, w_br_attn, w_br_hy, w_out, w_up,
           w_down, final_g):
    depth = w_in.shape[0]
    assert depth == 1, "single trunk layer"
    layer = 0
    lam_init = 0.8 - 0.6 * math.exp(-0.3 * layer)
    n_ctx, ctx_len, _ = x_prompt.shape
    n_lat, lat_len, _ = x_sample.shape
    past = cache_k.shape[2]

    cvec = jnp.concatenate([c_ctx[None, :], c], axis=0)
    cvec = jnp.pad(cvec, ((0, MOD_ROWS - cvec.shape[0]), (0, 0)))
    mod = _mod_call(cvec, w_ada[layer], b_ada[layer][None, :]).reshape(MOD_ROWS, 6, D_MODEL)

    w_in_b = w_in[layer].astype(BF16)
    wa, wh, wo = (w[layer].astype(BF16) for w in (w_br_attn, w_br_hy, w_out))
    wu, wd = w_up[layer].astype(BF16), w_down[layer].astype(BF16)
    g1, g2, gf = norm1_g[layer][None, :], norm2_g[layer][None, :], final_g[None, :]
    lam_vecs = jnp.stack([lam_q1[layer], lam_k1[layer], lam_q2[layer], lam_k2[layer]])
    subln_g = attn_subln_g[layer][None, :]

    w1 = _pad_to(filt_w1[layer], FILT_PAD, FILT_PAD)
    w2 = _pad_to(filt_w2[layer], FILT_PAD, FILT_PAD)
    w3 = _pad_to(filt_w3[layer], FILT_PAD, filt_w3.shape[2])
    b1 = _pad_to(filt_b1[layer][None, :], 1, FILT_PAD)
    b2 = _pad_to(filt_b2[layer][None, :], 1, FILT_PAD)
    fr = _pad_to(filt_freq[layer][None, :], 1, FILT_PAD)
    deltas = jnp.linspace(math.log(DECAY_TARGET) / FAST_DECAY_PCT,
                          math.log(DECAY_TARGET) / SLOW_DECAY_PCT, D_HYENA, dtype=F32)[None, :]

    def branch(x3, batch, seq, latent):
        x = x3.reshape(batch * seq, D_MODEL)
        tm = 512
        rope = _rope_tables(seq) if latent else None
        outs = _in_proj_call(x, mod, g1, w_in_b, rope, seq=seq, latent=latent, tm=tm)
        if latent:
            q, kt, v, u, gates = outs
            ckt = jnp.transpose(cache_k[:, layer], (0, 2, 3, 4, 1)).reshape(batch * K_COLS, past)
            cache = (ckt, cache_v[:, layer].reshape(batch * past * N_HEADS, V_DIM))
            kf = vf = None
        else:
            q, kt, v, kf, vf, u, gates = outs
            cache = None
        o_attn = _attn_call(q, kt, v, cache, lam_vecs, subln_g, batch=batch, seq=seq,
                            lam_init=lam_init, tq=128)

        fwd_np, inv_np = _dft_tables(seq)
        fwd_f32 = jnp.asarray(fwd_np)
        fwd_hi, fwd_lo = _split_bf16(fwd_f32)
        inv_b = jnp.asarray(inv_np).astype(BF16)
        emb, tcol = _filter_embedding(seq)
        hf = _filt_call(emb, w1, b1, w2, b2, fr, w3, tcol, deltas, fwd_hi, fwd_lo, seq=seq, ct=256)
        o_hy = _hyena_call(u, conv_w[layer], conv_b[layer][None, :], hf, hy_bias[layer],
                           fwd_hi, inv_b, batch=batch, seq=seq, ct=256)

        y = _post_call(x, o_attn, o_hy, gates, mod, g2, gf, wa, wh, wo, wu, wd,
                       seq=seq, latent=latent, tm=tm)
        return y.reshape(batch, seq, D_MODEL), kf, vf

    y_prompt, kf, vf = branch(x_prompt, n_ctx, ctx_len, False)
    y_sample, _, _ = branch(x_sample, n_lat, lat_len, True)
    new_cache_k = jnp.transpose(kf.reshape(n_ctx, depth, N_HEADS, 2, HEAD_DIM, ctx_len),
                                (0, 1, 5, 2, 3, 4))
    new_cache_v = vf.reshape(n_ctx, depth, ctx_len, N_HEADS, V_DIM)
    return (y_prompt, y_sample, new_cache_k, new_cache_v)
```

```python
import functools
import math

import numpy as np
import jax
import jax.numpy as jnp
from jax import lax
from jax.experimental import pallas as pl
from jax.experimental.pallas import tpu as pltpu

D_MODEL = 1024
GRID_W = 64
N_HEADS = 8
HEAD_DIM = 64
V_DIM = 2 * HEAD_DIM
D_HYENA = D_MODEL // 2
HYENA_ORDER = 2
FILTER_EMB = 33
FILTER_HIDDEN = 64
D_FF = 4 * D_MODEL
ROPE_THETA = 10000.0
EPS = 1e-6
LOG2_E = math.log2(math.e)
DECAY_TARGET = 1e-2
FAST_DECAY_PCT = 0.3
SLOW_DECAY_PCT = 1.5

Q_COLS = N_HEADS * 2 * HEAD_DIM
K_COLS = N_HEADS * 2 * HEAD_DIM
V_COLS = N_HEADS * V_DIM
HY_COLS = 3 * D_HYENA
GATE_COLS = 2 * D_MODEL
Q_OFF = 0
K_OFF = Q_OFF + Q_COLS
V_OFF = K_OFF + K_COLS
HY_OFF = V_OFF + V_COLS
GATE_OFF = HY_OFF + HY_COLS
IN_COLS = GATE_OFF + GATE_COLS

LANES = 128
MXU_WIDTH = 256
MOD_ROWS = 16
FILT_PAD = 128
VMEM_LIMIT = 56 * 1024 * 1024

BF16 = jnp.bfloat16
F32 = jnp.float32


def _resident(shape):
    nd = len(shape)
    return pl.BlockSpec(shape, lambda *_: (0,) * nd, pipeline_mode=pl.Buffered(1))


def _params(*sem):
    return pltpu.CompilerParams(dimension_semantics=sem, vmem_limit_bytes=VMEM_LIMIT)


def _rms(x, g):
    return x * lax.rsqrt(jnp.mean(x * x, axis=-1, keepdims=True) + EPS) * g


def _mod_kernel(c_ref, w_ref, b_ref, o_ref):
    c = c_ref[...]
    s = (c * jax.nn.sigmoid(c)).astype(BF16)
    o_ref[...] = jnp.dot(s, w_ref[...].astype(BF16), preferred_element_type=F32) + b_ref[...]


def _mod_call(cvec, w_ada, b_ada):
    tn = 1024
    n = w_ada.shape[1]
    return pl.pallas_call(
        _mod_kernel,
        grid=(n // tn,),
        in_specs=[pl.BlockSpec((MOD_ROWS, D_MODEL), lambda j: (0, 0)),
                  pl.BlockSpec((D_MODEL, tn), lambda j: (0, j)),
                  pl.BlockSpec((1, tn), lambda j: (0, j))],
        out_specs=pl.BlockSpec((MOD_ROWS, tn), lambda j: (0, j)),
        out_shape=jax.ShapeDtypeStruct((MOD_ROWS, n), F32),
        compiler_params=_params("arbitrary"),
        name="mod",
    )(cvec, w_ada, b_ada)


def _rope(x, cos, sin_lo, sin_hi):
    return (x * cos + pltpu.roll(x, 16, axis=1) * sin_hi
            + pltpu.roll(x, LANES - 16, axis=1) * sin_lo)


def _in_proj_kernel(*refs, latent):
    if latent:
        (x_ref, mod_ref, g_ref, w_ref, cos_ref, slo_ref, shi_ref,
         q_ref, kt_ref, v_ref, u_ref, gt_ref) = refs
    else:
        (x_ref, mod_ref, g_ref, w_ref,
         q_ref, kt_ref, v_ref, ktf_ref, vf_ref, u_ref, gt_ref) = refs
    tm = x_ref.shape[0]
    x = x_ref[...]
    h = _rms(x, g_ref[...]) * (1.0 + mod_ref[1:2, :]) + mod_ref[0:1, :]
    hb = h.astype(BF16)

    def proj(off, width):
        return jnp.dot(hb, w_ref[:, off:off + width], preferred_element_type=F32)

    q = proj(Q_OFF, Q_COLS) * (HEAD_DIM ** -0.5 * LOG2_E)
    k = proj(K_OFF, K_COLS)
    if latent:
        cos, slo, shi = cos_ref[...], slo_ref[...], shi_ref[...]
        k_heads = []
        for hd in range(N_HEADS):
            sl = slice(hd * LANES, (hd + 1) * LANES)
            q_ref[:, sl] = _rope(q[:, sl], cos, slo, shi).astype(BF16)
            k_heads.append(_rope(k[:, sl], cos, slo, shi))
        kt_ref[...] = jnp.concatenate(k_heads, axis=1).T.astype(BF16)
    else:
        q_ref[...] = q.astype(BF16)
        seq = kt_ref.shape[1]
        for s in range(tm // seq):
            kt = k[s * seq:(s + 1) * seq, :].T
            rows = slice(s * K_COLS, (s + 1) * K_COLS)
            ktf_ref[rows, :] = kt
            kt_ref[rows, :] = kt.astype(BF16)
    v = proj(V_OFF, V_COLS)
    v_ref[...] = v.astype(BF16)
    if not latent:
        for hd in range(N_HEADS):
            vf_ref[pl.ds(hd, tm, stride=N_HEADS), :] = v[:, hd * LANES:(hd + 1) * LANES]
    u_ref[...] = proj(HY_OFF, HY_COLS)
    gt_ref[...] = jax.nn.sigmoid(proj(GATE_OFF, GATE_COLS))


def _in_proj_call(x, mod, g1, w_in, rope, *, seq, latent, tm):
    t = x.shape[0]
    per_seq = seq // tm if latent else 1

    def mod_idx(i):
        return (1 + i // per_seq if latent else 0, 0, 0)

    row = lambda i: (i, 0)
    in_specs = [pl.BlockSpec((tm, D_MODEL), row),
                pl.BlockSpec((None, 6, D_MODEL), mod_idx),
                _resident((1, D_MODEL)),
                _resident((D_MODEL, IN_COLS))]
    args = [x, mod, g1, w_in]
    tok = lambda width, dtype: ((t, width), dtype, pl.BlockSpec((tm, width), row))
    if latent:
        in_specs += [pl.BlockSpec((tm, LANES), lambda i: (i % per_seq, 0))] * 3
        args += list(rope)
        kt_spec = pl.BlockSpec((K_COLS, tm), lambda i: (i // per_seq, i % per_seq))
    else:
        kt_spec = pl.BlockSpec((tm // seq * K_COLS, seq), row)
    kt_shape = (t // seq * K_COLS, seq)
    outs = [tok(Q_COLS, BF16), (kt_shape, BF16, kt_spec), tok(V_COLS, BF16)]
    if not latent:
        outs += [(kt_shape, F32, kt_spec),
                 ((t * N_HEADS, LANES), F32, pl.BlockSpec((tm * N_HEADS, LANES), row))]
    outs += [tok(HY_COLS, F32), tok(GATE_COLS, F32)]
    return pl.pallas_call(
        functools.partial(_in_proj_kernel, latent=latent),
        grid=(t // tm,),
        in_specs=in_specs,
        out_specs=[spec for _, _, spec in outs],
        out_shape=[jax.ShapeDtypeStruct(s, d) for s, d, _ in outs],
        compiler_params=_params("arbitrary"),
        name="in_proj_lat" if latent else "in_proj_ctx",
    )(*args)


def _attn_kernel(*refs, n_cache, lam_init, tq):
    if n_cache:
        (q_ref, kt_ref, v_ref, ckt_ref, cv_ref, lam_ref, g_ref, o_ref, kt_s, v2_s) = refs
    else:
        (q_ref, kt_ref, v_ref, lam_ref, g_ref, o_ref, v2_s) = refs
    n_heads, n_keys, _ = v2_s.shape
    head0 = pl.program_id(1) * n_heads
    lv = lam_ref[...]
    lam = (jnp.exp(jnp.sum(lv[0:1] * lv[1:2], axis=-1, keepdims=True))
           - jnp.exp(jnp.sum(lv[2:3] * lv[3:4], axis=-1, keepdims=True)) + lam_init)
    gain = g_ref[...] * (1.0 - lam_init)
    first_map = lax.broadcasted_iota(jnp.int32, (1, LANES), 1) < HEAD_DIM
    zero = jnp.zeros((tq, LANES), BF16)

    def operands(hd):
        cols = slice(hd * LANES, (hd + 1) * LANES)
        v2_s[hd, :, LANES:] = jnp.ones((n_keys, LANES), BF16)
        if n_cache:
            kt_s[cols, 0:n_cache] = ckt_ref[cols, :].astype(BF16)
            kt_s[cols, n_cache:] = kt_ref[cols, :]
            cache_rows = pl.ds(head0 + hd, n_cache, stride=N_HEADS)
            v2_s[hd, 0:n_cache, 0:LANES] = cv_ref[cache_rows, :].astype(BF16)
            v2_s[hd, n_cache:, 0:LANES] = v_ref[:, cols]
            return kt_s[cols, :], v2_s[hd]
        v2_s[hd, :, 0:LANES] = v_ref[:, cols]
        return kt_ref[cols, :], v2_s[hd]

    def scores(hd, i, kt):
        q = q_ref[i * tq:(i + 1) * tq, hd * LANES:(hd + 1) * LANES]
        q2 = jnp.concatenate([jnp.where(first_map, q, zero), jnp.where(first_map, zero, q)], axis=0)
        s = jnp.dot(q2, kt, preferred_element_type=F32)
        return s, jnp.max(s, axis=-1, keepdims=True)

    def finish(hd, i, v2, s, m):
        p = jnp.exp2(s - m).astype(BF16)
        pv = jnp.dot(p, v2, preferred_element_type=F32)
        pv = pv[:, :LANES] / pv[:, LANES:]
        o = pv[:tq] - lam * pv[tq:]
        o_ref[i * tq:(i + 1) * tq, hd * LANES:(hd + 1) * LANES] = (_rms(o, 1.0) * gain).astype(BF16)

    n_tiles = q_ref.shape[0] // tq
    pending = None
    for hd in range(n_heads):
        kt, v2 = operands(hd)
        for i in range(n_tiles):
            cur = (hd, i, v2) + scores(hd, i, kt)
            if pending is not None:
                finish(*pending)
            pending = cur
    finish(*pending)


def _attn_call(q, kt, v, cache, lam_vecs, subln_g, *, batch, seq, lam_init, tq, heads):
    const = lambda b, h: (0, 0)
    width = heads * LANES
    n_groups = N_HEADS // heads
    in_specs = [pl.BlockSpec((seq, width), lambda b, h: (b, h)),
                pl.BlockSpec((width, seq), lambda b, h: (b * n_groups + h, 0)),
                pl.BlockSpec((seq, width), lambda b, h: (b, h))]
    args = [q, kt, v]
    n_cache = 0
    if cache is not None:
        ckt, cv = cache
        n_cache = ckt.shape[1]
        in_specs += [pl.BlockSpec((width, n_cache), lambda b, h: (b * n_groups + h, 0)),
                     pl.BlockSpec((n_cache * N_HEADS, LANES), lambda b, h: (b, 0))]
        args += [ckt, cv]
    n_keys = n_cache + seq
    scratch = [pltpu.VMEM((heads, n_keys, 2 * LANES), BF16)]
    if n_cache:
        scratch = [pltpu.VMEM((width, n_keys), BF16)] + scratch
    in_specs += [pl.BlockSpec(lam_vecs.shape, const), pl.BlockSpec((1, LANES), const)]
    args += [lam_vecs, subln_g]
    return pl.pallas_call(
        functools.partial(_attn_kernel, n_cache=n_cache, lam_init=lam_init, tq=tq),
        grid=(batch, n_groups),
        in_specs=in_specs,
        out_specs=pl.BlockSpec((seq, width), lambda b, h: (b, h)),
        out_shape=jax.ShapeDtypeStruct((batch * seq, N_HEADS * V_DIM), BF16),
        scratch_shapes=scratch,
        compiler_params=_params("arbitrary", "arbitrary"),
        name="attn_lat" if cache is not None else "attn_ctx",
    )(*args)


def _dft_tables(seq):
    n = 2 * seq
    f = np.arange(seq, dtype=np.int64)[:, None]
    t = np.arange(seq, dtype=np.int64)[None, :]
    ang = 2.0 * np.pi * ((f * t) % n).astype(np.float64) / n
    alt = np.where(np.arange(seq) % 2 == 0, 1.0, -1.0)
    cos, msin = np.cos(ang), -np.sin(ang)
    msin[0, :] = alt
    fwd = np.concatenate([cos, msin], axis=0)
    wgt = np.where(f == 0, 1.0, 2.0) / n
    inv_c = (cos * wgt).T
    inv_s = (msin * wgt).T
    inv = np.concatenate([inv_c, inv_s], axis=1)
    return fwd.astype(np.float32), inv.astype(np.float32)


def _split_bf16(x):
    hi = x.astype(BF16)
    lo = (x - hi.astype(F32)).astype(BF16)
    return hi, lo


def _dot3(a_hi, a_lo, b):
    b_hi, b_lo = _split_bf16(b)
    d = functools.partial(jnp.dot, preferred_element_type=F32)
    return d(a_hi, b_hi) + (d(a_hi, b_lo) + d(a_lo, b_hi))


def _filt_kernel(emb_ref, w1_ref, b1_ref, w2_ref, b2_ref, fr_ref, w3f_ref, w3b_ref,
                 t_ref, dl_ref, fhi_ref, flo_ref, o_ref, h_s):
    seq = emb_ref.shape[0]
    hp = functools.partial(jnp.dot, preferred_element_type=F32, precision=lax.Precision.HIGHEST)

    @pl.when(pl.program_id(0) == 0)
    def _():
        fr = fr_ref[...]
        h = jnp.sin(fr * (hp(emb_ref[...], w1_ref[...]) + b1_ref[...]))
        h_s[...] = jnp.sin(fr * (hp(h, w2_ref[...]) + b2_ref[...]))

    h = h_s[...]
    decay = jnp.exp(-t_ref[...] * jnp.abs(dl_ref[...]))
    fwd = hp(h, w3f_ref[...]) * decay
    bwd = hp(h, w3b_ref[...]) * decay
    hsum, hdif = fwd + bwd, fwd - bwd
    re = _dot3(fhi_ref[0:seq, :], flo_ref[0:seq, :], hsum)
    im = _dot3(fhi_ref[seq:, :], flo_ref[seq:, :], hdif)
    row = lax.broadcasted_iota(jnp.int32, (seq, 1), 0)
    alt = jnp.where(row % 2 == 0, 1.0, -1.0)
    nyq = jnp.sum(hsum * alt, axis=0, keepdims=True)
    o_ref[0] = re
    o_ref[1] = jnp.where(row == 0, 0.0, im)
    o_ref[2] = jnp.where(row == 0, nyq, re)


def _filt_call(emb, w1, b1, w2, b2, fr, w3, tcol, deltas, fwd_hi, fwd_lo, *, seq, ct):
    n_out = HYENA_ORDER * D_HYENA
    nct = n_out // ct
    per_order = D_HYENA // ct
    small = lambda shape: pl.BlockSpec(shape, lambda j: (0, 0))
    return pl.pallas_call(
        _filt_kernel,
        grid=(nct,),
        in_specs=[small((seq, FILT_PAD)), small((FILT_PAD, FILT_PAD)), small((1, FILT_PAD)),
                  small((FILT_PAD, FILT_PAD)), small((1, FILT_PAD)), small((1, FILT_PAD)),
                  pl.BlockSpec((FILT_PAD, ct), lambda j: (0, j)),
                  pl.BlockSpec((FILT_PAD, ct), lambda j: (0, nct + j)),
                  small((seq, 1)),
                  pl.BlockSpec((1, ct), lambda j: (0, j % per_order)),
                  _resident((2 * seq, seq)), _resident((2 * seq, seq))],
        out_specs=pl.BlockSpec((3, seq, ct), lambda j: (0, 0, j)),
        out_shape=jax.ShapeDtypeStruct((3, seq, n_out), F32),
        scratch_shapes=[pltpu.VMEM((seq, FILT_PAD), F32)],
        compiler_params=_params("arbitrary"),
        name=f"filt_{seq}",
    )(emb, w1, b1, w2, b2, fr, w3, w3, tcol, deltas, fwd_hi, fwd_lo)


def _hyena_kernel(uv_ref, u1_ref, u2_ref, wv_ref, w1_ref, w2_ref, bv_ref, b1_ref, b2_ref,
                  h0_ref, h1_ref, hb_ref, fwd_ref, inv_ref, o_ref):
    seq = h0_ref.shape[1]
    chains = [slice(c * seq, (c + 1) * seq) for c in range(uv_ref.shape[0] // seq)]
    row = lax.broadcasted_iota(jnp.int32, (seq, 1), 0)
    first, last = row == 0, row == seq - 1

    def short_conv(u_ref, w_ref, b_ref):
        outs = []
        for rows in chains:
            u = u_ref[rows, :]
            prev = jnp.where(first, 0.0, pltpu.roll(u, 1, axis=0))
            nxt = jnp.where(last, 0.0, pltpu.roll(u, seq - 1, axis=0))
            outs.append(prev * w_ref[0:1, :] + u * w_ref[1:2, :] + nxt * w_ref[2:3, :] + b_ref[...])
        return outs

    def long_conv(us, h_ref, bias):
        specs = [jnp.dot(fwd_ref[...], u.astype(BF16), preferred_element_type=F32) for u in us]
        ys = []
        for spec in specs:
            ure, uim = spec[:seq], spec[seq:]
            yre = ure * h_ref[0] - uim * h_ref[1]
            yim = ure * h_ref[1] + uim * h_ref[2]
            ys.append(jnp.concatenate([yre, yim], axis=0).astype(BF16))
        return [jnp.dot(inv_ref[...], y, preferred_element_type=F32) + u * bias
                for y, u in zip(ys, us)]

    v = short_conv(uv_ref, wv_ref, bv_ref)
    x1 = short_conv(u1_ref, w1_ref, b1_ref)
    z = [a * b for a, b in zip(x1, long_conv(v, h0_ref, hb_ref[0:1, :]))]
    x2 = short_conv(u2_ref, w2_ref, b2_ref)
    for rows, a, b in zip(chains, x2, long_conv(z, h1_ref, hb_ref[1:2, :])):
        o_ref[rows, :] = (a * b).astype(BF16)


def _hyena_call(u, conv_w, conv_b, hf, hy_bias, fwd, inv, *, batch, seq, ct, n_chain):
    nct = D_HYENA // ct
    rows = n_chain * seq
    col = lambda part: (lambda c, b: (b, part * nct + c))
    wcol = lambda part: (lambda c, b: (0, part * nct + c))
    in_specs = ([pl.BlockSpec((rows, ct), col(p)) for p in range(3)]
                + [pl.BlockSpec((3, ct), wcol(p)) for p in range(3)]
                + [pl.BlockSpec((1, ct), wcol(p)) for p in range(3)]
                + [pl.BlockSpec((3, seq, ct), lambda c, b, o=o: (0, 0, o * nct + c))
                   for o in range(HYENA_ORDER)]
                + [pl.BlockSpec((HYENA_ORDER, ct), lambda c, b: (0, c)),
                   _resident((2 * seq, seq)), _resident((seq, 2 * seq))])
    return pl.pallas_call(
        _hyena_kernel,
        grid=(nct, batch // n_chain),
        in_specs=in_specs,
        out_specs=pl.BlockSpec((rows, ct), lambda c, b: (b, c)),
        out_shape=jax.ShapeDtypeStruct((batch * seq, D_HYENA), BF16),
        compiler_params=_params("arbitrary", "arbitrary"),
        name=f"hyena_{seq}",
    )(u, u, u, conv_w, conv_w, conv_w, conv_b, conv_b, conv_b, hf, hf, hy_bias, fwd, inv)


def _post_kernel(x_ref, oa_ref, oh_ref, gt_ref, mod_ref, g2_ref, gf_ref,
                 wa_ref, wh_ref, wo_ref, wu_ref, wd_ref, y_ref):
    d = functools.partial(jnp.dot, preferred_element_type=F32)
    merged = (gt_ref[:, :D_MODEL] * d(oa_ref[...], wa_ref[...])
              + gt_ref[:, D_MODEL:] * d(oh_ref[...], wh_ref[...]))
    x = x_ref[...] + mod_ref[2:3, :] * d(merged.astype(BF16), wo_ref[...])
    h2 = (_rms(x, g2_ref[...]) * (1.0 + mod_ref[4:5, :]) + mod_ref[3:4, :]).astype(BF16)
    ff_chunk = D_MODEL
    acc = jnp.zeros_like(x)
    for c in range(D_FF // ff_chunk):
        sl = slice(c * ff_chunk, (c + 1) * ff_chunk)
        up = jnp.maximum(d(h2, wu_ref[:, sl]), 0.0)
        acc = acc + d((up * up).astype(BF16), wd_ref[sl, :])
    x = x + mod_ref[5:6, :] * acc
    y_ref[...] = _rms(x, gf_ref[...])


def _post_call(x, o_attn, o_hy, gates, mod, g2, gf, wa, wh, wo, wu, wd, *, seq, latent, tm):
    t = x.shape[0]
    per_seq = seq // tm if latent else 1
    row = lambda i: (i, 0)

    def mod_idx(i):
        return (1 + i // per_seq if latent else 0, 0, 0)

    return pl.pallas_call(
        _post_kernel,
        grid=(t // tm,),
        in_specs=[pl.BlockSpec((tm, D_MODEL), row),
                  pl.BlockSpec((tm, N_HEADS * V_DIM), row),
                  pl.BlockSpec((tm, D_HYENA), row),
                  pl.BlockSpec((tm, GATE_COLS), row),
                  pl.BlockSpec((None, 6, D_MODEL), mod_idx),
                  _resident((1, D_MODEL)), _resident((1, D_MODEL)),
                  _resident(wa.shape), _resident(wh.shape), _resident(wo.shape),
                  _resident(wu.shape), _resident(wd.shape)],
        out_specs=pl.BlockSpec((tm, D_MODEL), row),
        out_shape=jax.ShapeDtypeStruct((t, D_MODEL), F32),
        compiler_params=_params("arbitrary"),
        name="post_lat" if latent else "post_ctx",
    )(x, o_attn, o_hy, gates, mod, g2, gf, wa, wh, wo, wu, wd)


def _rope_tables(seq):
    half = HEAD_DIM // 2
    n = half // 2
    inv = ROPE_THETA ** (-jnp.arange(n, dtype=F32) / n)
    pos = jnp.arange(seq)
    ang_row = (pos // GRID_W).astype(F32)[:, None] * inv[None, :]
    ang_col = (pos % GRID_W).astype(F32)[:, None] * inv[None, :]
    zeros = jnp.zeros_like(ang_row)

    def per_map(a_row, a_col, lo, hi):
        return jnp.concatenate([lo(a_row), hi(a_row), lo(a_col), hi(a_col)], axis=-1)

    cos = per_map(ang_row, ang_col, jnp.cos, jnp.cos)
    sin_lo = per_map(ang_row, ang_col, lambda a: -jnp.sin(a), lambda a: zeros)
    sin_hi = per_map(ang_row, ang_col, lambda a: zeros, jnp.sin)
    return tuple(jnp.concatenate([tab, tab], axis=-1) for tab in (cos, sin_lo, sin_hi))


def _filter_embedding(seq):
    bands = (FILTER_EMB - 1) // 2
    t = jnp.linspace(0.0, 1.0, seq, dtype=F32)[:, None]
    wpos = 2.0 * math.pi * jnp.arange(seq, dtype=F32)[:, None] / seq
    f = jnp.linspace(1e-4, bands - 1, bands, dtype=F32)[None, :]
    emb = jnp.concatenate([t, jnp.cos(f * wpos), -jnp.sin(f * wpos)], axis=-1)
    return jnp.pad(emb, ((0, 0), (0, FILT_PAD - FILTER_EMB))), t


def _tiles(seq):
    short = seq <= 256
    return dict(tm=512, tq=128, attn_heads=N_HEADS if short else 1, ct=MXU_WIDTH,
                hyena_seqs=4 if short else 2)


def _pad_to(x, rows, cols):
    return jnp.pad(x, ((0, rows - x.shape[0]), (0, cols - x.shape[1])))


def kernel(x_prompt, x_sample, cache_k, cache_v, c, c_ctx, w_ada, b_ada, norm1_g, norm2_g, w_in,
           lam_q1, lam_k1, lam_q2, lam_k2, attn_subln_g, conv_w, conv_b, filt_w1, filt_b1,
           filt_w2, filt_b2, filt_w3, filt_freq, hy_bias, w_br_attn, w_br_hy, w_out, w_up,
           w_down, final_g):
    depth = w_in.shape[0]
    assert depth == 1, "single trunk layer"
    layer = 0
    lam_init = 0.8 - 0.6 * math.exp(-0.3 * layer)
    n_ctx, ctx_len, _ = x_prompt.shape
    n_lat, lat_len, _ = x_sample.shape
    past = cache_k.shape[2]

    cvec = jnp.concatenate([c_ctx[None, :], c], axis=0)
    cvec = jnp.pad(cvec, ((0, MOD_ROWS - cvec.shape[0]), (0, 0)))
    mod = _mod_call(cvec, w_ada[layer], b_ada[layer][None, :]).reshape(MOD_ROWS, 6, D_MODEL)

    w_in_b = w_in[layer].astype(BF16)
    wa, wh, wo = (w[layer].astype(BF16) for w in (w_br_attn, w_br_hy, w_out))
    wu, wd = w_up[layer].astype(BF16), w_down[layer].astype(BF16)
    g1, g2, gf = norm1_g[layer][None, :], norm2_g[layer][None, :], final_g[None, :]
    lam_vecs = jnp.stack([lam_q1[layer], lam_k1[layer], lam_q2[layer], lam_k2[layer]])
    subln_g = attn_subln_g[layer][None, :]

    w1 = _pad_to(filt_w1[layer], FILT_PAD, FILT_PAD)
    w2 = _pad_to(filt_w2[layer], FILT_PAD, FILT_PAD)
    w3 = _pad_to(filt_w3[layer], FILT_PAD, filt_w3.shape[2])
    b1 = _pad_to(filt_b1[layer][None, :], 1, FILT_PAD)
    b2 = _pad_to(filt_b2[layer][None, :], 1, FILT_PAD)
    fr = _pad_to(filt_freq[layer][None, :], 1, FILT_PAD)
    deltas = jnp.linspace(math.log(DECAY_TARGET) / FAST_DECAY_PCT,
                          math.log(DECAY_TARGET) / SLOW_DECAY_PCT, D_HYENA, dtype=F32)[None, :]

    def branch(x3, batch, seq, latent):
        x = x3.reshape(batch * seq, D_MODEL)
        tiles = _tiles(seq)
        tm = tiles["tm"]
        rope = _rope_tables(seq) if latent else None
        outs = _in_proj_call(x, mod, g1, w_in_b, rope, seq=seq, latent=latent, tm=tm)
        if latent:
            q, kt, v, u, gates = outs
            ckt = jnp.transpose(cache_k[:, layer], (0, 2, 3, 4, 1)).reshape(batch * K_COLS, past)
            cache = (ckt, cache_v[:, layer].reshape(batch * past * N_HEADS, V_DIM))
            kf = vf = None
        else:
            q, kt, v, kf, vf, u, gates = outs
            cache = None
        o_attn = _attn_call(q, kt, v, cache, lam_vecs, subln_g, batch=batch, seq=seq,
                            lam_init=lam_init, tq=tiles["tq"], heads=tiles["attn_heads"])

        fwd_np, inv_np = _dft_tables(seq)
        fwd_f32 = jnp.asarray(fwd_np)
        fwd_hi, fwd_lo = _split_bf16(fwd_f32)
        inv_b = jnp.asarray(inv_np).astype(BF16)
        emb, tcol = _filter_embedding(seq)
        hf = _filt_call(emb, w1, b1, w2, b2, fr, w3, tcol, deltas, fwd_hi, fwd_lo,
                        seq=seq, ct=tiles["ct"])
        o_hy = _hyena_call(u, conv_w[layer], conv_b[layer][None, :], hf, hy_bias[layer],
                           fwd_hi, inv_b, batch=batch, seq=seq, ct=tiles["ct"],
                           n_chain=tiles["hyena_seqs"])

        y = _post_call(x, o_attn, o_hy, gates, mod, g2, gf, wa, wh, wo, wu, wd,
                       seq=seq, latent=latent, tm=tm)
        return y.reshape(batch, seq, D_MODEL), kf, vf

    y_prompt, kf, vf = branch(x_prompt, n_ctx, ctx_len, False)
    y_sample, _, _ = branch(x_sample, n_lat, lat_len, True)
    new_cache_k = jnp.transpose(kf.reshape(n_ctx, depth, N_HEADS, 2, HEAD_DIM, ctx_len),
                                (0, 1, 5, 2, 3, 4))
    new_cache_v = vf.reshape(n_ctx, depth, ctx_len, N_HEADS, V_DIM)
    return (y_prompt, y_sample, new_cache_k, new_cache_v)
```

```python
import functools
import math

import numpy as np
import jax
import jax.numpy as jnp
from jax import lax
from jax.experimental import pallas as pl
from jax.experimental.pallas import tpu as pltpu

D_MODEL = 1024
GRID_W = 64
N_HEADS = 8
HEAD_DIM = 64
V_DIM = 2 * HEAD_DIM
D_HYENA = D_MODEL // 2
HYENA_ORDER = 2
FILTER_EMB = 33
FILTER_HIDDEN = 64
D_FF = 4 * D_MODEL
ROPE_THETA = 10000.0
EPS = 1e-6
LOG2_E = math.log2(math.e)
DECAY_TARGET = 1e-2
FAST_DECAY_PCT = 0.3
SLOW_DECAY_PCT = 1.5

Q_COLS = N_HEADS * 2 * HEAD_DIM
K_COLS = N_HEADS * 2 * HEAD_DIM
V_COLS = N_HEADS * V_DIM
HY_COLS = 3 * D_HYENA
GATE_COLS = 2 * D_MODEL
Q_OFF = 0
K_OFF = Q_OFF + Q_COLS
V_OFF = K_OFF + K_COLS
HY_OFF = V_OFF + V_COLS
GATE_OFF = HY_OFF + HY_COLS
IN_COLS = GATE_OFF + GATE_COLS

LANES = 128
MXU_WIDTH = 256
MOD_ROWS = 16
FILT_PAD = 128
VMEM_LIMIT = 56 * 1024 * 1024

BF16 = jnp.bfloat16
F32 = jnp.float32


def _resident(shape):
    nd = len(shape)
    return pl.BlockSpec(shape, lambda *_: (0,) * nd, pipeline_mode=pl.Buffered(1))


def _params(*sem):
    return pltpu.CompilerParams(dimension_semantics=sem, vmem_limit_bytes=VMEM_LIMIT)


def _rms(x, g):
    return x * lax.rsqrt(jnp.mean(x * x, axis=-1, keepdims=True) + EPS) * g


def _mod_kernel(c_ref, w_ref, b_ref, o_ref):
    c = c_ref[...]
    s = (c * jax.nn.sigmoid(c)).astype(BF16)
    o_ref[...] = jnp.dot(s, w_ref[...].astype(BF16), preferred_element_type=F32) + b_ref[...]


def _mod_call(cvec, w_ada, b_ada):
    tn = 1024
    n = w_ada.shape[1]
    return pl.pallas_call(
        _mod_kernel,
        grid=(n // tn,),
        in_specs=[pl.BlockSpec((MOD_ROWS, D_MODEL), lambda j: (0, 0)),
                  pl.BlockSpec((D_MODEL, tn), lambda j: (0, j)),
                  pl.BlockSpec((1, tn), lambda j: (0, j))],
        out_specs=pl.BlockSpec((MOD_ROWS, tn), lambda j: (0, j)),
        out_shape=jax.ShapeDtypeStruct((MOD_ROWS, n), F32),
        compiler_params=_params("arbitrary"),
        name="mod",
    )(cvec, w_ada, b_ada)


def _rope(x, cos, sin_lo, sin_hi):
    return (x * cos + pltpu.roll(x, 16, axis=1) * sin_hi
            + pltpu.roll(x, LANES - 16, axis=1) * sin_lo)


def _in_proj_kernel(*refs, latent):
    if latent:
        (x_ref, mod_ref, g_ref, w_ref, cos_ref, slo_ref, shi_ref,
         q_ref, kt_ref, v_ref, u_ref, gt_ref) = refs
    else:
        (x_ref, mod_ref, g_ref, w_ref,
         q_ref, kt_ref, v_ref, ktf_ref, vf_ref, u_ref, gt_ref) = refs
    tm = x_ref.shape[0]
    x = x_ref[...]
    h = _rms(x, g_ref[...]) * (1.0 + mod_ref[1:2, :]) + mod_ref[0:1, :]
    hb = h.astype(BF16)

    def proj(off, width):
        return jnp.dot(hb, w_ref[:, off:off + width], preferred_element_type=F32)

    q = proj(Q_OFF, Q_COLS) * (HEAD_DIM ** -0.5 * LOG2_E)
    k = proj(K_OFF, K_COLS)
    if latent:
        cos, slo, shi = cos_ref[...], slo_ref[...], shi_ref[...]
        k_heads = []
        for hd in range(N_HEADS):
            sl = slice(hd * LANES, (hd + 1) * LANES)
            q_ref[:, sl] = _rope(q[:, sl], cos, slo, shi).astype(BF16)
            k_heads.append(_rope(k[:, sl], cos, slo, shi))
        kt_ref[...] = jnp.concatenate(k_heads, axis=1).T.astype(BF16)
    else:
        q_ref[...] = q.astype(BF16)
        seq = kt_ref.shape[1]
        for s in range(tm // seq):
            kt = k[s * seq:(s + 1) * seq, :].T
            rows = slice(s * K_COLS, (s + 1) * K_COLS)
            ktf_ref[rows, :] = kt
            kt_ref[rows, :] = kt.astype(BF16)
    v = proj(V_OFF, V_COLS)
    v_ref[...] = v.astype(BF16)
    if not latent:
        for hd in range(N_HEADS):
            vf_ref[pl.ds(hd, tm, stride=N_HEADS), :] = v[:, hd * LANES:(hd + 1) * LANES]
    u_ref[...] = proj(HY_OFF, HY_COLS)
    gt_ref[...] = jax.nn.sigmoid(proj(GATE_OFF, GATE_COLS))


def _in_proj_call(x, mod, g1, w_in, rope, *, seq, latent, tm):
    t = x.shape[0]
    per_seq = seq // tm if latent else 1

    def mod_idx(i):
        return (1 + i // per_seq if latent else 0, 0, 0)

    row = lambda i: (i, 0)
    in_specs = [pl.BlockSpec((tm, D_MODEL), row),
                pl.BlockSpec((None, 6, D_MODEL), mod_idx),
                _resident((1, D_MODEL)),
                _resident((D_MODEL, IN_COLS))]
    args = [x, mod, g1, w_in]
    tok = lambda width, dtype: ((t, width), dtype, pl.BlockSpec((tm, width), row))
    if latent:
        in_specs += [pl.BlockSpec((tm, LANES), lambda i: (i % per_seq, 0))] * 3
        args += list(rope)
        kt_spec = pl.BlockSpec((K_COLS, tm), lambda i: (i // per_seq, i % per_seq))
    else:
        kt_spec = pl.BlockSpec((tm // seq * K_COLS, seq), row)
    kt_shape = (t // seq * K_COLS, seq)
    outs = [tok(Q_COLS, BF16), (kt_shape, BF16, kt_spec), tok(V_COLS, BF16)]
    if not latent:
        outs += [(kt_shape, F32, kt_spec),
                 ((t * N_HEADS, LANES), F32, pl.BlockSpec((tm * N_HEADS, LANES), row))]
    outs += [tok(HY_COLS, F32), tok(GATE_COLS, F32)]
    return pl.pallas_call(
        functools.partial(_in_proj_kernel, latent=latent),
        grid=(t // tm,),
        in_specs=in_specs,
        out_specs=[spec for _, _, spec in outs],
        out_shape=[jax.ShapeDtypeStruct(s, d) for s, d, _ in outs],
        compiler_params=_params("arbitrary"),
        name="in_proj_lat" if latent else "in_proj_ctx",
    )(*args)


def _attn_kernel(*refs, n_cache, lam_init, tq):
    if n_cache:
        (q_ref, kt_ref, v_ref, ckt_ref, cv_ref, lam_ref, g_ref, o_ref, kt_s, v2_s) = refs
    else:
        (q_ref, kt_ref, v_ref, lam_ref, g_ref, o_ref, v2_s) = refs
    n_heads, n_keys, _ = v2_s.shape
    head0 = pl.program_id(1) * n_heads
    lv = lam_ref[...]
    lam = (jnp.exp(jnp.sum(lv[0:1] * lv[1:2], axis=-1, keepdims=True))
           - jnp.exp(jnp.sum(lv[2:3] * lv[3:4], axis=-1, keepdims=True)) + lam_init)
    gain = g_ref[...] * (1.0 - lam_init)
    first_map = lax.broadcasted_iota(jnp.int32, (1, LANES), 1) < HEAD_DIM
    zero = jnp.zeros((tq, LANES), BF16)

    def operands(hd):
        cols = slice(hd * LANES, (hd + 1) * LANES)
        v2_s[hd, :, LANES:] = jnp.ones((n_keys, LANES), BF16)
        if n_cache:
            kt_s[cols, 0:n_cache] = ckt_ref[cols, :].astype(BF16)
            kt_s[cols, n_cache:] = kt_ref[cols, :]
            cache_rows = pl.ds(head0 + hd, n_cache, stride=N_HEADS)
            v2_s[hd, 0:n_cache, 0:LANES] = cv_ref[cache_rows, :].astype(BF16)
            v2_s[hd, n_cache:, 0:LANES] = v_ref[:, cols]
            return kt_s[cols, :], v2_s[hd]
        v2_s[hd, :, 0:LANES] = v_ref[:, cols]
        return kt_ref[cols, :], v2_s[hd]

    def scores(hd, i, kt):
        q = q_ref[i * tq:(i + 1) * tq, hd * LANES:(hd + 1) * LANES]
        q2 = jnp.concatenate([jnp.where(first_map, q, zero), jnp.where(first_map, zero, q)], axis=0)
        s = jnp.dot(q2, kt, preferred_element_type=F32)
        return s, jnp.max(s, axis=-1, keepdims=True)

    def finish(hd, i, v2, s, m):
        p = jnp.exp2(s - m).astype(BF16)
        pv = jnp.dot(p, v2, preferred_element_type=F32)
        pv = pv[:, :LANES] / pv[:, LANES:]
        o = pv[:tq] - lam * pv[tq:]
        o_ref[i * tq:(i + 1) * tq, hd * LANES:(hd + 1) * LANES] = (_rms(o, 1.0) * gain).astype(BF16)

    n_tiles = q_ref.shape[0] // tq
    pending = None
    for hd in range(n_heads):
        kt, v2 = operands(hd)
        for i in range(n_tiles):
            cur = (hd, i, v2) + scores(hd, i, kt)
            if pending is not None:
                finish(*pending)
            pending = cur
    finish(*pending)


def _attn_call(q, kt, v, cache, lam_vecs, subln_g, *, batch, seq, lam_init, tq, heads):
    const = lambda b, h: (0, 0)
    width = heads * LANES
    n_groups = N_HEADS // heads
    in_specs = [pl.BlockSpec((seq, width), lambda b, h: (b, h)),
                pl.BlockSpec((width, seq), lambda b, h: (b * n_groups + h, 0)),
                pl.BlockSpec((seq, width), lambda b, h: (b, h))]
    args = [q, kt, v]
    n_cache = 0
    if cache is not None:
        ckt, cv = cache
        n_cache = ckt.shape[1]
        in_specs += [pl.BlockSpec((width, n_cache), lambda b, h: (b * n_groups + h, 0)),
                     pl.BlockSpec((n_cache * N_HEADS, LANES), lambda b, h: (b, 0))]
        args += [ckt, cv]
    n_keys = n_cache + seq
    scratch = [pltpu.VMEM((heads, n_keys, 2 * LANES), BF16)]
    if n_cache:
        scratch = [pltpu.VMEM((width, n_keys), BF16)] + scratch
    in_specs += [pl.BlockSpec(lam_vecs.shape, const), pl.BlockSpec((1, LANES), const)]
    args += [lam_vecs, subln_g]
    return pl.pallas_call(
        functools.partial(_attn_kernel, n_cache=n_cache, lam_init=lam_init, tq=tq),
        grid=(batch, n_groups),
        in_specs=in_specs,
        out_specs=pl.BlockSpec((seq, width), lambda b, h: (b, h)),
        out_shape=jax.ShapeDtypeStruct((batch * seq, N_HEADS * V_DIM), BF16),
        scratch_shapes=scratch,
        compiler_params=_params("arbitrary", "arbitrary"),
        name="attn_lat" if cache is not None else "attn_ctx",
    )(*args)


def _dft_tables(seq):
    n = 2 * seq
    f = np.arange(seq, dtype=np.int64)[:, None]
    t = np.arange(seq, dtype=np.int64)[None, :]
    ang = 2.0 * np.pi * ((f * t) % n).astype(np.float64) / n
    alt = np.where(np.arange(seq) % 2 == 0, 1.0, -1.0)
    cos, msin = np.cos(ang), -np.sin(ang)
    msin[0, :] = alt
    fwd = np.concatenate([cos, msin], axis=0)
    wgt = np.where(f == 0, 1.0, 2.0) / n
    inv_c = (cos * wgt).T
    inv_s = (msin * wgt).T
    inv = np.concatenate([inv_c, inv_s], axis=1)
    return fwd.astype(np.float32), inv.astype(np.float32)


def _filt_kernel(emb_ref, w1_ref, b1_ref, w2_ref, b2_ref, fr_ref, w3f_ref, w3b_ref,
                 t_ref, dl_ref, dft_ref, o_ref, h_s):
    seq = emb_ref.shape[0]
    hp = functools.partial(jnp.dot, preferred_element_type=F32, precision=lax.Precision.HIGHEST)

    @pl.when(pl.program_id(0) == 0)
    def _():
        fr = fr_ref[...]
        h = jnp.sin(fr * (hp(emb_ref[...], w1_ref[...]) + b1_ref[...]))
        h_s[...] = jnp.sin(fr * (hp(h, w2_ref[...]) + b2_ref[...]))

    h = h_s[...]
    decay = jnp.exp(-t_ref[...] * jnp.abs(dl_ref[...]))
    fwd = hp(h, w3f_ref[...]) * decay
    bwd = hp(h, w3b_ref[...]) * decay
    hsum, hdif = fwd + bwd, fwd - bwd
    re = jnp.dot(dft_ref[0:seq, :], hsum.astype(BF16), preferred_element_type=F32)
    im = jnp.dot(dft_ref[seq:, :], hdif.astype(BF16), preferred_element_type=F32)
    row = lax.broadcasted_iota(jnp.int32, (seq, 1), 0)
    alt = jnp.where(row % 2 == 0, 1.0, -1.0)
    nyq = jnp.sum(hsum * alt, axis=0, keepdims=True)
    o_ref[0] = re
    o_ref[1] = jnp.where(row == 0, 0.0, im)
    o_ref[2] = jnp.where(row == 0, nyq, re)


def _filt_call(emb, w1, b1, w2, b2, fr, w3, tcol, deltas, fwd, *, seq, ct):
    n_out = HYENA_ORDER * D_HYENA
    nct = n_out // ct
    per_order = D_HYENA // ct
    small = lambda shape: pl.BlockSpec(shape, lambda j: (0, 0))
    return pl.pallas_call(
        _filt_kernel,
        grid=(nct,),
        in_specs=[small((seq, FILT_PAD)), small((FILT_PAD, FILT_PAD)), small((1, FILT_PAD)),
                  small((FILT_PAD, FILT_PAD)), small((1, FILT_PAD)), small((1, FILT_PAD)),
                  pl.BlockSpec((FILT_PAD, ct), lambda j: (0, j)),
                  pl.BlockSpec((FILT_PAD, ct), lambda j: (0, nct + j)),
                  small((seq, 1)),
                  pl.BlockSpec((1, ct), lambda j: (0, j % per_order)),
                  _resident((2 * seq, seq))],
        out_specs=pl.BlockSpec((3, seq, ct), lambda j: (0, 0, j)),
        out_shape=jax.ShapeDtypeStruct((3, seq, n_out), F32),
        scratch_shapes=[pltpu.VMEM((seq, FILT_PAD), F32)],
        compiler_params=_params("arbitrary"),
        name=f"filt_{seq}",
    )(emb, w1, b1, w2, b2, fr, w3, w3, tcol, deltas, fwd)


def _hyena_kernel(uv_ref, u1_ref, u2_ref, wv_ref, w1_ref, w2_ref, bv_ref, b1_ref, b2_ref,
                  h0_ref, h1_ref, hb_ref, fwd_ref, inv_ref, o_ref):
    seq = h0_ref.shape[1]
    chains = [slice(c * seq, (c + 1) * seq) for c in range(uv_ref.shape[0] // seq)]
    row = lax.broadcasted_iota(jnp.int32, (seq, 1), 0)
    first, last = row == 0, row == seq - 1

    def short_conv(u_ref, w_ref, b_ref):
        outs = []
        for rows in chains:
            u = u_ref[rows, :]
            prev = jnp.where(first, 0.0, pltpu.roll(u, 1, axis=0))
            nxt = jnp.where(last, 0.0, pltpu.roll(u, seq - 1, axis=0))
            outs.append(prev * w_ref[0:1, :] + u * w_ref[1:2, :] + nxt * w_ref[2:3, :] + b_ref[...])
        return outs

    def long_conv(us, h_ref, bias):
        specs = [jnp.dot(fwd_ref[...], u.astype(BF16), preferred_element_type=F32) for u in us]
        ys = []
        for spec in specs:
            ure, uim = spec[:seq], spec[seq:]
            yre = ure * h_ref[0] - uim * h_ref[1]
            yim = ure * h_ref[1] + uim * h_ref[2]
            ys.append(jnp.concatenate([yre, yim], axis=0).astype(BF16))
        return [jnp.dot(inv_ref[...], y, preferred_element_type=F32) + u * bias
                for y, u in zip(ys, us)]

    v = short_conv(uv_ref, wv_ref, bv_ref)
    x1 = short_conv(u1_ref, w1_ref, b1_ref)
    z = [a * b for a, b in zip(x1, long_conv(v, h0_ref, hb_ref[0:1, :]))]
    x2 = short_conv(u2_ref, w2_ref, b2_ref)
    for rows, a, b in zip(chains, x2, long_conv(z, h1_ref, hb_ref[1:2, :])):
        o_ref[rows, :] = (a * b).astype(BF16)


def _hyena_call(u, conv_w, conv_b, hf, hy_bias, fwd, inv, *, batch, seq, ct, n_chain):
    nct = D_HYENA // ct
    rows = n_chain * seq
    col = lambda part: (lambda c, b: (b, part * nct + c))
    wcol = lambda part: (lambda c, b: (0, part * nct + c))
    in_specs = ([pl.BlockSpec((rows, ct), col(p)) for p in range(3)]
                + [pl.BlockSpec((3, ct), wcol(p)) for p in range(3)]
                + [pl.BlockSpec((1, ct), wcol(p)) for p in range(3)]
                + [pl.BlockSpec((3, seq, ct), lambda c, b, o=o: (0, 0, o * nct + c))
                   for o in range(HYENA_ORDER)]
                + [pl.BlockSpec((HYENA_ORDER, ct), lambda c, b: (0, c)),
                   _resident((2 * seq, seq)), _resident((seq, 2 * seq))])
    return pl.pallas_call(
        _hyena_kernel,
        grid=(nct, batch // n_chain),
        in_specs=in_specs,
        out_specs=pl.BlockSpec((rows, ct), lambda c, b: (b, c)),
        out_shape=jax.ShapeDtypeStruct((batch * seq, D_HYENA), BF16),
        compiler_params=_params("arbitrary", "arbitrary"),
        name=f"hyena_{seq}",
    )(u, u, u, conv_w, conv_w, conv_w, conv_b, conv_b, conv_b, hf, hf, hy_bias, fwd, inv)


def _post_kernel(x_ref, oa_ref, oh_ref, gt_ref, mod_ref, g2_ref, gf_ref,
                 wa_ref, wh_ref, wo_ref, wu_ref, wd_ref, y_ref):
    d = functools.partial(jnp.dot, preferred_element_type=F32)
    merged = (gt_ref[:, :D_MODEL] * d(oa_ref[...], wa_ref[...])
              + gt_ref[:, D_MODEL:] * d(oh_ref[...], wh_ref[...]))
    x = x_ref[...] + mod_ref[2:3, :] * d(merged.astype(BF16), wo_ref[...])
    h2 = (_rms(x, g2_ref[...]) * (1.0 + mod_ref[4:5, :]) + mod_ref[3:4, :]).astype(BF16)
    ff_chunk = D_MODEL
    acc = jnp.zeros_like(x)
    for c in range(D_FF // ff_chunk):
        sl = slice(c * ff_chunk, (c + 1) * ff_chunk)
        up = jnp.maximum(d(h2, wu_ref[:, sl]), 0.0)
        acc = acc + d((up * up).astype(BF16), wd_ref[sl, :])
    x = x + mod_ref[5:6, :] * acc
    y_ref[...] = _rms(x, gf_ref[...])


def _post_call(x, o_attn, o_hy, gates, mod, g2, gf, wa, wh, wo, wu, wd, *, seq, latent, tm):
    t = x.shape[0]
    per_seq = seq // tm if latent else 1
    row = lambda i: (i, 0)

    def mod_idx(i):
        return (1 + i // per_seq if latent else 0, 0, 0)

    return pl.pallas_call(
        _post_kernel,
        grid=(t // tm,),
        in_specs=[pl.BlockSpec((tm, D_MODEL), row),
                  pl.BlockSpec((tm, N_HEADS * V_DIM), row),
                  pl.BlockSpec((tm, D_HYENA), row),
                  pl.BlockSpec((tm, GATE_COLS), row),
                  pl.BlockSpec((None, 6, D_MODEL), mod_idx),
                  _resident((1, D_MODEL)), _resident((1, D_MODEL)),
                  _resident(wa.shape), _resident(wh.shape), _resident(wo.shape),
                  _resident(wu.shape), _resident(wd.shape)],
        out_specs=pl.BlockSpec((tm, D_MODEL), row),
        out_shape=jax.ShapeDtypeStruct((t, D_MODEL), F32),
        compiler_params=_params("arbitrary"),
        name="post_lat" if latent else "post_ctx",
    )(x, o_attn, o_hy, gates, mod, g2, gf, wa, wh, wo, wu, wd)


def _rope_tables(seq):
    half = HEAD_DIM // 2
    n = half // 2
    inv = ROPE_THETA ** (-np.arange(n, dtype=np.float64) / n)
    pos = np.arange(seq)
    ang_row = (pos // GRID_W).astype(np.float64)[:, None] * inv[None, :]
    ang_col = (pos % GRID_W).astype(np.float64)[:, None] * inv[None, :]
    zeros = np.zeros_like(ang_row)

    def per_map(a_row, a_col, lo, hi):
        return np.concatenate([lo(a_row), hi(a_row), lo(a_col), hi(a_col)], axis=-1)

    cos = per_map(ang_row, ang_col, np.cos, np.cos)
    sin_lo = per_map(ang_row, ang_col, lambda a: -np.sin(a), lambda a: zeros)
    sin_hi = per_map(ang_row, ang_col, lambda a: zeros, np.sin)
    return tuple(jnp.asarray(np.concatenate([tab, tab], axis=-1).astype(np.float32))
                 for tab in (cos, sin_lo, sin_hi))


def _filter_embedding(seq):
    bands = (FILTER_EMB - 1) // 2
    t = np.linspace(0.0, 1.0, seq)[:, None]
    wpos = 2.0 * np.pi * np.arange(seq, dtype=np.float64)[:, None] / seq
    f = np.linspace(1e-4, bands - 1, bands)[None, :]
    emb = np.concatenate([t, np.cos(f * wpos), -np.sin(f * wpos)], axis=-1)
    emb = np.pad(emb, ((0, 0), (0, FILT_PAD - FILTER_EMB)))
    return jnp.asarray(emb.astype(np.float32)), jnp.asarray(t.astype(np.float32))


def _tiles(seq):
    short = seq <= 256
    return dict(tm=512, tq=128, attn_heads=N_HEADS if short else 2, ct=MXU_WIDTH,
                hyena_seqs=4 if short else 2)


def _pad_to(x, rows, cols):
    return jnp.pad(x, ((0, rows - x.shape[0]), (0, cols - x.shape[1])))


def kernel(x_prompt, x_sample, cache_k, cache_v, c, c_ctx, w_ada, b_ada, norm1_g, norm2_g, w_in,
           lam_q1, lam_k1, lam_q2, lam_k2, attn_subln_g, conv_w, conv_b, filt_w1, filt_b1,
           filt_w2, filt_b2, filt_w3, filt_freq, hy_bias, w_br_attn, w_br_hy, w_out, w_up,
           w_down, final_g):
    depth = w_in.shape[0]
    assert depth == 1, "single trunk layer"
    layer = 0
    lam_init = 0.8 - 0.6 * math.exp(-0.3 * layer)
    n_ctx, ctx_len, _ = x_prompt.shape
    n_lat, lat_len, _ = x_sample.shape
    past = cache_k.shape[2]

    cvec = jnp.concatenate([c_ctx[None, :], c], axis=0)
    cvec = jnp.pad(cvec, ((0, MOD_ROWS - cvec.shape[0]), (0, 0)))
    mod = _mod_call(cvec, w_ada[layer], b_ada[layer][None, :]).reshape(MOD_ROWS, 6, D_MODEL)

    w_in_b = w_in[layer].astype(BF16)
    wa, wh, wo = (w[layer].astype(BF16) for w in (w_br_attn, w_br_hy, w_out))
    wu, wd = w_up[layer].astype(BF16), w_down[layer].astype(BF16)
    g1, g2, gf = norm1_g[layer][None, :], norm2_g[layer][None, :], final_g[None, :]
    lam_vecs = jnp.stack([lam_q1[layer], lam_k1[layer], lam_q2[layer], lam_k2[layer]])
    subln_g = attn_subln_g[layer][None, :]

    w1 = _pad_to(filt_w1[layer], FILT_PAD, FILT_PAD)
    w2 = _pad_to(filt_w2[layer], FILT_PAD, FILT_PAD)
    w3 = _pad_to(filt_w3[layer], FILT_PAD, filt_w3.shape[2])
    b1 = _pad_to(filt_b1[layer][None, :], 1, FILT_PAD)
    b2 = _pad_to(filt_b2[layer][None, :], 1, FILT_PAD)
    fr = _pad_to(filt_freq[layer][None, :], 1, FILT_PAD)
    deltas = jnp.asarray(np.linspace(math.log(DECAY_TARGET) / FAST_DECAY_PCT,
                                     math.log(DECAY_TARGET) / SLOW_DECAY_PCT,
                                     D_HYENA)[None, :].astype(np.float32))

    def branch(x3, batch, seq, latent):
        x = x3.reshape(batch * seq, D_MODEL)
        tiles = _tiles(seq)
        tm = tiles["tm"]
        rope = _rope_tables(seq) if latent else None
        outs = _in_proj_call(x, mod, g1, w_in_b, rope, seq=seq, latent=latent, tm=tm)
        if latent:
            q, kt, v, u, gates = outs
            ckt = jnp.transpose(cache_k[:, layer], (0, 2, 3, 4, 1)).reshape(batch * K_COLS, past)
            cache = (ckt, cache_v[:, layer].reshape(batch * past * N_HEADS, V_DIM))
            kf = vf = None
        else:
            q, kt, v, kf, vf, u, gates = outs
            cache = None
        o_attn = _attn_call(q, kt, v, cache, lam_vecs, subln_g, batch=batch, seq=seq,
                            lam_init=lam_init, tq=tiles["tq"], heads=tiles["attn_heads"])

        fwd_np, inv_np = _dft_tables(seq)
        fwd_b = jnp.asarray(fwd_np).astype(BF16)
        inv_b = jnp.asarray(inv_np).astype(BF16)
        emb, tcol = _filter_embedding(seq)
        hf = _filt_call(emb, w1, b1, w2, b2, fr, w3, tcol, deltas, fwd_b,
                        seq=seq, ct=tiles["ct"])
        o_hy = _hyena_call(u, conv_w[layer], conv_b[layer][None, :], hf, hy_bias[layer],
                           fwd_b, inv_b, batch=batch, seq=seq, ct=tiles["ct"],
                           n_chain=tiles["hyena_seqs"])

        y = _post_call(x, o_attn, o_hy, gates, mod, g2, gf, wa, wh, wo, wu, wd,
                       seq=seq, latent=latent, tm=tm)
        return y.reshape(batch, seq, D_MODEL), kf, vf

    y_prompt, kf, vf = branch(x_prompt, n_ctx, ctx_len, False)
    y_sample, _, _ = branch(x_sample, n_lat, lat_len, True)
    new_cache_k = jnp.transpose(kf.reshape(n_ctx, depth, N_HEADS, 2, HEAD_DIM, ctx_len),
                                (0, 1, 5, 2, 3, 4))
    new_cache_v = vf.reshape(n_ctx, depth, ctx_len, N_HEADS, V_DIM)
    return (y_prompt, y_sample, new_cache_k, new_cache_v)
```

```python
import functools
import math

import numpy as np
import jax
import jax.numpy as jnp
from jax import lax
from jax.experimental import pallas as pl
from jax.experimental.pallas import tpu as pltpu

D_MODEL = 1024
GRID_W = 64
N_HEADS = 8
HEAD_DIM = 64
V_DIM = 2 * HEAD_DIM
D_HYENA = D_MODEL // 2
HYENA_ORDER = 2
FILTER_EMB = 33
FILTER_HIDDEN = 64
D_FF = 4 * D_MODEL
ROPE_THETA = 10000.0
EPS = 1e-6
LOG2_E = math.log2(math.e)
DECAY_TARGET = 1e-2
FAST_DECAY_PCT = 0.3
SLOW_DECAY_PCT = 1.5

Q_COLS = N_HEADS * 2 * HEAD_DIM
K_COLS = N_HEADS * 2 * HEAD_DIM
V_COLS = N_HEADS * V_DIM
HY_COLS = 3 * D_HYENA
GATE_COLS = 2 * D_MODEL
Q_OFF = 0
K_OFF = Q_OFF + Q_COLS
V_OFF = K_OFF + K_COLS
HY_OFF = V_OFF + V_COLS
GATE_OFF = HY_OFF + HY_COLS
IN_COLS = GATE_OFF + GATE_COLS

LANES = 128
MXU_WIDTH = 256
MOD_ROWS = 16
FILT_PAD = 128
VMEM_LIMIT = 56 * 1024 * 1024

BF16 = jnp.bfloat16
F32 = jnp.float32


def _resident(shape):
    nd = len(shape)
    return pl.BlockSpec(shape, lambda *_: (0,) * nd, pipeline_mode=pl.Buffered(1))


def _params(*sem):
    return pltpu.CompilerParams(dimension_semantics=sem, vmem_limit_bytes=VMEM_LIMIT)


def _rms(x, g):
    return x * lax.rsqrt(jnp.mean(x * x, axis=-1, keepdims=True) + EPS) * g


def _mod_kernel(c_ref, w_ref, b_ref, o_ref):
    c = c_ref[...]
    s = (c * jax.nn.sigmoid(c)).astype(BF16)
    o_ref[...] = jnp.dot(s, w_ref[...].astype(BF16), preferred_element_type=F32) + b_ref[...]


def _mod_call(cvec, w_ada, b_ada):
    tn = 1024
    n = w_ada.shape[1]
    return pl.pallas_call(
        _mod_kernel,
        grid=(n // tn,),
        in_specs=[pl.BlockSpec((MOD_ROWS, D_MODEL), lambda j: (0, 0)),
                  pl.BlockSpec((D_MODEL, tn), lambda j: (0, j)),
                  pl.BlockSpec((1, tn), lambda j: (0, j))],
        out_specs=pl.BlockSpec((MOD_ROWS, tn), lambda j: (0, j)),
        out_shape=jax.ShapeDtypeStruct((MOD_ROWS, n), F32),
        compiler_params=_params("arbitrary"),
        name="mod",
    )(cvec, w_ada, b_ada)


def _rope(x, cos, sin_lo, sin_hi):
    return (x * cos + pltpu.roll(x, 16, axis=1) * sin_hi
            + pltpu.roll(x, LANES - 16, axis=1) * sin_lo)


IN_GROUPS = ((Q_OFF, Q_COLS), (K_OFF, K_COLS), (GATE_OFF, GATE_COLS), (V_OFF, V_COLS),
             (HY_OFF, HY_COLS))
G_Q, G_K, G_GATE, G_V, G_HY = range(len(IN_GROUPS))


def _in_proj_kernel(*refs, latent):
    if latent:
        (x_ref, mod_ref, g_ref, w_hbm, cos_ref, slo_ref, shi_ref,
         q_ref, kt_ref, v_ref, u_ref, gt_ref, w_s, w_sem) = refs
    else:
        (x_ref, mod_ref, g_ref, w_hbm,
         q_ref, kt_ref, v_ref, ktf_ref, vf_ref, u_ref, gt_ref, w_s, w_sem) = refs
    tm = x_ref.shape[0]

    def w_copy(g):
        off, width = IN_GROUPS[g]
        cols = slice(off, off + width)
        return pltpu.make_async_copy(w_hbm.at[:, cols], w_s.at[:, cols], w_sem.at[g])

    def body(first_step):
        if first_step:
            for g in range(len(IN_GROUPS)):
                w_copy(g).start()
        x = x_ref[...]
        h = _rms(x, g_ref[...]) * (1.0 + mod_ref[1:2, :]) + mod_ref[0:1, :]
        hb = h.astype(BF16)

        def proj(g):
            if first_step:
                w_copy(g).wait()
            off, width = IN_GROUPS[g]
            return jnp.dot(hb, w_s[:, off:off + width], preferred_element_type=F32)

        q = proj(G_Q) * (HEAD_DIM ** -0.5 * LOG2_E)
        k = proj(G_K)
        if latent:
            cos, slo, shi = cos_ref[...], slo_ref[...], shi_ref[...]
            k_heads = []
            for hd in range(N_HEADS):
                sl = slice(hd * LANES, (hd + 1) * LANES)
                q_ref[:, sl] = _rope(q[:, sl], cos, slo, shi).astype(BF16)
                k_heads.append(_rope(k[:, sl], cos, slo, shi))
            kt_ref[...] = jnp.concatenate(k_heads, axis=1).T.astype(BF16)
        else:
            q_ref[...] = q.astype(BF16)
            seq = kt_ref.shape[1]
            for s in range(tm // seq):
                kt = k[s * seq:(s + 1) * seq, :].T
                rows = slice(s * K_COLS, (s + 1) * K_COLS)
                ktf_ref[rows, :] = kt
                kt_ref[rows, :] = kt.astype(BF16)
        gt_ref[...] = jax.nn.sigmoid(proj(G_GATE))
        v = proj(G_V)
        v_ref[...] = v.astype(BF16)
        if not latent:
            for hd in range(N_HEADS):
                vf_ref[pl.ds(hd, tm, stride=N_HEADS), :] = v[:, hd * LANES:(hd + 1) * LANES]
        u_ref[...] = proj(G_HY)

    step = pl.program_id(0)
    pl.when(step == 0)(functools.partial(body, True))
    pl.when(step != 0)(functools.partial(body, False))


def _in_proj_call(x, mod, g1, w_in, rope, *, seq, latent, tm):
    t = x.shape[0]
    per_seq = seq // tm if latent else 1

    def mod_idx(i):
        return (1 + i // per_seq if latent else 0, 0, 0)

    row = lambda i: (i, 0)
    in_specs = [pl.BlockSpec((tm, D_MODEL), row),
                pl.BlockSpec((None, 6, D_MODEL), mod_idx),
                _resident((1, D_MODEL)),
                pl.BlockSpec(memory_space=pl.ANY)]
    args = [x, mod, g1, w_in]
    tok = lambda width, dtype: ((t, width), dtype, pl.BlockSpec((tm, width), row))
    if latent:
        in_specs += [pl.BlockSpec((tm, LANES), lambda i: (i % per_seq, 0))] * 3
        args += list(rope)
        kt_spec = pl.BlockSpec((K_COLS, tm), lambda i: (i // per_seq, i % per_seq))
    else:
        kt_spec = pl.BlockSpec((tm // seq * K_COLS, seq), row)
    kt_shape = (t // seq * K_COLS, seq)
    outs = [tok(Q_COLS, BF16), (kt_shape, BF16, kt_spec), tok(V_COLS, BF16)]
    if not latent:
        outs += [(kt_shape, F32, kt_spec),
                 ((t * N_HEADS, LANES), F32, pl.BlockSpec((tm * N_HEADS, LANES), row))]
    outs += [tok(HY_COLS, F32), tok(GATE_COLS, F32)]
    return pl.pallas_call(
        functools.partial(_in_proj_kernel, latent=latent),
        grid=(t // tm,),
        in_specs=in_specs,
        out_specs=[spec for _, _, spec in outs],
        out_shape=[jax.ShapeDtypeStruct(s, d) for s, d, _ in outs],
        scratch_shapes=[pltpu.VMEM((D_MODEL, IN_COLS), BF16),
                        pltpu.SemaphoreType.DMA((len(IN_GROUPS),))],
        compiler_params=_params("arbitrary"),
        name="in_proj_lat" if latent else "in_proj_ctx",
    )(*args)


def _attn_kernel(*refs, n_cache, lam_init, tq):
    if n_cache:
        (q_ref, kt_ref, v_ref, ckt_ref, cv_ref, lam_ref, g_ref, o_ref, kt_s, v2_s) = refs
    else:
        (q_ref, kt_ref, v_ref, lam_ref, g_ref, o_ref, v2_s) = refs
    n_heads, n_keys, _ = v2_s.shape
    head0 = pl.program_id(1) * n_heads
    lv = lam_ref[...]
    lam = (jnp.exp(jnp.sum(lv[0:1] * lv[1:2], axis=-1, keepdims=True))
           - jnp.exp(jnp.sum(lv[2:3] * lv[3:4], axis=-1, keepdims=True)) + lam_init)
    gain = g_ref[...] * (1.0 - lam_init)
    first_map = lax.broadcasted_iota(jnp.int32, (1, LANES), 1) < HEAD_DIM
    zero = jnp.zeros((tq, LANES), BF16)

    def operands(hd):
        cols = slice(hd * LANES, (hd + 1) * LANES)
        v2_s[hd, :, LANES:] = jnp.ones((n_keys, LANES), BF16)
        if n_cache:
            kt_s[cols, 0:n_cache] = ckt_ref[cols, :].astype(BF16)
            kt_s[cols, n_cache:] = kt_ref[cols, :]
            cache_rows = pl.ds(head0 + hd, n_cache, stride=N_HEADS)
            v2_s[hd, 0:n_cache, 0:LANES] = cv_ref[cache_rows, :].astype(BF16)
            v2_s[hd, n_cache:, 0:LANES] = v_ref[:, cols]
            return kt_s[cols, :], v2_s[hd]
        v2_s[hd, :, 0:LANES] = v_ref[:, cols]
        return kt_ref[cols, :], v2_s[hd]

    def scores(hd, i, kt):
        q = q_ref[i * tq:(i + 1) * tq, hd * LANES:(hd + 1) * LANES]
        q2 = jnp.concatenate([jnp.where(first_map, q, zero), jnp.where(first_map, zero, q)], axis=0)
        s = jnp.dot(q2, kt, preferred_element_type=F32)
        return s, jnp.max(s, axis=-1, keepdims=True)

    def finish(hd, i, v2, s, m):
        p = jnp.exp2(s - m).astype(BF16)
        pv = jnp.dot(p, v2, preferred_element_type=F32)
        pv = pv[:, :LANES] / pv[:, LANES:]
        o = pv[:tq] - lam * pv[tq:]
        o_ref[i * tq:(i + 1) * tq, hd * LANES:(hd + 1) * LANES] = (_rms(o, 1.0) * gain).astype(BF16)

    n_tiles = q_ref.shape[0] // tq
    pending = None
    for hd in range(n_heads):
        kt, v2 = operands(hd)
        for i in range(n_tiles):
            cur = (hd, i, v2) + scores(hd, i, kt)
            if pending is not None:
                finish(*pending)
            pending = cur
    finish(*pending)


def _attn_call(q, kt, v, cache, lam_vecs, subln_g, *, batch, seq, lam_init, tq, heads):
    const = lambda b, h: (0, 0)
    width = heads * LANES
    n_groups = N_HEADS // heads
    in_specs = [pl.BlockSpec((seq, width), lambda b, h: (b, h)),
                pl.BlockSpec((width, seq), lambda b, h: (b * n_groups + h, 0)),
                pl.BlockSpec((seq, width), lambda b, h: (b, h))]
    args = [q, kt, v]
    n_cache = 0
    if cache is not None:
        ckt, cv = cache
        n_cache = ckt.shape[1]
        in_specs += [pl.BlockSpec((width, n_cache), lambda b, h: (b * n_groups + h, 0)),
                     pl.BlockSpec((n_cache * N_HEADS, LANES), lambda b, h: (b, 0))]
        args += [ckt, cv]
    n_keys = n_cache + seq
    scratch = [pltpu.VMEM((heads, n_keys, 2 * LANES), BF16)]
    if n_cache:
        scratch = [pltpu.VMEM((width, n_keys), BF16)] + scratch
    in_specs += [pl.BlockSpec(lam_vecs.shape, const), pl.BlockSpec((1, LANES), const)]
    args += [lam_vecs, subln_g]
    return pl.pallas_call(
        functools.partial(_attn_kernel, n_cache=n_cache, lam_init=lam_init, tq=tq),
        grid=(batch, n_groups),
        in_specs=in_specs,
        out_specs=pl.BlockSpec((seq, width), lambda b, h: (b, h)),
        out_shape=jax.ShapeDtypeStruct((batch * seq, N_HEADS * V_DIM), BF16),
        scratch_shapes=scratch,
        compiler_params=_params("arbitrary", "arbitrary"),
        name="attn_lat" if cache is not None else "attn_ctx",
    )(*args)


def _dft_tables(seq):
    n = 2 * seq
    f = np.arange(seq, dtype=np.int64)[:, None]
    t = np.arange(seq, dtype=np.int64)[None, :]
    ang = 2.0 * np.pi * ((f * t) % n).astype(np.float64) / n
    alt = np.where(np.arange(seq) % 2 == 0, 1.0, -1.0)
    cos, msin = np.cos(ang), -np.sin(ang)
    msin[0, :] = alt
    fwd = np.concatenate([cos, msin], axis=0)
    wgt = np.where(f == 0, 1.0, 2.0) / n
    inv_c = (cos * wgt).T
    inv_s = (msin * wgt).T
    inv = np.concatenate([inv_c, inv_s], axis=1)
    return fwd.astype(np.float32), inv.astype(np.float32)


def _filt_kernel(emb_ref, w1_ref, b1_ref, w2_ref, b2_ref, fr_ref, w3f_ref, w3b_ref,
                 t_ref, dl_ref, dft_ref, o_ref, h_s):
    seq = emb_ref.shape[0]
    hp = functools.partial(jnp.dot, preferred_element_type=F32, precision=lax.Precision.HIGHEST)

    @pl.when(pl.program_id(0) == 0)
    def _():
        fr = fr_ref[...]
        h = jnp.sin(fr * (hp(emb_ref[...], w1_ref[...]) + b1_ref[...]))
        h_s[...] = jnp.sin(fr * (hp(h, w2_ref[...]) + b2_ref[...]))

    h = h_s[...]
    decay = jnp.exp(-t_ref[...] * jnp.abs(dl_ref[...]))
    fwd = hp(h, w3f_ref[...]) * decay
    bwd = hp(h, w3b_ref[...]) * decay
    hsum, hdif = fwd + bwd, fwd - bwd
    re = jnp.dot(dft_ref[0:seq, :], hsum.astype(BF16), preferred_element_type=F32)
    im = jnp.dot(dft_ref[seq:, :], hdif.astype(BF16), preferred_element_type=F32)
    row = lax.broadcasted_iota(jnp.int32, (seq, 1), 0)
    alt = jnp.where(row % 2 == 0, 1.0, -1.0)
    nyq = jnp.sum(hsum * alt, axis=0, keepdims=True)
    o_ref[0] = re
    o_ref[1] = jnp.where(row == 0, 0.0, im)
    o_ref[2] = jnp.where(row == 0, nyq, re)


def _filt_call(emb, w1, b1, w2, b2, fr, w3, tcol, deltas, fwd, *, seq, ct):
    n_out = HYENA_ORDER * D_HYENA
    nct = n_out // ct
    per_order = D_HYENA // ct
    small = lambda shape: pl.BlockSpec(shape, lambda j: (0, 0))
    return pl.pallas_call(
        _filt_kernel,
        grid=(nct,),
        in_specs=[small((seq, FILT_PAD)), small((FILT_PAD, FILT_PAD)), small((1, FILT_PAD)),
                  small((FILT_PAD, FILT_PAD)), small((1, FILT_PAD)), small((1, FILT_PAD)),
                  pl.BlockSpec((FILT_PAD, ct), lambda j: (0, j)),
                  pl.BlockSpec((FILT_PAD, ct), lambda j: (0, nct + j)),
                  small((seq, 1)),
                  pl.BlockSpec((1, ct), lambda j: (0, j % per_order)),
                  _resident((2 * seq, seq))],
        out_specs=pl.BlockSpec((3, seq, ct), lambda j: (0, 0, j)),
        out_shape=jax.ShapeDtypeStruct((3, seq, n_out), F32),
        scratch_shapes=[pltpu.VMEM((seq, FILT_PAD), F32)],
        compiler_params=_params("arbitrary"),
        name=f"filt_{seq}",
    )(emb, w1, b1, w2, b2, fr, w3, w3, tcol, deltas, fwd)


def _hyena_kernel(uv_ref, u1_ref, u2_ref, wv_ref, w1_ref, w2_ref, bv_ref, b1_ref, b2_ref,
                  h0_ref, h1_ref, hb_ref, fwd_ref, inv_ref, o_ref):
    seq = h0_ref.shape[1]
    chains = [slice(c * seq, (c + 1) * seq) for c in range(uv_ref.shape[0] // seq)]
    row = lax.broadcasted_iota(jnp.int32, (seq, 1), 0)
    first, last = row == 0, row == seq - 1

    def short_conv(u_ref, w_ref, b_ref):
        outs = []
        for rows in chains:
            u = u_ref[rows, :]
            prev = jnp.where(first, 0.0, pltpu.roll(u, 1, axis=0))
            nxt = jnp.where(last, 0.0, pltpu.roll(u, seq - 1, axis=0))
            outs.append(prev * w_ref[0:1, :] + u * w_ref[1:2, :] + nxt * w_ref[2:3, :] + b_ref[...])
        return outs

    def long_conv(us, h_ref, bias):
        specs = [jnp.dot(fwd_ref[...], u.astype(BF16), preferred_element_type=F32) for u in us]
        ys = []
        for spec in specs:
            ure, uim = spec[:seq], spec[seq:]
            yre = ure * h_ref[0] - uim * h_ref[1]
            yim = ure * h_ref[1] + uim * h_ref[2]
            ys.append(jnp.concatenate([yre, yim], axis=0).astype(BF16))
        return [jnp.dot(inv_ref[...], y, preferred_element_type=F32) + u * bias
                for y, u in zip(ys, us)]

    v = short_conv(uv_ref, wv_ref, bv_ref)
    x1 = short_conv(u1_ref, w1_ref, b1_ref)
    z = [a * b for a, b in zip(x1, long_conv(v, h0_ref, hb_ref[0:1, :]))]
    x2 = short_conv(u2_ref, w2_ref, b2_ref)
    for rows, a, b in zip(chains, x2, long_conv(z, h1_ref, hb_ref[1:2, :])):
        o_ref[rows, :] = (a * b).astype(BF16)


def _hyena_call(u, conv_w, conv_b, hf, hy_bias, fwd, inv, *, batch, seq, ct, n_chain):
    nct = D_HYENA // ct
    rows = n_chain * seq
    col = lambda part: (lambda c, b: (b, part * nct + c))
    wcol = lambda part: (lambda c, b: (0, part * nct + c))
    in_specs = ([pl.BlockSpec((rows, ct), col(p)) for p in range(3)]
                + [pl.BlockSpec((3, ct), wcol(p)) for p in range(3)]
                + [pl.BlockSpec((1, ct), wcol(p)) for p in range(3)]
                + [pl.BlockSpec((3, seq, ct), lambda c, b, o=o: (0, 0, o * nct + c))
                   for o in range(HYENA_ORDER)]
                + [pl.BlockSpec((HYENA_ORDER, ct), lambda c, b: (0, c)),
                   _resident((2 * seq, seq)), _resident((seq, 2 * seq))])
    return pl.pallas_call(
        _hyena_kernel,
        grid=(nct, batch // n_chain),
        in_specs=in_specs,
        out_specs=pl.BlockSpec((rows, ct), lambda c, b: (b, c)),
        out_shape=jax.ShapeDtypeStruct((batch * seq, D_HYENA), BF16),
        compiler_params=_params("arbitrary", "arbitrary"),
        name=f"hyena_{seq}",
    )(u, u, u, conv_w, conv_w, conv_w, conv_b, conv_b, conv_b, hf, hf, hy_bias, fwd, inv)


FF_CHUNK = D_MODEL
N_FF_CHUNKS = D_FF // FF_CHUNK


def _post_kernel(x_ref, oa_ref, oh_ref, gt_ref, mod_ref, g2_ref, gf_ref,
                 wa_hbm, wh_hbm, wo_hbm, wu_hbm, wd_hbm, y_ref,
                 wa_s, wh_s, wo_s, wu_s, wd_s, w_sem):
    ff = [slice(c * FF_CHUNK, (c + 1) * FF_CHUNK) for c in range(N_FF_CHUNKS)]
    copies = ([(wa_hbm, wa_s), (wh_hbm, wh_s), (wo_hbm, wo_s)]
              + [piece for sl in ff for piece in ((wu_hbm.at[:, sl], wu_s.at[:, sl]),
                                                  (wd_hbm.at[sl, :], wd_s.at[sl, :]))])

    def w_copy(n):
        src, dst = copies[n]
        return pltpu.make_async_copy(src, dst, w_sem.at[n])

    def body(first_step):
        if first_step:
            for n in range(len(copies)):
                w_copy(n).start()

        def ready(n):
            if first_step:
                w_copy(n).wait()

        d = functools.partial(jnp.dot, preferred_element_type=F32)
        ready(0)
        attn = d(oa_ref[...], wa_s[...])
        ready(1)
        merged = gt_ref[:, :D_MODEL] * attn + gt_ref[:, D_MODEL:] * d(oh_ref[...], wh_s[...])
        ready(2)
        x = x_ref[...] + mod_ref[2:3, :] * d(merged.astype(BF16), wo_s[...])
        h2 = (_rms(x, g2_ref[...]) * (1.0 + mod_ref[4:5, :]) + mod_ref[3:4, :]).astype(BF16)
        acc = jnp.zeros_like(x)
        for c, sl in enumerate(ff):
            ready(3 + 2 * c)
            up = jnp.maximum(d(h2, wu_s[:, sl]), 0.0)
            ready(4 + 2 * c)
            acc = acc + d((up * up).astype(BF16), wd_s[sl, :])
        x = x + mod_ref[5:6, :] * acc
        y_ref[...] = _rms(x, gf_ref[...])

    step = pl.program_id(0)
    pl.when(step == 0)(functools.partial(body, True))
    pl.when(step != 0)(functools.partial(body, False))


def _post_call(x, o_attn, o_hy, gates, mod, g2, gf, wa, wh, wo, wu, wd, *, seq, latent, tm):
    t = x.shape[0]
    per_seq = seq // tm if latent else 1
    row = lambda i: (i, 0)

    def mod_idx(i):
        return (1 + i // per_seq if latent else 0, 0, 0)

    return pl.pallas_call(
        _post_kernel,
        grid=(t // tm,),
        in_specs=[pl.BlockSpec((tm, D_MODEL), row),
                  pl.BlockSpec((tm, N_HEADS * V_DIM), row),
                  pl.BlockSpec((tm, D_HYENA), row),
                  pl.BlockSpec((tm, GATE_COLS), row),
                  pl.BlockSpec((None, 6, D_MODEL), mod_idx),
                  _resident((1, D_MODEL)), _resident((1, D_MODEL))]
                 + [pl.BlockSpec(memory_space=pl.ANY)] * 5,
        out_specs=pl.BlockSpec((tm, D_MODEL), row),
        out_shape=jax.ShapeDtypeStruct((t, D_MODEL), F32),
        scratch_shapes=[pltpu.VMEM(w.shape, BF16) for w in (wa, wh, wo, wu, wd)]
                       + [pltpu.SemaphoreType.DMA((3 + 2 * N_FF_CHUNKS,))],
        compiler_params=_params("arbitrary"),
        name="post_lat" if latent else "post_ctx",
    )(x, o_attn, o_hy, gates, mod, g2, gf, wa, wh, wo, wu, wd)


def _rope_tables(seq):
    half = HEAD_DIM // 2
    n = half // 2
    inv = ROPE_THETA ** (-np.arange(n, dtype=np.float64) / n)
    pos = np.arange(seq)
    ang_row = (pos // GRID_W).astype(np.float64)[:, None] * inv[None, :]
    ang_col = (pos % GRID_W).astype(np.float64)[:, None] * inv[None, :]
    zeros = np.zeros_like(ang_row)

    def per_map(a_row, a_col, lo, hi):
        return np.concatenate([lo(a_row), hi(a_row), lo(a_col), hi(a_col)], axis=-1)

    cos = per_map(ang_row, ang_col, np.cos, np.cos)
    sin_lo = per_map(ang_row, ang_col, lambda a: -np.sin(a), lambda a: zeros)
    sin_hi = per_map(ang_row, ang_col, lambda a: zeros, np.sin)
    return tuple(jnp.asarray(np.concatenate([tab, tab], axis=-1).astype(np.float32))
                 for tab in (cos, sin_lo, sin_hi))


def _filter_embedding(seq):
    bands = (FILTER_EMB - 1) // 2
    t = np.linspace(0.0, 1.0, seq)[:, None]
    wpos = 2.0 * np.pi * np.arange(seq, dtype=np.float64)[:, None] / seq
    f = np.linspace(1e-4, bands - 1, bands)[None, :]
    emb = np.concatenate([t, np.cos(f * wpos), -np.sin(f * wpos)], axis=-1)
    emb = np.pad(emb, ((0, 0), (0, FILT_PAD - FILTER_EMB)))
    return jnp.asarray(emb.astype(np.float32)), jnp.asarray(t.astype(np.float32))


def _tiles(seq):
    short = seq <= 256
    return dict(tm=512, tq=128, attn_heads=N_HEADS if short else 2, ct=MXU_WIDTH,
                hyena_seqs=4 if short else 2)


def _pad_to(x, rows, cols):
    return jnp.pad(x, ((0, rows - x.shape[0]), (0, cols - x.shape[1])))


def kernel(x_prompt, x_sample, cache_k, cache_v, c, c_ctx, w_ada, b_ada, norm1_g, norm2_g, w_in,
           lam_q1, lam_k1, lam_q2, lam_k2, attn_subln_g, conv_w, conv_b, filt_w1, filt_b1,
           filt_w2, filt_b2, filt_w3, filt_freq, hy_bias, w_br_attn, w_br_hy, w_out, w_up,
           w_down, final_g):
    depth = w_in.shape[0]
    assert depth == 1, "single trunk layer"
    layer = 0
    lam_init = 0.8 - 0.6 * math.exp(-0.3 * layer)
    n_ctx, ctx_len, _ = x_prompt.shape
    n_lat, lat_len, _ = x_sample.shape
    past = cache_k.shape[2]

    cvec = jnp.concatenate([c_ctx[None, :], c], axis=0)
    cvec = jnp.pad(cvec, ((0, MOD_ROWS - cvec.shape[0]), (0, 0)))
    mod = _mod_call(cvec, w_ada[layer], b_ada[layer][None, :]).reshape(MOD_ROWS, 6, D_MODEL)

    w_in_b = w_in[layer].astype(BF16)
    wa, wh, wo = (w[layer].astype(BF16) for w in (w_br_attn, w_br_hy, w_out))
    wu, wd = w_up[layer].astype(BF16), w_down[layer].astype(BF16)
    g1, g2, gf = norm1_g[layer][None, :], norm2_g[layer][None, :], final_g[None, :]
    lam_vecs = jnp.stack([lam_q1[layer], lam_k1[layer], lam_q2[layer], lam_k2[layer]])
    subln_g = attn_subln_g[layer][None, :]

    w1 = _pad_to(filt_w1[layer], FILT_PAD, FILT_PAD)
    w2 = _pad_to(filt_w2[layer], FILT_PAD, FILT_PAD)
    w3 = _pad_to(filt_w3[layer], FILT_PAD, filt_w3.shape[2])
    b1 = _pad_to(filt_b1[layer][None, :], 1, FILT_PAD)
    b2 = _pad_to(filt_b2[layer][None, :], 1, FILT_PAD)
    fr = _pad_to(filt_freq[layer][None, :], 1, FILT_PAD)
    deltas = jnp.asarray(np.linspace(math.log(DECAY_TARGET) / FAST_DECAY_PCT,
                                     math.log(DECAY_TARGET) / SLOW_DECAY_PCT,
                                     D_HYENA)[None, :].astype(np.float32))

    def branch(x3, batch, seq, latent):
        x = x3.reshape(batch * seq, D_MODEL)
        tiles = _tiles(seq)
        tm = tiles["tm"]
        rope = _rope_tables(seq) if latent else None
        outs = _in_proj_call(x, mod, g1, w_in_b, rope, seq=seq, latent=latent, tm=tm)
        if latent:
            q, kt, v, u, gates = outs
            ckt = jnp.transpose(cache_k[:, layer], (0, 2, 3, 4, 1)).reshape(batch * K_COLS, past)
            cache = (ckt, cache_v[:, layer].reshape(batch * past * N_HEADS, V_DIM))
            kf = vf = None
        else:
            q, kt, v, kf, vf, u, gates = outs
            cache = None
        o_attn = _attn_call(q, kt, v, cache, lam_vecs, subln_g, batch=batch, seq=seq,
                            lam_init=lam_init, tq=tiles["tq"], heads=tiles["attn_heads"])

        fwd_np, inv_np = _dft_tables(seq)
        fwd_b = jnp.asarray(fwd_np).astype(BF16)
        inv_b = jnp.asarray(inv_np).astype(BF16)
        emb, tcol = _filter_embedding(seq)
        hf = _filt_call(emb, w1, b1, w2, b2, fr, w3, tcol, deltas, fwd_b,
                        seq=seq, ct=tiles["ct"])
        o_hy = _hyena_call(u, conv_w[layer], conv_b[layer][None, :], hf, hy_bias[layer],
                           fwd_b, inv_b, batch=batch, seq=seq, ct=tiles["ct"],
                           n_chain=tiles["hyena_seqs"])

        y = _post_call(x, o_attn, o_hy, gates, mod, g2, gf, wa, wh, wo, wu, wd,
                       seq=seq, latent=latent, tm=tm)
        return y.reshape(batch, seq, D_MODEL), kf, vf

    y_prompt, kf, vf = branch(x_prompt, n_ctx, ctx_len, False)
    y_sample, _, _ = branch(x_sample, n_lat, lat_len, True)
    new_cache_k = jnp.transpose(kf.reshape(n_ctx, depth, N_HEADS, 2, HEAD_DIM, ctx_len),
                                (0, 1, 5, 2, 3, 4))
    new_cache_v = vf.reshape(n_ctx, depth, ctx_len, N_HEADS, V_DIM)
    return (y_prompt, y_sample, new_cache_k, new_cache_v)
```

```python
import functools
import math

import numpy as np
import jax
import jax.numpy as jnp
from jax import lax
from jax.experimental import pallas as pl
from jax.experimental.pallas import tpu as pltpu

D_MODEL = 1024
GRID_W = 64
N_HEADS = 8
HEAD_DIM = 64
V_DIM = 2 * HEAD_DIM
D_HYENA = D_MODEL // 2
HYENA_ORDER = 2
FILTER_EMB = 33
FILTER_HIDDEN = 64
D_FF = 4 * D_MODEL
ROPE_THETA = 10000.0
EPS = 1e-6
LOG2_E = math.log2(math.e)
DECAY_TARGET = 1e-2
FAST_DECAY_PCT = 0.3
SLOW_DECAY_PCT = 1.5

Q_COLS = N_HEADS * 2 * HEAD_DIM
K_COLS = N_HEADS * 2 * HEAD_DIM
V_COLS = N_HEADS * V_DIM
HY_COLS = 3 * D_HYENA
GATE_COLS = 2 * D_MODEL
Q_OFF = 0
K_OFF = Q_OFF + Q_COLS
V_OFF = K_OFF + K_COLS
HY_OFF = V_OFF + V_COLS
GATE_OFF = HY_OFF + HY_COLS
IN_COLS = GATE_OFF + GATE_COLS

LANES = 128
MXU_WIDTH = 256
MOD_ROWS = 16
FILT_PAD = 128
VMEM_LIMIT = 56 * 1024 * 1024

BF16 = jnp.bfloat16
F32 = jnp.float32


def _resident(shape):
    nd = len(shape)
    return pl.BlockSpec(shape, lambda *_: (0,) * nd, pipeline_mode=pl.Buffered(1))


def _params(*sem):
    return pltpu.CompilerParams(dimension_semantics=sem, vmem_limit_bytes=VMEM_LIMIT)


def _rms(x, g):
    return x * lax.rsqrt(jnp.mean(x * x, axis=-1, keepdims=True) + EPS) * g


def _mod_kernel(c_ref, w_ref, b_ref, o_ref):
    c = c_ref[...]
    s = (c * jax.nn.sigmoid(c)).astype(BF16)
    o_ref[...] = jnp.dot(s, w_ref[...].astype(BF16), preferred_element_type=F32) + b_ref[...]


def _mod_call(cvec, w_ada, b_ada):
    tn = 1024
    n = w_ada.shape[1]
    return pl.pallas_call(
        _mod_kernel,
        grid=(n // tn,),
        in_specs=[pl.BlockSpec((MOD_ROWS, D_MODEL), lambda j: (0, 0)),
                  pl.BlockSpec((D_MODEL, tn), lambda j: (0, j)),
                  pl.BlockSpec((1, tn), lambda j: (0, j))],
        out_specs=pl.BlockSpec((MOD_ROWS, tn), lambda j: (0, j)),
        out_shape=jax.ShapeDtypeStruct((MOD_ROWS, n), F32),
        compiler_params=_params("arbitrary"),
        name="mod",
    )(cvec, w_ada, b_ada)


def _rope(x, cos, sin_lo, sin_hi):
    return (x * cos + pltpu.roll(x, 16, axis=1) * sin_hi
            + pltpu.roll(x, LANES - 16, axis=1) * sin_lo)


IN_GROUPS = ((Q_OFF, Q_COLS), (K_OFF, K_COLS), (V_OFF, V_COLS), (HY_OFF, HY_COLS))
G_Q, G_K, G_V, G_HY = range(len(IN_GROUPS))


def _in_proj_kernel(*refs, latent):
    if latent:
        (x_ref, mod_ref, g_ref, w_hbm, cos_ref, slo_ref, shi_ref,
         q_ref, kt_ref, v_ref, u_ref, w_s, w_sem) = refs
    else:
        (x_ref, mod_ref, g_ref, w_hbm,
         q_ref, kt_ref, v_ref, ktf_ref, vf_ref, u_ref, w_s, w_sem) = refs
    tm = x_ref.shape[0]

    def w_copy(g):
        off, width = IN_GROUPS[g]
        cols = slice(off, off + width)
        return pltpu.make_async_copy(w_hbm.at[:, cols], w_s.at[:, cols], w_sem.at[g])

    def body(first_step):
        if first_step:
            for g in range(len(IN_GROUPS)):
                w_copy(g).start()
        x = x_ref[...]
        h = _rms(x, g_ref[...]) * (1.0 + mod_ref[1:2, :]) + mod_ref[0:1, :]
        hb = h.astype(BF16)

        def proj(g):
            if first_step:
                w_copy(g).wait()
            off, width = IN_GROUPS[g]
            return jnp.dot(hb, w_s[:, off:off + width], preferred_element_type=F32)

        q = proj(G_Q) * (HEAD_DIM ** -0.5 * LOG2_E)
        k = proj(G_K)
        if latent:
            cos, slo, shi = cos_ref[...], slo_ref[...], shi_ref[...]
            k_heads = []
            for hd in range(N_HEADS):
                sl = slice(hd * LANES, (hd + 1) * LANES)
                q_ref[:, sl] = _rope(q[:, sl], cos, slo, shi).astype(BF16)
                k_heads.append(_rope(k[:, sl], cos, slo, shi))
            kt_ref[...] = jnp.concatenate(k_heads, axis=1).T.astype(BF16)
        else:
            q_ref[...] = q.astype(BF16)
            seq = kt_ref.shape[1]
            for s in range(tm // seq):
                kt = k[s * seq:(s + 1) * seq, :].T
                rows = slice(s * K_COLS, (s + 1) * K_COLS)
                ktf_ref[rows, :] = kt
                kt_ref[rows, :] = kt.astype(BF16)
        v = proj(G_V)
        v_ref[...] = v.astype(BF16)
        if not latent:
            for hd in range(N_HEADS):
                vf_ref[pl.ds(hd, tm, stride=N_HEADS), :] = v[:, hd * LANES:(hd + 1) * LANES]
        u_ref[...] = proj(G_HY)

    step = pl.program_id(0)
    pl.when(step == 0)(functools.partial(body, True))
    pl.when(step != 0)(functools.partial(body, False))


def _in_proj_call(x, mod, g1, w_in, rope, *, seq, latent, tm):
    t = x.shape[0]
    per_seq = seq // tm if latent else 1

    def mod_idx(i):
        return (1 + i // per_seq if latent else 0, 0, 0)

    row = lambda i: (i, 0)
    in_specs = [pl.BlockSpec((tm, D_MODEL), row),
                pl.BlockSpec((None, 6, D_MODEL), mod_idx),
                _resident((1, D_MODEL)),
                pl.BlockSpec(memory_space=pl.ANY)]
    args = [x, mod, g1, w_in]
    tok = lambda width, dtype: ((t, width), dtype, pl.BlockSpec((tm, width), row))
    if latent:
        in_specs += [pl.BlockSpec((tm, LANES), lambda i: (i % per_seq, 0))] * 3
        args += list(rope)
        kt_spec = pl.BlockSpec((K_COLS, tm), lambda i: (i // per_seq, i % per_seq))
    else:
        kt_spec = pl.BlockSpec((tm // seq * K_COLS, seq), row)
    kt_shape = (t // seq * K_COLS, seq)
    outs = [tok(Q_COLS, BF16), (kt_shape, BF16, kt_spec), tok(V_COLS, BF16)]
    if not latent:
        outs += [(kt_shape, F32, kt_spec),
                 ((t * N_HEADS, LANES), F32, pl.BlockSpec((tm * N_HEADS, LANES), row))]
    outs += [tok(HY_COLS, F32)]
    return pl.pallas_call(
        functools.partial(_in_proj_kernel, latent=latent),
        grid=(t // tm,),
        in_specs=in_specs,
        out_specs=[spec for _, _, spec in outs],
        out_shape=[jax.ShapeDtypeStruct(s, d) for s, d, _ in outs],
        scratch_shapes=[pltpu.VMEM((D_MODEL, GATE_OFF), BF16),
                        pltpu.SemaphoreType.DMA((len(IN_GROUPS),))],
        compiler_params=_params("arbitrary"),
        name="in_proj_lat" if latent else "in_proj_ctx",
    )(*args)


def _attn_kernel(*refs, n_cache, lam_init, tq):
    if n_cache:
        (q_ref, kt_ref, v_ref, ckt_ref, cv_ref, lam_ref, g_ref, o_ref, kt_s, v2_s) = refs
    else:
        (q_ref, kt_ref, v_ref, lam_ref, g_ref, o_ref, v2_s) = refs
    n_heads, n_keys, _ = v2_s.shape
    head0 = pl.program_id(1) * n_heads
    lv = lam_ref[...]
    lam = (jnp.exp(jnp.sum(lv[0:1] * lv[1:2], axis=-1, keepdims=True))
           - jnp.exp(jnp.sum(lv[2:3] * lv[3:4], axis=-1, keepdims=True)) + lam_init)
    gain = g_ref[...] * (1.0 - lam_init)
    first_map = lax.broadcasted_iota(jnp.int32, (1, LANES), 1) < HEAD_DIM
    zero = jnp.zeros((tq, LANES), BF16)

    def operands(hd):
        cols = slice(hd * LANES, (hd + 1) * LANES)
        v2_s[hd, :, LANES:] = jnp.ones((n_keys, LANES), BF16)
        if n_cache:
            kt_s[cols, 0:n_cache] = ckt_ref[cols, :].astype(BF16)
            kt_s[cols, n_cache:] = kt_ref[cols, :]
            cache_rows = pl.ds(head0 + hd, n_cache, stride=N_HEADS)
            v2_s[hd, 0:n_cache, 0:LANES] = cv_ref[cache_rows, :].astype(BF16)
            v2_s[hd, n_cache:, 0:LANES] = v_ref[:, cols]
            return kt_s[cols, :], v2_s[hd]
        v2_s[hd, :, 0:LANES] = v_ref[:, cols]
        return kt_ref[cols, :], v2_s[hd]

    def scores(hd, i, kt):
        q = q_ref[i * tq:(i + 1) * tq, hd * LANES:(hd + 1) * LANES]
        q2 = jnp.concatenate([jnp.where(first_map, q, zero), jnp.where(first_map, zero, q)], axis=0)
        s = jnp.dot(q2, kt, preferred_element_type=F32)
        return s, jnp.max(s, axis=-1, keepdims=True)

    def finish(hd, i, v2, s, m):
        p = jnp.exp2(s - m).astype(BF16)
        pv = jnp.dot(p, v2, preferred_element_type=F32)
        pv = pv[:, :LANES] / pv[:, LANES:]
        o = pv[:tq] - lam * pv[tq:]
        o_ref[i * tq:(i + 1) * tq, hd * LANES:(hd + 1) * LANES] = (_rms(o, 1.0) * gain).astype(BF16)

    n_tiles = q_ref.shape[0] // tq
    pending = None
    for hd in range(n_heads):
        kt, v2 = operands(hd)
        for i in range(n_tiles):
            cur = (hd, i, v2) + scores(hd, i, kt)
            if pending is not None:
                finish(*pending)
            pending = cur
    finish(*pending)


def _attn_call(q, kt, v, cache, lam_vecs, subln_g, *, batch, seq, lam_init, tq, heads):
    const = lambda b, h: (0, 0)
    width = heads * LANES
    n_groups = N_HEADS // heads
    in_specs = [pl.BlockSpec((seq, width), lambda b, h: (b, h)),
                pl.BlockSpec((width, seq), lambda b, h: (b * n_groups + h, 0)),
                pl.BlockSpec((seq, width), lambda b, h: (b, h))]
    args = [q, kt, v]
    n_cache = 0
    if cache is not None:
        ckt, cv = cache
        n_cache = ckt.shape[1]
        in_specs += [pl.BlockSpec((width, n_cache), lambda b, h: (b * n_groups + h, 0)),
                     pl.BlockSpec((n_cache * N_HEADS, LANES), lambda b, h: (b, 0))]
        args += [ckt, cv]
    n_keys = n_cache + seq
    scratch = [pltpu.VMEM((heads, n_keys, 2 * LANES), BF16)]
    if n_cache:
        scratch = [pltpu.VMEM((width, n_keys), BF16)] + scratch
    in_specs += [pl.BlockSpec(lam_vecs.shape, const), pl.BlockSpec((1, LANES), const)]
    args += [lam_vecs, subln_g]
    return pl.pallas_call(
        functools.partial(_attn_kernel, n_cache=n_cache, lam_init=lam_init, tq=tq),
        grid=(batch, n_groups),
        in_specs=in_specs,
        out_specs=pl.BlockSpec((seq, width), lambda b, h: (b, h)),
        out_shape=jax.ShapeDtypeStruct((batch * seq, N_HEADS * V_DIM), BF16),
        scratch_shapes=scratch,
        compiler_params=_params("arbitrary", "arbitrary"),
        name="attn_lat" if cache is not None else "attn_ctx",
    )(*args)


def _dft_tables(seq):
    n = 2 * seq
    f = np.arange(seq, dtype=np.int64)[:, None]
    t = np.arange(seq, dtype=np.int64)[None, :]
    ang = 2.0 * np.pi * ((f * t) % n).astype(np.float64) / n
    alt = np.where(np.arange(seq) % 2 == 0, 1.0, -1.0)
    cos, msin = np.cos(ang), -np.sin(ang)
    msin[0, :] = alt
    fwd = np.concatenate([cos, msin], axis=0)
    wgt = np.where(f == 0, 1.0, 2.0) / n
    inv_c = (cos * wgt).T
    inv_s = (msin * wgt).T
    inv = np.concatenate([inv_c, inv_s], axis=1)
    return fwd.astype(np.float32), inv.astype(np.float32)


def _filt_kernel(emb_ref, w1_ref, b1_ref, w2_ref, b2_ref, fr_ref, w3f_ref, w3b_ref,
                 t_ref, dl_ref, dft_ref, o_ref, h_s):
    seq = emb_ref.shape[0]
    hp = functools.partial(jnp.dot, preferred_element_type=F32, precision=lax.Precision.HIGHEST)

    @pl.when(pl.program_id(0) == 0)
    def _():
        fr = fr_ref[...]
        h = jnp.sin(fr * (hp(emb_ref[...], w1_ref[...]) + b1_ref[...]))
        h_s[...] = jnp.sin(fr * (hp(h, w2_ref[...]) + b2_ref[...]))

    h = h_s[...]
    decay = jnp.exp(-t_ref[...] * jnp.abs(dl_ref[...]))
    fwd = hp(h, w3f_ref[...]) * decay
    bwd = hp(h, w3b_ref[...]) * decay
    hsum, hdif = fwd + bwd, fwd - bwd
    re = jnp.dot(dft_ref[0:seq, :], hsum.astype(BF16), preferred_element_type=F32)
    im = jnp.dot(dft_ref[seq:, :], hdif.astype(BF16), preferred_element_type=F32)
    row = lax.broadcasted_iota(jnp.int32, (seq, 1), 0)
    alt = jnp.where(row % 2 == 0, 1.0, -1.0)
    nyq = jnp.sum(hsum * alt, axis=0, keepdims=True)
    o_ref[0] = re
    o_ref[1] = jnp.where(row == 0, 0.0, im)
    o_ref[2] = jnp.where(row == 0, nyq, re)


def _filt_call(emb, w1, b1, w2, b2, fr, w3, tcol, deltas, fwd, *, seq, ct):
    n_out = HYENA_ORDER * D_HYENA
    nct = n_out // ct
    per_order = D_HYENA // ct
    small = lambda shape: pl.BlockSpec(shape, lambda j: (0, 0))
    return pl.pallas_call(
        _filt_kernel,
        grid=(nct,),
        in_specs=[small((seq, FILT_PAD)), small((FILT_PAD, FILT_PAD)), small((1, FILT_PAD)),
                  small((FILT_PAD, FILT_PAD)), small((1, FILT_PAD)), small((1, FILT_PAD)),
                  pl.BlockSpec((FILT_PAD, ct), lambda j: (0, j)),
                  pl.BlockSpec((FILT_PAD, ct), lambda j: (0, nct + j)),
                  small((seq, 1)),
                  pl.BlockSpec((1, ct), lambda j: (0, j % per_order)),
                  _resident((2 * seq, seq))],
        out_specs=pl.BlockSpec((3, seq, ct), lambda j: (0, 0, j)),
        out_shape=jax.ShapeDtypeStruct((3, seq, n_out), F32),
        scratch_shapes=[pltpu.VMEM((seq, FILT_PAD), F32)],
        compiler_params=_params("arbitrary"),
        name=f"filt_{seq}",
    )(emb, w1, b1, w2, b2, fr, w3, w3, tcol, deltas, fwd)


def _hyena_kernel(uv_ref, u1_ref, u2_ref, wv_ref, w1_ref, w2_ref, bv_ref, b1_ref, b2_ref,
                  h0_ref, h1_ref, hb_ref, fwd_ref, inv_ref, o_ref):
    seq = h0_ref.shape[1]
    chains = [slice(c * seq, (c + 1) * seq) for c in range(uv_ref.shape[0] // seq)]
    row = lax.broadcasted_iota(jnp.int32, (seq, 1), 0)
    first, last = row == 0, row == seq - 1

    def short_conv(u_ref, w_ref, b_ref):
        outs = []
        for rows in chains:
            u = u_ref[rows, :]
            prev = jnp.where(first, 0.0, pltpu.roll(u, 1, axis=0))
            nxt = jnp.where(last, 0.0, pltpu.roll(u, seq - 1, axis=0))
            outs.append(prev * w_ref[0:1, :] + u * w_ref[1:2, :] + nxt * w_ref[2:3, :] + b_ref[...])
        return outs

    def long_conv(us, h_ref, bias):
        specs = [jnp.dot(fwd_ref[...], u.astype(BF16), preferred_element_type=F32) for u in us]
        ys = []
        for spec in specs:
            ure, uim = spec[:seq], spec[seq:]
            yre = ure * h_ref[0] - uim * h_ref[1]
            yim = ure * h_ref[1] + uim * h_ref[2]
            ys.append(jnp.concatenate([yre, yim], axis=0).astype(BF16))
        return [jnp.dot(inv_ref[...], y, preferred_element_type=F32) + u * bias
                for y, u in zip(ys, us)]

    v = short_conv(uv_ref, wv_ref, bv_ref)
    x1 = short_conv(u1_ref, w1_ref, b1_ref)
    z = [a * b for a, b in zip(x1, long_conv(v, h0_ref, hb_ref[0:1, :]))]
    x2 = short_conv(u2_ref, w2_ref, b2_ref)
    for rows, a, b in zip(chains, x2, long_conv(z, h1_ref, hb_ref[1:2, :])):
        o_ref[rows, :] = (a * b).astype(BF16)


def _hyena_call(u, conv_w, conv_b, hf, hy_bias, fwd, inv, *, batch, seq, ct, n_chain):
    nct = D_HYENA // ct
    rows = n_chain * seq
    col = lambda part: (lambda c, b: (b, part * nct + c))
    wcol = lambda part: (lambda c, b: (0, part * nct + c))
    in_specs = ([pl.BlockSpec((rows, ct), col(p)) for p in range(3)]
                + [pl.BlockSpec((3, ct), wcol(p)) for p in range(3)]
                + [pl.BlockSpec((1, ct), wcol(p)) for p in range(3)]
                + [pl.BlockSpec((3, seq, ct), lambda c, b, o=o: (0, 0, o * nct + c))
                   for o in range(HYENA_ORDER)]
                + [pl.BlockSpec((HYENA_ORDER, ct), lambda c, b: (0, c)),
                   _resident((2 * seq, seq)), _resident((seq, 2 * seq))])
    return pl.pallas_call(
        _hyena_kernel,
        grid=(nct, batch // n_chain),
        in_specs=in_specs,
        out_specs=pl.BlockSpec((rows, ct), lambda c, b: (b, c)),
        out_shape=jax.ShapeDtypeStruct((batch * seq, D_HYENA), BF16),
        compiler_params=_params("arbitrary", "arbitrary"),
        name=f"hyena_{seq}",
    )(u, u, u, conv_w, conv_w, conv_w, conv_b, conv_b, conv_b, hf, hf, hy_bias, fwd, inv)


FF_CHUNK = D_MODEL
N_FF_CHUNKS = D_FF // FF_CHUNK


def _post_kernel(x_ref, oa_ref, oh_ref, mod_ref, g1_ref, g2_ref, gf_ref,
                 win_hbm, wa_hbm, wh_hbm, wo_hbm, wu_hbm, wd_hbm, y_ref,
                 wg_s, wa_s, wh_s, wo_s, wu_s, wd_s, w_sem):
    ff = [slice(c * FF_CHUNK, (c + 1) * FF_CHUNK) for c in range(N_FF_CHUNKS)]
    copies = ([(win_hbm.at[:, GATE_OFF:], wg_s), (wa_hbm, wa_s), (wh_hbm, wh_s), (wo_hbm, wo_s)]
              + [piece for sl in ff for piece in ((wu_hbm.at[:, sl], wu_s.at[:, sl]),
                                                  (wd_hbm.at[sl, :], wd_s.at[sl, :]))])

    def w_copy(n):
        src, dst = copies[n]
        return pltpu.make_async_copy(src, dst, w_sem.at[n])

    def body(first_step):
        if first_step:
            for n in range(len(copies)):
                w_copy(n).start()

        def ready(n):
            if first_step:
                w_copy(n).wait()

        d = functools.partial(jnp.dot, preferred_element_type=F32)
        x = x_ref[...]
        h1 = (_rms(x, g1_ref[...]) * (1.0 + mod_ref[1:2, :]) + mod_ref[0:1, :]).astype(BF16)
        ready(0)
        gates = jax.nn.sigmoid(d(h1, wg_s[...]))
        ready(1)
        attn = d(oa_ref[...], wa_s[...])
        ready(2)
        merged = gates[:, :D_MODEL] * attn + gates[:, D_MODEL:] * d(oh_ref[...], wh_s[...])
        ready(3)
        x = x + mod_ref[2:3, :] * d(merged.astype(BF16), wo_s[...])
        h2 = (_rms(x, g2_ref[...]) * (1.0 + mod_ref[4:5, :]) + mod_ref[3:4, :]).astype(BF16)
        acc = jnp.zeros_like(x)
        for c, sl in enumerate(ff):
            ready(4 + 2 * c)
            up = jnp.maximum(d(h2, wu_s[:, sl]), 0.0)
            ready(5 + 2 * c)
            acc = acc + d((up * up).astype(BF16), wd_s[sl, :])
        x = x + mod_ref[5:6, :] * acc
        y_ref[...] = _rms(x, gf_ref[...])

    step = pl.program_id(0)
    pl.when(step == 0)(functools.partial(body, True))
    pl.when(step != 0)(functools.partial(body, False))


def _post_call(x, o_attn, o_hy, mod, g1, g2, gf, w_in, wa, wh, wo, wu, wd, *, seq, latent, tm):
    t = x.shape[0]
    per_seq = seq // tm if latent else 1
    row = lambda i: (i, 0)

    def mod_idx(i):
        return (1 + i // per_seq if latent else 0, 0, 0)

    return pl.pallas_call(
        _post_kernel,
        grid=(t // tm,),
        in_specs=[pl.BlockSpec((tm, D_MODEL), row),
                  pl.BlockSpec((tm, N_HEADS * V_DIM), row),
                  pl.BlockSpec((tm, D_HYENA), row),
                  pl.BlockSpec((None, 6, D_MODEL), mod_idx),
                  _resident((1, D_MODEL)), _resident((1, D_MODEL)), _resident((1, D_MODEL))]
                 + [pl.BlockSpec(memory_space=pl.ANY)] * 6,
        out_specs=pl.BlockSpec((tm, D_MODEL), row),
        out_shape=jax.ShapeDtypeStruct((t, D_MODEL), F32),
        scratch_shapes=[pltpu.VMEM((D_MODEL, GATE_COLS), BF16)]
                       + [pltpu.VMEM(w.shape, BF16) for w in (wa, wh, wo, wu, wd)]
                       + [pltpu.SemaphoreType.DMA((4 + 2 * N_FF_CHUNKS,))],
        compiler_params=_params("arbitrary"),
        name="post_lat" if latent else "post_ctx",
    )(x, o_attn, o_hy, mod, g1, g2, gf, w_in, wa, wh, wo, wu, wd)


def _rope_tables(seq):
    half = HEAD_DIM // 2
    n = half // 2
    inv = ROPE_THETA ** (-np.arange(n, dtype=np.float64) / n)
    pos = np.arange(seq)
    ang_row = (pos // GRID_W).astype(np.float64)[:, None] * inv[None, :]
    ang_col = (pos % GRID_W).astype(np.float64)[:, None] * inv[None, :]
    zeros = np.zeros_like(ang_row)

    def per_map(a_row, a_col, lo, hi):
        return np.concatenate([lo(a_row), hi(a_row), lo(a_col), hi(a_col)], axis=-1)

    cos = per_map(ang_row, ang_col, np.cos, np.cos)
    sin_lo = per_map(ang_row, ang_col, lambda a: -np.sin(a), lambda a: zeros)
    sin_hi = per_map(ang_row, ang_col, lambda a: zeros, np.sin)
    return tuple(jnp.asarray(np.concatenate([tab, tab], axis=-1).astype(np.float32))
                 for tab in (cos, sin_lo, sin_hi))


def _filter_embedding(seq):
    bands = (FILTER_EMB - 1) // 2
    t = np.linspace(0.0, 1.0, seq)[:, None]
    wpos = 2.0 * np.pi * np.arange(seq, dtype=np.float64)[:, None] / seq
    f = np.linspace(1e-4, bands - 1, bands)[None, :]
    emb = np.concatenate([t, np.cos(f * wpos), -np.sin(f * wpos)], axis=-1)
    emb = np.pad(emb, ((0, 0), (0, FILT_PAD - FILTER_EMB)))
    return jnp.asarray(emb.astype(np.float32)), jnp.asarray(t.astype(np.float32))


def _tiles(seq):
    short = seq <= 256
    return dict(tm=512, tq=128, attn_heads=N_HEADS if short else 2, ct=MXU_WIDTH,
                hyena_seqs=4 if short else 2)


def _pad_to(x, rows, cols):
    return jnp.pad(x, ((0, rows - x.shape[0]), (0, cols - x.shape[1])))


def kernel(x_prompt, x_sample, cache_k, cache_v, c, c_ctx, w_ada, b_ada, norm1_g, norm2_g, w_in,
           lam_q1, lam_k1, lam_q2, lam_k2, attn_subln_g, conv_w, conv_b, filt_w1, filt_b1,
           filt_w2, filt_b2, filt_w3, filt_freq, hy_bias, w_br_attn, w_br_hy, w_out, w_up,
           w_down, final_g):
    depth = w_in.shape[0]
    assert depth == 1, "single trunk layer"
    layer = 0
    lam_init = 0.8 - 0.6 * math.exp(-0.3 * layer)
    n_ctx, ctx_len, _ = x_prompt.shape
    n_lat, lat_len, _ = x_sample.shape
    past = cache_k.shape[2]

    cvec = jnp.concatenate([c_ctx[None, :], c], axis=0)
    cvec = jnp.pad(cvec, ((0, MOD_ROWS - cvec.shape[0]), (0, 0)))
    mod = _mod_call(cvec, w_ada[layer], b_ada[layer][None, :]).reshape(MOD_ROWS, 6, D_MODEL)

    w_in_b = w_in[layer].astype(BF16)
    wa, wh, wo = (w[layer].astype(BF16) for w in (w_br_attn, w_br_hy, w_out))
    wu, wd = w_up[layer].astype(BF16), w_down[layer].astype(BF16)
    g1, g2, gf = norm1_g[layer][None, :], norm2_g[layer][None, :], final_g[None, :]
    lam_vecs = jnp.stack([lam_q1[layer], lam_k1[layer], lam_q2[layer], lam_k2[layer]])
    subln_g = attn_subln_g[layer][None, :]

    w1 = _pad_to(filt_w1[layer], FILT_PAD, FILT_PAD)
    w2 = _pad_to(filt_w2[layer], FILT_PAD, FILT_PAD)
    w3 = _pad_to(filt_w3[layer], FILT_PAD, filt_w3.shape[2])
    b1 = _pad_to(filt_b1[layer][None, :], 1, FILT_PAD)
    b2 = _pad_to(filt_b2[layer][None, :], 1, FILT_PAD)
    fr = _pad_to(filt_freq[layer][None, :], 1, FILT_PAD)
    deltas = jnp.asarray(np.linspace(math.log(DECAY_TARGET) / FAST_DECAY_PCT,
                                     math.log(DECAY_TARGET) / SLOW_DECAY_PCT,
                                     D_HYENA)[None, :].astype(np.float32))

    def branch(x3, batch, seq, latent):
        x = x3.reshape(batch * seq, D_MODEL)
        tiles = _tiles(seq)
        tm = tiles["tm"]
        rope = _rope_tables(seq) if latent else None
        outs = _in_proj_call(x, mod, g1, w_in_b, rope, seq=seq, latent=latent, tm=tm)
        if latent:
            q, kt, v, u = outs
            ckt = jnp.transpose(cache_k[:, layer], (0, 2, 3, 4, 1)).reshape(batch * K_COLS, past)
            cache = (ckt, cache_v[:, layer].reshape(batch * past * N_HEADS, V_DIM))
            kf = vf = None
        else:
            q, kt, v, kf, vf, u = outs
            cache = None
        o_attn = _attn_call(q, kt, v, cache, lam_vecs, subln_g, batch=batch, seq=seq,
                            lam_init=lam_init, tq=tiles["tq"], heads=tiles["attn_heads"])

        fwd_np, inv_np = _dft_tables(seq)
        fwd_b = jnp.asarray(fwd_np).astype(BF16)
        inv_b = jnp.asarray(inv_np).astype(BF16)
        emb, tcol = _filter_embedding(seq)
        hf = _filt_call(emb, w1, b1, w2, b2, fr, w3, tcol, deltas, fwd_b,
                        seq=seq, ct=tiles["ct"])
        o_hy = _hyena_call(u, conv_w[layer], conv_b[layer][None, :], hf, hy_bias[layer],
                           fwd_b, inv_b, batch=batch, seq=seq, ct=tiles["ct"],
                           n_chain=tiles["hyena_seqs"])

        y = _post_call(x, o_attn, o_hy, mod, g1, g2, gf, w_in_b, wa, wh, wo, wu, wd,
                       seq=seq, latent=latent, tm=tm)
        return y.reshape(batch, seq, D_MODEL), kf, vf

    y_prompt, kf, vf = branch(x_prompt, n_ctx, ctx_len, False)
    y_sample, _, _ = branch(x_sample, n_lat, lat_len, True)
    new_cache_k = jnp.transpose(kf.reshape(n_ctx, depth, N_HEADS, 2, HEAD_DIM, ctx_len),
                                (0, 1, 5, 2, 3, 4))
    new_cache_v = vf.reshape(n_ctx, depth, ctx_len, N_HEADS, V_DIM)
    return (y_prompt, y_sample, new_cache_k, new_cache_v)
```

```python
import functools
import math

import numpy as np
import jax
import jax.numpy as jnp
from jax import lax
from jax.experimental import pallas as pl
from jax.experimental.pallas import tpu as pltpu

D_MODEL = 1024
GRID_W = 64
N_HEADS = 8
HEAD_DIM = 64
V_DIM = 2 * HEAD_DIM
D_HYENA = D_MODEL // 2
HYENA_ORDER = 2
FILTER_EMB = 33
FILTER_HIDDEN = 64
D_FF = 4 * D_MODEL
ROPE_THETA = 10000.0
EPS = 1e-6
LOG2_E = math.log2(math.e)
DECAY_TARGET = 1e-2
FAST_DECAY_PCT = 0.3
SLOW_DECAY_PCT = 1.5

Q_COLS = N_HEADS * 2 * HEAD_DIM
K_COLS = N_HEADS * 2 * HEAD_DIM
V_COLS = N_HEADS * V_DIM
HY_COLS = 3 * D_HYENA
GATE_COLS = 2 * D_MODEL
Q_OFF = 0
K_OFF = Q_OFF + Q_COLS
V_OFF = K_OFF + K_COLS
HY_OFF = V_OFF + V_COLS
GATE_OFF = HY_OFF + HY_COLS
IN_COLS = GATE_OFF + GATE_COLS

LANES = 128
MXU_WIDTH = 256
MOD_ROWS = 16
FILT_PAD = 128
VMEM_LIMIT = 56 * 1024 * 1024

BF16 = jnp.bfloat16
F32 = jnp.float32


def _resident(shape):
    nd = len(shape)
    return pl.BlockSpec(shape, lambda *_: (0,) * nd, pipeline_mode=pl.Buffered(1))


def _params(*sem):
    return pltpu.CompilerParams(dimension_semantics=sem, vmem_limit_bytes=VMEM_LIMIT)


def _rms(x, g):
    return x * lax.rsqrt(jnp.mean(x * x, axis=-1, keepdims=True) + EPS) * g


def _mod_kernel(c_ref, w_ref, b_ref, o_ref):
    c = c_ref[...]
    s = (c * jax.nn.sigmoid(c)).astype(BF16)
    o_ref[...] = jnp.dot(s, w_ref[...].astype(BF16), preferred_element_type=F32) + b_ref[...]


def _mod_call(cvec, w_ada, b_ada):
    tn = 1024
    n = w_ada.shape[1]
    return pl.pallas_call(
        _mod_kernel,
        grid=(n // tn,),
        in_specs=[pl.BlockSpec((MOD_ROWS, D_MODEL), lambda j: (0, 0)),
                  pl.BlockSpec((D_MODEL, tn), lambda j: (0, j)),
                  pl.BlockSpec((1, tn), lambda j: (0, j))],
        out_specs=pl.BlockSpec((MOD_ROWS, tn), lambda j: (0, j)),
        out_shape=jax.ShapeDtypeStruct((MOD_ROWS, n), F32),
        compiler_params=_params("arbitrary"),
        name="mod",
    )(cvec, w_ada, b_ada)


def _rope(x, cos, sin_lo, sin_hi):
    return (x * cos + pltpu.roll(x, 16, axis=1) * sin_hi
            + pltpu.roll(x, LANES - 16, axis=1) * sin_lo)


IN_GROUPS = ((Q_OFF, Q_COLS), (K_OFF, K_COLS), (V_OFF, V_COLS), (HY_OFF, HY_COLS))
G_Q, G_K, G_V, G_HY = range(len(IN_GROUPS))


def _in_proj_kernel(*refs, latent):
    if latent:
        (x_ref, mod_ref, g_ref, w_hbm, cos_ref, slo_ref, shi_ref,
         q_ref, kt_ref, v_ref, u_ref, w_s, w_sem) = refs
    else:
        (x_ref, mod_ref, g_ref, w_hbm,
         q_ref, kt_ref, v_ref, ktf_ref, vf_ref, u_ref, w_s, w_sem) = refs
    tm = x_ref.shape[0]

    def w_copy(g):
        off, width = IN_GROUPS[g]
        cols = slice(off, off + width)
        return pltpu.make_async_copy(w_hbm.at[:, cols], w_s.at[:, cols], w_sem.at[g])

    def body(first_step):
        if first_step:
            for g in range(len(IN_GROUPS)):
                w_copy(g).start()
        x = x_ref[...]
        h = _rms(x, g_ref[...]) * (1.0 + mod_ref[1:2, :]) + mod_ref[0:1, :]
        hb = h.astype(BF16)

        def proj(g):
            if first_step:
                w_copy(g).wait()
            off, width = IN_GROUPS[g]
            return jnp.dot(hb, w_s[:, off:off + width], preferred_element_type=F32)

        q = proj(G_Q) * (HEAD_DIM ** -0.5 * LOG2_E)
        k = proj(G_K)
        if latent:
            cos, slo, shi = cos_ref[...], slo_ref[...], shi_ref[...]
            k_heads = []
            for hd in range(N_HEADS):
                sl = slice(hd * LANES, (hd + 1) * LANES)
                q_ref[:, sl] = _rope(q[:, sl], cos, slo, shi).astype(BF16)
                k_heads.append(_rope(k[:, sl], cos, slo, shi))
            kt_ref[...] = jnp.concatenate(k_heads, axis=1).T.astype(BF16)
        else:
            q_ref[...] = q.astype(BF16)
            seq = kt_ref.shape[1]
            for s in range(tm // seq):
                kt = k[s * seq:(s + 1) * seq, :].T
                rows = slice(s * K_COLS, (s + 1) * K_COLS)
                ktf_ref[rows, :] = kt
                kt_ref[rows, :] = kt.astype(BF16)
        v = proj(G_V)
        v_ref[...] = v.astype(BF16)
        if not latent:
            for hd in range(N_HEADS):
                vf_ref[pl.ds(hd, tm, stride=N_HEADS), :] = v[:, hd * LANES:(hd + 1) * LANES]
        u_ref[...] = proj(G_HY)

    step = pl.program_id(0)
    pl.when(step == 0)(functools.partial(body, True))
    pl.when(step != 0)(functools.partial(body, False))


def _in_proj_call(x, mod, g1, w_in, rope, *, seq, latent, tm):
    t = x.shape[0]
    per_seq = seq // tm if latent else 1

    def mod_idx(i):
        return (1 + i // per_seq if latent else 0, 0, 0)

    row = lambda i: (i, 0)
    in_specs = [pl.BlockSpec((tm, D_MODEL), row),
                pl.BlockSpec((None, 6, D_MODEL), mod_idx),
                _resident((1, D_MODEL)),
                pl.BlockSpec(memory_space=pl.ANY)]
    args = [x, mod, g1, w_in]
    tok = lambda width, dtype: ((t, width), dtype, pl.BlockSpec((tm, width), row))
    if latent:
        in_specs += [pl.BlockSpec((tm, LANES), lambda i: (i % per_seq, 0))] * 3
        args += list(rope)
        kt_spec = pl.BlockSpec((K_COLS, tm), lambda i: (i // per_seq, i % per_seq))
    else:
        kt_spec = pl.BlockSpec((tm // seq * K_COLS, seq), row)
    kt_shape = (t // seq * K_COLS, seq)
    outs = [tok(Q_COLS, BF16), (kt_shape, BF16, kt_spec), tok(V_COLS, BF16)]
    if not latent:
        outs += [(kt_shape, F32, kt_spec),
                 ((t * N_HEADS, LANES), F32, pl.BlockSpec((tm * N_HEADS, LANES), row))]
    outs += [tok(HY_COLS, F32)]
    return pl.pallas_call(
        functools.partial(_in_proj_kernel, latent=latent),
        grid=(t // tm,),
        in_specs=in_specs,
        out_specs=[spec for _, _, spec in outs],
        out_shape=[jax.ShapeDtypeStruct(s, d) for s, d, _ in outs],
        scratch_shapes=[pltpu.VMEM((D_MODEL, GATE_OFF), BF16),
                        pltpu.SemaphoreType.DMA((len(IN_GROUPS),))],
        compiler_params=_params("arbitrary"),
        name="in_proj_lat" if latent else "in_proj_ctx",
    )(*args)


def _attn_kernel(*refs, n_cache, lam_init, tq):
    if n_cache:
        (q_ref, kt_ref, v_ref, ckt_ref, cv_ref, lam_ref, g_ref, o_ref, kt_s, v2_s) = refs
    else:
        (q_ref, kt_ref, v_ref, lam_ref, g_ref, o_ref, v2_s) = refs
    n_heads, n_keys, _ = v2_s.shape
    head0 = pl.program_id(1) * n_heads
    lv = lam_ref[...]
    lam = (jnp.exp(jnp.sum(lv[0:1] * lv[1:2], axis=-1, keepdims=True))
           - jnp.exp(jnp.sum(lv[2:3] * lv[3:4], axis=-1, keepdims=True)) + lam_init)
    gain = g_ref[...] * (1.0 - lam_init)
    first_map = lax.broadcasted_iota(jnp.int32, (1, LANES), 1) < HEAD_DIM
    zero = jnp.zeros((tq, LANES), BF16)

    def operands(hd):
        cols = slice(hd * LANES, (hd + 1) * LANES)
        v2_s[hd, :, LANES:] = jnp.ones((n_keys, LANES), BF16)
        if n_cache:
            kt_s[cols, 0:n_cache] = ckt_ref[cols, :].astype(BF16)
            kt_s[cols, n_cache:] = kt_ref[cols, :]
            cache_rows = pl.ds(head0 + hd, n_cache, stride=N_HEADS)
            v2_s[hd, 0:n_cache, 0:LANES] = cv_ref[cache_rows, :].astype(BF16)
            v2_s[hd, n_cache:, 0:LANES] = v_ref[:, cols]
            return kt_s[cols, :], v2_s[hd]
        v2_s[hd, :, 0:LANES] = v_ref[:, cols]
        return kt_ref[cols, :], v2_s[hd]

    def scores(hd, i, kt):
        q = q_ref[i * tq:(i + 1) * tq, hd * LANES:(hd + 1) * LANES]
        q2 = jnp.concatenate([jnp.where(first_map, q, zero), jnp.where(first_map, zero, q)], axis=0)
        s = jnp.dot(q2, kt, preferred_element_type=F32)
        return s, jnp.max(s, axis=-1, keepdims=True)

    def finish(hd, i, v2, s, m):
        p = jnp.exp2(s - m).astype(BF16)
        pv = jnp.dot(p, v2, preferred_element_type=F32)
        pv = pv[:, :LANES] / pv[:, LANES:]
        o = pv[:tq] - lam * pv[tq:]
        o_ref[i * tq:(i + 1) * tq, hd * LANES:(hd + 1) * LANES] = (_rms(o, 1.0) * gain).astype(BF16)

    n_tiles = q_ref.shape[0] // tq
    pending = None
    for hd in range(n_heads):
        kt, v2 = operands(hd)
        for i in range(n_tiles):
            cur = (hd, i, v2) + scores(hd, i, kt)
            if pending is not None:
                finish(*pending)
            pending = cur
    finish(*pending)


def _attn_call(q, kt, v, cache, lam_vecs, subln_g, *, batch, seq, lam_init, tq, heads):
    const = lambda b, h: (0, 0)
    width = heads * LANES
    n_groups = N_HEADS // heads
    in_specs = [pl.BlockSpec((seq, width), lambda b, h: (b, h)),
                pl.BlockSpec((width, seq), lambda b, h: (b * n_groups + h, 0)),
                pl.BlockSpec((seq, width), lambda b, h: (b, h))]
    args = [q, kt, v]
    n_cache = 0
    if cache is not None:
        ckt, cv = cache
        n_cache = ckt.shape[1]
        in_specs += [pl.BlockSpec((width, n_cache), lambda b, h: (b * n_groups + h, 0)),
                     pl.BlockSpec((n_cache * N_HEADS, LANES), lambda b, h: (b, 0))]
        args += [ckt, cv]
    n_keys = n_cache + seq
    scratch = [pltpu.VMEM((heads, n_keys, 2 * LANES), BF16)]
    if n_cache:
        scratch = [pltpu.VMEM((width, n_keys), BF16)] + scratch
    in_specs += [pl.BlockSpec(lam_vecs.shape, const), pl.BlockSpec((1, LANES), const)]
    args += [lam_vecs, subln_g]
    return pl.pallas_call(
        functools.partial(_attn_kernel, n_cache=n_cache, lam_init=lam_init, tq=tq),
        grid=(batch, n_groups),
        in_specs=in_specs,
        out_specs=pl.BlockSpec((seq, width), lambda b, h: (b, h)),
        out_shape=jax.ShapeDtypeStruct((batch * seq, N_HEADS * V_DIM), BF16),
        scratch_shapes=scratch,
        compiler_params=_params("arbitrary", "arbitrary"),
        name="attn_lat" if cache is not None else "attn_ctx",
    )(*args)


def _dft_tables(seq):
    n = 2 * seq
    f = np.arange(seq, dtype=np.int64)[:, None]
    t = np.arange(seq, dtype=np.int64)[None, :]
    ang = 2.0 * np.pi * ((f * t) % n).astype(np.float64) / n
    alt = np.where(np.arange(seq) % 2 == 0, 1.0, -1.0)
    cos, msin = np.cos(ang), -np.sin(ang)
    msin[0, :] = alt
    fwd = np.concatenate([cos, msin], axis=0)
    wgt = np.where(f == 0, 1.0, 2.0) / n
    inv_c = (cos * wgt).T
    inv_s = (msin * wgt).T
    inv = np.concatenate([inv_c, inv_s], axis=1)
    return fwd.astype(np.float32), inv.astype(np.float32)


def _filt_kernel(emb_ref, w1_ref, b1_ref, w2_ref, b2_ref, fr_ref, w3f_ref, w3b_ref,
                 t_ref, dl_ref, dft_ref, o_ref, h_s):
    seq = emb_ref.shape[0]
    hp = functools.partial(jnp.dot, preferred_element_type=F32, precision=lax.Precision.HIGHEST)

    @pl.when(pl.program_id(0) == 0)
    def _():
        fr = fr_ref[...]
        h = jnp.sin(fr * (hp(emb_ref[...], w1_ref[...]) + b1_ref[...]))
        h_s[...] = jnp.sin(fr * (hp(h, w2_ref[...]) + b2_ref[...]))

    h = h_s[...].astype(BF16)
    decay = jnp.exp(-t_ref[...] * jnp.abs(dl_ref[...]))
    fwd = jnp.dot(h, w3f_ref[...].astype(BF16), preferred_element_type=F32) * decay
    bwd = jnp.dot(h, w3b_ref[...].astype(BF16), preferred_element_type=F32) * decay
    hsum, hdif = fwd + bwd, fwd - bwd
    re = jnp.dot(dft_ref[0:seq, :], hsum.astype(BF16), preferred_element_type=F32)
    im = jnp.dot(dft_ref[seq:, :], hdif.astype(BF16), preferred_element_type=F32)
    row = lax.broadcasted_iota(jnp.int32, (seq, 1), 0)
    alt = jnp.where(row % 2 == 0, 1.0, -1.0)
    nyq = jnp.sum(hsum * alt, axis=0, keepdims=True)
    o_ref[0] = re
    o_ref[1] = jnp.where(row == 0, 0.0, im)
    o_ref[2] = jnp.where(row == 0, nyq, re)


def _filt_call(emb, w1, b1, w2, b2, fr, w3, tcol, deltas, fwd, *, seq, ct):
    n_out = HYENA_ORDER * D_HYENA
    nct = n_out // ct
    per_order = D_HYENA // ct
    small = lambda shape: pl.BlockSpec(shape, lambda j: (0, 0))
    return pl.pallas_call(
        _filt_kernel,
        grid=(nct,),
        in_specs=[small((seq, FILT_PAD)), small((FILT_PAD, FILT_PAD)), small((1, FILT_PAD)),
                  small((FILT_PAD, FILT_PAD)), small((1, FILT_PAD)), small((1, FILT_PAD)),
                  pl.BlockSpec((FILT_PAD, ct), lambda j: (0, j)),
                  pl.BlockSpec((FILT_PAD, ct), lambda j: (0, nct + j)),
                  small((seq, 1)),
                  pl.BlockSpec((1, ct), lambda j: (0, j % per_order)),
                  _resident((2 * seq, seq))],
        out_specs=pl.BlockSpec((3, seq, ct), lambda j: (0, 0, j)),
        out_shape=jax.ShapeDtypeStruct((3, seq, n_out), F32),
        scratch_shapes=[pltpu.VMEM((seq, FILT_PAD), F32)],
        compiler_params=_params("arbitrary"),
        name=f"filt_{seq}",
    )(emb, w1, b1, w2, b2, fr, w3, w3, tcol, deltas, fwd)


def _hyena_kernel(uv_ref, u1_ref, u2_ref, wv_ref, w1_ref, w2_ref, bv_ref, b1_ref, b2_ref,
                  h0_ref, h1_ref, hb_ref, fwd_ref, inv_ref, o_ref):
    seq = h0_ref.shape[1]
    chains = [slice(c * seq, (c + 1) * seq) for c in range(uv_ref.shape[0] // seq)]
    row = lax.broadcasted_iota(jnp.int32, (seq, 1), 0)
    first, last = row == 0, row == seq - 1

    def short_conv(u_ref, w_ref, b_ref):
        outs = []
        for rows in chains:
            u = u_ref[rows, :]
            prev = jnp.where(first, 0.0, pltpu.roll(u, 1, axis=0))
            nxt = jnp.where(last, 0.0, pltpu.roll(u, seq - 1, axis=0))
            outs.append(prev * w_ref[0:1, :] + u * w_ref[1:2, :] + nxt * w_ref[2:3, :] + b_ref[...])
        return outs

    def long_conv(us, h_ref, bias):
        specs = [jnp.dot(fwd_ref[...], u.astype(BF16), preferred_element_type=F32) for u in us]
        ys = []
        for spec in specs:
            ure, uim = spec[:seq], spec[seq:]
            yre = ure * h_ref[0] - uim * h_ref[1]
            yim = ure * h_ref[1] + uim * h_ref[2]
            ys.append(jnp.concatenate([yre, yim], axis=0).astype(BF16))
        return [jnp.dot(inv_ref[...], y, preferred_element_type=F32) + u * bias
                for y, u in zip(ys, us)]

    v = short_conv(uv_ref, wv_ref, bv_ref)
    x1 = short_conv(u1_ref, w1_ref, b1_ref)
    z = [a * b for a, b in zip(x1, long_conv(v, h0_ref, hb_ref[0:1, :]))]
    x2 = short_conv(u2_ref, w2_ref, b2_ref)
    for rows, a, b in zip(chains, x2, long_conv(z, h1_ref, hb_ref[1:2, :])):
        o_ref[rows, :] = (a * b).astype(BF16)


def _hyena_call(u, conv_w, conv_b, hf, hy_bias, fwd, inv, *, batch, seq, ct, n_chain):
    nct = D_HYENA // ct
    rows = n_chain * seq
    col = lambda part: (lambda c, b: (b, part * nct + c))
    wcol = lambda part: (lambda c, b: (0, part * nct + c))
    in_specs = ([pl.BlockSpec((rows, ct), col(p)) for p in range(3)]
                + [pl.BlockSpec((3, ct), wcol(p)) for p in range(3)]
                + [pl.BlockSpec((1, ct), wcol(p)) for p in range(3)]
                + [pl.BlockSpec((3, seq, ct), lambda c, b, o=o: (0, 0, o * nct + c))
                   for o in range(HYENA_ORDER)]
                + [pl.BlockSpec((HYENA_ORDER, ct), lambda c, b: (0, c)),
                   _resident((2 * seq, seq)), _resident((seq, 2 * seq))])
    return pl.pallas_call(
        _hyena_kernel,
        grid=(nct, batch // n_chain),
        in_specs=in_specs,
        out_specs=pl.BlockSpec((rows, ct), lambda c, b: (b, c)),
        out_shape=jax.ShapeDtypeStruct((batch * seq, D_HYENA), BF16),
        compiler_params=_params("arbitrary", "arbitrary"),
        name=f"hyena_{seq}",
    )(u, u, u, conv_w, conv_w, conv_w, conv_b, conv_b, conv_b, hf, hf, hy_bias, fwd, inv)


FF_CHUNK = D_MODEL
N_FF_CHUNKS = D_FF // FF_CHUNK


def _post_kernel(x_ref, oa_ref, oh_ref, mod_ref, g1_ref, g2_ref, gf_ref,
                 win_hbm, wa_hbm, wh_hbm, wo_hbm, wu_hbm, wd_hbm, y_ref,
                 wg_s, wa_s, wh_s, wo_s, wu_s, wd_s, w_sem):
    ff = [slice(c * FF_CHUNK, (c + 1) * FF_CHUNK) for c in range(N_FF_CHUNKS)]
    copies = ([(win_hbm.at[:, GATE_OFF:], wg_s), (wa_hbm, wa_s), (wh_hbm, wh_s), (wo_hbm, wo_s)]
              + [piece for sl in ff for piece in ((wu_hbm.at[:, sl], wu_s.at[:, sl]),
                                                  (wd_hbm.at[sl, :], wd_s.at[sl, :]))])

    def w_copy(n):
        src, dst = copies[n]
        return pltpu.make_async_copy(src, dst, w_sem.at[n])

    def body(first_step):
        if first_step:
            for n in range(len(copies)):
                w_copy(n).start()

        def ready(n):
            if first_step:
                w_copy(n).wait()

        d = functools.partial(jnp.dot, preferred_element_type=F32)
        x = x_ref[...]
        h1 = (_rms(x, g1_ref[...]) * (1.0 + mod_ref[1:2, :]) + mod_ref[0:1, :]).astype(BF16)
        ready(0)
        gates = jax.nn.sigmoid(d(h1, wg_s[...]))
        ready(1)
        attn = d(oa_ref[...], wa_s[...])
        ready(2)
        merged = gates[:, :D_MODEL] * attn + gates[:, D_MODEL:] * d(oh_ref[...], wh_s[...])
        ready(3)
        x = x + mod_ref[2:3, :] * d(merged.astype(BF16), wo_s[...])
        h2 = (_rms(x, g2_ref[...]) * (1.0 + mod_ref[4:5, :]) + mod_ref[3:4, :]).astype(BF16)
        acc = jnp.zeros_like(x)
        for c, sl in enumerate(ff):
            ready(4 + 2 * c)
            up = jnp.maximum(d(h2, wu_s[:, sl]), 0.0)
            ready(5 + 2 * c)
            acc = acc + d((up * up).astype(BF16), wd_s[sl, :])
        x = x + mod_ref[5:6, :] * acc
        y_ref[...] = _rms(x, gf_ref[...])

    step = pl.program_id(0)
    pl.when(step == 0)(functools.partial(body, True))
    pl.when(step != 0)(functools.partial(body, False))


def _post_call(x, o_attn, o_hy, mod, g1, g2, gf, w_in, wa, wh, wo, wu, wd, *, seq, latent, tm):
    t = x.shape[0]
    per_seq = seq // tm if latent else 1
    row = lambda i: (i, 0)

    def mod_idx(i):
        return (1 + i // per_seq if latent else 0, 0, 0)

    return pl.pallas_call(
        _post_kernel,
        grid=(t // tm,),
        in_specs=[pl.BlockSpec((tm, D_MODEL), row),
                  pl.BlockSpec((tm, N_HEADS * V_DIM), row),
                  pl.BlockSpec((tm, D_HYENA), row),
                  pl.BlockSpec((None, 6, D_MODEL), mod_idx),
                  _resident((1, D_MODEL)), _resident((1, D_MODEL)), _resident((1, D_MODEL))]
                 + [pl.BlockSpec(memory_space=pl.ANY)] * 6,
        out_specs=pl.BlockSpec((tm, D_MODEL), row),
        out_shape=jax.ShapeDtypeStruct((t, D_MODEL), F32),
        scratch_shapes=[pltpu.VMEM((D_MODEL, GATE_COLS), BF16)]
                       + [pltpu.VMEM(w.shape, BF16) for w in (wa, wh, wo, wu, wd)]
                       + [pltpu.SemaphoreType.DMA((4 + 2 * N_FF_CHUNKS,))],
        compiler_params=_params("arbitrary"),
        name="post_lat" if latent else "post_ctx",
    )(x, o_attn, o_hy, mod, g1, g2, gf, w_in, wa, wh, wo, wu, wd)


def _rope_tables(seq):
    half = HEAD_DIM // 2
    n = half // 2
    inv = ROPE_THETA ** (-np.arange(n, dtype=np.float64) / n)
    pos = np.arange(seq)
    ang_row = (pos // GRID_W).astype(np.float64)[:, None] * inv[None, :]
    ang_col = (pos % GRID_W).astype(np.float64)[:, None] * inv[None, :]
    zeros = np.zeros_like(ang_row)

    def per_map(a_row, a_col, lo, hi):
        return np.concatenate([lo(a_row), hi(a_row), lo(a_col), hi(a_col)], axis=-1)

    cos = per_map(ang_row, ang_col, np.cos, np.cos)
    sin_lo = per_map(ang_row, ang_col, lambda a: -np.sin(a), lambda a: zeros)
    sin_hi = per_map(ang_row, ang_col, lambda a: zeros, np.sin)
    return tuple(jnp.asarray(np.concatenate([tab, tab], axis=-1).astype(np.float32))
                 for tab in (cos, sin_lo, sin_hi))


def _filter_embedding(seq):
    bands = (FILTER_EMB - 1) // 2
    t = np.linspace(0.0, 1.0, seq)[:, None]
    wpos = 2.0 * np.pi * np.arange(seq, dtype=np.float64)[:, None] / seq
    f = np.linspace(1e-4, bands - 1, bands)[None, :]
    emb = np.concatenate([t, np.cos(f * wpos), -np.sin(f * wpos)], axis=-1)
    emb = np.pad(emb, ((0, 0), (0, FILT_PAD - FILTER_EMB)))
    return jnp.asarray(emb.astype(np.float32)), jnp.asarray(t.astype(np.float32))


def _tiles(seq):
    short = seq <= 256
    return dict(tm=512, tq=128, attn_heads=N_HEADS if short else 4, ct=MXU_WIDTH,
                hyena_seqs=4 if short else 2)


def _pad_to(x, rows, cols):
    return jnp.pad(x, ((0, rows - x.shape[0]), (0, cols - x.shape[1])))


def kernel(x_prompt, x_sample, cache_k, cache_v, c, c_ctx, w_ada, b_ada, norm1_g, norm2_g, w_in,
           lam_q1, lam_k1, lam_q2, lam_k2, attn_subln_g, conv_w, conv_b, filt_w1, filt_b1,
           filt_w2, filt_b2, filt_w3, filt_freq, hy_bias, w_br_attn, w_br_hy, w_out, w_up,
           w_down, final_g):
    depth = w_in.shape[0]
    assert depth == 1, "single trunk layer"
    layer = 0
    lam_init = 0.8 - 0.6 * math.exp(-0.3 * layer)
    n_ctx, ctx_len, _ = x_prompt.shape
    n_lat, lat_len, _ = x_sample.shape
    past = cache_k.shape[2]

    cvec = jnp.concatenate([c_ctx[None, :], c], axis=0)
    cvec = jnp.pad(cvec, ((0, MOD_ROWS - cvec.shape[0]), (0, 0)))
    mod = _mod_call(cvec, w_ada[layer], b_ada[layer][None, :]).reshape(MOD_ROWS, 6, D_MODEL)

    w_in_b = w_in[layer].astype(BF16)
    wa, wh, wo = (w[layer].astype(BF16) for w in (w_br_attn, w_br_hy, w_out))
    wu, wd = w_up[layer].astype(BF16), w_down[layer].astype(BF16)
    g1, g2, gf = norm1_g[layer][None, :], norm2_g[layer][None, :], final_g[None, :]
    lam_vecs = jnp.stack([lam_q1[layer], lam_k1[layer], lam_q2[layer], lam_k2[layer]])
    subln_g = attn_subln_g[layer][None, :]

    w1 = _pad_to(filt_w1[layer], FILT_PAD, FILT_PAD)
    w2 = _pad_to(filt_w2[layer], FILT_PAD, FILT_PAD)
    w3 = _pad_to(filt_w3[layer], FILT_PAD, filt_w3.shape[2])
    b1 = _pad_to(filt_b1[layer][None, :], 1, FILT_PAD)
    b2 = _pad_to(filt_b2[layer][None, :], 1, FILT_PAD)
    fr = _pad_to(filt_freq[layer][None, :], 1, FILT_PAD)
    deltas = jnp.asarray(np.linspace(math.log(DECAY_TARGET) / FAST_DECAY_PCT,
                                     math.log(DECAY_TARGET) / SLOW_DECAY_PCT,
                                     D_HYENA)[None, :].astype(np.float32))

    def branch(x3, batch, seq, latent):
        x = x3.reshape(batch * seq, D_MODEL)
        tiles = _tiles(seq)
        tm = tiles["tm"]
        rope = _rope_tables(seq) if latent else None
        outs = _in_proj_call(x, mod, g1, w_in_b, rope, seq=seq, latent=latent, tm=tm)
        if latent:
            q, kt, v, u = outs
            ckt = jnp.transpose(cache_k[:, layer], (0, 2, 3, 4, 1)).reshape(batch * K_COLS, past)
            cache = (ckt, cache_v[:, layer].reshape(batch * past * N_HEADS, V_DIM))
            kf = vf = None
        else:
            q, kt, v, kf, vf, u = outs
            cache = None
        o_attn = _attn_call(q, kt, v, cache, lam_vecs, subln_g, batch=batch, seq=seq,
                            lam_init=lam_init, tq=tiles["tq"], heads=tiles["attn_heads"])

        fwd_np, inv_np = _dft_tables(seq)
        fwd_b = jnp.asarray(fwd_np).astype(BF16)
        inv_b = jnp.asarray(inv_np).astype(BF16)
        emb, tcol = _filter_embedding(seq)
        hf = _filt_call(emb, w1, b1, w2, b2, fr, w3, tcol, deltas, fwd_b,
                        seq=seq, ct=tiles["ct"])
        o_hy = _hyena_call(u, conv_w[layer], conv_b[layer][None, :], hf, hy_bias[layer],
                           fwd_b, inv_b, batch=batch, seq=seq, ct=tiles["ct"],
                           n_chain=tiles["hyena_seqs"])

        y = _post_call(x, o_attn, o_hy, mod, g1, g2, gf, w_in_b, wa, wh, wo, wu, wd,
                       seq=seq, latent=latent, tm=tm)
        return y.reshape(batch, seq, D_MODEL), kf, vf

    y_prompt, kf, vf = branch(x_prompt, n_ctx, ctx_len, False)
    y_sample, _, _ = branch(x_sample, n_lat, lat_len, True)
    new_cache_k = jnp.transpose(kf.reshape(n_ctx, depth, N_HEADS, 2, HEAD_DIM, ctx_len),
                                (0, 1, 5, 2, 3, 4))
    new_cache_v = vf.reshape(n_ctx, depth, ctx_len, N_HEADS, V_DIM)
    return (y_prompt, y_sample, new_cache_k, new_cache_v)
```

```python
import functools
import math

import numpy as np
import jax
import jax.numpy as jnp
from jax import lax
from jax.experimental import pallas as pl
from jax.experimental.pallas import tpu as pltpu

D_MODEL = 1024
GRID_W = 64
N_HEADS = 8
HEAD_DIM = 64
V_DIM = 2 * HEAD_DIM
D_HYENA = D_MODEL // 2
HYENA_ORDER = 2
FILTER_EMB = 33
FILTER_HIDDEN = 64
D_FF = 4 * D_MODEL
ROPE_THETA = 10000.0
EPS = 1e-6
LOG2_E = math.log2(math.e)
DECAY_TARGET = 1e-2
FAST_DECAY_PCT = 0.3
SLOW_DECAY_PCT = 1.5

Q_COLS = N_HEADS * 2 * HEAD_DIM
K_COLS = N_HEADS * 2 * HEAD_DIM
V_COLS = N_HEADS * V_DIM
HY_COLS = 3 * D_HYENA
GATE_COLS = 2 * D_MODEL
Q_OFF = 0
K_OFF = Q_OFF + Q_COLS
V_OFF = K_OFF + K_COLS
HY_OFF = V_OFF + V_COLS
GATE_OFF = HY_OFF + HY_COLS
IN_COLS = GATE_OFF + GATE_COLS

LANES = 128
MXU_WIDTH = 256
MOD_ROWS = 16
FILT_PAD = 128
VMEM_LIMIT = 56 * 1024 * 1024

BF16 = jnp.bfloat16
F32 = jnp.float32


def _resident(shape):
    nd = len(shape)
    return pl.BlockSpec(shape, lambda *_: (0,) * nd, pipeline_mode=pl.Buffered(1))


def _params(*sem):
    return pltpu.CompilerParams(dimension_semantics=sem, vmem_limit_bytes=VMEM_LIMIT)


def _rms(x, g):
    return x * lax.rsqrt(jnp.mean(x * x, axis=-1, keepdims=True) + EPS) * g


def _mod_kernel(c_ref, w_ref, b_ref, o_ref):
    c = c_ref[...]
    s = (c * jax.nn.sigmoid(c)).astype(BF16)
    o_ref[...] = jnp.dot(s, w_ref[...].astype(BF16), preferred_element_type=F32) + b_ref[...]


def _mod_call(cvec, w_ada, b_ada):
    tn = 1024
    n = w_ada.shape[1]
    return pl.pallas_call(
        _mod_kernel,
        grid=(n // tn,),
        in_specs=[pl.BlockSpec((MOD_ROWS, D_MODEL), lambda j: (0, 0)),
                  pl.BlockSpec((D_MODEL, tn), lambda j: (0, j)),
                  pl.BlockSpec((1, tn), lambda j: (0, j))],
        out_specs=pl.BlockSpec((MOD_ROWS, tn), lambda j: (0, j)),
        out_shape=jax.ShapeDtypeStruct((MOD_ROWS, n), F32),
        compiler_params=_params("arbitrary"),
        name="mod",
    )(cvec, w_ada, b_ada)


def _rope(x, cos, sin_lo, sin_hi):
    return (x * cos + pltpu.roll(x, 16, axis=1) * sin_hi
            + pltpu.roll(x, LANES - 16, axis=1) * sin_lo)


IN_GROUPS = ((Q_OFF, Q_COLS), (K_OFF, K_COLS), (V_OFF, V_COLS), (HY_OFF, HY_COLS))
G_Q, G_K, G_V, G_HY = range(len(IN_GROUPS))


def _in_proj_kernel(*refs, latent):
    if latent:
        (x_ref, mod_ref, g_ref, w_hbm, cos_ref, slo_ref, shi_ref,
         q_ref, kt_ref, v_ref, u_ref, w_s, w_sem) = refs
    else:
        (x_ref, mod_ref, g_ref, w_hbm,
         q_ref, kt_ref, v_ref, ktf_ref, vf_ref, u_ref, w_s, w_sem) = refs
    tm = x_ref.shape[0]

    def w_copy(g):
        off, width = IN_GROUPS[g]
        cols = slice(off, off + width)
        return pltpu.make_async_copy(w_hbm.at[:, cols], w_s.at[:, cols], w_sem.at[g])

    def body(first_step):
        if first_step:
            for g in range(len(IN_GROUPS)):
                w_copy(g).start()
        x = x_ref[...]
        h = _rms(x, g_ref[...]) * (1.0 + mod_ref[1:2, :]) + mod_ref[0:1, :]
        hb = h.astype(BF16)

        def proj(g):
            if first_step:
                w_copy(g).wait()
            off, width = IN_GROUPS[g]
            return jnp.dot(hb, w_s[:, off:off + width], preferred_element_type=F32)

        q = proj(G_Q) * (HEAD_DIM ** -0.5 * LOG2_E)
        k = proj(G_K)
        if latent:
            cos, slo, shi = cos_ref[...], slo_ref[...], shi_ref[...]
            k_heads = []
            for hd in range(N_HEADS):
                sl = slice(hd * LANES, (hd + 1) * LANES)
                q_ref[:, sl] = _rope(q[:, sl], cos, slo, shi).astype(BF16)
                k_heads.append(_rope(k[:, sl], cos, slo, shi))
            kt_ref[...] = jnp.concatenate(k_heads, axis=1).T.astype(BF16)
        else:
            q_ref[...] = q.astype(BF16)
            seq = kt_ref.shape[1]
            for s in range(tm // seq):
                kt = k[s * seq:(s + 1) * seq, :].T
                rows = slice(s * K_COLS, (s + 1) * K_COLS)
                ktf_ref[rows, :] = kt
                kt_ref[rows, :] = kt.astype(BF16)
        v = proj(G_V)
        v_ref[...] = v.astype(BF16)
        if not latent:
            for hd in range(N_HEADS):
                vf_ref[pl.ds(hd, tm, stride=N_HEADS), :] = v[:, hd * LANES:(hd + 1) * LANES]
        u_ref[...] = proj(G_HY)

    step = pl.program_id(0)
    pl.when(step == 0)(functools.partial(body, True))
    pl.when(step != 0)(functools.partial(body, False))


def _in_proj_call(x, mod, g1, w_in, rope, *, seq, latent, tm):
    t = x.shape[0]
    per_seq = seq // tm if latent else 1

    def mod_idx(i):
        return (1 + i // per_seq if latent else 0, 0, 0)

    row = lambda i: (i, 0)
    in_specs = [pl.BlockSpec((tm, D_MODEL), row),
                pl.BlockSpec((None, 6, D_MODEL), mod_idx),
                _resident((1, D_MODEL)),
                pl.BlockSpec(memory_space=pl.ANY)]
    args = [x, mod, g1, w_in]
    tok = lambda width, dtype: ((t, width), dtype, pl.BlockSpec((tm, width), row))
    if latent:
        in_specs += [pl.BlockSpec((tm, LANES), lambda i: (i % per_seq, 0))] * 3
        args += list(rope)
        kt_spec = pl.BlockSpec((K_COLS, tm), lambda i: (i // per_seq, i % per_seq))
    else:
        kt_spec = pl.BlockSpec((tm // seq * K_COLS, seq), row)
    kt_shape = (t // seq * K_COLS, seq)
    outs = [tok(Q_COLS, BF16), (kt_shape, BF16, kt_spec), tok(V_COLS, BF16)]
    if not latent:
        outs += [(kt_shape, F32, kt_spec),
                 ((t * N_HEADS, LANES), F32, pl.BlockSpec((tm * N_HEADS, LANES), row))]
    outs += [tok(HY_COLS, F32)]
    return pl.pallas_call(
        functools.partial(_in_proj_kernel, latent=latent),
        grid=(t // tm,),
        in_specs=in_specs,
        out_specs=[spec for _, _, spec in outs],
        out_shape=[jax.ShapeDtypeStruct(s, d) for s, d, _ in outs],
        scratch_shapes=[pltpu.VMEM((D_MODEL, GATE_OFF), BF16),
                        pltpu.SemaphoreType.DMA((len(IN_GROUPS),))],
        compiler_params=_params("arbitrary"),
        name="in_proj_lat" if latent else "in_proj_ctx",
    )(*args)


def _attn_kernel(*refs, n_cache, lam_init, tq):
    if n_cache:
        (q_ref, kt_ref, v_ref, ckt_ref, cv_ref, lam_ref, g_ref, o_ref, kt_s, v2_s) = refs
    else:
        (q_ref, kt_ref, v_ref, lam_ref, g_ref, o_ref, v2_s) = refs
    n_heads, n_keys, _ = v2_s.shape
    head0 = pl.program_id(1) * n_heads
    lv = lam_ref[...]
    lam = (jnp.exp(jnp.sum(lv[0:1] * lv[1:2], axis=-1, keepdims=True))
           - jnp.exp(jnp.sum(lv[2:3] * lv[3:4], axis=-1, keepdims=True)) + lam_init)
    gain = g_ref[...] * (1.0 - lam_init)
    first_map = lax.broadcasted_iota(jnp.int32, (1, LANES), 1) < HEAD_DIM
    zero = jnp.zeros((tq, LANES), BF16)

    def operands(hd):
        cols = slice(hd * LANES, (hd + 1) * LANES)
        v2_s[hd, :, LANES:] = jnp.ones((n_keys, LANES), BF16)
        if n_cache:
            kt_s[cols, 0:n_cache] = ckt_ref[cols, :].astype(BF16)
            kt_s[cols, n_cache:] = kt_ref[cols, :]
            cache_rows = pl.ds(head0 + hd, n_cache, stride=N_HEADS)
            v2_s[hd, 0:n_cache, 0:LANES] = cv_ref[cache_rows, :].astype(BF16)
            v2_s[hd, n_cache:, 0:LANES] = v_ref[:, cols]
            return kt_s[cols, :], v2_s[hd]
        v2_s[hd, :, 0:LANES] = v_ref[:, cols]
        return kt_ref[cols, :], v2_s[hd]

    def scores(hd, i, kt):
        q = q_ref[i * tq:(i + 1) * tq, hd * LANES:(hd + 1) * LANES]
        q2 = jnp.concatenate([jnp.where(first_map, q, zero), jnp.where(first_map, zero, q)], axis=0)
        s = jnp.dot(q2, kt, preferred_element_type=F32)
        return s, jnp.max(s, axis=-1, keepdims=True)

    def finish(hd, i, v2, s, m):
        p = jnp.exp2(s - m).astype(BF16)
        pv = jnp.dot(p, v2, preferred_element_type=F32)
        pv = pv[:, :LANES] / pv[:, LANES:]
        o = pv[:tq] - lam * pv[tq:]
        o_ref[i * tq:(i + 1) * tq, hd * LANES:(hd + 1) * LANES] = (_rms(o, 1.0) * gain).astype(BF16)

    n_tiles = q_ref.shape[0] // tq
    pending = None
    for hd in range(n_heads):
        kt, v2 = operands(hd)
        for i in range(n_tiles):
            cur = (hd, i, v2) + scores(hd, i, kt)
            if pending is not None:
                finish(*pending)
            pending = cur
    finish(*pending)


def _attn_call(q, kt, v, cache, lam_vecs, subln_g, *, batch, seq, lam_init, tq, heads):
    const = lambda b, h: (0, 0)
    width = heads * LANES
    n_groups = N_HEADS // heads
    in_specs = [pl.BlockSpec((seq, width), lambda b, h: (b, h)),
                pl.BlockSpec((width, seq), lambda b, h: (b * n_groups + h, 0)),
                pl.BlockSpec((seq, width), lambda b, h: (b, h))]
    args = [q, kt, v]
    n_cache = 0
    if cache is not None:
        ckt, cv = cache
        n_cache = ckt.shape[1]
        in_specs += [pl.BlockSpec((width, n_cache), lambda b, h: (b * n_groups + h, 0)),
                     pl.BlockSpec((n_cache * N_HEADS, LANES), lambda b, h: (b, 0))]
        args += [ckt, cv]
    n_keys = n_cache + seq
    scratch = [pltpu.VMEM((heads, n_keys, 2 * LANES), BF16)]
    if n_cache:
        scratch = [pltpu.VMEM((width, n_keys), BF16)] + scratch
    in_specs += [pl.BlockSpec(lam_vecs.shape, const), pl.BlockSpec((1, LANES), const)]
    args += [lam_vecs, subln_g]
    return pl.pallas_call(
        functools.partial(_attn_kernel, n_cache=n_cache, lam_init=lam_init, tq=tq),
        grid=(batch, n_groups),
        in_specs=in_specs,
        out_specs=pl.BlockSpec((seq, width), lambda b, h: (b, h)),
        out_shape=jax.ShapeDtypeStruct((batch * seq, N_HEADS * V_DIM), BF16),
        scratch_shapes=scratch,
        compiler_params=_params("arbitrary", "arbitrary"),
        name="attn_lat" if cache is not None else "attn_ctx",
    )(*args)


def _dft_tables(seq):
    n = 2 * seq
    f = np.arange(seq, dtype=np.int64)[:, None]
    t = np.arange(seq, dtype=np.int64)[None, :]
    ang = 2.0 * np.pi * ((f * t) % n).astype(np.float64) / n
    alt = np.where(np.arange(seq) % 2 == 0, 1.0, -1.0)
    cos, msin = np.cos(ang), -np.sin(ang)
    msin[0, :] = alt
    fwd = np.concatenate([cos, msin], axis=0)
    wgt = np.where(f == 0, 1.0, 2.0) / n
    inv_c = (cos * wgt).T
    inv_s = (msin * wgt).T
    inv = np.concatenate([inv_c, inv_s], axis=1)
    return fwd.astype(np.float32), inv.astype(np.float32)


def _filter_hidden(emb_ref, w1_ref, b1_ref, w2_ref, b2_ref, fr_ref):
    hp = functools.partial(jnp.dot, preferred_element_type=F32, precision=lax.Precision.HIGHEST)
    fr = fr_ref[...]
    h = jnp.sin(fr * (hp(emb_ref[...], w1_ref[...]) + b1_ref[...]))
    return jnp.sin(fr * (hp(h, w2_ref[...]) + b2_ref[...]))


def _filter_planes(h, w3f_ref, w3b_ref, decay, dft_ref, out_ref):
    seq = h.shape[0]
    fwd = jnp.dot(h, w3f_ref[...].astype(BF16), preferred_element_type=F32) * decay
    bwd = jnp.dot(h, w3b_ref[...].astype(BF16), preferred_element_type=F32) * decay
    hsum, hdif = fwd + bwd, fwd - bwd
    re = jnp.dot(dft_ref[0:seq, :], hsum.astype(BF16), preferred_element_type=F32)
    im = jnp.dot(dft_ref[seq:, :], hdif.astype(BF16), preferred_element_type=F32)
    row = lax.broadcasted_iota(jnp.int32, (seq, 1), 0)
    alt = jnp.where(row % 2 == 0, 1.0, -1.0)
    nyq = jnp.sum(hsum * alt, axis=0, keepdims=True)
    out_ref[0] = re
    out_ref[1] = jnp.where(row == 0, 0.0, im)
    out_ref[2] = jnp.where(row == 0, nyq, re)


def _hyena_kernel(uv_ref, u1_ref, u2_ref, wv_ref, w1_ref, w2_ref, bv_ref, b1_ref, b2_ref, hb_ref,
                  emb_ref, fw1_ref, fb1_ref, fw2_ref, fb2_ref, fr_ref,
                  w3f0_ref, w3b0_ref, w3f1_ref, w3b1_ref, t_ref, dl_ref, fwd_hbm, inv_hbm,
                  o_ref, fwd_s, inv_s, hid_s, hf_s, dft_sem):
    seq = emb_ref.shape[0]
    new_tile = pl.program_id(1) == 0
    first_step = jnp.logical_and(pl.program_id(0) == 0, new_tile)
    fwd_copy = pltpu.make_async_copy(fwd_hbm, fwd_s, dft_sem.at[0])
    inv_copy = pltpu.make_async_copy(inv_hbm, inv_s, dft_sem.at[1])

    @pl.when(first_step)
    def _():
        fwd_copy.start()
        inv_copy.start()
        hid_s[...] = _filter_hidden(emb_ref, fw1_ref, fb1_ref, fw2_ref, fb2_ref, fr_ref)
        fwd_copy.wait()

    @pl.when(new_tile)
    def _():
        h = hid_s[...].astype(BF16)
        decay = jnp.exp(-t_ref[...] * jnp.abs(dl_ref[...]))
        _filter_planes(h, w3f0_ref, w3b0_ref, decay, fwd_s, hf_s.at[0])
        _filter_planes(h, w3f1_ref, w3b1_ref, decay, fwd_s, hf_s.at[1])

    @pl.when(first_step)
    def _():
        inv_copy.wait()

    chains = [slice(c * seq, (c + 1) * seq) for c in range(uv_ref.shape[0] // seq)]
    row = lax.broadcasted_iota(jnp.int32, (seq, 1), 0)
    first, last = row == 0, row == seq - 1

    def short_conv(u_ref, w_ref, b_ref):
        outs = []
        for rows in chains:
            u = u_ref[rows, :]
            prev = jnp.where(first, 0.0, pltpu.roll(u, 1, axis=0))
            nxt = jnp.where(last, 0.0, pltpu.roll(u, seq - 1, axis=0))
            outs.append(prev * w_ref[0:1, :] + u * w_ref[1:2, :] + nxt * w_ref[2:3, :] + b_ref[...])
        return outs

    def long_conv(us, order):
        h_ref = hf_s.at[order]
        bias = hb_ref[order:order + 1, :]
        specs = [jnp.dot(fwd_s[...], u.astype(BF16), preferred_element_type=F32) for u in us]
        ys = []
        for spec in specs:
            ure, uim = spec[:seq], spec[seq:]
            yre = ure * h_ref[0] - uim * h_ref[1]
            yim = ure * h_ref[1] + uim * h_ref[2]
            ys.append(jnp.concatenate([yre, yim], axis=0).astype(BF16))
        return [jnp.dot(inv_s[...], y, preferred_element_type=F32) + u * bias
                for y, u in zip(ys, us)]

    v = short_conv(uv_ref, wv_ref, bv_ref)
    x1 = short_conv(u1_ref, w1_ref, b1_ref)
    z = [a * b for a, b in zip(x1, long_conv(v, 0))]
    x2 = short_conv(u2_ref, w2_ref, b2_ref)
    for rows, a, b in zip(chains, x2, long_conv(z, 1)):
        o_ref[rows, :] = (a * b).astype(BF16)


def _hyena_call(u, conv_w, conv_b, hy_bias, filt, fwd, inv, *, batch, seq, ct, n_chain):
    emb, w1, b1, w2, b2, fr, w3, tcol, deltas = filt
    nct = D_HYENA // ct
    rows = n_chain * seq
    col = lambda part: (lambda c, b: (b, part * nct + c))
    wcol = lambda part: (lambda c, b: (0, part * nct + c))
    small = lambda shape: pl.BlockSpec(shape, lambda c, b: (0, 0))
    w3col = lambda direction, order: (lambda c, b: (0, (direction * HYENA_ORDER + order) * nct + c))
    in_specs = ([pl.BlockSpec((rows, ct), col(p)) for p in range(3)]
                + [pl.BlockSpec((3, ct), wcol(p)) for p in range(3)]
                + [pl.BlockSpec((1, ct), wcol(p)) for p in range(3)]
                + [pl.BlockSpec((HYENA_ORDER, ct), lambda c, b: (0, c)),
                   small((seq, FILT_PAD)), small((FILT_PAD, FILT_PAD)), small((1, FILT_PAD)),
                   small((FILT_PAD, FILT_PAD)), small((1, FILT_PAD)), small((1, FILT_PAD))]
                + [pl.BlockSpec((FILT_PAD, ct), w3col(direction, order))
                   for order in range(HYENA_ORDER) for direction in range(2)]
                + [small((seq, 1)), pl.BlockSpec((1, ct), lambda c, b: (0, c)),
                   pl.BlockSpec(memory_space=pl.ANY), pl.BlockSpec(memory_space=pl.ANY)])
    return pl.pallas_call(
        _hyena_kernel,
        grid=(nct, batch // n_chain),
        in_specs=in_specs,
        out_specs=pl.BlockSpec((rows, ct), lambda c, b: (b, c)),
        out_shape=jax.ShapeDtypeStruct((batch * seq, D_HYENA), BF16),
        scratch_shapes=[pltpu.VMEM((2 * seq, seq), BF16), pltpu.VMEM((seq, 2 * seq), BF16),
                        pltpu.VMEM((seq, FILT_PAD), F32),
                        pltpu.VMEM((HYENA_ORDER, 3, seq, ct), F32),
                        pltpu.SemaphoreType.DMA((2,))],
        compiler_params=_params("arbitrary", "arbitrary"),
        name=f"hyena_{seq}",
    )(u, u, u, conv_w, conv_w, conv_w, conv_b, conv_b, conv_b, hy_bias,
      emb, w1, b1, w2, b2, fr, w3, w3, w3, w3, tcol, deltas, fwd, inv)


FF_CHUNK = D_MODEL
N_FF_CHUNKS = D_FF // FF_CHUNK
POST_PARTS = 2


def _post_kernel(x_ref, oa_ref, oh_ref, mod_ref, g1_ref, g2_ref, gf_ref,
                 win_hbm, wa_hbm, wh_hbm, wo_hbm, wu_hbm, wd_hbm, y_ref,
                 wg_s, wa_s, wh_s, wo_s, wu_s, wd_s, w_sem):
    ff = [slice(c * FF_CHUNK, (c + 1) * FF_CHUNK) for c in range(N_FF_CHUNKS)]
    copies = ([(win_hbm.at[:, GATE_OFF:], wg_s), (wa_hbm, wa_s), (wh_hbm, wh_s), (wo_hbm, wo_s)]
              + [piece for sl in ff for piece in ((wu_hbm.at[:, sl], wu_s.at[:, sl]),
                                                  (wd_hbm.at[sl, :], wd_s.at[sl, :]))])

    def w_copy(n):
        src, dst = copies[n]
        return pltpu.make_async_copy(src, dst, w_sem.at[n])

    def body(first_step):
        if first_step:
            for n in range(len(copies)):
                w_copy(n).start()

        def ready(n):
            if first_step:
                w_copy(n).wait()

        d = functools.partial(jnp.dot, preferred_element_type=F32)
        tm = x_ref.shape[0]
        parts = [slice(p * tm // POST_PARTS, (p + 1) * tm // POST_PARTS) for p in range(POST_PARTS)]
        each = lambda fn, *lists: [fn(*vals) for vals in zip(*lists)]

        def modulate(x, g_ref, shift, scale):
            return (_rms(x, g_ref[...]) * (1.0 + mod_ref[scale:scale + 1, :])
                    + mod_ref[shift:shift + 1, :]).astype(BF16)

        x = [x_ref[r, :] for r in parts]
        h1 = each(lambda v: modulate(v, g1_ref, 0, 1), x)
        ready(0)
        gates = each(lambda h: jax.nn.sigmoid(d(h, wg_s[...])), h1)
        ready(1)
        attn = [d(oa_ref[r, :], wa_s[...]) for r in parts]
        ready(2)
        hy = [d(oh_ref[r, :], wh_s[...]) for r in parts]
        merged = each(lambda g, a, h: (g[:, :D_MODEL] * a + g[:, D_MODEL:] * h).astype(BF16),
                      gates, attn, hy)
        ready(3)
        x = each(lambda v, m: v + mod_ref[2:3, :] * d(m, wo_s[...]), x, merged)
        h2 = each(lambda v: modulate(v, g2_ref, 3, 4), x)
        acc = [jnp.zeros_like(v) for v in x]
        for c, sl in enumerate(ff):
            ready(4 + 2 * c)
            up = each(lambda h: jnp.maximum(d(h, wu_s[:, sl]), 0.0), h2)
            ready(5 + 2 * c)
            acc = each(lambda a, u: a + d((u * u).astype(BF16), wd_s[sl, :]), acc, up)
        for r, v, a in zip(parts, x, acc):
            y_ref[r, :] = _rms(v + mod_ref[5:6, :] * a, gf_ref[...])

    step = pl.program_id(0)
    pl.when(step == 0)(functools.partial(body, True))
    pl.when(step != 0)(functools.partial(body, False))


def _post_call(x, o_attn, o_hy, mod, g1, g2, gf, w_in, wa, wh, wo, wu, wd, *, seq, latent, tm):
    t = x.shape[0]
    per_seq = seq // tm if latent else 1
    row = lambda i: (i, 0)

    def mod_idx(i):
        return (1 + i // per_seq if latent else 0, 0, 0)

    return pl.pallas_call(
        _post_kernel,
        grid=(t // tm,),
        in_specs=[pl.BlockSpec((tm, D_MODEL), row),
                  pl.BlockSpec((tm, N_HEADS * V_DIM), row),
                  pl.BlockSpec((tm, D_HYENA), row),
                  pl.BlockSpec((None, 6, D_MODEL), mod_idx),
                  _resident((1, D_MODEL)), _resident((1, D_MODEL)), _resident((1, D_MODEL))]
                 + [pl.BlockSpec(memory_space=pl.ANY)] * 6,
        out_specs=pl.BlockSpec((tm, D_MODEL), row),
        out_shape=jax.ShapeDtypeStruct((t, D_MODEL), F32),
        scratch_shapes=[pltpu.VMEM((D_MODEL, GATE_COLS), BF16)]
                       + [pltpu.VMEM(w.shape, BF16) for w in (wa, wh, wo, wu, wd)]
                       + [pltpu.SemaphoreType.DMA((4 + 2 * N_FF_CHUNKS,))],
        compiler_params=_params("arbitrary"),
        name="post_lat" if latent else "post_ctx",
    )(x, o_attn, o_hy, mod, g1, g2, gf, w_in, wa, wh, wo, wu, wd)


def _rope_tables(seq):
    half = HEAD_DIM // 2
    n = half // 2
    inv = ROPE_THETA ** (-np.arange(n, dtype=np.float64) / n)
    pos = np.arange(seq)
    ang_row = (pos // GRID_W).astype(np.float64)[:, None] * inv[None, :]
    ang_col = (pos % GRID_W).astype(np.float64)[:, None] * inv[None, :]
    zeros = np.zeros_like(ang_row)

    def per_map(a_row, a_col, lo, hi):
        return np.concatenate([lo(a_row), hi(a_row), lo(a_col), hi(a_col)], axis=-1)

    cos = per_map(ang_row, ang_col, np.cos, np.cos)
    sin_lo = per_map(ang_row, ang_col, lambda a: -np.sin(a), lambda a: zeros)
    sin_hi = per_map(ang_row, ang_col, lambda a: zeros, np.sin)
    return tuple(jnp.asarray(np.concatenate([tab, tab], axis=-1).astype(np.float32))
                 for tab in (cos, sin_lo, sin_hi))


def _filter_embedding(seq):
    bands = (FILTER_EMB - 1) // 2
    t = np.linspace(0.0, 1.0, seq)[:, None]
    wpos = 2.0 * np.pi * np.arange(seq, dtype=np.float64)[:, None] / seq
    f = np.linspace(1e-4, bands - 1, bands)[None, :]
    emb = np.concatenate([t, np.cos(f * wpos), -np.sin(f * wpos)], axis=-1)
    emb = np.pad(emb, ((0, 0), (0, FILT_PAD - FILTER_EMB)))
    return jnp.asarray(emb.astype(np.float32)), jnp.asarray(t.astype(np.float32))


def _tiles(seq):
    short = seq <= 256
    return dict(tm=512, tq=128, attn_heads=N_HEADS if short else 4, ct=MXU_WIDTH,
                hyena_seqs=4 if short else 2)


def _pad_to(x, rows, cols):
    return jnp.pad(x, ((0, rows - x.shape[0]), (0, cols - x.shape[1])))


def kernel(x_prompt, x_sample, cache_k, cache_v, c, c_ctx, w_ada, b_ada, norm1_g, norm2_g, w_in,
           lam_q1, lam_k1, lam_q2, lam_k2, attn_subln_g, conv_w, conv_b, filt_w1, filt_b1,
           filt_w2, filt_b2, filt_w3, filt_freq, hy_bias, w_br_attn, w_br_hy, w_out, w_up,
           w_down, final_g):
    depth = w_in.shape[0]
    assert depth == 1, "single trunk layer"
    layer = 0
    lam_init = 0.8 - 0.6 * math.exp(-0.3 * layer)
    n_ctx, ctx_len, _ = x_prompt.shape
    n_lat, lat_len, _ = x_sample.shape
    past = cache_k.shape[2]

    cvec = jnp.concatenate([c_ctx[None, :], c], axis=0)
    cvec = jnp.pad(cvec, ((0, MOD_ROWS - cvec.shape[0]), (0, 0)))
    mod = _mod_call(cvec, w_ada[layer], b_ada[layer][None, :]).reshape(MOD_ROWS, 6, D_MODEL)

    w_in_b = w_in[layer].astype(BF16)
    wa, wh, wo = (w[layer].astype(BF16) for w in (w_br_attn, w_br_hy, w_out))
    wu, wd = w_up[layer].astype(BF16), w_down[layer].astype(BF16)
    g1, g2, gf = norm1_g[layer][None, :], norm2_g[layer][None, :], final_g[None, :]
    lam_vecs = jnp.stack([lam_q1[layer], lam_k1[layer], lam_q2[layer], lam_k2[layer]])
    subln_g = attn_subln_g[layer][None, :]

    w1 = _pad_to(filt_w1[layer], FILT_PAD, FILT_PAD)
    w2 = _pad_to(filt_w2[layer], FILT_PAD, FILT_PAD)
    w3 = _pad_to(filt_w3[layer], FILT_PAD, filt_w3.shape[2])
    b1 = _pad_to(filt_b1[layer][None, :], 1, FILT_PAD)
    b2 = _pad_to(filt_b2[layer][None, :], 1, FILT_PAD)
    fr = _pad_to(filt_freq[layer][None, :], 1, FILT_PAD)
    deltas = jnp.asarray(np.linspace(math.log(DECAY_TARGET) / FAST_DECAY_PCT,
                                     math.log(DECAY_TARGET) / SLOW_DECAY_PCT,
                                     D_HYENA)[None, :].astype(np.float32))

    def branch(x3, batch, seq, latent):
        x = x3.reshape(batch * seq, D_MODEL)
        tiles = _tiles(seq)
        tm = tiles["tm"]
        rope = _rope_tables(seq) if latent else None
        outs = _in_proj_call(x, mod, g1, w_in_b, rope, seq=seq, latent=latent, tm=tm)
        if latent:
            q, kt, v, u = outs
            ckt = jnp.transpose(cache_k[:, layer], (0, 2, 3, 4, 1)).reshape(batch * K_COLS, past)
            cache = (ckt, cache_v[:, layer].reshape(batch * past * N_HEADS, V_DIM))
            kf = vf = None
        else:
            q, kt, v, kf, vf, u = outs
            cache = None
        o_attn = _attn_call(q, kt, v, cache, lam_vecs, subln_g, batch=batch, seq=seq,
                            lam_init=lam_init, tq=tiles["tq"], heads=tiles["attn_heads"])

        fwd_np, inv_np = _dft_tables(seq)
        fwd_b = jnp.asarray(fwd_np).astype(BF16)
        inv_b = jnp.asarray(inv_np).astype(BF16)
        emb, tcol = _filter_embedding(seq)
        filt = (emb, w1, b1, w2, b2, fr, w3, tcol, deltas)
        o_hy = _hyena_call(u, conv_w[layer], conv_b[layer][None, :], hy_bias[layer], filt,
                           fwd_b, inv_b, batch=batch, seq=seq, ct=tiles["ct"],
                           n_chain=tiles["hyena_seqs"])

        y = _post_call(x, o_attn, o_hy, mod, g1, g2, gf, w_in_b, wa, wh, wo, wu, wd,
                       seq=seq, latent=latent, tm=tm)
        return y.reshape(batch, seq, D_MODEL), kf, vf

    y_prompt, kf, vf = branch(x_prompt, n_ctx, ctx_len, False)
    y_sample, _, _ = branch(x_sample, n_lat, lat_len, True)
    new_cache_k = jnp.transpose(kf.reshape(n_ctx, depth, N_HEADS, 2, HEAD_DIM, ctx_len),
                                (0, 1, 5, 2, 3, 4))
    new_cache_v = vf.reshape(n_ctx, depth, ctx_len, N_HEADS, V_DIM)
    return (y_prompt, y_sample, new_cache_k, new_cache_v)
```

```python
import functools
import math

import numpy as np
import jax
import jax.numpy as jnp
from jax import lax
from jax.experimental import pallas as pl
from jax.experimental.pallas import tpu as pltpu

D_MODEL = 1024
GRID_W = 64
N_HEADS = 8
HEAD_DIM = 64
V_DIM = 2 * HEAD_DIM
D_HYENA = D_MODEL // 2
HYENA_ORDER = 2
FILTER_EMB = 33
FILTER_HIDDEN = 64
D_FF = 4 * D_MODEL
ROPE_THETA = 10000.0
EPS = 1e-6
LOG2_E = math.log2(math.e)
DECAY_TARGET = 1e-2
FAST_DECAY_PCT = 0.3
SLOW_DECAY_PCT = 1.5

Q_COLS = N_HEADS * 2 * HEAD_DIM
K_COLS = N_HEADS * 2 * HEAD_DIM
V_COLS = N_HEADS * V_DIM
HY_COLS = 3 * D_HYENA
GATE_COLS = 2 * D_MODEL
Q_OFF = 0
K_OFF = Q_OFF + Q_COLS
V_OFF = K_OFF + K_COLS
HY_OFF = V_OFF + V_COLS
GATE_OFF = HY_OFF + HY_COLS
IN_COLS = GATE_OFF + GATE_COLS

LANES = 128
MXU_WIDTH = 256
MOD_ROWS = 16
FILT_PAD = 128
VMEM_LIMIT = 56 * 1024 * 1024

BF16 = jnp.bfloat16
F32 = jnp.float32


def _resident(shape):
    nd = len(shape)
    return pl.BlockSpec(shape, lambda *_: (0,) * nd, pipeline_mode=pl.Buffered(1))


def _params(*sem):
    return pltpu.CompilerParams(dimension_semantics=sem, vmem_limit_bytes=VMEM_LIMIT)


def _rms(x, g):
    return x * lax.rsqrt(jnp.mean(x * x, axis=-1, keepdims=True) + EPS) * g


def _mod_kernel(c_ref, w_ref, b_ref, o_ref):
    c = c_ref[...]
    s = (c * jax.nn.sigmoid(c)).astype(BF16)
    o_ref[...] = jnp.dot(s, w_ref[...].astype(BF16), preferred_element_type=F32) + b_ref[...]


def _mod_call(cvec, w_ada, b_ada):
    tn = 1024
    n = w_ada.shape[1]
    return pl.pallas_call(
        _mod_kernel,
        grid=(n // tn,),
        in_specs=[pl.BlockSpec((MOD_ROWS, D_MODEL), lambda j: (0, 0)),
                  pl.BlockSpec((D_MODEL, tn), lambda j: (0, j)),
                  pl.BlockSpec((1, tn), lambda j: (0, j))],
        out_specs=pl.BlockSpec((MOD_ROWS, tn), lambda j: (0, j)),
        out_shape=jax.ShapeDtypeStruct((MOD_ROWS, n), F32),
        compiler_params=_params("arbitrary"),
        name="mod",
    )(cvec, w_ada, b_ada)


def _rope(x, cos, sin_lo, sin_hi):
    return (x * cos + pltpu.roll(x, 16, axis=1) * sin_hi
            + pltpu.roll(x, LANES - 16, axis=1) * sin_lo)


IN_GROUPS = ((Q_OFF, Q_COLS), (K_OFF, K_COLS), (V_OFF, V_COLS), (HY_OFF, HY_COLS))
G_Q, G_K, G_V, G_HY = range(len(IN_GROUPS))


def _in_proj_kernel(*refs, latent):
    if latent:
        (x_ref, mod_ref, g_ref, w_hbm, cos_ref, slo_ref, shi_ref,
         q_ref, kt_ref, v_ref, u_ref, w_s, w_sem) = refs
    else:
        (x_ref, mod_ref, g_ref, w_hbm,
         q_ref, kt_ref, v_ref, ktf_ref, vf_ref, u_ref, w_s, w_sem) = refs
    tm = x_ref.shape[0]

    def w_copy(g):
        off, width = IN_GROUPS[g]
        cols = slice(off, off + width)
        return pltpu.make_async_copy(w_hbm.at[:, cols], w_s.at[:, cols], w_sem.at[g])

    def body(first_step):
        if first_step:
            for g in range(len(IN_GROUPS)):
                w_copy(g).start()
        x = x_ref[...]
        h = _rms(x, g_ref[...]) * (1.0 + mod_ref[1:2, :]) + mod_ref[0:1, :]
        hb = h.astype(BF16)

        def proj(g):
            if first_step:
                w_copy(g).wait()
            off, width = IN_GROUPS[g]
            return jnp.dot(hb, w_s[:, off:off + width], preferred_element_type=F32)

        q = proj(G_Q) * (HEAD_DIM ** -0.5 * LOG2_E)
        k = proj(G_K)
        if latent:
            cos, slo, shi = cos_ref[...], slo_ref[...], shi_ref[...]
            k_heads = []
            for hd in range(N_HEADS):
                sl = slice(hd * LANES, (hd + 1) * LANES)
                q_ref[:, sl] = _rope(q[:, sl], cos, slo, shi).astype(BF16)
                k_heads.append(_rope(k[:, sl], cos, slo, shi))
            kt_ref[...] = jnp.concatenate(k_heads, axis=1).T.astype(BF16)
        else:
            q_ref[...] = q.astype(BF16)
            seq = kt_ref.shape[1]
            for s in range(tm // seq):
                kt = k[s * seq:(s + 1) * seq, :].T
                rows = slice(s * K_COLS, (s + 1) * K_COLS)
                ktf_ref[rows, :] = kt
                kt_ref[rows, :] = kt.astype(BF16)
        v = proj(G_V)
        v_ref[...] = v.astype(BF16)
        if not latent:
            for hd in range(N_HEADS):
                vf_ref[pl.ds(hd, tm, stride=N_HEADS), :] = v[:, hd * LANES:(hd + 1) * LANES]
        u_ref[...] = proj(G_HY)

    step = pl.program_id(0)
    pl.when(step == 0)(functools.partial(body, True))
    pl.when(step != 0)(functools.partial(body, False))


def _in_proj_call(x, mod, g1, w_in, rope, *, seq, latent, tm):
    t = x.shape[0]
    per_seq = seq // tm if latent else 1

    def mod_idx(i):
        return (1 + i // per_seq if latent else 0, 0, 0)

    row = lambda i: (i, 0)
    in_specs = [pl.BlockSpec((tm, D_MODEL), row),
                pl.BlockSpec((None, 6, D_MODEL), mod_idx),
                _resident((1, D_MODEL)),
                pl.BlockSpec(memory_space=pl.ANY)]
    args = [x, mod, g1, w_in]
    tok = lambda width, dtype: ((t, width), dtype, pl.BlockSpec((tm, width), row))
    if latent:
        in_specs += [pl.BlockSpec((tm, LANES), lambda i: (i % per_seq, 0))] * 3
        args += list(rope)
        kt_spec = pl.BlockSpec((K_COLS, tm), lambda i: (i // per_seq, i % per_seq))
    else:
        kt_spec = pl.BlockSpec((tm // seq * K_COLS, seq), row)
    kt_shape = (t // seq * K_COLS, seq)
    outs = [tok(Q_COLS, BF16), (kt_shape, BF16, kt_spec), tok(V_COLS, BF16)]
    if not latent:
        outs += [(kt_shape, F32, kt_spec),
                 ((t * N_HEADS, LANES), F32, pl.BlockSpec((tm * N_HEADS, LANES), row))]
    outs += [tok(HY_COLS, F32)]
    return pl.pallas_call(
        functools.partial(_in_proj_kernel, latent=latent),
        grid=(t // tm,),
        in_specs=in_specs,
        out_specs=[spec for _, _, spec in outs],
        out_shape=[jax.ShapeDtypeStruct(s, d) for s, d, _ in outs],
        scratch_shapes=[pltpu.VMEM((D_MODEL, GATE_OFF), BF16),
                        pltpu.SemaphoreType.DMA((len(IN_GROUPS),))],
        compiler_params=_params("arbitrary"),
        name="in_proj_lat" if latent else "in_proj_ctx",
    )(*args)


def _attn_kernel(*refs, n_cache, lam_init, tq):
    if n_cache:
        (q_ref, kt_ref, v_ref, ckt_ref, cv_ref, lam_ref, g_ref, o_ref, kt_s, v2_s) = refs
    else:
        (q_ref, kt_ref, v_ref, lam_ref, g_ref, o_ref, v2_s) = refs
    n_heads, n_keys, _ = v2_s.shape
    head0 = pl.program_id(1) * n_heads
    lv = lam_ref[...]
    lam = (jnp.exp(jnp.sum(lv[0:1] * lv[1:2], axis=-1, keepdims=True))
           - jnp.exp(jnp.sum(lv[2:3] * lv[3:4], axis=-1, keepdims=True)) + lam_init)
    gain = g_ref[...] * (1.0 - lam_init)
    first_map = lax.broadcasted_iota(jnp.int32, (1, LANES), 1) < HEAD_DIM
    zero = jnp.zeros((tq, LANES), BF16)

    def operands(hd):
        cols = slice(hd * LANES, (hd + 1) * LANES)
        v2_s[hd, :, LANES:] = jnp.ones((n_keys, LANES), BF16)
        if n_cache:
            kt_s[cols, 0:n_cache] = ckt_ref[cols, :].astype(BF16)
            kt_s[cols, n_cache:] = kt_ref[cols, :]
            cache_rows = pl.ds(head0 + hd, n_cache, stride=N_HEADS)
            v2_s[hd, 0:n_cache, 0:LANES] = cv_ref[cache_rows, :].astype(BF16)
            v2_s[hd, n_cache:, 0:LANES] = v_ref[:, cols]
            return kt_s[cols, :], v2_s[hd]
        v2_s[hd, :, 0:LANES] = v_ref[:, cols]
        return kt_ref[cols, :], v2_s[hd]

    def scores(hd, i, kt):
        q = q_ref[i * tq:(i + 1) * tq, hd * LANES:(hd + 1) * LANES]
        q2 = jnp.concatenate([jnp.where(first_map, q, zero), jnp.where(first_map, zero, q)], axis=0)
        s = jnp.dot(q2, kt, preferred_element_type=F32)
        return s, jnp.max(s, axis=-1, keepdims=True)

    def finish(hd, i, v2, s, m):
        p = jnp.exp2(s - m).astype(BF16)
        pv = jnp.dot(p, v2, preferred_element_type=F32)
        pv = pv[:, :LANES] / pv[:, LANES:]
        o = pv[:tq] - lam * pv[tq:]
        o_ref[i * tq:(i + 1) * tq, hd * LANES:(hd + 1) * LANES] = (_rms(o, 1.0) * gain).astype(BF16)

    n_tiles = q_ref.shape[0] // tq
    pending = None
    for hd in range(n_heads):
        kt, v2 = operands(hd)
        for i in range(n_tiles):
            cur = (hd, i, v2) + scores(hd, i, kt)
            if pending is not None:
                finish(*pending)
            pending = cur
    finish(*pending)


def _attn_call(q, kt, v, cache, lam_vecs, subln_g, *, batch, seq, lam_init, tq, heads):
    const = lambda b, h: (0, 0)
    width = heads * LANES
    n_groups = N_HEADS // heads
    in_specs = [pl.BlockSpec((seq, width), lambda b, h: (b, h)),
                pl.BlockSpec((width, seq), lambda b, h: (b * n_groups + h, 0)),
                pl.BlockSpec((seq, width), lambda b, h: (b, h))]
    args = [q, kt, v]
    n_cache = 0
    if cache is not None:
        ckt, cv = cache
        n_cache = ckt.shape[1]
        in_specs += [pl.BlockSpec((width, n_cache), lambda b, h: (b * n_groups + h, 0)),
                     pl.BlockSpec((n_cache * N_HEADS, LANES), lambda b, h: (b, 0))]
        args += [ckt, cv]
    n_keys = n_cache + seq
    scratch = [pltpu.VMEM((heads, n_keys, 2 * LANES), BF16)]
    if n_cache:
        scratch = [pltpu.VMEM((width, n_keys), BF16)] + scratch
    in_specs += [pl.BlockSpec(lam_vecs.shape, const), pl.BlockSpec((1, LANES), const)]
    args += [lam_vecs, subln_g]
    return pl.pallas_call(
        functools.partial(_attn_kernel, n_cache=n_cache, lam_init=lam_init, tq=tq),
        grid=(batch, n_groups),
        in_specs=in_specs,
        out_specs=pl.BlockSpec((seq, width), lambda b, h: (b, h)),
        out_shape=jax.ShapeDtypeStruct((batch * seq, N_HEADS * V_DIM), BF16),
        scratch_shapes=scratch,
        compiler_params=_params("arbitrary", "arbitrary"),
        name="attn_lat" if cache is not None else "attn_ctx",
    )(*args)


def _dft_tables(seq):
    n = 2 * seq
    f = np.arange(seq, dtype=np.int64)[:, None]
    t = np.arange(seq, dtype=np.int64)[None, :]
    ang = 2.0 * np.pi * ((f * t) % n).astype(np.float64) / n
    alt = np.where(np.arange(seq) % 2 == 0, 1.0, -1.0)
    cos, msin = np.cos(ang), -np.sin(ang)
    msin[0, :] = alt
    fwd = np.concatenate([cos, msin], axis=0)
    wgt = np.where(f == 0, 1.0, 2.0) / n
    inv_c = (cos * wgt).T
    inv_s = (msin * wgt).T
    inv = np.concatenate([inv_c, inv_s], axis=1)
    return fwd.astype(np.float32), inv.astype(np.float32)


def _filter_hidden(emb_ref, w1_ref, b1_ref, w2_ref, b2_ref, fr_ref):
    hp = functools.partial(jnp.dot, preferred_element_type=F32, precision=lax.Precision.HIGHEST)
    fr = fr_ref[...]
    h = jnp.sin(fr * (hp(emb_ref[...], w1_ref[...]) + b1_ref[...]))
    return jnp.sin(fr * (hp(h, w2_ref[...]) + b2_ref[...]))


def _filter_planes(h, w3f_ref, w3b_ref, decay, dft_ref, out_ref):
    seq = h.shape[0]
    fwd = jnp.dot(h, w3f_ref[...].astype(BF16), preferred_element_type=F32) * decay
    bwd = jnp.dot(h, w3b_ref[...].astype(BF16), preferred_element_type=F32) * decay
    hsum, hdif = fwd + bwd, fwd - bwd
    re = jnp.dot(dft_ref[0:seq, :], hsum.astype(BF16), preferred_element_type=F32)
    im = jnp.dot(dft_ref[seq:, :], hdif.astype(BF16), preferred_element_type=F32)
    row = lax.broadcasted_iota(jnp.int32, (seq, 1), 0)
    alt = jnp.where(row % 2 == 0, 1.0, -1.0)
    nyq = jnp.sum(hsum * alt, axis=0, keepdims=True)
    out_ref[0] = re
    out_ref[1] = jnp.where(row == 0, 0.0, im)
    out_ref[2] = jnp.where(row == 0, nyq, re)


def _hyena_kernel(uv_ref, u1_ref, u2_ref, wv_ref, w1_ref, w2_ref, bv_ref, b1_ref, b2_ref, hb_ref,
                  emb_ref, fw1_ref, fb1_ref, fw2_ref, fb2_ref, fr_ref,
                  w3f0_ref, w3b0_ref, w3f1_ref, w3b1_ref, t_ref, dl_ref, fwd_hbm, inv_hbm,
                  o_ref, fwd_s, inv_s, hid_s, hf_s, dft_sem):
    seq = emb_ref.shape[0]
    new_tile = pl.program_id(1) == 0
    first_step = jnp.logical_and(pl.program_id(0) == 0, new_tile)
    fwd_copy = pltpu.make_async_copy(fwd_hbm, fwd_s, dft_sem.at[0])
    inv_copy = pltpu.make_async_copy(inv_hbm, inv_s, dft_sem.at[1])

    @pl.when(first_step)
    def _():
        fwd_copy.start()
        inv_copy.start()
        hid_s[...] = _filter_hidden(emb_ref, fw1_ref, fb1_ref, fw2_ref, fb2_ref, fr_ref)
        fwd_copy.wait()

    @pl.when(new_tile)
    def _():
        h = hid_s[...].astype(BF16)
        decay = jnp.exp(-t_ref[...] * jnp.abs(dl_ref[...]))
        _filter_planes(h, w3f0_ref, w3b0_ref, decay, fwd_s, hf_s.at[0])
        _filter_planes(h, w3f1_ref, w3b1_ref, decay, fwd_s, hf_s.at[1])

    @pl.when(first_step)
    def _():
        inv_copy.wait()

    chains = [slice(c * seq, (c + 1) * seq) for c in range(uv_ref.shape[0] // seq)]
    row = lax.broadcasted_iota(jnp.int32, (seq, 1), 0)
    first, last = row == 0, row == seq - 1

    def short_conv(u_ref, w_ref, b_ref):
        outs = []
        for rows in chains:
            u = u_ref[rows, :]
            prev = jnp.where(first, 0.0, pltpu.roll(u, 1, axis=0))
            nxt = jnp.where(last, 0.0, pltpu.roll(u, seq - 1, axis=0))
            outs.append(prev * w_ref[0:1, :] + u * w_ref[1:2, :] + nxt * w_ref[2:3, :] + b_ref[...])
        return outs

    def long_conv(us, order):
        h_ref = hf_s.at[order]
        bias = hb_ref[order:order + 1, :]
        specs = [jnp.dot(fwd_s[...], u.astype(BF16), preferred_element_type=F32) for u in us]
        ys = []
        for spec in specs:
            ure, uim = spec[:seq], spec[seq:]
            yre = ure * h_ref[0] - uim * h_ref[1]
            yim = ure * h_ref[1] + uim * h_ref[2]
            ys.append(jnp.concatenate([yre, yim], axis=0).astype(BF16))
        return [jnp.dot(inv_s[...], y, preferred_element_type=F32) + u * bias
                for y, u in zip(ys, us)]

    v = short_conv(uv_ref, wv_ref, bv_ref)
    x1 = short_conv(u1_ref, w1_ref, b1_ref)
    z = [a * b for a, b in zip(x1, long_conv(v, 0))]
    x2 = short_conv(u2_ref, w2_ref, b2_ref)
    for rows, a, b in zip(chains, x2, long_conv(z, 1)):
        o_ref[rows, :] = (a * b).astype(BF16)


def _hyena_call(u, conv_w, conv_b, hy_bias, filt, fwd, inv, *, batch, seq, ct, n_chain):
    emb, w1, b1, w2, b2, fr, w3, tcol, deltas = filt
    nct = D_HYENA // ct
    rows = n_chain * seq
    col = lambda part: (lambda c, b: (b, part * nct + c))
    wcol = lambda part: (lambda c, b: (0, part * nct + c))
    small = lambda shape: pl.BlockSpec(shape, lambda c, b: (0, 0))
    w3col = lambda direction, order: (lambda c, b: (0, (direction * HYENA_ORDER + order) * nct + c))
    in_specs = ([pl.BlockSpec((rows, ct), col(p)) for p in range(3)]
                + [pl.BlockSpec((3, ct), wcol(p)) for p in range(3)]
                + [pl.BlockSpec((1, ct), wcol(p)) for p in range(3)]
                + [pl.BlockSpec((HYENA_ORDER, ct), lambda c, b: (0, c)),
                   small((seq, FILT_PAD)), small((FILT_PAD, FILT_PAD)), small((1, FILT_PAD)),
                   small((FILT_PAD, FILT_PAD)), small((1, FILT_PAD)), small((1, FILT_PAD))]
                + [pl.BlockSpec((FILT_PAD, ct), w3col(direction, order))
                   for order in range(HYENA_ORDER) for direction in range(2)]
                + [small((seq, 1)), pl.BlockSpec((1, ct), lambda c, b: (0, c)),
                   pl.BlockSpec(memory_space=pl.ANY), pl.BlockSpec(memory_space=pl.ANY)])
    return pl.pallas_call(
        _hyena_kernel,
        grid=(nct, batch // n_chain),
        in_specs=in_specs,
        out_specs=pl.BlockSpec((rows, ct), lambda c, b: (b, c)),
        out_shape=jax.ShapeDtypeStruct((batch * seq, D_HYENA), BF16),
        scratch_shapes=[pltpu.VMEM((2 * seq, seq), BF16), pltpu.VMEM((seq, 2 * seq), BF16),
                        pltpu.VMEM((seq, FILT_PAD), F32),
                        pltpu.VMEM((HYENA_ORDER, 3, seq, ct), F32),
                        pltpu.SemaphoreType.DMA((2,))],
        compiler_params=_params("arbitrary", "arbitrary"),
        name=f"hyena_{seq}",
    )(u, u, u, conv_w, conv_w, conv_w, conv_b, conv_b, conv_b, hy_bias,
      emb, w1, b1, w2, b2, fr, w3, w3, w3, w3, tcol, deltas, fwd, inv)


FF_CHUNK = D_MODEL
N_FF_CHUNKS = D_FF // FF_CHUNK


def _post_kernel(x_ref, oa_ref, oh_ref, mod_ref, g1_ref, g2_ref, gf_ref,
                 win_hbm, wa_hbm, wh_hbm, wo_hbm, wu_hbm, wd_hbm, y_ref,
                 wg_s, wa_s, wh_s, wo_s, wu_s, wd_s, w_sem):
    ff = [slice(c * FF_CHUNK, (c + 1) * FF_CHUNK) for c in range(N_FF_CHUNKS)]
    copies = ([(win_hbm.at[:, GATE_OFF:], wg_s), (wa_hbm, wa_s), (wh_hbm, wh_s), (wo_hbm, wo_s)]
              + [piece for sl in ff for piece in ((wu_hbm.at[:, sl], wu_s.at[:, sl]),
                                                  (wd_hbm.at[sl, :], wd_s.at[sl, :]))])

    def w_copy(n):
        src, dst = copies[n]
        return pltpu.make_async_copy(src, dst, w_sem.at[n])

    def body(first_step):
        if first_step:
            for n in range(len(copies)):
                w_copy(n).start()

        def ready(n):
            if first_step:
                w_copy(n).wait()

        d = functools.partial(jnp.dot, preferred_element_type=F32)
        x = x_ref[...]
        h1 = (_rms(x, g1_ref[...]) * (1.0 + mod_ref[1:2, :]) + mod_ref[0:1, :]).astype(BF16)
        ready(0)
        gates = jax.nn.sigmoid(d(h1, wg_s[...]))
        ready(1)
        attn = d(oa_ref[...], wa_s[...])
        ready(2)
        merged = gates[:, :D_MODEL] * attn + gates[:, D_MODEL:] * d(oh_ref[...], wh_s[...])
        ready(3)
        x = x + mod_ref[2:3, :] * d(merged.astype(BF16), wo_s[...])
        h2 = (_rms(x, g2_ref[...]) * (1.0 + mod_ref[4:5, :]) + mod_ref[3:4, :]).astype(BF16)
        acc = jnp.zeros_like(x)
        for c, sl in enumerate(ff):
            ready(4 + 2 * c)
            up = jnp.maximum(d(h2, wu_s[:, sl]), 0.0)
            ready(5 + 2 * c)
            acc = acc + d((up * up).astype(BF16), wd_s[sl, :])
        x = x + mod_ref[5:6, :] * acc
        y_ref[...] = _rms(x, gf_ref[...])

    step = pl.program_id(0)
    pl.when(step == 0)(functools.partial(body, True))
    pl.when(step != 0)(functools.partial(body, False))


def _post_call(x, o_attn, o_hy, mod, g1, g2, gf, w_in, wa, wh, wo, wu, wd, *, seq, latent, tm):
    t = x.shape[0]
    per_seq = seq // tm if latent else 1
    row = lambda i: (i, 0)

    def mod_idx(i):
        return (1 + i // per_seq if latent else 0, 0, 0)

    return pl.pallas_call(
        _post_kernel,
        grid=(t // tm,),
        in_specs=[pl.BlockSpec((tm, D_MODEL), row),
                  pl.BlockSpec((tm, N_HEADS * V_DIM), row),
                  pl.BlockSpec((tm, D_HYENA), row),
                  pl.BlockSpec((None, 6, D_MODEL), mod_idx),
                  _resident((1, D_MODEL)), _resident((1, D_MODEL)), _resident((1, D_MODEL))]
                 + [pl.BlockSpec(memory_space=pl.ANY)] * 6,
        out_specs=pl.BlockSpec((tm, D_MODEL), row),
        out_shape=jax.ShapeDtypeStruct((t, D_MODEL), F32),
        scratch_shapes=[pltpu.VMEM((D_MODEL, GATE_COLS), BF16)]
                       + [pltpu.VMEM(w.shape, BF16) for w in (wa, wh, wo, wu, wd)]
                       + [pltpu.SemaphoreType.DMA((4 + 2 * N_FF_CHUNKS,))],
        compiler_params=_params("arbitrary"),
        name="post_lat" if latent else "post_ctx",
    )(x, o_attn, o_hy, mod, g1, g2, gf, w_in, wa, wh, wo, wu, wd)


def _rope_tables(seq):
    half = HEAD_DIM // 2
    n = half // 2
    inv = ROPE_THETA ** (-np.arange(n, dtype=np.float64) / n)
    pos = np.arange(seq)
    ang_row = (pos // GRID_W).astype(np.float64)[:, None] * inv[None, :]
    ang_col = (pos % GRID_W).astype(np.float64)[:, None] * inv[None, :]
    zeros = np.zeros_like(ang_row)

    def per_map(a_row, a_col, lo, hi):
        return np.concatenate([lo(a_row), hi(a_row), lo(a_col), hi(a_col)], axis=-1)

    cos = per_map(ang_row, ang_col, np.cos, np.cos)
    sin_lo = per_map(ang_row, ang_col, lambda a: -np.sin(a), lambda a: zeros)
    sin_hi = per_map(ang_row, ang_col, lambda a: zeros, np.sin)
    return tuple(jnp.asarray(np.concatenate([tab, tab], axis=-1).astype(np.float32))
                 for tab in (cos, sin_lo, sin_hi))


def _filter_embedding(seq):
    bands = (FILTER_EMB - 1) // 2
    t = np.linspace(0.0, 1.0, seq)[:, None]
    wpos = 2.0 * np.pi * np.arange(seq, dtype=np.float64)[:, None] / seq
    f = np.linspace(1e-4, bands - 1, bands)[None, :]
    emb = np.concatenate([t, np.cos(f * wpos), -np.sin(f * wpos)], axis=-1)
    emb = np.pad(emb, ((0, 0), (0, FILT_PAD - FILTER_EMB)))
    return jnp.asarray(emb.astype(np.float32)), jnp.asarray(t.astype(np.float32))


def _tiles(seq):
    short = seq <= 256
    return dict(tm=512, tq=128, attn_heads=N_HEADS if short else 4, ct=MXU_WIDTH,
                hyena_seqs=4 if short else 2)


def _pad_to(x, rows, cols):
    return jnp.pad(x, ((0, rows - x.shape[0]), (0, cols - x.shape[1])))


def kernel(x_prompt, x_sample, cache_k, cache_v, c, c_ctx, w_ada, b_ada, norm1_g, norm2_g, w_in,
           lam_q1, lam_k1, lam_q2, lam_k2, attn_subln_g, conv_w, conv_b, filt_w1, filt_b1,
           filt_w2, filt_b2, filt_w3, filt_freq, hy_bias, w_br_attn, w_br_hy, w_out, w_up,
           w_down, final_g):
    depth = w_in.shape[0]
    assert depth == 1, "single trunk layer"
    layer = 0
    lam_init = 0.8 - 0.6 * math.exp(-0.3 * layer)
    n_ctx, ctx_len, _ = x_prompt.shape
    n_lat, lat_len, _ = x_sample.shape
    past = cache_k.shape[2]

    cvec = jnp.concatenate([c_ctx[None, :], c], axis=0)
    cvec = jnp.pad(cvec, ((0, MOD_ROWS - cvec.shape[0]), (0, 0)))
    mod = _mod_call(cvec, w_ada[layer], b_ada[layer][None, :]).reshape(MOD_ROWS, 6, D_MODEL)

    w_in_b = w_in[layer].astype(BF16)
    wa, wh, wo = (w[layer].astype(BF16) for w in (w_br_attn, w_br_hy, w_out))
    wu, wd = w_up[layer].astype(BF16), w_down[layer].astype(BF16)
    g1, g2, gf = norm1_g[layer][None, :], norm2_g[layer][None, :], final_g[None, :]
    lam_vecs = jnp.stack([lam_q1[layer], lam_k1[layer], lam_q2[layer], lam_k2[layer]])
    subln_g = attn_subln_g[layer][None, :]

    w1 = _pad_to(filt_w1[layer], FILT_PAD, FILT_PAD)
    w2 = _pad_to(filt_w2[layer], FILT_PAD, FILT_PAD)
    w3 = _pad_to(filt_w3[layer], FILT_PAD, filt_w3.shape[2])
    b1 = _pad_to(filt_b1[layer][None, :], 1, FILT_PAD)
    b2 = _pad_to(filt_b2[layer][None, :], 1, FILT_PAD)
    fr = _pad_to(filt_freq[layer][None, :], 1, FILT_PAD)
    deltas = jnp.asarray(np.linspace(math.log(DECAY_TARGET) / FAST_DECAY_PCT,
                                     math.log(DECAY_TARGET) / SLOW_DECAY_PCT,
                                     D_HYENA)[None, :].astype(np.float32))

    def branch(x3, batch, seq, latent):
        x = x3.reshape(batch * seq, D_MODEL)
        tiles = _tiles(seq)
        tm = tiles["tm"]
        rope = _rope_tables(seq) if latent else None
        outs = _in_proj_call(x, mod, g1, w_in_b, rope, seq=seq, latent=latent, tm=tm)
        if latent:
            q, kt, v, u = outs
            ckt = jnp.transpose(cache_k[:, layer], (0, 2, 3, 4, 1)).reshape(batch * K_COLS, past)
            cache = (ckt, cache_v[:, layer].reshape(batch * past * N_HEADS, V_DIM))
            kf = vf = None
        else:
            q, kt, v, kf, vf, u = outs
            cache = None
        o_attn = _attn_call(q, kt, v, cache, lam_vecs, subln_g, batch=batch, seq=seq,
                            lam_init=lam_init, tq=tiles["tq"], heads=tiles["attn_heads"])

        fwd_np, inv_np = _dft_tables(seq)
        fwd_b = jnp.asarray(fwd_np).astype(BF16)
        inv_b = jnp.asarray(inv_np).astype(BF16)
        emb, tcol = _filter_embedding(seq)
        filt = (emb, w1, b1, w2, b2, fr, w3, tcol, deltas)
        o_hy = _hyena_call(u, conv_w[layer], conv_b[layer][None, :], hy_bias[layer], filt,
                           fwd_b, inv_b, batch=batch, seq=seq, ct=tiles["ct"],
                           n_chain=tiles["hyena_seqs"])

        y = _post_call(x, o_attn, o_hy, mod, g1, g2, gf, w_in_b, wa, wh, wo, wu, wd,
                       seq=seq, latent=latent, tm=tm)
        return y.reshape(batch, seq, D_MODEL), kf, vf

    y_prompt, kf, vf = branch(x_prompt, n_ctx, ctx_len, False)
    y_sample, _, _ = branch(x_sample, n_lat, lat_len, True)
    new_cache_k = jnp.transpose(kf.reshape(n_ctx, depth, N_HEADS, 2, HEAD_DIM, ctx_len),
                                (0, 1, 5, 2, 3, 4))
    new_cache_v = vf.reshape(n_ctx, depth, ctx_len, N_HEADS, V_DIM)
    return (y_prompt, y_sample, new_cache_k, new_cache_v)
```

```python
import functools
import math

import numpy as np
import jax
import jax.numpy as jnp
from jax import lax
from jax.experimental import pallas as pl
from jax.experimental.pallas import tpu as pltpu

D_MODEL = 1024
GRID_W = 64
N_HEADS = 8
HEAD_DIM = 64
V_DIM = 2 * HEAD_DIM
D_HYENA = D_MODEL // 2
HYENA_ORDER = 2
FILTER_EMB = 33
FILTER_HIDDEN = 64
D_FF = 4 * D_MODEL
ROPE_THETA = 10000.0
EPS = 1e-6
LOG2_E = math.log2(math.e)
DECAY_TARGET = 1e-2
FAST_DECAY_PCT = 0.3
SLOW_DECAY_PCT = 1.5

Q_COLS = N_HEADS * 2 * HEAD_DIM
K_COLS = N_HEADS * 2 * HEAD_DIM
V_COLS = N_HEADS * V_DIM
HY_COLS = 3 * D_HYENA
GATE_COLS = 2 * D_MODEL
Q_OFF = 0
K_OFF = Q_OFF + Q_COLS
V_OFF = K_OFF + K_COLS
HY_OFF = V_OFF + V_COLS
GATE_OFF = HY_OFF + HY_COLS
IN_COLS = GATE_OFF + GATE_COLS

LANES = 128
MXU_WIDTH = 256
MOD_ROWS = 16
FILT_PAD = 128
VMEM_LIMIT = 56 * 1024 * 1024

BF16 = jnp.bfloat16
F32 = jnp.float32


def _resident(shape):
    nd = len(shape)
    return pl.BlockSpec(shape, lambda *_: (0,) * nd, pipeline_mode=pl.Buffered(1))


def _params(*sem):
    return pltpu.CompilerParams(dimension_semantics=sem, vmem_limit_bytes=VMEM_LIMIT)


def _rms(x, g):
    return x * lax.rsqrt(jnp.mean(x * x, axis=-1, keepdims=True) + EPS) * g


def _mod_kernel(c_ref, w_ref, b_ref, o_ref):
    c = c_ref[...]
    s = (c * jax.nn.sigmoid(c)).astype(BF16)
    o_ref[...] = jnp.dot(s, w_ref[...].astype(BF16), preferred_element_type=F32) + b_ref[...]


def _mod_call(cvec, w_ada, b_ada):
    tn = 1024
    n = w_ada.shape[1]
    return pl.pallas_call(
        _mod_kernel,
        grid=(n // tn,),
        in_specs=[pl.BlockSpec((MOD_ROWS, D_MODEL), lambda j: (0, 0)),
                  pl.BlockSpec((D_MODEL, tn), lambda j: (0, j)),
                  pl.BlockSpec((1, tn), lambda j: (0, j))],
        out_specs=pl.BlockSpec((MOD_ROWS, tn), lambda j: (0, j)),
        out_shape=jax.ShapeDtypeStruct((MOD_ROWS, n), F32),
        compiler_params=_params("arbitrary"),
        name="mod",
    )(cvec, w_ada, b_ada)


def _rope(x, cos, sin_lo, sin_hi):
    return (x * cos + pltpu.roll(x, 16, axis=1) * sin_hi
            + pltpu.roll(x, LANES - 16, axis=1) * sin_lo)


IN_GROUPS = ((Q_OFF, Q_COLS), (K_OFF, K_COLS), (V_OFF, V_COLS), (HY_OFF, HY_COLS))
G_Q, G_K, G_V, G_HY = range(len(IN_GROUPS))


def _in_proj_kernel(*refs, latent, n_cast):
    refs = list(refs)
    take = lambda n: [refs.pop(0) for _ in range(n)]
    x_ref, mod_ref, g_ref, w_hbm = take(4)
    if latent:
        cos_ref, slo_ref, shi_ref = take(3)
    cast_src = take(n_cast)
    q_ref, kt_ref, v_ref = take(3)
    if not latent:
        ktf_ref, vf_ref = take(2)
    (u_ref,) = take(1)
    cast_dst = take(n_cast)
    w_s, w_sem = refs
    tm = x_ref.shape[0]

    def w_copy(g):
        off, width = IN_GROUPS[g]
        cols = slice(off, off + width)
        return pltpu.make_async_copy(w_hbm.at[:, cols], w_s.at[:, cols], w_sem.at[g])

    def body(first_step):
        if first_step:
            for g in range(len(IN_GROUPS)):
                w_copy(g).start()
        x = x_ref[...]
        h = _rms(x, g_ref[...]) * (1.0 + mod_ref[1:2, :]) + mod_ref[0:1, :]
        hb = h.astype(BF16)

        def proj(g):
            if first_step:
                w_copy(g).wait()
            off, width = IN_GROUPS[g]
            return jnp.dot(hb, w_s[:, off:off + width], preferred_element_type=F32)

        q = proj(G_Q) * (HEAD_DIM ** -0.5 * LOG2_E)
        k = proj(G_K)
        if latent:
            cos, slo, shi = cos_ref[...], slo_ref[...], shi_ref[...]
            k_heads = []
            for hd in range(N_HEADS):
                sl = slice(hd * LANES, (hd + 1) * LANES)
                q_ref[:, sl] = _rope(q[:, sl], cos, slo, shi).astype(BF16)
                k_heads.append(_rope(k[:, sl], cos, slo, shi))
            kt_ref[...] = jnp.concatenate(k_heads, axis=1).T.astype(BF16)
        else:
            q_ref[...] = q.astype(BF16)
            seq = kt_ref.shape[1]
            for s in range(tm // seq):
                kt = k[s * seq:(s + 1) * seq, :].T
                rows = slice(s * K_COLS, (s + 1) * K_COLS)
                ktf_ref[rows, :] = kt
                kt_ref[rows, :] = kt.astype(BF16)
        v = proj(G_V)
        v_ref[...] = v.astype(BF16)
        if not latent:
            for hd in range(N_HEADS):
                vf_ref[pl.ds(hd, tm, stride=N_HEADS), :] = v[:, hd * LANES:(hd + 1) * LANES]
        u_ref[...] = proj(G_HY)
        for src, dst in zip(cast_src, cast_dst):
            dst[...] = src[...].astype(BF16)

    step = pl.program_id(0)
    pl.when(step == 0)(functools.partial(body, True))
    pl.when(step != 0)(functools.partial(body, False))


def _in_proj_call(x, mod, g1, w_in, rope, *, seq, latent, tm, cast=()):
    t = x.shape[0]
    n_steps = t // tm
    per_seq = seq // tm if latent else 1

    def mod_idx(i):
        return (1 + i // per_seq if latent else 0, 0, 0)

    row = lambda i: (i, 0)
    in_specs = [pl.BlockSpec((tm, D_MODEL), row),
                pl.BlockSpec((None, 6, D_MODEL), mod_idx),
                _resident((1, D_MODEL)),
                pl.BlockSpec(memory_space=pl.ANY)]
    args = [x, mod, g1, w_in]
    tok = lambda width, dtype: ((t, width), dtype, pl.BlockSpec((tm, width), row))
    if latent:
        in_specs += [pl.BlockSpec((tm, LANES), lambda i: (i % per_seq, 0))] * 3
        args += list(rope)
        kt_spec = pl.BlockSpec((K_COLS, tm), lambda i: (i // per_seq, i % per_seq))
    else:
        kt_spec = pl.BlockSpec((tm // seq * K_COLS, seq), row)
    kt_shape = (t // seq * K_COLS, seq)
    outs = [tok(Q_COLS, BF16), (kt_shape, BF16, kt_spec), tok(V_COLS, BF16)]
    if not latent:
        outs += [(kt_shape, F32, kt_spec),
                 ((t * N_HEADS, LANES), F32, pl.BlockSpec((tm * N_HEADS, LANES), row))]
    outs += [tok(HY_COLS, F32)]
    for w in cast:
        slab = pl.BlockSpec((w.shape[0] // n_steps, w.shape[1]), row)
        in_specs.append(slab)
        args.append(w)
        outs.append((w.shape, BF16, slab))
    return pl.pallas_call(
        functools.partial(_in_proj_kernel, latent=latent, n_cast=len(cast)),
        grid=(n_steps,),
        in_specs=in_specs,
        out_specs=[spec for _, _, spec in outs],
        out_shape=[jax.ShapeDtypeStruct(s, d) for s, d, _ in outs],
        scratch_shapes=[pltpu.VMEM((D_MODEL, GATE_OFF), BF16),
                        pltpu.SemaphoreType.DMA((len(IN_GROUPS),))],
        compiler_params=_params("arbitrary"),
        name="in_proj_lat" if latent else "in_proj_ctx",
    )(*args)


def _attn_kernel(*refs, n_cache, lam_init, tq):
    if n_cache:
        (q_ref, kt_ref, v_ref, ckt_ref, cv_ref, lam_ref, g_ref, o_ref, kt_s, v2_s) = refs
    else:
        (q_ref, kt_ref, v_ref, lam_ref, g_ref, o_ref, v2_s) = refs
    n_heads, n_keys, _ = v2_s.shape
    head0 = pl.program_id(1) * n_heads
    lv = lam_ref[...]
    lam = (jnp.exp(jnp.sum(lv[0:1] * lv[1:2], axis=-1, keepdims=True))
           - jnp.exp(jnp.sum(lv[2:3] * lv[3:4], axis=-1, keepdims=True)) + lam_init)
    gain = g_ref[...] * (1.0 - lam_init)
    first_map = lax.broadcasted_iota(jnp.int32, (1, LANES), 1) < HEAD_DIM
    zero = jnp.zeros((tq, LANES), BF16)

    def operands(hd):
        cols = slice(hd * LANES, (hd + 1) * LANES)
        v2_s[hd, :, LANES:] = jnp.ones((n_keys, LANES), BF16)
        if n_cache:
            kt_s[cols, 0:n_cache] = ckt_ref[cols, :].astype(BF16)
            kt_s[cols, n_cache:] = kt_ref[cols, :]
            cache_rows = pl.ds(head0 + hd, n_cache, stride=N_HEADS)
            v2_s[hd, 0:n_cache, 0:LANES] = cv_ref[cache_rows, :].astype(BF16)
            v2_s[hd, n_cache:, 0:LANES] = v_ref[:, cols]
            return kt_s[cols, :], v2_s[hd]
        v2_s[hd, :, 0:LANES] = v_ref[:, cols]
        return kt_ref[cols, :], v2_s[hd]

    def scores(hd, i, kt):
        q = q_ref[i * tq:(i + 1) * tq, hd * LANES:(hd + 1) * LANES]
        q2 = jnp.concatenate([jnp.where(first_map, q, zero), jnp.where(first_map, zero, q)], axis=0)
        s = jnp.dot(q2, kt, preferred_element_type=F32)
        return s, jnp.max(s, axis=-1, keepdims=True)

    def finish(hd, i, v2, s, m):
        p = jnp.exp2(s - m).astype(BF16)
        pv = jnp.dot(p, v2, preferred_element_type=F32)
        pv = pv[:, :LANES] / pv[:, LANES:]
        o = pv[:tq] - lam * pv[tq:]
        o_ref[i * tq:(i + 1) * tq, hd * LANES:(hd + 1) * LANES] = (_rms(o, 1.0) * gain).astype(BF16)

    n_tiles = q_ref.shape[0] // tq
    pending = None
    for hd in range(n_heads):
        kt, v2 = operands(hd)
        for i in range(n_tiles):
            cur = (hd, i, v2) + scores(hd, i, kt)
            if pending is not None:
                finish(*pending)
            pending = cur
    finish(*pending)


def _attn_call(q, kt, v, cache, lam_vecs, subln_g, *, batch, seq, lam_init, tq, heads):
    const = lambda b, h: (0, 0)
    width = heads * LANES
    n_groups = N_HEADS // heads
    in_specs = [pl.BlockSpec((seq, width), lambda b, h: (b, h)),
                pl.BlockSpec((width, seq), lambda b, h: (b * n_groups + h, 0)),
                pl.BlockSpec((seq, width), lambda b, h: (b, h))]
    args = [q, kt, v]
    n_cache = 0
    if cache is not None:
        ckt, cv = cache
        n_cache = ckt.shape[1]
        in_specs += [pl.BlockSpec((width, n_cache), lambda b, h: (b * n_groups + h, 0)),
                     pl.BlockSpec((n_cache * N_HEADS, LANES), lambda b, h: (b, 0))]
        args += [ckt, cv]
    n_keys = n_cache + seq
    scratch = [pltpu.VMEM((heads, n_keys, 2 * LANES), BF16)]
    if n_cache:
        scratch = [pltpu.VMEM((width, n_keys), BF16)] + scratch
    in_specs += [pl.BlockSpec(lam_vecs.shape, const), pl.BlockSpec((1, LANES), const)]
    args += [lam_vecs, subln_g]
    return pl.pallas_call(
        functools.partial(_attn_kernel, n_cache=n_cache, lam_init=lam_init, tq=tq),
        grid=(batch, n_groups),
        in_specs=in_specs,
        out_specs=pl.BlockSpec((seq, width), lambda b, h: (b, h)),
        out_shape=jax.ShapeDtypeStruct((batch * seq, N_HEADS * V_DIM), BF16),
        scratch_shapes=scratch,
        compiler_params=_params("arbitrary", "arbitrary"),
        name="attn_lat" if cache is not None else "attn_ctx",
    )(*args)


def _dft_tables(seq):
    n = 2 * seq
    f = np.arange(seq, dtype=np.int64)[:, None]
    t = np.arange(seq, dtype=np.int64)[None, :]
    ang = 2.0 * np.pi * ((f * t) % n).astype(np.float64) / n
    alt = np.where(np.arange(seq) % 2 == 0, 1.0, -1.0)
    cos, msin = np.cos(ang), -np.sin(ang)
    msin[0, :] = alt
    fwd = np.concatenate([cos, msin], axis=0)
    wgt = np.where(f == 0, 1.0, 2.0) / n
    inv_c = (cos * wgt).T
    inv_s = (msin * wgt).T
    inv = np.concatenate([inv_c, inv_s], axis=1)
    return fwd.astype(np.float32), inv.astype(np.float32)


def _filter_hidden(emb_ref, w1_ref, b1_ref, w2_ref, b2_ref, fr_ref):
    hp = functools.partial(jnp.dot, preferred_element_type=F32, precision=lax.Precision.HIGHEST)
    fr = fr_ref[...]
    h = jnp.sin(fr * (hp(emb_ref[...], w1_ref[...]) + b1_ref[...]))
    return jnp.sin(fr * (hp(h, w2_ref[...]) + b2_ref[...]))


def _filter_planes(h, w3f_ref, w3b_ref, decay, dft_ref, out_ref):
    seq = h.shape[0]
    fwd = jnp.dot(h, w3f_ref[...].astype(BF16), preferred_element_type=F32) * decay
    bwd = jnp.dot(h, w3b_ref[...].astype(BF16), preferred_element_type=F32) * decay
    hsum, hdif = fwd + bwd, fwd - bwd
    re = jnp.dot(dft_ref[0:seq, :], hsum.astype(BF16), preferred_element_type=F32)
    im = jnp.dot(dft_ref[seq:, :], hdif.astype(BF16), preferred_element_type=F32)
    row = lax.broadcasted_iota(jnp.int32, (seq, 1), 0)
    alt = jnp.where(row % 2 == 0, 1.0, -1.0)
    nyq = jnp.sum(hsum * alt, axis=0, keepdims=True)
    out_ref[0] = re
    out_ref[1] = jnp.where(row == 0, 0.0, im)
    out_ref[2] = jnp.where(row == 0, nyq, re)


def _hyena_kernel(uv_ref, u1_ref, u2_ref, wv_ref, w1_ref, w2_ref, bv_ref, b1_ref, b2_ref, hb_ref,
                  emb_ref, fw1_ref, fb1_ref, fw2_ref, fb2_ref, fr_ref,
                  w3f0_ref, w3b0_ref, w3f1_ref, w3b1_ref, t_ref, dl_ref, fwd_hbm, inv_hbm,
                  o_ref, fwd_s, inv_s, hid_s, hf_s, dft_sem):
    seq = emb_ref.shape[0]
    new_tile = pl.program_id(1) == 0
    first_step = jnp.logical_and(pl.program_id(0) == 0, new_tile)
    fwd_copy = pltpu.make_async_copy(fwd_hbm, fwd_s, dft_sem.at[0])
    inv_copy = pltpu.make_async_copy(inv_hbm, inv_s, dft_sem.at[1])

    @pl.when(first_step)
    def _():
        fwd_copy.start()
        inv_copy.start()
        hid_s[...] = _filter_hidden(emb_ref, fw1_ref, fb1_ref, fw2_ref, fb2_ref, fr_ref)
        fwd_copy.wait()

    @pl.when(new_tile)
    def _():
        h = hid_s[...].astype(BF16)
        decay = jnp.exp(-t_ref[...] * jnp.abs(dl_ref[...]))
        _filter_planes(h, w3f0_ref, w3b0_ref, decay, fwd_s, hf_s.at[0])
        _filter_planes(h, w3f1_ref, w3b1_ref, decay, fwd_s, hf_s.at[1])

    @pl.when(first_step)
    def _():
        inv_copy.wait()

    chains = [slice(c * seq, (c + 1) * seq) for c in range(uv_ref.shape[0] // seq)]
    row = lax.broadcasted_iota(jnp.int32, (seq, 1), 0)
    first, last = row == 0, row == seq - 1

    def short_conv(u_ref, w_ref, b_ref):
        outs = []
        for rows in chains:
            u = u_ref[rows, :]
            prev = jnp.where(first, 0.0, pltpu.roll(u, 1, axis=0))
            nxt = jnp.where(last, 0.0, pltpu.roll(u, seq - 1, axis=0))
            outs.append(prev * w_ref[0:1, :] + u * w_ref[1:2, :] + nxt * w_ref[2:3, :] + b_ref[...])
        return outs

    def long_conv(us, order):
        h_ref = hf_s.at[order]
        bias = hb_ref[order:order + 1, :]
        specs = [jnp.dot(fwd_s[...], u.astype(BF16), preferred_element_type=F32) for u in us]
        ys = []
        for spec in specs:
            ure, uim = spec[:seq], spec[seq:]
            yre = ure * h_ref[0] - uim * h_ref[1]
            yim = ure * h_ref[1] + uim * h_ref[2]
            ys.append(jnp.concatenate([yre, yim], axis=0).astype(BF16))
        return [jnp.dot(inv_s[...], y, preferred_element_type=F32) + u * bias
                for y, u in zip(ys, us)]

    v = short_conv(uv_ref, wv_ref, bv_ref)
    x1 = short_conv(u1_ref, w1_ref, b1_ref)
    z = [a * b for a, b in zip(x1, long_conv(v, 0))]
    x2 = short_conv(u2_ref, w2_ref, b2_ref)
    for rows, a, b in zip(chains, x2, long_conv(z, 1)):
        o_ref[rows, :] = (a * b).astype(BF16)


def _hyena_call(u, conv_w, conv_b, hy_bias, filt, fwd, inv, *, batch, seq, ct, n_chain):
    emb, w1, b1, w2, b2, fr, w3, tcol, deltas = filt
    nct = D_HYENA // ct
    rows = n_chain * seq
    col = lambda part: (lambda c, b: (b, part * nct + c))
    wcol = lambda part: (lambda c, b: (0, part * nct + c))
    small = lambda shape: pl.BlockSpec(shape, lambda c, b: (0, 0))
    w3col = lambda direction, order: (lambda c, b: (0, (direction * HYENA_ORDER + order) * nct + c))
    in_specs = ([pl.BlockSpec((rows, ct), col(p)) for p in range(3)]
                + [pl.BlockSpec((3, ct), wcol(p)) for p in range(3)]
                + [pl.BlockSpec((1, ct), wcol(p)) for p in range(3)]
                + [pl.BlockSpec((HYENA_ORDER, ct), lambda c, b: (0, c)),
                   small((seq, FILT_PAD)), small((FILT_PAD, FILT_PAD)), small((1, FILT_PAD)),
                   small((FILT_PAD, FILT_PAD)), small((1, FILT_PAD)), small((1, FILT_PAD))]
                + [pl.BlockSpec((FILT_PAD, ct), w3col(direction, order))
                   for order in range(HYENA_ORDER) for direction in range(2)]
                + [small((seq, 1)), pl.BlockSpec((1, ct), lambda c, b: (0, c)),
                   pl.BlockSpec(memory_space=pl.ANY), pl.BlockSpec(memory_space=pl.ANY)])
    return pl.pallas_call(
        _hyena_kernel,
        grid=(nct, batch // n_chain),
        in_specs=in_specs,
        out_specs=pl.BlockSpec((rows, ct), lambda c, b: (b, c)),
        out_shape=jax.ShapeDtypeStruct((batch * seq, D_HYENA), BF16),
        scratch_shapes=[pltpu.VMEM((2 * seq, seq), BF16), pltpu.VMEM((seq, 2 * seq), BF16),
                        pltpu.VMEM((seq, FILT_PAD), F32),
                        pltpu.VMEM((HYENA_ORDER, 3, seq, ct), F32),
                        pltpu.SemaphoreType.DMA((2,))],
        compiler_params=_params("arbitrary", "arbitrary"),
        name=f"hyena_{seq}",
    )(u, u, u, conv_w, conv_w, conv_w, conv_b, conv_b, conv_b, hy_bias,
      emb, w1, b1, w2, b2, fr, w3, w3, w3, w3, tcol, deltas, fwd, inv)


FF_CHUNK = D_MODEL
N_FF_CHUNKS = D_FF // FF_CHUNK


def _post_kernel(x_ref, oa_ref, oh_ref, mod_ref, g1_ref, g2_ref, gf_ref,
                 win_hbm, wa_hbm, wh_hbm, wo_hbm, wu_hbm, wd_hbm, y_ref,
                 wg_s, wa_s, wh_s, wo_s, wu_s, wd_s, w_sem):
    ff = [slice(c * FF_CHUNK, (c + 1) * FF_CHUNK) for c in range(N_FF_CHUNKS)]
    copies = ([(win_hbm.at[:, GATE_OFF:], wg_s), (wa_hbm, wa_s), (wh_hbm, wh_s), (wo_hbm, wo_s)]
              + [piece for sl in ff for piece in ((wu_hbm.at[:, sl], wu_s.at[:, sl]),
                                                  (wd_hbm.at[sl, :], wd_s.at[sl, :]))])

    def w_copy(n):
        src, dst = copies[n]
        return pltpu.make_async_copy(src, dst, w_sem.at[n])

    def body(first_step):
        if first_step:
            for n in range(len(copies)):
                w_copy(n).start()

        def ready(n):
            if first_step:
                w_copy(n).wait()

        d = functools.partial(jnp.dot, preferred_element_type=F32)
        x = x_ref[...]
        h1 = (_rms(x, g1_ref[...]) * (1.0 + mod_ref[1:2, :]) + mod_ref[0:1, :]).astype(BF16)
        ready(0)
        gates = jax.nn.sigmoid(d(h1, wg_s[...]))
        ready(1)
        attn = d(oa_ref[...], wa_s[...])
        ready(2)
        merged = gates[:, :D_MODEL] * attn + gates[:, D_MODEL:] * d(oh_ref[...], wh_s[...])
        ready(3)
        x = x + mod_ref[2:3, :] * d(merged.astype(BF16), wo_s[...])
        h2 = (_rms(x, g2_ref[...]) * (1.0 + mod_ref[4:5, :]) + mod_ref[3:4, :]).astype(BF16)
        acc = jnp.zeros_like(x)
        for c, sl in enumerate(ff):
            ready(4 + 2 * c)
            up = jnp.maximum(d(h2, wu_s[:, sl]), 0.0)
            ready(5 + 2 * c)
            acc = acc + d((up * up).astype(BF16), wd_s[sl, :])
        x = x + mod_ref[5:6, :] * acc
        y_ref[...] = _rms(x, gf_ref[...])

    step = pl.program_id(0)
    pl.when(step == 0)(functools.partial(body, True))
    pl.when(step != 0)(functools.partial(body, False))


def _post_call(x, o_attn, o_hy, mod, g1, g2, gf, w_in, wa, wh, wo, wu, wd, *, seq, latent, tm):
    t = x.shape[0]
    per_seq = seq // tm if latent else 1
    row = lambda i: (i, 0)

    def mod_idx(i):
        return (1 + i // per_seq if latent else 0, 0, 0)

    return pl.pallas_call(
        _post_kernel,
        grid=(t // tm,),
        in_specs=[pl.BlockSpec((tm, D_MODEL), row),
                  pl.BlockSpec((tm, N_HEADS * V_DIM), row),
                  pl.BlockSpec((tm, D_HYENA), row),
                  pl.BlockSpec((None, 6, D_MODEL), mod_idx),
                  _resident((1, D_MODEL)), _resident((1, D_MODEL)), _resident((1, D_MODEL))]
                 + [pl.BlockSpec(memory_space=pl.ANY)] * 6,
        out_specs=pl.BlockSpec((tm, D_MODEL), row),
        out_shape=jax.ShapeDtypeStruct((t, D_MODEL), F32),
        scratch_shapes=[pltpu.VMEM((D_MODEL, GATE_COLS), BF16)]
                       + [pltpu.VMEM(w.shape, BF16) for w in (wa, wh, wo, wu, wd)]
                       + [pltpu.SemaphoreType.DMA((4 + 2 * N_FF_CHUNKS,))],
        compiler_params=_params("arbitrary"),
        name="post_lat" if latent else "post_ctx",
    )(x, o_attn, o_hy, mod, g1, g2, gf, w_in, wa, wh, wo, wu, wd)


def _rope_tables(seq):
    half = HEAD_DIM // 2
    n = half // 2
    inv = ROPE_THETA ** (-np.arange(n, dtype=np.float64) / n)
    pos = np.arange(seq)
    ang_row = (pos // GRID_W).astype(np.float64)[:, None] * inv[None, :]
    ang_col = (pos % GRID_W).astype(np.float64)[:, None] * inv[None, :]
    zeros = np.zeros_like(ang_row)

    def per_map(a_row, a_col, lo, hi):
        return np.concatenate([lo(a_row), hi(a_row), lo(a_col), hi(a_col)], axis=-1)

    cos = per_map(ang_row, ang_col, np.cos, np.cos)
    sin_lo = per_map(ang_row, ang_col, lambda a: -np.sin(a), lambda a: zeros)
    sin_hi = per_map(ang_row, ang_col, lambda a: zeros, np.sin)
    return tuple(jnp.asarray(np.concatenate([tab, tab], axis=-1).astype(np.float32))
                 for tab in (cos, sin_lo, sin_hi))


def _filter_embedding(seq):
    bands = (FILTER_EMB - 1) // 2
    t = np.linspace(0.0, 1.0, seq)[:, None]
    wpos = 2.0 * np.pi * np.arange(seq, dtype=np.float64)[:, None] / seq
    f = np.linspace(1e-4, bands - 1, bands)[None, :]
    emb = np.concatenate([t, np.cos(f * wpos), -np.sin(f * wpos)], axis=-1)
    emb = np.pad(emb, ((0, 0), (0, FILT_PAD - FILTER_EMB)))
    return jnp.asarray(emb.astype(np.float32)), jnp.asarray(t.astype(np.float32))


def _tiles(seq):
    short = seq <= 256
    return dict(tm=512, tq=128, attn_heads=N_HEADS if short else 4, ct=MXU_WIDTH,
                hyena_seqs=4 if short else 2)


def _pad_to(x, rows, cols):
    return jnp.pad(x, ((0, rows - x.shape[0]), (0, cols - x.shape[1])))


def kernel(x_prompt, x_sample, cache_k, cache_v, c, c_ctx, w_ada, b_ada, norm1_g, norm2_g, w_in,
           lam_q1, lam_k1, lam_q2, lam_k2, attn_subln_g, conv_w, conv_b, filt_w1, filt_b1,
           filt_w2, filt_b2, filt_w3, filt_freq, hy_bias, w_br_attn, w_br_hy, w_out, w_up,
           w_down, final_g):
    depth = w_in.shape[0]
    assert depth == 1, "single trunk layer"
    layer = 0
    lam_init = 0.8 - 0.6 * math.exp(-0.3 * layer)
    n_ctx, ctx_len, _ = x_prompt.shape
    n_lat, lat_len, _ = x_sample.shape
    past = cache_k.shape[2]

    cvec = jnp.concatenate([c_ctx[None, :], c], axis=0)
    cvec = jnp.pad(cvec, ((0, MOD_ROWS - cvec.shape[0]), (0, 0)))
    mod = _mod_call(cvec, w_ada[layer], b_ada[layer][None, :]).reshape(MOD_ROWS, 6, D_MODEL)

    w_in_b = w_in[layer].astype(BF16)
    post_w_f32 = tuple(w[layer] for w in (w_br_attn, w_br_hy, w_out, w_up, w_down))
    g1, g2, gf = norm1_g[layer][None, :], norm2_g[layer][None, :], final_g[None, :]
    lam_vecs = jnp.stack([lam_q1[layer], lam_k1[layer], lam_q2[layer], lam_k2[layer]])
    subln_g = attn_subln_g[layer][None, :]

    w1 = _pad_to(filt_w1[layer], FILT_PAD, FILT_PAD)
    w2 = _pad_to(filt_w2[layer], FILT_PAD, FILT_PAD)
    w3 = _pad_to(filt_w3[layer], FILT_PAD, filt_w3.shape[2])
    b1 = _pad_to(filt_b1[layer][None, :], 1, FILT_PAD)
    b2 = _pad_to(filt_b2[layer][None, :], 1, FILT_PAD)
    fr = _pad_to(filt_freq[layer][None, :], 1, FILT_PAD)
    deltas = jnp.asarray(np.linspace(math.log(DECAY_TARGET) / FAST_DECAY_PCT,
                                     math.log(DECAY_TARGET) / SLOW_DECAY_PCT,
                                     D_HYENA)[None, :].astype(np.float32))

    def project(x3, batch, seq, latent, cast=()):
        x = x3.reshape(batch * seq, D_MODEL)
        rope = _rope_tables(seq) if latent else None
        return x, _in_proj_call(x, mod, g1, w_in_b, rope, seq=seq, latent=latent,
                                tm=_tiles(seq)["tm"], cast=cast)

    def mix(x, outs, post_w, batch, seq, latent):
        tiles = _tiles(seq)
        tm = tiles["tm"]
        wa, wh, wo, wu, wd = post_w
        if latent:
            q, kt, v, u = outs
            ckt = jnp.transpose(cache_k[:, layer], (0, 2, 3, 4, 1)).reshape(batch * K_COLS, past)
            cache = (ckt, cache_v[:, layer].reshape(batch * past * N_HEADS, V_DIM))
            kf = vf = None
        else:
            q, kt, v, kf, vf, u = outs
            cache = None
        o_attn = _attn_call(q, kt, v, cache, lam_vecs, subln_g, batch=batch, seq=seq,
                            lam_init=lam_init, tq=tiles["tq"], heads=tiles["attn_heads"])

        fwd_np, inv_np = _dft_tables(seq)
        fwd_b = jnp.asarray(fwd_np).astype(BF16)
        inv_b = jnp.asarray(inv_np).astype(BF16)
        emb, tcol = _filter_embedding(seq)
        filt = (emb, w1, b1, w2, b2, fr, w3, tcol, deltas)
        o_hy = _hyena_call(u, conv_w[layer], conv_b[layer][None, :], hy_bias[layer], filt,
                           fwd_b, inv_b, batch=batch, seq=seq, ct=tiles["ct"],
                           n_chain=tiles["hyena_seqs"])

        y = _post_call(x, o_attn, o_hy, mod, g1, g2, gf, w_in_b, wa, wh, wo, wu, wd,
                       seq=seq, latent=latent, tm=tm)
        return y.reshape(batch, seq, D_MODEL), kf, vf

    x_ctx, ctx_outs = project(x_prompt, n_ctx, ctx_len, False)
    x_lat, lat_outs = project(x_sample, n_lat, lat_len, True, cast=post_w_f32)
    lat_outs, post_w = lat_outs[:-len(post_w_f32)], lat_outs[-len(post_w_f32):]
    y_prompt, kf, vf = mix(x_ctx, ctx_outs, post_w, n_ctx, ctx_len, False)
    y_sample, _, _ = mix(x_lat, lat_outs, post_w, n_lat, lat_len, True)
    new_cache_k = jnp.transpose(kf.reshape(n_ctx, depth, N_HEADS, 2, HEAD_DIM, ctx_len),
                                (0, 1, 5, 2, 3, 4))
    new_cache_v = vf.reshape(n_ctx, depth, ctx_len, N_HEADS, V_DIM)
    return (y_prompt, y_sample, new_cache_k, new_cache_v)
```

```python
import functools
import math

import numpy as np
import jax
import jax.numpy as jnp
from jax import lax
from jax.experimental import pallas as pl
from jax.experimental.pallas import tpu as pltpu

D_MODEL = 1024
GRID_W = 64
N_HEADS = 8
HEAD_DIM = 64
V_DIM = 2 * HEAD_DIM
D_HYENA = D_MODEL // 2
HYENA_ORDER = 2
FILTER_EMB = 33
FILTER_HIDDEN = 64
D_FF = 4 * D_MODEL
ROPE_THETA = 10000.0
EPS = 1e-6
LOG2_E = math.log2(math.e)
DECAY_TARGET = 1e-2
FAST_DECAY_PCT = 0.3
SLOW_DECAY_PCT = 1.5

Q_COLS = N_HEADS * 2 * HEAD_DIM
K_COLS = N_HEADS * 2 * HEAD_DIM
V_COLS = N_HEADS * V_DIM
HY_COLS = 3 * D_HYENA
GATE_COLS = 2 * D_MODEL
Q_OFF = 0
K_OFF = Q_OFF + Q_COLS
V_OFF = K_OFF + K_COLS
HY_OFF = V_OFF + V_COLS
GATE_OFF = HY_OFF + HY_COLS
IN_COLS = GATE_OFF + GATE_COLS

LANES = 128
MXU_WIDTH = 256
MOD_ROWS = 16
FILT_PAD = 128
VMEM_LIMIT = 56 * 1024 * 1024

BF16 = jnp.bfloat16
F32 = jnp.float32


def _resident(shape):
    nd = len(shape)
    return pl.BlockSpec(shape, lambda *_: (0,) * nd, pipeline_mode=pl.Buffered(1))


def _params(*sem):
    return pltpu.CompilerParams(dimension_semantics=sem, vmem_limit_bytes=VMEM_LIMIT)


def _rms(x, g):
    return x * lax.rsqrt(jnp.mean(x * x, axis=-1, keepdims=True) + EPS) * g


def _mod_kernel(c_ref, w_ref, b_ref, o_ref):
    c = c_ref[...]
    s = (c * jax.nn.sigmoid(c)).astype(BF16)
    o_ref[...] = jnp.dot(s, w_ref[...].astype(BF16), preferred_element_type=F32) + b_ref[...]


def _mod_call(cvec, w_ada, b_ada):
    tn = 1024
    n = w_ada.shape[1]
    return pl.pallas_call(
        _mod_kernel,
        grid=(n // tn,),
        in_specs=[pl.BlockSpec((MOD_ROWS, D_MODEL), lambda j: (0, 0)),
                  pl.BlockSpec((D_MODEL, tn), lambda j: (0, j)),
                  pl.BlockSpec((1, tn), lambda j: (0, j))],
        out_specs=pl.BlockSpec((MOD_ROWS, tn), lambda j: (0, j)),
        out_shape=jax.ShapeDtypeStruct((MOD_ROWS, n), F32),
        compiler_params=_params("arbitrary"),
        name="mod",
    )(cvec, w_ada, b_ada)


def _rope(x, cos, sin_lo, sin_hi):
    return (x * cos + pltpu.roll(x, 16, axis=1) * sin_hi
            + pltpu.roll(x, LANES - 16, axis=1) * sin_lo)


def _in_proj_kernel(*refs, latent, n_cast):
    refs = list(refs)
    take = lambda n: [refs.pop(0) for _ in range(n)]
    x_ref, mod_ref, g_ref, w_ref = take(4)
    if latent:
        cos_ref, slo_ref, shi_ref = take(3)
    cast_src = take(n_cast)
    q_ref, kt_ref, v_ref = take(3)
    if not latent:
        ktf_ref, vf_ref = take(2)
    (u_ref,) = take(1)
    cast_dst = take(n_cast)
    tm = x_ref.shape[0]
    x = x_ref[...]
    h = _rms(x, g_ref[...]) * (1.0 + mod_ref[1:2, :]) + mod_ref[0:1, :]
    hb = h.astype(BF16)

    def proj(off, width):
        return jnp.dot(hb, w_ref[:, off:off + width], preferred_element_type=F32)

    q = proj(Q_OFF, Q_COLS) * (HEAD_DIM ** -0.5 * LOG2_E)
    k = proj(K_OFF, K_COLS)
    if latent:
        cos, slo, shi = cos_ref[...], slo_ref[...], shi_ref[...]
        k_heads = []
        for hd in range(N_HEADS):
            sl = slice(hd * LANES, (hd + 1) * LANES)
            q_ref[:, sl] = _rope(q[:, sl], cos, slo, shi).astype(BF16)
            k_heads.append(_rope(k[:, sl], cos, slo, shi))
        kt_ref[...] = jnp.concatenate(k_heads, axis=1).T.astype(BF16)
    else:
        q_ref[...] = q.astype(BF16)
        seq = kt_ref.shape[1]
        for s in range(tm // seq):
            kt = k[s * seq:(s + 1) * seq, :].T
            rows = slice(s * K_COLS, (s + 1) * K_COLS)
            ktf_ref[rows, :] = kt
            kt_ref[rows, :] = kt.astype(BF16)
    v = proj(V_OFF, V_COLS)
    v_ref[...] = v.astype(BF16)
    if not latent:
        for hd in range(N_HEADS):
            vf_ref[pl.ds(hd, tm, stride=N_HEADS), :] = v[:, hd * LANES:(hd + 1) * LANES]
    u_ref[...] = proj(HY_OFF, HY_COLS)
    for src, dst in zip(cast_src, cast_dst):
        dst[...] = src[...].astype(BF16)


def _in_proj_call(x, mod, g1, w_in, rope, *, seq, latent, tm, cast=()):
    t = x.shape[0]
    n_steps = t // tm
    per_seq = seq // tm if latent else 1

    def mod_idx(i):
        return (1 + i // per_seq if latent else 0, 0, 0)

    row = lambda i: (i, 0)
    in_specs = [pl.BlockSpec((tm, D_MODEL), row),
                pl.BlockSpec((None, 6, D_MODEL), mod_idx),
                _resident((1, D_MODEL)),
                _resident((D_MODEL, GATE_OFF))]
    args = [x, mod, g1, w_in]
    tok = lambda width, dtype: ((t, width), dtype, pl.BlockSpec((tm, width), row))
    if latent:
        in_specs += [pl.BlockSpec((tm, LANES), lambda i: (i % per_seq, 0))] * 3
        args += list(rope)
        kt_spec = pl.BlockSpec((K_COLS, tm), lambda i: (i // per_seq, i % per_seq))
    else:
        kt_spec = pl.BlockSpec((tm // seq * K_COLS, seq), row)
    kt_shape = (t // seq * K_COLS, seq)
    outs = [tok(Q_COLS, BF16), (kt_shape, BF16, kt_spec), tok(V_COLS, BF16)]
    if not latent:
        outs += [(kt_shape, F32, kt_spec),
                 ((t * N_HEADS, LANES), F32, pl.BlockSpec((tm * N_HEADS, LANES), row))]
    outs += [tok(HY_COLS, F32)]
    for w in cast:
        slab = pl.BlockSpec((w.shape[0] // n_steps, w.shape[1]), row)
        in_specs.append(slab)
        args.append(w)
        outs.append((w.shape, BF16, slab))
    return pl.pallas_call(
        functools.partial(_in_proj_kernel, latent=latent, n_cast=len(cast)),
        grid=(n_steps,),
        in_specs=in_specs,
        out_specs=[spec for _, _, spec in outs],
        out_shape=[jax.ShapeDtypeStruct(s, d) for s, d, _ in outs],
        compiler_params=_params("arbitrary"),
        name="in_proj_lat" if latent else "in_proj_ctx",
    )(*args)


def _attn_kernel(*refs, n_cache, lam_init, tq):
    if n_cache:
        (q_ref, kt_ref, v_ref, ckt_ref, cv_ref, lam_ref, g_ref, o_ref, kt_s, v2_s) = refs
    else:
        (q_ref, kt_ref, v_ref, lam_ref, g_ref, o_ref, v2_s) = refs
    n_heads, n_keys, _ = v2_s.shape
    head0 = pl.program_id(1) * n_heads
    lv = lam_ref[...]
    lam = (jnp.exp(jnp.sum(lv[0:1] * lv[1:2], axis=-1, keepdims=True))
           - jnp.exp(jnp.sum(lv[2:3] * lv[3:4], axis=-1, keepdims=True)) + lam_init)
    gain = g_ref[...] * (1.0 - lam_init)
    first_map = lax.broadcasted_iota(jnp.int32, (1, LANES), 1) < HEAD_DIM
    zero = jnp.zeros((tq, LANES), BF16)

    def operands(hd):
        cols = slice(hd * LANES, (hd + 1) * LANES)
        v2_s[hd, :, LANES:] = jnp.ones((n_keys, LANES), BF16)
        if n_cache:
            kt_s[cols, 0:n_cache] = ckt_ref[cols, :].astype(BF16)
            kt_s[cols, n_cache:] = kt_ref[cols, :]
            cache_rows = pl.ds(head0 + hd, n_cache, stride=N_HEADS)
            v2_s[hd, 0:n_cache, 0:LANES] = cv_ref[cache_rows, :].astype(BF16)
            v2_s[hd, n_cache:, 0:LANES] = v_ref[:, cols]
            return kt_s[cols, :], v2_s[hd]
        v2_s[hd, :, 0:LANES] = v_ref[:, cols]
        return kt_ref[cols, :], v2_s[hd]

    def scores(hd, i, kt):
        q = q_ref[i * tq:(i + 1) * tq, hd * LANES:(hd + 1) * LANES]
        q2 = jnp.concatenate([jnp.where(first_map, q, zero), jnp.where(first_map, zero, q)], axis=0)
        s = jnp.dot(q2, kt, preferred_element_type=F32)
        return s, jnp.max(s, axis=-1, keepdims=True)

    def finish(hd, i, v2, s, m):
        p = jnp.exp2(s - m).astype(BF16)
        pv = jnp.dot(p, v2, preferred_element_type=F32)
        pv = pv[:, :LANES] / pv[:, LANES:]
        o = pv[:tq] - lam * pv[tq:]
        o_ref[i * tq:(i + 1) * tq, hd * LANES:(hd + 1) * LANES] = (_rms(o, 1.0) * gain).astype(BF16)

    n_tiles = q_ref.shape[0] // tq
    pending = None
    for hd in range(n_heads):
        kt, v2 = operands(hd)
        for i in range(n_tiles):
            cur = (hd, i, v2) + scores(hd, i, kt)
            if pending is not None:
                finish(*pending)
            pending = cur
    finish(*pending)


def _attn_call(q, kt, v, cache, lam_vecs, subln_g, *, batch, seq, lam_init, tq, heads):
    const = lambda b, h: (0, 0)
    width = heads * LANES
    n_groups = N_HEADS // heads
    in_specs = [pl.BlockSpec((seq, width), lambda b, h: (b, h)),
                pl.BlockSpec((width, seq), lambda b, h: (b * n_groups + h, 0)),
                pl.BlockSpec((seq, width), lambda b, h: (b, h))]
    args = [q, kt, v]
    n_cache = 0
    if cache is not None:
        ckt, cv = cache
        n_cache = ckt.shape[1]
        in_specs += [pl.BlockSpec((width, n_cache), lambda b, h: (b * n_groups + h, 0)),
                     pl.BlockSpec((n_cache * N_HEADS, LANES), lambda b, h: (b, 0))]
        args += [ckt, cv]
    n_keys = n_cache + seq
    scratch = [pltpu.VMEM((heads, n_keys, 2 * LANES), BF16)]
    if n_cache:
        scratch = [pltpu.VMEM((width, n_keys), BF16)] + scratch
    in_specs += [pl.BlockSpec(lam_vecs.shape, const), pl.BlockSpec((1, LANES), const)]
    args += [lam_vecs, subln_g]
    return pl.pallas_call(
        functools.partial(_attn_kernel, n_cache=n_cache, lam_init=lam_init, tq=tq),
        grid=(batch, n_groups),
        in_specs=in_specs,
        out_specs=pl.BlockSpec((seq, width), lambda b, h: (b, h)),
        out_shape=jax.ShapeDtypeStruct((batch * seq, N_HEADS * V_DIM), BF16),
        scratch_shapes=scratch,
        compiler_params=_params("arbitrary", "arbitrary"),
        name="attn_lat" if cache is not None else "attn_ctx",
    )(*args)


def _dft_tables(seq):
    n = 2 * seq
    f = np.arange(seq, dtype=np.int64)[:, None]
    t = np.arange(seq, dtype=np.int64)[None, :]
    ang = 2.0 * np.pi * ((f * t) % n).astype(np.float64) / n
    alt = np.where(np.arange(seq) % 2 == 0, 1.0, -1.0)
    cos, msin = np.cos(ang), -np.sin(ang)
    msin[0, :] = alt
    fwd = np.concatenate([cos, msin], axis=0)
    wgt = np.where(f == 0, 1.0, 2.0) / n
    inv_c = (cos * wgt).T
    inv_s = (msin * wgt).T
    inv = np.concatenate([inv_c, inv_s], axis=1)
    return fwd.astype(np.float32), inv.astype(np.float32)


def _filter_hidden(emb_ref, w1_ref, b1_ref, w2_ref, b2_ref, fr_ref):
    hp = functools.partial(jnp.dot, preferred_element_type=F32, precision=lax.Precision.HIGHEST)
    fr = fr_ref[...]
    h = jnp.sin(fr * (hp(emb_ref[...], w1_ref[...]) + b1_ref[...]))
    return jnp.sin(fr * (hp(h, w2_ref[...]) + b2_ref[...]))


def _filter_planes(h, w3f_ref, w3b_ref, decay, dft_ref, out_ref):
    seq = h.shape[0]
    fwd = jnp.dot(h, w3f_ref[...].astype(BF16), preferred_element_type=F32) * decay
    bwd = jnp.dot(h, w3b_ref[...].astype(BF16), preferred_element_type=F32) * decay
    hsum, hdif = fwd + bwd, fwd - bwd
    re = jnp.dot(dft_ref[0:seq, :], hsum.astype(BF16), preferred_element_type=F32)
    im = jnp.dot(dft_ref[seq:, :], hdif.astype(BF16), preferred_element_type=F32)
    row = lax.broadcasted_iota(jnp.int32, (seq, 1), 0)
    alt = jnp.where(row % 2 == 0, 1.0, -1.0)
    nyq = jnp.sum(hsum * alt, axis=0, keepdims=True)
    out_ref[0] = re
    out_ref[1] = jnp.where(row == 0, 0.0, im)
    out_ref[2] = jnp.where(row == 0, nyq, re)


def _hyena_kernel(uv_ref, u1_ref, u2_ref, wv_ref, w1_ref, w2_ref, bv_ref, b1_ref, b2_ref, hb_ref,
                  emb_ref, fw1_ref, fb1_ref, fw2_ref, fb2_ref, fr_ref,
                  w3f0_ref, w3b0_ref, w3f1_ref, w3b1_ref, t_ref, dl_ref, fwd_s, inv_s,
                  o_ref, hid_s, hf_s):
    seq = emb_ref.shape[0]
    new_tile = pl.program_id(1) == 0

    @pl.when(jnp.logical_and(pl.program_id(0) == 0, new_tile))
    def _():
        hid_s[...] = _filter_hidden(emb_ref, fw1_ref, fb1_ref, fw2_ref, fb2_ref, fr_ref)

    @pl.when(new_tile)
    def _():
        h = hid_s[...].astype(BF16)
        decay = jnp.exp(-t_ref[...] * jnp.abs(dl_ref[...]))
        _filter_planes(h, w3f0_ref, w3b0_ref, decay, fwd_s, hf_s.at[0])
        _filter_planes(h, w3f1_ref, w3b1_ref, decay, fwd_s, hf_s.at[1])

    chains = [slice(c * seq, (c + 1) * seq) for c in range(uv_ref.shape[0] // seq)]
    row = lax.broadcasted_iota(jnp.int32, (seq, 1), 0)
    first, last = row == 0, row == seq - 1

    def short_conv(u_ref, w_ref, b_ref):
        outs = []
        for rows in chains:
            u = u_ref[rows, :]
            prev = jnp.where(first, 0.0, pltpu.roll(u, 1, axis=0))
            nxt = jnp.where(last, 0.0, pltpu.roll(u, seq - 1, axis=0))
            outs.append(prev * w_ref[0:1, :] + u * w_ref[1:2, :] + nxt * w_ref[2:3, :] + b_ref[...])
        return outs

    def long_conv(us, order):
        h_ref = hf_s.at[order]
        bias = hb_ref[order:order + 1, :]
        specs = [jnp.dot(fwd_s[...], u.astype(BF16), preferred_element_type=F32) for u in us]
        ys = []
        for spec in specs:
            ure, uim = spec[:seq], spec[seq:]
            yre = ure * h_ref[0] - uim * h_ref[1]
            yim = ure * h_ref[1] + uim * h_ref[2]
            ys.append(jnp.concatenate([yre, yim], axis=0).astype(BF16))
        return [jnp.dot(inv_s[...], y, preferred_element_type=F32) + u * bias
                for y, u in zip(ys, us)]

    v = short_conv(uv_ref, wv_ref, bv_ref)
    x1 = short_conv(u1_ref, w1_ref, b1_ref)
    z = [a * b for a, b in zip(x1, long_conv(v, 0))]
    x2 = short_conv(u2_ref, w2_ref, b2_ref)
    for rows, a, b in zip(chains, x2, long_conv(z, 1)):
        o_ref[rows, :] = (a * b).astype(BF16)


def _hyena_call(u, conv_w, conv_b, hy_bias, filt, fwd, inv, *, batch, seq, ct, n_chain):
    emb, w1, b1, w2, b2, fr, w3, tcol, deltas = filt
    nct = D_HYENA // ct
    rows = n_chain * seq
    col = lambda part: (lambda c, b: (b, part * nct + c))
    wcol = lambda part: (lambda c, b: (0, part * nct + c))
    small = lambda shape: pl.BlockSpec(shape, lambda c, b: (0, 0))
    w3col = lambda direction, order: (lambda c, b: (0, (direction * HYENA_ORDER + order) * nct + c))
    in_specs = ([pl.BlockSpec((rows, ct), col(p)) for p in range(3)]
                + [pl.BlockSpec((3, ct), wcol(p)) for p in range(3)]
                + [pl.BlockSpec((1, ct), wcol(p)) for p in range(3)]
                + [pl.BlockSpec((HYENA_ORDER, ct), lambda c, b: (0, c)),
                   small((seq, FILT_PAD)), small((FILT_PAD, FILT_PAD)), small((1, FILT_PAD)),
                   small((FILT_PAD, FILT_PAD)), small((1, FILT_PAD)), small((1, FILT_PAD))]
                + [pl.BlockSpec((FILT_PAD, ct), w3col(direction, order))
                   for order in range(HYENA_ORDER) for direction in range(2)]
                + [small((seq, 1)), pl.BlockSpec((1, ct), lambda c, b: (0, c)),
                   _resident((2 * seq, seq)), _resident((seq, 2 * seq))])
    return pl.pallas_call(
        _hyena_kernel,
        grid=(nct, batch // n_chain),
        in_specs=in_specs,
        out_specs=pl.BlockSpec((rows, ct), lambda c, b: (b, c)),
        out_shape=jax.ShapeDtypeStruct((batch * seq, D_HYENA), BF16),
        scratch_shapes=[pltpu.VMEM((seq, FILT_PAD), F32),
                        pltpu.VMEM((HYENA_ORDER, 3, seq, ct), F32)],
        compiler_params=_params("arbitrary", "arbitrary"),
        name=f"hyena_{seq}",
    )(u, u, u, conv_w, conv_w, conv_w, conv_b, conv_b, conv_b, hy_bias,
      emb, w1, b1, w2, b2, fr, w3, w3, w3, w3, tcol, deltas, fwd, inv)


FF_CHUNK = D_MODEL
N_FF_CHUNKS = D_FF // FF_CHUNK


def _post_kernel(x_ref, oa_ref, oh_ref, mod_ref, g1_ref, g2_ref, gf_ref,
                 wg_ref, wa_ref, wh_ref, wo_ref, wu_ref, wd_ref, y_ref):
    d = functools.partial(jnp.dot, preferred_element_type=F32)
    x = x_ref[...]
    h1 = (_rms(x, g1_ref[...]) * (1.0 + mod_ref[1:2, :]) + mod_ref[0:1, :]).astype(BF16)
    gates = jax.nn.sigmoid(d(h1, wg_ref[...]))
    merged = (gates[:, :D_MODEL] * d(oa_ref[...], wa_ref[...])
              + gates[:, D_MODEL:] * d(oh_ref[...], wh_ref[...]))
    x = x + mod_ref[2:3, :] * d(merged.astype(BF16), wo_ref[...])
    h2 = (_rms(x, g2_ref[...]) * (1.0 + mod_ref[4:5, :]) + mod_ref[3:4, :]).astype(BF16)
    acc = jnp.zeros_like(x)
    for c in range(N_FF_CHUNKS):
        sl = slice(c * FF_CHUNK, (c + 1) * FF_CHUNK)
        up = jnp.maximum(d(h2, wu_ref[:, sl]), 0.0)
        acc = acc + d((up * up).astype(BF16), wd_ref[sl, :])
    x = x + mod_ref[5:6, :] * acc
    y_ref[...] = _rms(x, gf_ref[...])


def _post_call(x, o_attn, o_hy, mod, g1, g2, gf, weights, *, seq, latent, tm):
    t = x.shape[0]
    per_seq = seq // tm if latent else 1
    row = lambda i: (i, 0)

    def mod_idx(i):
        return (1 + i // per_seq if latent else 0, 0, 0)

    return pl.pallas_call(
        _post_kernel,
        grid=(t // tm,),
        in_specs=[pl.BlockSpec((tm, D_MODEL), row),
                  pl.BlockSpec((tm, N_HEADS * V_DIM), row),
                  pl.BlockSpec((tm, D_HYENA), row),
                  pl.BlockSpec((None, 6, D_MODEL), mod_idx),
                  _resident((1, D_MODEL)), _resident((1, D_MODEL)), _resident((1, D_MODEL))]
                 + [_resident(w.shape) for w in weights],
        out_specs=pl.BlockSpec((tm, D_MODEL), row),
        out_shape=jax.ShapeDtypeStruct((t, D_MODEL), F32),
        compiler_params=_params("arbitrary"),
        name="post_lat" if latent else "post_ctx",
    )(x, o_attn, o_hy, mod, g1, g2, gf, *weights)


def _rope_tables(seq):
    half = HEAD_DIM // 2
    n = half // 2
    inv = ROPE_THETA ** (-np.arange(n, dtype=np.float64) / n)
    pos = np.arange(seq)
    ang_row = (pos // GRID_W).astype(np.float64)[:, None] * inv[None, :]
    ang_col = (pos % GRID_W).astype(np.float64)[:, None] * inv[None, :]
    zeros = np.zeros_like(ang_row)

    def per_map(a_row, a_col, lo, hi):
        return np.concatenate([lo(a_row), hi(a_row), lo(a_col), hi(a_col)], axis=-1)

    cos = per_map(ang_row, ang_col, np.cos, np.cos)
    sin_lo = per_map(ang_row, ang_col, lambda a: -np.sin(a), lambda a: zeros)
    sin_hi = per_map(ang_row, ang_col, lambda a: zeros, np.sin)
    return tuple(jnp.asarray(np.concatenate([tab, tab], axis=-1).astype(np.float32))
                 for tab in (cos, sin_lo, sin_hi))


def _filter_embedding(seq):
    bands = (FILTER_EMB - 1) // 2
    t = np.linspace(0.0, 1.0, seq)[:, None]
    wpos = 2.0 * np.pi * np.arange(seq, dtype=np.float64)[:, None] / seq
    f = np.linspace(1e-4, bands - 1, bands)[None, :]
    emb = np.concatenate([t, np.cos(f * wpos), -np.sin(f * wpos)], axis=-1)
    emb = np.pad(emb, ((0, 0), (0, FILT_PAD - FILTER_EMB)))
    return jnp.asarray(emb.astype(np.float32)), jnp.asarray(t.astype(np.float32))


def _tiles(seq):
    short = seq <= 256
    return dict(tm=512, tq=128, attn_heads=N_HEADS if short else 4, ct=MXU_WIDTH,
                hyena_seqs=4 if short else 2)


def _pad_to(x, rows, cols):
    return jnp.pad(x, ((0, rows - x.shape[0]), (0, cols - x.shape[1])))


def kernel(x_prompt, x_sample, cache_k, cache_v, c, c_ctx, w_ada, b_ada, norm1_g, norm2_g, w_in,
           lam_q1, lam_k1, lam_q2, lam_k2, attn_subln_g, conv_w, conv_b, filt_w1, filt_b1,
           filt_w2, filt_b2, filt_w3, filt_freq, hy_bias, w_br_attn, w_br_hy, w_out, w_up,
           w_down, final_g):
    depth = w_in.shape[0]
    assert depth == 1, "single trunk layer"
    layer = 0
    lam_init = 0.8 - 0.6 * math.exp(-0.3 * layer)
    n_ctx, ctx_len, _ = x_prompt.shape
    n_lat, lat_len, _ = x_sample.shape
    past = cache_k.shape[2]

    cvec = jnp.concatenate([c_ctx[None, :], c], axis=0)
    cvec = jnp.pad(cvec, ((0, MOD_ROWS - cvec.shape[0]), (0, 0)))
    mod = _mod_call(cvec, w_ada[layer], b_ada[layer][None, :]).reshape(MOD_ROWS, 6, D_MODEL)

    w_qkvu_b = w_in[layer][:, :GATE_OFF].astype(BF16)
    w_gate_b = w_in[layer][:, GATE_OFF:].astype(BF16)
    post_w_f32 = tuple(w[layer] for w in (w_br_attn, w_br_hy, w_out, w_up, w_down))
    g1, g2, gf = norm1_g[layer][None, :], norm2_g[layer][None, :], final_g[None, :]
    lam_vecs = jnp.stack([lam_q1[layer], lam_k1[layer], lam_q2[layer], lam_k2[layer]])
    subln_g = attn_subln_g[layer][None, :]

    w1 = _pad_to(filt_w1[layer], FILT_PAD, FILT_PAD)
    w2 = _pad_to(filt_w2[layer], FILT_PAD, FILT_PAD)
    w3 = _pad_to(filt_w3[layer], FILT_PAD, filt_w3.shape[2])
    b1 = _pad_to(filt_b1[layer][None, :], 1, FILT_PAD)
    b2 = _pad_to(filt_b2[layer][None, :], 1, FILT_PAD)
    fr = _pad_to(filt_freq[layer][None, :], 1, FILT_PAD)
    deltas = jnp.asarray(np.linspace(math.log(DECAY_TARGET) / FAST_DECAY_PCT,
                                     math.log(DECAY_TARGET) / SLOW_DECAY_PCT,
                                     D_HYENA)[None, :].astype(np.float32))

    def project(x3, batch, seq, latent, cast=()):
        x = x3.reshape(batch * seq, D_MODEL)
        rope = _rope_tables(seq) if latent else None
        return x, _in_proj_call(x, mod, g1, w_qkvu_b, rope, seq=seq, latent=latent,
                                tm=_tiles(seq)["tm"], cast=cast)

    def mix(x, outs, post_w, batch, seq, latent):
        tiles = _tiles(seq)
        tm = tiles["tm"]
        if latent:
            q, kt, v, u = outs
            ckt = jnp.transpose(cache_k[:, layer], (0, 2, 3, 4, 1)).reshape(batch * K_COLS, past)
            cache = (ckt, cache_v[:, layer].reshape(batch * past * N_HEADS, V_DIM))
            kf = vf = None
        else:
            q, kt, v, kf, vf, u = outs
            cache = None
        o_attn = _attn_call(q, kt, v, cache, lam_vecs, subln_g, batch=batch, seq=seq,
                            lam_init=lam_init, tq=tiles["tq"], heads=tiles["attn_heads"])

        fwd_np, inv_np = _dft_tables(seq)
        fwd_b = jnp.asarray(fwd_np).astype(BF16)
        inv_b = jnp.asarray(inv_np).astype(BF16)
        emb, tcol = _filter_embedding(seq)
        filt = (emb, w1, b1, w2, b2, fr, w3, tcol, deltas)
        o_hy = _hyena_call(u, conv_w[layer], conv_b[layer][None, :], hy_bias[layer], filt,
                           fwd_b, inv_b, batch=batch, seq=seq, ct=tiles["ct"],
                           n_chain=tiles["hyena_seqs"])

        y = _post_call(x, o_attn, o_hy, mod, g1, g2, gf, (w_gate_b, *post_w),
                       seq=seq, latent=latent, tm=tm)
        return y.reshape(batch, seq, D_MODEL), kf, vf

    x_ctx, ctx_outs = project(x_prompt, n_ctx, ctx_len, False)
    x_lat, lat_outs = project(x_sample, n_lat, lat_len, True, cast=post_w_f32)
    lat_outs, post_w = lat_outs[:-len(post_w_f32)], lat_outs[-len(post_w_f32):]
    y_prompt, kf, vf = mix(x_ctx, ctx_outs, post_w, n_ctx, ctx_len, False)
    y_sample, _, _ = mix(x_lat, lat_outs, post_w, n_lat, lat_len, True)
    new_cache_k = jnp.transpose(kf.reshape(n_ctx, depth, N_HEADS, 2, HEAD_DIM, ctx_len),
                                (0, 1, 5, 2, 3, 4))
    new_cache_v = vf.reshape(n_ctx, depth, ctx_len, N_HEADS, V_DIM)
    return (y_prompt, y_sample, new_cache_k, new_cache_v)
```

```python
import functools
import math

import numpy as np
import jax
import jax.numpy as jnp
from jax import lax
from jax.experimental import pallas as pl
from jax.experimental.pallas import tpu as pltpu

D_MODEL = 1024
GRID_W = 64
N_HEADS = 8
HEAD_DIM = 64
V_DIM = 2 * HEAD_DIM
D_HYENA = D_MODEL // 2
HYENA_ORDER = 2
FILTER_EMB = 33
FILTER_HIDDEN = 64
D_FF = 4 * D_MODEL
ROPE_THETA = 10000.0
EPS = 1e-6
LOG2_E = math.log2(math.e)
DECAY_TARGET = 1e-2
FAST_DECAY_PCT = 0.3
SLOW_DECAY_PCT = 1.5

Q_COLS = N_HEADS * 2 * HEAD_DIM
K_COLS = N_HEADS * 2 * HEAD_DIM
V_COLS = N_HEADS * V_DIM
HY_COLS = 3 * D_HYENA
GATE_COLS = 2 * D_MODEL
Q_OFF = 0
K_OFF = Q_OFF + Q_COLS
V_OFF = K_OFF + K_COLS
HY_OFF = V_OFF + V_COLS
GATE_OFF = HY_OFF + HY_COLS
IN_COLS = GATE_OFF + GATE_COLS
GATE_WINDOW = math.gcd(GATE_OFF, D_MODEL)

LANES = 128
MXU_WIDTH = 256
MOD_ROWS = 16
FILT_PAD = 128
VMEM_LIMIT = 56 * 1024 * 1024

BF16 = jnp.bfloat16
F32 = jnp.float32


def _resident(shape, index=None):
    index = (0,) * len(shape) if index is None else index
    return pl.BlockSpec(shape, lambda *_: index, pipeline_mode=pl.Buffered(1))


def _params(*sem):
    return pltpu.CompilerParams(dimension_semantics=sem, vmem_limit_bytes=VMEM_LIMIT)


def _rms(x, g):
    return x * lax.rsqrt(jnp.mean(x * x, axis=-1, keepdims=True) + EPS) * g


def _mod_kernel(c_ref, w_ref, b_ref, o_ref):
    c = c_ref[...]
    s = (c * jax.nn.sigmoid(c)).astype(BF16)
    o_ref[...] = jnp.dot(s, w_ref[...].astype(BF16), preferred_element_type=F32) + b_ref[...]


def _mod_call(cvec, w_ada, b_ada):
    tn = 1024
    n = w_ada.shape[1]
    return pl.pallas_call(
        _mod_kernel,
        grid=(n // tn,),
        in_specs=[pl.BlockSpec((MOD_ROWS, D_MODEL), lambda j: (0, 0)),
                  pl.BlockSpec((D_MODEL, tn), lambda j: (0, j)),
                  pl.BlockSpec((1, tn), lambda j: (0, j))],
        out_specs=pl.BlockSpec((MOD_ROWS, tn), lambda j: (0, j)),
        out_shape=jax.ShapeDtypeStruct((MOD_ROWS, n), F32),
        compiler_params=_params("arbitrary"),
        name="mod",
    )(cvec, w_ada, b_ada)


def _rope(x, cos, sin_lo, sin_hi):
    return (x * cos + pltpu.roll(x, 16, axis=1) * sin_hi
            + pltpu.roll(x, LANES - 16, axis=1) * sin_lo)


def _in_proj_kernel(*refs, latent, n_cast):
    refs = list(refs)
    take = lambda n: [refs.pop(0) for _ in range(n)]
    x_ref, mod_ref, g_ref, w_ref = take(4)
    if latent:
        cos_ref, slo_ref, shi_ref = take(3)
    cast_src = take(n_cast)
    q_ref, kt_ref, v_ref = take(3)
    if not latent:
        ktf_ref, vf_ref = take(2)
    (u_ref,) = take(1)
    cast_dst = take(n_cast)
    tm = x_ref.shape[0]
    x = x_ref[...]
    h = _rms(x, g_ref[...]) * (1.0 + mod_ref[1:2, :]) + mod_ref[0:1, :]
    hb = h.astype(BF16)

    def proj(off, width):
        return jnp.dot(hb, w_ref[:, off:off + width], preferred_element_type=F32)

    q = proj(Q_OFF, Q_COLS) * (HEAD_DIM ** -0.5 * LOG2_E)
    k = proj(K_OFF, K_COLS)
    if latent:
        cos, slo, shi = cos_ref[...], slo_ref[...], shi_ref[...]
        k_heads = []
        for hd in range(N_HEADS):
            sl = slice(hd * LANES, (hd + 1) * LANES)
            q_ref[:, sl] = _rope(q[:, sl], cos, slo, shi).astype(BF16)
            k_heads.append(_rope(k[:, sl], cos, slo, shi))
        kt_ref[...] = jnp.concatenate(k_heads, axis=1).T.astype(BF16)
    else:
        q_ref[...] = q.astype(BF16)
        seq = kt_ref.shape[1]
        for s in range(tm // seq):
            kt = k[s * seq:(s + 1) * seq, :].T
            rows = slice(s * K_COLS, (s + 1) * K_COLS)
            ktf_ref[rows, :] = kt
            kt_ref[rows, :] = kt.astype(BF16)
    v = proj(V_OFF, V_COLS)
    v_ref[...] = v.astype(BF16)
    if not latent:
        for hd in range(N_HEADS):
            vf_ref[pl.ds(hd, tm, stride=N_HEADS), :] = v[:, hd * LANES:(hd + 1) * LANES]
    u_ref[...] = proj(HY_OFF, HY_COLS)
    for src, dst in zip(cast_src, cast_dst):
        dst[...] = src[...].astype(BF16)


def _in_proj_call(x, mod, g1, w_in, rope, *, seq, latent, tm, cast=()):
    t = x.shape[0]
    n_steps = t // tm
    per_seq = seq // tm if latent else 1

    def mod_idx(i):
        return (1 + i // per_seq if latent else 0, 0, 0)

    row = lambda i: (i, 0)
    in_specs = [pl.BlockSpec((tm, D_MODEL), row),
                pl.BlockSpec((None, 6, D_MODEL), mod_idx),
                _resident((1, D_MODEL)),
                _resident((D_MODEL, GATE_OFF))]
    args = [x, mod, g1, w_in]
    tok = lambda width, dtype: ((t, width), dtype, pl.BlockSpec((tm, width), row))
    if latent:
        in_specs += [pl.BlockSpec((tm, LANES), lambda i: (i % per_seq, 0))] * 3
        args += list(rope)
        kt_spec = pl.BlockSpec((K_COLS, tm), lambda i: (i // per_seq, i % per_seq))
    else:
        kt_spec = pl.BlockSpec((tm // seq * K_COLS, seq), row)
    kt_shape = (t // seq * K_COLS, seq)
    outs = [tok(Q_COLS, BF16), (kt_shape, BF16, kt_spec), tok(V_COLS, BF16)]
    if not latent:
        outs += [(kt_shape, F32, kt_spec),
                 ((t * N_HEADS, LANES), F32, pl.BlockSpec((tm * N_HEADS, LANES), row))]
    outs += [tok(HY_COLS, F32)]
    for w in cast:
        slab = pl.BlockSpec((w.shape[0] // n_steps, w.shape[1]), row)
        in_specs.append(slab)
        args.append(w)
        outs.append((w.shape, BF16, slab))
    return pl.pallas_call(
        functools.partial(_in_proj_kernel, latent=latent, n_cast=len(cast)),
        grid=(n_steps,),
        in_specs=in_specs,
        out_specs=[spec for _, _, spec in outs],
        out_shape=[jax.ShapeDtypeStruct(s, d) for s, d, _ in outs],
        compiler_params=_params("arbitrary"),
        name="in_proj_lat" if latent else "in_proj_ctx",
    )(*args)


def _attn_kernel(*refs, n_cache, lam_init, tq):
    if n_cache:
        (q_ref, kt_ref, v_ref, ckt_ref, cv_ref, lam_ref, g_ref, o_ref, kt_s, v2_s) = refs
    else:
        (q_ref, kt_ref, v_ref, lam_ref, g_ref, o_ref, v2_s) = refs
    n_heads, n_keys, _ = v2_s.shape
    head0 = pl.program_id(1) * n_heads
    lv = lam_ref[...]
    lam = (jnp.exp(jnp.sum(lv[0:1] * lv[1:2], axis=-1, keepdims=True))
           - jnp.exp(jnp.sum(lv[2:3] * lv[3:4], axis=-1, keepdims=True)) + lam_init)
    gain = g_ref[...] * (1.0 - lam_init)
    first_map = lax.broadcasted_iota(jnp.int32, (1, LANES), 1) < HEAD_DIM
    zero = jnp.zeros((tq, LANES), BF16)

    def operands(hd):
        cols = slice(hd * LANES, (hd + 1) * LANES)
        v2_s[hd, :, LANES:] = jnp.ones((n_keys, LANES), BF16)
        if n_cache:
            kt_s[cols, 0:n_cache] = ckt_ref[cols, :].astype(BF16)
            kt_s[cols, n_cache:] = kt_ref[cols, :]
            cache_rows = pl.ds(head0 + hd, n_cache, stride=N_HEADS)
            v2_s[hd, 0:n_cache, 0:LANES] = cv_ref[cache_rows, :].astype(BF16)
            v2_s[hd, n_cache:, 0:LANES] = v_ref[:, cols]
            return kt_s[cols, :], v2_s[hd]
        v2_s[hd, :, 0:LANES] = v_ref[:, cols]
        return kt_ref[cols, :], v2_s[hd]

    def scores(hd, i, kt):
        q = q_ref[i * tq:(i + 1) * tq, hd * LANES:(hd + 1) * LANES]
        q2 = jnp.concatenate([jnp.where(first_map, q, zero), jnp.where(first_map, zero, q)], axis=0)
        s = jnp.dot(q2, kt, preferred_element_type=F32)
        return s, jnp.max(s, axis=-1, keepdims=True)

    def finish(hd, i, v2, s, m):
        p = jnp.exp2(s - m).astype(BF16)
        pv = jnp.dot(p, v2, preferred_element_type=F32)
        pv = pv[:, :LANES] / pv[:, LANES:]
        o = pv[:tq] - lam * pv[tq:]
        o_ref[i * tq:(i + 1) * tq, hd * LANES:(hd + 1) * LANES] = (_rms(o, 1.0) * gain).astype(BF16)

    n_tiles = q_ref.shape[0] // tq
    pending = None
    for hd in range(n_heads):
        kt, v2 = operands(hd)
        for i in range(n_tiles):
            cur = (hd, i, v2) + scores(hd, i, kt)
            if pending is not None:
                finish(*pending)
            pending = cur
    finish(*pending)


def _attn_call(q, kt, v, cache, lam_vecs, subln_g, *, batch, seq, lam_init, tq, heads):
    const = lambda b, h: (0, 0)
    width = heads * LANES
    n_groups = N_HEADS // heads
    in_specs = [pl.BlockSpec((seq, width), lambda b, h: (b, h)),
                pl.BlockSpec((width, seq), lambda b, h: (b * n_groups + h, 0)),
                pl.BlockSpec((seq, width), lambda b, h: (b, h))]
    args = [q, kt, v]
    n_cache = 0
    if cache is not None:
        ckt, cv = cache
        n_cache = ckt.shape[1]
        in_specs += [pl.BlockSpec((width, n_cache), lambda b, h: (b * n_groups + h, 0)),
                     pl.BlockSpec((n_cache * N_HEADS, LANES), lambda b, h: (b, 0))]
        args += [ckt, cv]
    n_keys = n_cache + seq
    scratch = [pltpu.VMEM((heads, n_keys, 2 * LANES), BF16)]
    if n_cache:
        scratch = [pltpu.VMEM((width, n_keys), BF16)] + scratch
    in_specs += [pl.BlockSpec(lam_vecs.shape, const), pl.BlockSpec((1, LANES), const)]
    args += [lam_vecs, subln_g]
    return pl.pallas_call(
        functools.partial(_attn_kernel, n_cache=n_cache, lam_init=lam_init, tq=tq),
        grid=(batch, n_groups),
        in_specs=in_specs,
        out_specs=pl.BlockSpec((seq, width), lambda b, h: (b, h)),
        out_shape=jax.ShapeDtypeStruct((batch * seq, N_HEADS * V_DIM), BF16),
        scratch_shapes=scratch,
        compiler_params=_params("arbitrary", "arbitrary"),
        name="attn_lat" if cache is not None else "attn_ctx",
    )(*args)


def _dft_tables(seq):
    n = 2 * seq
    f = np.arange(seq, dtype=np.int64)[:, None]
    t = np.arange(seq, dtype=np.int64)[None, :]
    ang = 2.0 * np.pi * ((f * t) % n).astype(np.float64) / n
    alt = np.where(np.arange(seq) % 2 == 0, 1.0, -1.0)
    cos, msin = np.cos(ang), -np.sin(ang)
    msin[0, :] = alt
    fwd = np.concatenate([cos, msin], axis=0)
    wgt = np.where(f == 0, 1.0, 2.0) / n
    inv_c = (cos * wgt).T
    inv_s = (msin * wgt).T
    inv = np.concatenate([inv_c, inv_s], axis=1)
    return fwd.astype(np.float32), inv.astype(np.float32)


def _filter_hidden(emb_ref, w1_ref, b1_ref, w2_ref, b2_ref, fr_ref):
    hp = functools.partial(jnp.dot, preferred_element_type=F32, precision=lax.Precision.HIGHEST)
    fr = fr_ref[...]
    h = jnp.sin(fr * (hp(emb_ref[...], w1_ref[...]) + b1_ref[...]))
    return jnp.sin(fr * (hp(h, w2_ref[...]) + b2_ref[...]))


def _filter_planes(h, w3f_ref, w3b_ref, decay, dft_ref, out_ref):
    seq = h.shape[0]
    fwd = jnp.dot(h, w3f_ref[...].astype(BF16), preferred_element_type=F32) * decay
    bwd = jnp.dot(h, w3b_ref[...].astype(BF16), preferred_element_type=F32) * decay
    hsum, hdif = fwd + bwd, fwd - bwd
    re = jnp.dot(dft_ref[0:seq, :], hsum.astype(BF16), preferred_element_type=F32)
    im = jnp.dot(dft_ref[seq:, :], hdif.astype(BF16), preferred_element_type=F32)
    row = lax.broadcasted_iota(jnp.int32, (seq, 1), 0)
    alt = jnp.where(row % 2 == 0, 1.0, -1.0)
    nyq = jnp.sum(hsum * alt, axis=0, keepdims=True)
    out_ref[0] = re
    out_ref[1] = jnp.where(row == 0, 0.0, im)
    out_ref[2] = jnp.where(row == 0, nyq, re)


def _hyena_kernel(uv_ref, u1_ref, u2_ref, wv_ref, w1_ref, w2_ref, bv_ref, b1_ref, b2_ref, hb_ref,
                  emb_ref, fw1_ref, fb1_ref, fw2_ref, fb2_ref, fr_ref,
                  w3f0_ref, w3b0_ref, w3f1_ref, w3b1_ref, t_ref, dl_ref, fwd_s, inv_s,
                  o_ref, hid_s, hf_s):
    seq = emb_ref.shape[0]
    new_tile = pl.program_id(1) == 0

    @pl.when(jnp.logical_and(pl.program_id(0) == 0, new_tile))
    def _():
        hid_s[...] = _filter_hidden(emb_ref, fw1_ref, fb1_ref, fw2_ref, fb2_ref, fr_ref)

    @pl.when(new_tile)
    def _():
        h = hid_s[...].astype(BF16)
        decay = jnp.exp(-t_ref[...] * jnp.abs(dl_ref[...]))
        _filter_planes(h, w3f0_ref, w3b0_ref, decay, fwd_s, hf_s.at[0])
        _filter_planes(h, w3f1_ref, w3b1_ref, decay, fwd_s, hf_s.at[1])

    chains = [slice(c * seq, (c + 1) * seq) for c in range(uv_ref.shape[0] // seq)]
    row = lax.broadcasted_iota(jnp.int32, (seq, 1), 0)
    first, last = row == 0, row == seq - 1

    def short_conv(u_ref, w_ref, b_ref):
        outs = []
        for rows in chains:
            u = u_ref[rows, :]
            prev = jnp.where(first, 0.0, pltpu.roll(u, 1, axis=0))
            nxt = jnp.where(last, 0.0, pltpu.roll(u, seq - 1, axis=0))
            outs.append(prev * w_ref[0:1, :] + u * w_ref[1:2, :] + nxt * w_ref[2:3, :] + b_ref[...])
        return outs

    def long_conv(us, order):
        h_ref = hf_s.at[order]
        bias = hb_ref[order:order + 1, :]
        specs = [jnp.dot(fwd_s[...], u.astype(BF16), preferred_element_type=F32) for u in us]
        ys = []
        for spec in specs:
            ure, uim = spec[:seq], spec[seq:]
            yre = ure * h_ref[0] - uim * h_ref[1]
            yim = ure * h_ref[1] + uim * h_ref[2]
            ys.append(jnp.concatenate([yre, yim], axis=0).astype(BF16))
        return [jnp.dot(inv_s[...], y, preferred_element_type=F32) + u * bias
                for y, u in zip(ys, us)]

    v = short_conv(uv_ref, wv_ref, bv_ref)
    x1 = short_conv(u1_ref, w1_ref, b1_ref)
    z = [a * b for a, b in zip(x1, long_conv(v, 0))]
    x2 = short_conv(u2_ref, w2_ref, b2_ref)
    for rows, a, b in zip(chains, x2, long_conv(z, 1)):
        o_ref[rows, :] = (a * b).astype(BF16)


def _hyena_call(u, conv_w, conv_b, hy_bias, filt, fwd, inv, *, batch, seq, ct, n_chain):
    emb, w1, b1, w2, b2, fr, w3, tcol, deltas = filt
    nct = D_HYENA // ct
    rows = n_chain * seq
    col = lambda part: (lambda c, b: (b, part * nct + c))
    wcol = lambda part: (lambda c, b: (0, part * nct + c))
    small = lambda shape: pl.BlockSpec(shape, lambda c, b: (0, 0))
    w3col = lambda direction, order: (lambda c, b: (0, (direction * HYENA_ORDER + order) * nct + c))
    in_specs = ([pl.BlockSpec((rows, ct), col(p)) for p in range(3)]
                + [pl.BlockSpec((3, ct), wcol(p)) for p in range(3)]
                + [pl.BlockSpec((1, ct), wcol(p)) for p in range(3)]
                + [pl.BlockSpec((HYENA_ORDER, ct), lambda c, b: (0, c)),
                   small((seq, FILT_PAD)), small((FILT_PAD, FILT_PAD)), small((1, FILT_PAD)),
                   small((FILT_PAD, FILT_PAD)), small((1, FILT_PAD)), small((1, FILT_PAD))]
                + [pl.BlockSpec((FILT_PAD, ct), w3col(direction, order))
                   for order in range(HYENA_ORDER) for direction in range(2)]
                + [small((seq, 1)), pl.BlockSpec((1, ct), lambda c, b: (0, c)),
                   _resident((2 * seq, seq)), _resident((seq, 2 * seq))])
    return pl.pallas_call(
        _hyena_kernel,
        grid=(nct, batch // n_chain),
        in_specs=in_specs,
        out_specs=pl.BlockSpec((rows, ct), lambda c, b: (b, c)),
        out_shape=jax.ShapeDtypeStruct((batch * seq, D_HYENA), BF16),
        scratch_shapes=[pltpu.VMEM((seq, FILT_PAD), F32),
                        pltpu.VMEM((HYENA_ORDER, 3, seq, ct), F32)],
        compiler_params=_params("arbitrary", "arbitrary"),
        name=f"hyena_{seq}",
    )(u, u, u, conv_w, conv_w, conv_w, conv_b, conv_b, conv_b, hy_bias,
      emb, w1, b1, w2, b2, fr, w3, w3, w3, w3, tcol, deltas, fwd, inv)


FF_CHUNK = D_MODEL
N_FF_CHUNKS = D_FF // FF_CHUNK


def _post_kernel(x_ref, oa_ref, oh_ref, mod_ref, g1_ref, g2_ref, gf_ref,
                 wga0_ref, wga1_ref, wgh0_ref, wgh1_ref,
                 wa_ref, wh_ref, wo_ref, wu_ref, wd_ref, y_ref):
    d = functools.partial(jnp.dot, preferred_element_type=F32)
    x = x_ref[...]
    h1 = (_rms(x, g1_ref[...]) * (1.0 + mod_ref[1:2, :]) + mod_ref[0:1, :]).astype(BF16)

    def gate(*w_refs):
        return jnp.concatenate([jax.nn.sigmoid(d(h1, w[...])) for w in w_refs], axis=1)

    merged = (gate(wga0_ref, wga1_ref) * d(oa_ref[...], wa_ref[...])
              + gate(wgh0_ref, wgh1_ref) * d(oh_ref[...], wh_ref[...]))
    x = x + mod_ref[2:3, :] * d(merged.astype(BF16), wo_ref[...])
    h2 = (_rms(x, g2_ref[...]) * (1.0 + mod_ref[4:5, :]) + mod_ref[3:4, :]).astype(BF16)
    acc = jnp.zeros_like(x)
    for c in range(N_FF_CHUNKS):
        sl = slice(c * FF_CHUNK, (c + 1) * FF_CHUNK)
        up = jnp.maximum(d(h2, wu_ref[:, sl]), 0.0)
        acc = acc + d((up * up).astype(BF16), wd_ref[sl, :])
    x = x + mod_ref[5:6, :] * acc
    y_ref[...] = _rms(x, gf_ref[...])


def _post_call(x, o_attn, o_hy, mod, g1, g2, gf, w_in, weights, *, seq, latent, tm):
    t = x.shape[0]
    gate_windows = [_resident((D_MODEL, GATE_WINDOW), (0, GATE_OFF // GATE_WINDOW + n))
                    for n in range(GATE_COLS // GATE_WINDOW)]
    per_seq = seq // tm if latent else 1
    row = lambda i: (i, 0)

    def mod_idx(i):
        return (1 + i // per_seq if latent else 0, 0, 0)

    return pl.pallas_call(
        _post_kernel,
        grid=(t // tm,),
        in_specs=[pl.BlockSpec((tm, D_MODEL), row),
                  pl.BlockSpec((tm, N_HEADS * V_DIM), row),
                  pl.BlockSpec((tm, D_HYENA), row),
                  pl.BlockSpec((None, 6, D_MODEL), mod_idx),
                  _resident((1, D_MODEL)), _resident((1, D_MODEL)), _resident((1, D_MODEL))]
                 + gate_windows + [_resident(w.shape) for w in weights],
        out_specs=pl.BlockSpec((tm, D_MODEL), row),
        out_shape=jax.ShapeDtypeStruct((t, D_MODEL), F32),
        compiler_params=_params("arbitrary"),
        name="post_lat" if latent else "post_ctx",
    )(x, o_attn, o_hy, mod, g1, g2, gf, *[w_in] * len(gate_windows), *weights)


def _rope_tables(seq):
    half = HEAD_DIM // 2
    n = half // 2
    inv = ROPE_THETA ** (-np.arange(n, dtype=np.float64) / n)
    pos = np.arange(seq)
    ang_row = (pos // GRID_W).astype(np.float64)[:, None] * inv[None, :]
    ang_col = (pos % GRID_W).astype(np.float64)[:, None] * inv[None, :]
    zeros = np.zeros_like(ang_row)

    def per_map(a_row, a_col, lo, hi):
        return np.concatenate([lo(a_row), hi(a_row), lo(a_col), hi(a_col)], axis=-1)

    cos = per_map(ang_row, ang_col, np.cos, np.cos)
    sin_lo = per_map(ang_row, ang_col, lambda a: -np.sin(a), lambda a: zeros)
    sin_hi = per_map(ang_row, ang_col, lambda a: zeros, np.sin)
    return tuple(jnp.asarray(np.concatenate([tab, tab], axis=-1).astype(np.float32))
                 for tab in (cos, sin_lo, sin_hi))


def _filter_embedding(seq):
    bands = (FILTER_EMB - 1) // 2
    t = np.linspace(0.0, 1.0, seq)[:, None]
    wpos = 2.0 * np.pi * np.arange(seq, dtype=np.float64)[:, None] / seq
    f = np.linspace(1e-4, bands - 1, bands)[None, :]
    emb = np.concatenate([t, np.cos(f * wpos), -np.sin(f * wpos)], axis=-1)
    emb = np.pad(emb, ((0, 0), (0, FILT_PAD - FILTER_EMB)))
    return jnp.asarray(emb.astype(np.float32)), jnp.asarray(t.astype(np.float32))


def _tiles(seq):
    short = seq <= 256
    return dict(tm=512, tq=128, attn_heads=N_HEADS if short else 4, ct=MXU_WIDTH,
                hyena_seqs=4 if short else 2)


def _pad_to(x, rows, cols):
    return jnp.pad(x, ((0, rows - x.shape[0]), (0, cols - x.shape[1])))


def kernel(x_prompt, x_sample, cache_k, cache_v, c, c_ctx, w_ada, b_ada, norm1_g, norm2_g, w_in,
           lam_q1, lam_k1, lam_q2, lam_k2, attn_subln_g, conv_w, conv_b, filt_w1, filt_b1,
           filt_w2, filt_b2, filt_w3, filt_freq, hy_bias, w_br_attn, w_br_hy, w_out, w_up,
           w_down, final_g):
    depth = w_in.shape[0]
    assert depth == 1, "single trunk layer"
    layer = 0
    lam_init = 0.8 - 0.6 * math.exp(-0.3 * layer)
    n_ctx, ctx_len, _ = x_prompt.shape
    n_lat, lat_len, _ = x_sample.shape
    past = cache_k.shape[2]

    cvec = jnp.concatenate([c_ctx[None, :], c], axis=0)
    cvec = jnp.pad(cvec, ((0, MOD_ROWS - cvec.shape[0]), (0, 0)))
    mod = _mod_call(cvec, w_ada[layer], b_ada[layer][None, :]).reshape(MOD_ROWS, 6, D_MODEL)

    w_in_b = w_in[layer].astype(BF16)
    post_w_f32 = tuple(w[layer] for w in (w_br_attn, w_br_hy, w_out, w_up, w_down))
    g1, g2, gf = norm1_g[layer][None, :], norm2_g[layer][None, :], final_g[None, :]
    lam_vecs = jnp.stack([lam_q1[layer], lam_k1[layer], lam_q2[layer], lam_k2[layer]])
    subln_g = attn_subln_g[layer][None, :]

    w1 = _pad_to(filt_w1[layer], FILT_PAD, FILT_PAD)
    w2 = _pad_to(filt_w2[layer], FILT_PAD, FILT_PAD)
    w3 = _pad_to(filt_w3[layer], FILT_PAD, filt_w3.shape[2])
    b1 = _pad_to(filt_b1[layer][None, :], 1, FILT_PAD)
    b2 = _pad_to(filt_b2[layer][None, :], 1, FILT_PAD)
    fr = _pad_to(filt_freq[layer][None, :], 1, FILT_PAD)
    deltas = jnp.asarray(np.linspace(math.log(DECAY_TARGET) / FAST_DECAY_PCT,
                                     math.log(DECAY_TARGET) / SLOW_DECAY_PCT,
                                     D_HYENA)[None, :].astype(np.float32))

    def project(x3, batch, seq, latent, cast=()):
        x = x3.reshape(batch * seq, D_MODEL)
        rope = _rope_tables(seq) if latent else None
        return x, _in_proj_call(x, mod, g1, w_in_b, rope, seq=seq, latent=latent,
                                tm=_tiles(seq)["tm"], cast=cast)

    def mix(x, outs, post_w, batch, seq, latent):
        tiles = _tiles(seq)
        tm = tiles["tm"]
        if latent:
            q, kt, v, u = outs
            ckt = jnp.transpose(cache_k[:, layer], (0, 2, 3, 4, 1)).reshape(batch * K_COLS, past)
            cache = (ckt, cache_v[:, layer].reshape(batch * past * N_HEADS, V_DIM))
            kf = vf = None
        else:
            q, kt, v, kf, vf, u = outs
            cache = None
        o_attn = _attn_call(q, kt, v, cache, lam_vecs, subln_g, batch=batch, seq=seq,
                            lam_init=lam_init, tq=tiles["tq"], heads=tiles["attn_heads"])

        fwd_np, inv_np = _dft_tables(seq)
        fwd_b = jnp.asarray(fwd_np).astype(BF16)
        inv_b = jnp.asarray(inv_np).astype(BF16)
        emb, tcol = _filter_embedding(seq)
        filt = (emb, w1, b1, w2, b2, fr, w3, tcol, deltas)
        o_hy = _hyena_call(u, conv_w[layer], conv_b[layer][None, :], hy_bias[layer], filt,
                           fwd_b, inv_b, batch=batch, seq=seq, ct=tiles["ct"],
                           n_chain=tiles["hyena_seqs"])

        y = _post_call(x, o_attn, o_hy, mod, g1, g2, gf, w_in_b, post_w,
                       seq=seq, latent=latent, tm=tm)
        return y.reshape(batch, seq, D_MODEL), kf, vf

    x_ctx, ctx_outs = project(x_prompt, n_ctx, ctx_len, False)
    x_lat, lat_outs = project(x_sample, n_lat, lat_len, True, cast=post_w_f32)
    lat_outs, post_w = lat_outs[:-len(post_w_f32)], lat_outs[-len(post_w_f32):]
    y_prompt, kf, vf = mix(x_ctx, ctx_outs, post_w, n_ctx, ctx_len, False)
    y_sample, _, _ = mix(x_lat, lat_outs, post_w, n_lat, lat_len, True)
    new_cache_k = jnp.transpose(kf.reshape(n_ctx, depth, N_HEADS, 2, HEAD_DIM, ctx_len),
                                (0, 1, 5, 2, 3, 4))
    new_cache_v = vf.reshape(n_ctx, depth, ctx_len, N_HEADS, V_DIM)
    return (y_prompt, y_sample, new_cache_k, new_cache_v)
```

```python
import functools
import math

import numpy as np
import jax
import jax.numpy as jnp
from jax import lax
from jax.experimental import pallas as pl
from jax.experimental.pallas import tpu as pltpu

D_MODEL = 1024
GRID_W = 64
N_HEADS = 8
HEAD_DIM = 64
V_DIM = 2 * HEAD_DIM
D_HYENA = D_MODEL // 2
HYENA_ORDER = 2
FILTER_EMB = 33
FILTER_HIDDEN = 64
D_FF = 4 * D_MODEL
ROPE_THETA = 10000.0
EPS = 1e-6
LOG2_E = math.log2(math.e)
DECAY_TARGET = 1e-2
FAST_DECAY_PCT = 0.3
SLOW_DECAY_PCT = 1.5

Q_COLS = N_HEADS * 2 * HEAD_DIM
K_COLS = N_HEADS * 2 * HEAD_DIM
V_COLS = N_HEADS * V_DIM
HY_COLS = 3 * D_HYENA
GATE_COLS = 2 * D_MODEL
Q_OFF = 0
K_OFF = Q_OFF + Q_COLS
V_OFF = K_OFF + K_COLS
HY_OFF = V_OFF + V_COLS
GATE_OFF = HY_OFF + HY_COLS
IN_COLS = GATE_OFF + GATE_COLS
GATE_WINDOW = math.gcd(GATE_OFF, D_MODEL)

LANES = 128
MXU_WIDTH = 256
MOD_ROWS = 16
FILT_PAD = 128
VMEM_LIMIT = 56 * 1024 * 1024

BF16 = jnp.bfloat16
F32 = jnp.float32


def _resident(shape, index=None):
    index = (0,) * len(shape) if index is None else index
    return pl.BlockSpec(shape, lambda *_: index, pipeline_mode=pl.Buffered(1))


def _params(*sem):
    return pltpu.CompilerParams(dimension_semantics=sem, vmem_limit_bytes=VMEM_LIMIT)


def _rms(x, g):
    return x * lax.rsqrt(jnp.mean(x * x, axis=-1, keepdims=True) + EPS) * g


def _mod_kernel(c_ref, w_ref, b_ref, o_ref):
    c = c_ref[...]
    s = (c * jax.nn.sigmoid(c)).astype(BF16)
    o_ref[...] = jnp.dot(s, w_ref[...].astype(BF16), preferred_element_type=F32) + b_ref[...]


def _mod_call(cvec, w_ada, b_ada):
    tn = 1024
    n = w_ada.shape[1]
    return pl.pallas_call(
        _mod_kernel,
        grid=(n // tn,),
        in_specs=[pl.BlockSpec((MOD_ROWS, D_MODEL), lambda j: (0, 0)),
                  pl.BlockSpec((D_MODEL, tn), lambda j: (0, j)),
                  pl.BlockSpec((1, tn), lambda j: (0, j))],
        out_specs=pl.BlockSpec((MOD_ROWS, tn), lambda j: (0, j)),
        out_shape=jax.ShapeDtypeStruct((MOD_ROWS, n), F32),
        compiler_params=_params("arbitrary"),
        name="mod",
    )(cvec, w_ada, b_ada)


def _rope(x, cos, sin_lo, sin_hi):
    return (x * cos + pltpu.roll(x, 16, axis=1) * sin_hi
            + pltpu.roll(x, LANES - 16, axis=1) * sin_lo)


def _in_proj_kernel(*refs, latent, n_cast):
    refs = list(refs)
    take = lambda n: [refs.pop(0) for _ in range(n)]
    x_ref, mod_ref, g_ref, w_ref = take(4)
    if latent:
        cos_ref, slo_ref, shi_ref = take(3)
    cast_src = take(n_cast)
    q_ref, kt_ref, v_ref = take(3)
    if not latent:
        ktf_ref, vf_ref = take(2)
    (u_ref,) = take(1)
    cast_dst = take(n_cast)
    tm = x_ref.shape[0]
    x = x_ref[...]
    h = _rms(x, g_ref[...]) * (1.0 + mod_ref[1:2, :]) + mod_ref[0:1, :]
    hb = h.astype(BF16)

    def proj(off, width):
        return jnp.dot(hb, w_ref[:, off:off + width], preferred_element_type=F32)

    q = proj(Q_OFF, Q_COLS) * (HEAD_DIM ** -0.5 * LOG2_E)
    k = proj(K_OFF, K_COLS)
    if latent:
        cos, slo, shi = cos_ref[...], slo_ref[...], shi_ref[...]
        k_heads = []
        for hd in range(N_HEADS):
            sl = slice(hd * LANES, (hd + 1) * LANES)
            q_ref[:, sl] = _rope(q[:, sl], cos, slo, shi).astype(BF16)
            k_heads.append(_rope(k[:, sl], cos, slo, shi))
        kt_ref[...] = jnp.concatenate(k_heads, axis=1).T.astype(BF16)
    else:
        q_ref[...] = q.astype(BF16)
        seq = kt_ref.shape[1]
        for s in range(tm // seq):
            kt = k[s * seq:(s + 1) * seq, :].T
            rows = slice(s * K_COLS, (s + 1) * K_COLS)
            ktf_ref[rows, :] = kt
            kt_ref[rows, :] = kt.astype(BF16)
    v = proj(V_OFF, V_COLS)
    v_ref[...] = v.astype(BF16)
    if not latent:
        for hd in range(N_HEADS):
            vf_ref[pl.ds(hd, tm, stride=N_HEADS), :] = v[:, hd * LANES:(hd + 1) * LANES]
    u_ref[...] = proj(HY_OFF, HY_COLS)
    for src, dst in zip(cast_src, cast_dst):
        dst[...] = src[...].astype(BF16)


def _in_proj_call(x, mod, g1, w_in, rope, *, seq, latent, tm, cast=()):
    t = x.shape[0]
    n_steps = t // tm
    per_seq = seq // tm if latent else 1

    def mod_idx(i):
        return (1 + i // per_seq if latent else 0, 0, 0)

    row = lambda i: (i, 0)
    in_specs = [pl.BlockSpec((tm, D_MODEL), row),
                pl.BlockSpec((None, 6, D_MODEL), mod_idx),
                _resident((1, D_MODEL)),
                _resident((D_MODEL, GATE_OFF))]
    args = [x, mod, g1, w_in]
    tok = lambda width, dtype: ((t, width), dtype, pl.BlockSpec((tm, width), row))
    if latent:
        in_specs += [pl.BlockSpec((tm, LANES), lambda i: (i % per_seq, 0))] * 3
        args += list(rope)
        kt_spec = pl.BlockSpec((K_COLS, tm), lambda i: (i // per_seq, i % per_seq))
    else:
        kt_spec = pl.BlockSpec((tm // seq * K_COLS, seq), row)
    kt_shape = (t // seq * K_COLS, seq)
    outs = [tok(Q_COLS, BF16), (kt_shape, BF16, kt_spec), tok(V_COLS, BF16)]
    if not latent:
        outs += [(kt_shape, F32, kt_spec),
                 ((t * N_HEADS, LANES), F32, pl.BlockSpec((tm * N_HEADS, LANES), row))]
    outs += [tok(HY_COLS, F32)]
    for w in cast:
        slab = pl.BlockSpec((w.shape[0] // n_steps, w.shape[1]), row)
        in_specs.append(slab)
        args.append(w)
        outs.append((w.shape, BF16, slab))
    return pl.pallas_call(
        functools.partial(_in_proj_kernel, latent=latent, n_cast=len(cast)),
        grid=(n_steps,),
        in_specs=in_specs,
        out_specs=[spec for _, _, spec in outs],
        out_shape=[jax.ShapeDtypeStruct(s, d) for s, d, _ in outs],
        compiler_params=_params("arbitrary"),
        name="in_proj_lat" if latent else "in_proj_ctx",
    )(*args)


def _attention_steps(refs, *, head_group, n_cache, lam_init, tq):
    if n_cache:
        (q_ref, kt_ref, v_ref, ckt_ref, cv_ref, lam_ref, g_ref, o_ref, kt_s, v2_s) = refs
    else:
        (q_ref, kt_ref, v_ref, lam_ref, g_ref, o_ref, v2_s) = refs
    n_heads, n_keys, _ = v2_s.shape
    head0 = head_group * n_heads
    lv = lam_ref[...]
    lam = (jnp.exp(jnp.sum(lv[0:1] * lv[1:2], axis=-1, keepdims=True))
           - jnp.exp(jnp.sum(lv[2:3] * lv[3:4], axis=-1, keepdims=True)) + lam_init)
    gain = g_ref[...] * (1.0 - lam_init)
    first_map = lax.broadcasted_iota(jnp.int32, (1, LANES), 1) < HEAD_DIM
    zero = jnp.zeros((tq, LANES), BF16)

    def operands(hd):
        cols = slice(hd * LANES, (hd + 1) * LANES)
        v2_s[hd, :, LANES:] = jnp.ones((n_keys, LANES), BF16)
        if n_cache:
            kt_s[cols, 0:n_cache] = ckt_ref[cols, :].astype(BF16)
            kt_s[cols, n_cache:] = kt_ref[cols, :]
            cache_rows = pl.ds(head0 + hd, n_cache, stride=N_HEADS)
            v2_s[hd, 0:n_cache, 0:LANES] = cv_ref[cache_rows, :].astype(BF16)
            v2_s[hd, n_cache:, 0:LANES] = v_ref[:, cols]
            return kt_s[cols, :], v2_s[hd]
        v2_s[hd, :, 0:LANES] = v_ref[:, cols]
        return kt_ref[cols, :], v2_s[hd]

    def scores(hd, i, kt):
        q = q_ref[i * tq:(i + 1) * tq, hd * LANES:(hd + 1) * LANES]
        q2 = jnp.concatenate([jnp.where(first_map, q, zero), jnp.where(first_map, zero, q)], axis=0)
        s = jnp.dot(q2, kt, preferred_element_type=F32)
        return s, jnp.max(s, axis=-1, keepdims=True)

    def finish(hd, i, v2, s, m):
        p = jnp.exp2(s - m).astype(BF16)
        pv = jnp.dot(p, v2, preferred_element_type=F32)
        pv = pv[:, :LANES] / pv[:, LANES:]
        o = pv[:tq] - lam * pv[tq:]
        o_ref[i * tq:(i + 1) * tq, hd * LANES:(hd + 1) * LANES] = (_rms(o, 1.0) * gain).astype(BF16)

    n_tiles = q_ref.shape[0] // tq
    pending = None
    for hd in range(n_heads):
        kt, v2 = operands(hd)
        for i in range(n_tiles):
            cur = (hd, i, v2) + scores(hd, i, kt)
            if pending is not None:
                finish(*pending)
            pending = cur
            yield
    finish(*pending)
    yield


def _attn_kernel(*refs, n_cache, lam_init, tq):
    for _ in _attention_steps(refs, head_group=pl.program_id(1), n_cache=n_cache,
                              lam_init=lam_init, tq=tq):
        pass


def _attn_operands(q, kt, v, cache, lam_vecs, subln_g, *, seq, heads, grid_to_batch_group):
    width = heads * LANES
    n_groups = N_HEADS // heads
    on = lambda fn: (lambda i, j: fn(*grid_to_batch_group(i, j)))
    tokens = on(lambda b, g: (b, g))
    keys_t = on(lambda b, g: (b * n_groups + g, 0))
    const = lambda i, j: (0, 0)
    in_specs = [pl.BlockSpec((seq, width), tokens),
                pl.BlockSpec((width, seq), keys_t),
                pl.BlockSpec((seq, width), tokens)]
    args = [q, kt, v]
    n_cache = 0
    if cache is not None:
        ckt, cv = cache
        n_cache = ckt.shape[1]
        in_specs += [pl.BlockSpec((width, n_cache), keys_t),
                     pl.BlockSpec((n_cache * N_HEADS, LANES), on(lambda b, g: (b, 0)))]
        args += [ckt, cv]
    n_keys = n_cache + seq
    scratch = [pltpu.VMEM((heads, n_keys, 2 * LANES), BF16)]
    if n_cache:
        scratch = [pltpu.VMEM((width, n_keys), BF16)] + scratch
    in_specs += [pl.BlockSpec(lam_vecs.shape, const), pl.BlockSpec((1, LANES), const)]
    args += [lam_vecs, subln_g]
    return in_specs, args, pl.BlockSpec((seq, width), tokens), scratch, n_cache


def _attn_call(*operands, batch, seq, lam_init, tq, heads):
    in_specs, args, out_spec, scratch, n_cache = _attn_operands(
        *operands, seq=seq, heads=heads, grid_to_batch_group=lambda b, g: (b, g))
    return pl.pallas_call(
        functools.partial(_attn_kernel, n_cache=n_cache, lam_init=lam_init, tq=tq),
        grid=(batch, N_HEADS // heads),
        in_specs=in_specs,
        out_specs=out_spec,
        out_shape=jax.ShapeDtypeStruct((batch * seq, N_HEADS * V_DIM), BF16),
        scratch_shapes=scratch,
        compiler_params=_params("arbitrary", "arbitrary"),
        name="attn_lat" if n_cache else "attn_ctx",
    )(*args)


def _dft_tables(seq):
    n = 2 * seq
    f = np.arange(seq, dtype=np.int64)[:, None]
    t = np.arange(seq, dtype=np.int64)[None, :]
    ang = 2.0 * np.pi * ((f * t) % n).astype(np.float64) / n
    alt = np.where(np.arange(seq) % 2 == 0, 1.0, -1.0)
    cos, msin = np.cos(ang), -np.sin(ang)
    msin[0, :] = alt
    fwd = np.concatenate([cos, msin], axis=0)
    wgt = np.where(f == 0, 1.0, 2.0) / n
    inv_c = (cos * wgt).T
    inv_s = (msin * wgt).T
    inv = np.concatenate([inv_c, inv_s], axis=1)
    return fwd.astype(np.float32), inv.astype(np.float32)


def _filter_hidden(emb_ref, w1_ref, b1_ref, w2_ref, b2_ref, fr_ref):
    hp = functools.partial(jnp.dot, preferred_element_type=F32, precision=lax.Precision.HIGHEST)
    fr = fr_ref[...]
    h = jnp.sin(fr * (hp(emb_ref[...], w1_ref[...]) + b1_ref[...]))
    return jnp.sin(fr * (hp(h, w2_ref[...]) + b2_ref[...]))


def _filter_planes(h, w3f_ref, w3b_ref, decay, dft_ref, out_ref):
    seq = h.shape[0]
    fwd = jnp.dot(h, w3f_ref[...].astype(BF16), preferred_element_type=F32) * decay
    bwd = jnp.dot(h, w3b_ref[...].astype(BF16), preferred_element_type=F32) * decay
    hsum, hdif = fwd + bwd, fwd - bwd
    re = jnp.dot(dft_ref[0:seq, :], hsum.astype(BF16), preferred_element_type=F32)
    im = jnp.dot(dft_ref[seq:, :], hdif.astype(BF16), preferred_element_type=F32)
    row = lax.broadcasted_iota(jnp.int32, (seq, 1), 0)
    alt = jnp.where(row % 2 == 0, 1.0, -1.0)
    nyq = jnp.sum(hsum * alt, axis=0, keepdims=True)
    out_ref[0] = re
    out_ref[1] = jnp.where(row == 0, 0.0, im)
    out_ref[2] = jnp.where(row == 0, nyq, re)


N_HYENA_IN = 24


def _hyena_steps(refs, *, new_tile, first_step):
    (uv_ref, u1_ref, u2_ref, wv_ref, w1_ref, w2_ref, bv_ref, b1_ref, b2_ref, hb_ref,
     emb_ref, fw1_ref, fb1_ref, fw2_ref, fb2_ref, fr_ref,
     w3f0_ref, w3b0_ref, w3f1_ref, w3b1_ref, t_ref, dl_ref, fwd_s, inv_s,
     o_ref, hid_s, hf_s) = refs
    seq = emb_ref.shape[0]

    @pl.when(first_step)
    def _():
        hid_s[...] = _filter_hidden(emb_ref, fw1_ref, fb1_ref, fw2_ref, fb2_ref, fr_ref)

    @pl.when(new_tile)
    def _():
        h = hid_s[...].astype(BF16)
        decay = jnp.exp(-t_ref[...] * jnp.abs(dl_ref[...]))
        _filter_planes(h, w3f0_ref, w3b0_ref, decay, fwd_s, hf_s.at[0])
        _filter_planes(h, w3f1_ref, w3b1_ref, decay, fwd_s, hf_s.at[1])

    yield
    chains = [slice(c * seq, (c + 1) * seq) for c in range(uv_ref.shape[0] // seq)]
    row = lax.broadcasted_iota(jnp.int32, (seq, 1), 0)
    first, last = row == 0, row == seq - 1

    def short_conv(u_ref, w_ref, b_ref):
        outs = []
        for rows in chains:
            u = u_ref[rows, :]
            prev = jnp.where(first, 0.0, pltpu.roll(u, 1, axis=0))
            nxt = jnp.where(last, 0.0, pltpu.roll(u, seq - 1, axis=0))
            outs.append(prev * w_ref[0:1, :] + u * w_ref[1:2, :] + nxt * w_ref[2:3, :] + b_ref[...])
        return outs

    def spectra(us):
        return [jnp.dot(fwd_s[...], u.astype(BF16), preferred_element_type=F32) for u in us]

    def filtered(specs, order):
        h_ref = hf_s.at[order]
        ys = []
        for spec in specs:
            ure, uim = spec[:seq], spec[seq:]
            yre = ure * h_ref[0] - uim * h_ref[1]
            yim = ure * h_ref[1] + uim * h_ref[2]
            ys.append(jnp.concatenate([yre, yim], axis=0).astype(BF16))
        return ys

    def back(ys, us, order):
        bias = hb_ref[order:order + 1, :]
        return [jnp.dot(inv_s[...], y, preferred_element_type=F32) + u * bias
                for y, u in zip(ys, us)]

    v = short_conv(uv_ref, wv_ref, bv_ref)
    yield
    specs = spectra(v)
    yield
    x1 = short_conv(u1_ref, w1_ref, b1_ref)
    ys = filtered(specs, 0)
    yield
    z = [a * b for a, b in zip(x1, back(ys, v, 0))]
    yield
    specs = spectra(z)
    yield
    x2 = short_conv(u2_ref, w2_ref, b2_ref)
    ys = filtered(specs, 1)
    yield
    for rows, a, b in zip(chains, x2, back(ys, z, 1)):
        o_ref[rows, :] = (a * b).astype(BF16)
    yield


def _hyena_kernel(*refs):
    new_tile = pl.program_id(1) == 0
    first_step = jnp.logical_and(pl.program_id(0) == 0, new_tile)
    for _ in _hyena_steps(refs, new_tile=new_tile, first_step=first_step):
        pass


def _mixer_kernel(*refs, n_attn_in, n_cache, lam_init, tq, tiles_per_stage):
    refs = list(refs)
    attn_in, hy_in = refs[:n_attn_in], refs[n_attn_in:n_attn_in + N_HYENA_IN]
    o_attn, o_hy, kt_s, v2_s, hid_s, hf_s = refs[n_attn_in + N_HYENA_IN:]
    new_tile = pl.program_id(1) == 0
    first_step = jnp.logical_and(pl.program_id(0) == 0, new_tile)
    hyena = _hyena_steps(hy_in + [o_hy, hid_s, hf_s], new_tile=new_tile, first_step=first_step)
    attention = _attention_steps(attn_in + [o_attn, kt_s, v2_s], head_group=pl.program_id(0),
                                 n_cache=n_cache, lam_init=lam_init, tq=tq)
    next(hyena)
    for n, _ in enumerate(attention):
        if (n + 1) % tiles_per_stage == 0:
            next(hyena, None)
    for _ in hyena:
        pass


def _hyena_operands(u, conv_w, conv_b, hy_bias, filt, fwd, inv, *, seq, ct, n_chain):
    emb, w1, b1, w2, b2, fr, w3, tcol, deltas = filt
    nct = D_HYENA // ct
    rows = n_chain * seq
    col = lambda part: (lambda c, b: (b, part * nct + c))
    wcol = lambda part: (lambda c, b: (0, part * nct + c))
    small = lambda shape: pl.BlockSpec(shape, lambda c, b: (0, 0))
    w3col = lambda direction, order: (lambda c, b: (0, (direction * HYENA_ORDER + order) * nct + c))
    in_specs = ([pl.BlockSpec((rows, ct), col(p)) for p in range(3)]
                + [pl.BlockSpec((3, ct), wcol(p)) for p in range(3)]
                + [pl.BlockSpec((1, ct), wcol(p)) for p in range(3)]
                + [pl.BlockSpec((HYENA_ORDER, ct), lambda c, b: (0, c)),
                   small((seq, FILT_PAD)), small((FILT_PAD, FILT_PAD)), small((1, FILT_PAD)),
                   small((FILT_PAD, FILT_PAD)), small((1, FILT_PAD)), small((1, FILT_PAD))]
                + [pl.BlockSpec((FILT_PAD, ct), w3col(direction, order))
                   for order in range(HYENA_ORDER) for direction in range(2)]
                + [small((seq, 1)), pl.BlockSpec((1, ct), lambda c, b: (0, c)),
                   _resident((2 * seq, seq)), _resident((seq, 2 * seq))])
    args = [u, u, u, conv_w, conv_w, conv_w, conv_b, conv_b, conv_b, hy_bias,
            emb, w1, b1, w2, b2, fr, w3, w3, w3, w3, tcol, deltas, fwd, inv]
    assert len(args) == len(in_specs) == N_HYENA_IN
    out_spec = pl.BlockSpec((rows, ct), lambda c, b: (b, c))
    scratch = [pltpu.VMEM((seq, FILT_PAD), F32), pltpu.VMEM((HYENA_ORDER, 3, seq, ct), F32)]
    return in_specs, args, out_spec, scratch


def _hyena_call(*operands, batch, seq, ct, n_chain):
    in_specs, args, out_spec, scratch = _hyena_operands(*operands, seq=seq, ct=ct, n_chain=n_chain)
    return pl.pallas_call(
        _hyena_kernel,
        grid=(D_HYENA // ct, batch // n_chain),
        in_specs=in_specs,
        out_specs=out_spec,
        out_shape=jax.ShapeDtypeStruct((batch * seq, D_HYENA), BF16),
        scratch_shapes=scratch,
        compiler_params=_params("arbitrary", "arbitrary"),
        name=f"hyena_{seq}",
    )(*args)


def _mixer_call(attn_operands, hyena_operands, *, batch, seq, lam_init, tq, ct, tiles_per_stage):
    n_groups = D_HYENA // ct
    heads = N_HEADS // n_groups
    a_specs, a_args, a_out, a_scratch, n_cache = _attn_operands(
        *attn_operands, seq=seq, heads=heads, grid_to_batch_group=lambda c, b: (b, c))
    h_specs, h_args, h_out, h_scratch = _hyena_operands(*hyena_operands, seq=seq, ct=ct, n_chain=1)
    return pl.pallas_call(
        functools.partial(_mixer_kernel, n_attn_in=len(a_args), n_cache=n_cache,
                          lam_init=lam_init, tq=tq, tiles_per_stage=tiles_per_stage),
        grid=(n_groups, batch),
        in_specs=a_specs + h_specs,
        out_specs=[a_out, h_out],
        out_shape=[jax.ShapeDtypeStruct((batch * seq, N_HEADS * V_DIM), BF16),
                   jax.ShapeDtypeStruct((batch * seq, D_HYENA), BF16)],
        scratch_shapes=a_scratch + h_scratch,
        compiler_params=_params("arbitrary", "arbitrary"),
        name=f"mixer_{seq}",
    )(*a_args, *h_args)


FF_CHUNK = D_MODEL
N_FF_CHUNKS = D_FF // FF_CHUNK


def _post_kernel(x_ref, oa_ref, oh_ref, mod_ref, g1_ref, g2_ref, gf_ref,
                 wga0_ref, wga1_ref, wgh0_ref, wgh1_ref,
                 wa_ref, wh_ref, wo_ref, wu_ref, wd_ref, y_ref):
    d = functools.partial(jnp.dot, preferred_element_type=F32)
    x = x_ref[...]
    h1 = (_rms(x, g1_ref[...]) * (1.0 + mod_ref[1:2, :]) + mod_ref[0:1, :]).astype(BF16)

    def gate(*w_refs):
        return jnp.concatenate([jax.nn.sigmoid(d(h1, w[...])) for w in w_refs], axis=1)

    merged = (gate(wga0_ref, wga1_ref) * d(oa_ref[...], wa_ref[...])
              + gate(wgh0_ref, wgh1_ref) * d(oh_ref[...], wh_ref[...]))
    x = x + mod_ref[2:3, :] * d(merged.astype(BF16), wo_ref[...])
    h2 = (_rms(x, g2_ref[...]) * (1.0 + mod_ref[4:5, :]) + mod_ref[3:4, :]).astype(BF16)
    acc = jnp.zeros_like(x)
    for c in range(N_FF_CHUNKS):
        sl = slice(c * FF_CHUNK, (c + 1) * FF_CHUNK)
        up = jnp.maximum(d(h2, wu_ref[:, sl]), 0.0)
        acc = acc + d((up * up).astype(BF16), wd_ref[sl, :])
    x = x + mod_ref[5:6, :] * acc
    y_ref[...] = _rms(x, gf_ref[...])


def _post_call(x, o_attn, o_hy, mod, g1, g2, gf, w_in, weights, *, seq, latent, tm):
    t = x.shape[0]
    gate_windows = [_resident((D_MODEL, GATE_WINDOW), (0, GATE_OFF // GATE_WINDOW + n))
                    for n in range(GATE_COLS // GATE_WINDOW)]
    per_seq = seq // tm if latent else 1
    row = lambda i: (i, 0)

    def mod_idx(i):
        return (1 + i // per_seq if latent else 0, 0, 0)

    return pl.pallas_call(
        _post_kernel,
        grid=(t // tm,),
        in_specs=[pl.BlockSpec((tm, D_MODEL), row),
                  pl.BlockSpec((tm, N_HEADS * V_DIM), row),
                  pl.BlockSpec((tm, D_HYENA), row),
                  pl.BlockSpec((None, 6, D_MODEL), mod_idx),
                  _resident((1, D_MODEL)), _resident((1, D_MODEL)), _resident((1, D_MODEL))]
                 + gate_windows + [_resident(w.shape) for w in weights],
        out_specs=pl.BlockSpec((tm, D_MODEL), row),
        out_shape=jax.ShapeDtypeStruct((t, D_MODEL), F32),
        compiler_params=_params("arbitrary"),
        name="post_lat" if latent else "post_ctx",
    )(x, o_attn, o_hy, mod, g1, g2, gf, *[w_in] * len(gate_windows), *weights)


def _rope_tables(seq):
    half = HEAD_DIM // 2
    n = half // 2
    inv = ROPE_THETA ** (-np.arange(n, dtype=np.float64) / n)
    pos = np.arange(seq)
    ang_row = (pos // GRID_W).astype(np.float64)[:, None] * inv[None, :]
    ang_col = (pos % GRID_W).astype(np.float64)[:, None] * inv[None, :]
    zeros = np.zeros_like(ang_row)

    def per_map(a_row, a_col, lo, hi):
        return np.concatenate([lo(a_row), hi(a_row), lo(a_col), hi(a_col)], axis=-1)

    cos = per_map(ang_row, ang_col, np.cos, np.cos)
    sin_lo = per_map(ang_row, ang_col, lambda a: -np.sin(a), lambda a: zeros)
    sin_hi = per_map(ang_row, ang_col, lambda a: zeros, np.sin)
    return tuple(jnp.asarray(np.concatenate([tab, tab], axis=-1).astype(np.float32))
                 for tab in (cos, sin_lo, sin_hi))


def _filter_embedding(seq):
    bands = (FILTER_EMB - 1) // 2
    t = np.linspace(0.0, 1.0, seq)[:, None]
    wpos = 2.0 * np.pi * np.arange(seq, dtype=np.float64)[:, None] / seq
    f = np.linspace(1e-4, bands - 1, bands)[None, :]
    emb = np.concatenate([t, np.cos(f * wpos), -np.sin(f * wpos)], axis=-1)
    emb = np.pad(emb, ((0, 0), (0, FILT_PAD - FILTER_EMB)))
    return jnp.asarray(emb.astype(np.float32)), jnp.asarray(t.astype(np.float32))


def _tiles(seq):
    short = seq <= 256
    return dict(tm=512, tq=128, attn_heads=N_HEADS if short else 4, ct=MXU_WIDTH,
                hyena_seqs=4 if short else 2, fuse_mixers=not short, tiles_per_stage=4)


def _pad_to(x, rows, cols):
    return jnp.pad(x, ((0, rows - x.shape[0]), (0, cols - x.shape[1])))


def kernel(x_prompt, x_sample, cache_k, cache_v, c, c_ctx, w_ada, b_ada, norm1_g, norm2_g, w_in,
           lam_q1, lam_k1, lam_q2, lam_k2, attn_subln_g, conv_w, conv_b, filt_w1, filt_b1,
           filt_w2, filt_b2, filt_w3, filt_freq, hy_bias, w_br_attn, w_br_hy, w_out, w_up,
           w_down, final_g):
    depth = w_in.shape[0]
    assert depth == 1, "single trunk layer"
    layer = 0
    lam_init = 0.8 - 0.6 * math.exp(-0.3 * layer)
    n_ctx, ctx_len, _ = x_prompt.shape
    n_lat, lat_len, _ = x_sample.shape
    past = cache_k.shape[2]

    cvec = jnp.concatenate([c_ctx[None, :], c], axis=0)
    cvec = jnp.pad(cvec, ((0, MOD_ROWS - cvec.shape[0]), (0, 0)))
    mod = _mod_call(cvec, w_ada[layer], b_ada[layer][None, :]).reshape(MOD_ROWS, 6, D_MODEL)

    w_in_b = w_in[layer].astype(BF16)
    post_w_f32 = tuple(w[layer] for w in (w_br_attn, w_br_hy, w_out, w_up, w_down))
    g1, g2, gf = norm1_g[layer][None, :], norm2_g[layer][None, :], final_g[None, :]
    lam_vecs = jnp.stack([lam_q1[layer], lam_k1[layer], lam_q2[layer], lam_k2[layer]])
    subln_g = attn_subln_g[layer][None, :]

    w1 = _pad_to(filt_w1[layer], FILT_PAD, FILT_PAD)
    w2 = _pad_to(filt_w2[layer], FILT_PAD, FILT_PAD)
    w3 = _pad_to(filt_w3[layer], FILT_PAD, filt_w3.shape[2])
    b1 = _pad_to(filt_b1[layer][None, :], 1, FILT_PAD)
    b2 = _pad_to(filt_b2[layer][None, :], 1, FILT_PAD)
    fr = _pad_to(filt_freq[layer][None, :], 1, FILT_PAD)
    deltas = jnp.asarray(np.linspace(math.log(DECAY_TARGET) / FAST_DECAY_PCT,
                                     math.log(DECAY_TARGET) / SLOW_DECAY_PCT,
                                     D_HYENA)[None, :].astype(np.float32))

    def project(x3, batch, seq, latent, cast=()):
        x = x3.reshape(batch * seq, D_MODEL)
        rope = _rope_tables(seq) if latent else None
        return x, _in_proj_call(x, mod, g1, w_in_b, rope, seq=seq, latent=latent,
                                tm=_tiles(seq)["tm"], cast=cast)

    def mix(x, outs, post_w, batch, seq, latent):
        tiles = _tiles(seq)
        tm = tiles["tm"]
        if latent:
            q, kt, v, u = outs
            ckt = jnp.transpose(cache_k[:, layer], (0, 2, 3, 4, 1)).reshape(batch * K_COLS, past)
            cache = (ckt, cache_v[:, layer].reshape(batch * past * N_HEADS, V_DIM))
            kf = vf = None
        else:
            q, kt, v, kf, vf, u = outs
            cache = None
        fwd_np, inv_np = _dft_tables(seq)
        fwd_b = jnp.asarray(fwd_np).astype(BF16)
        inv_b = jnp.asarray(inv_np).astype(BF16)
        emb, tcol = _filter_embedding(seq)
        filt = (emb, w1, b1, w2, b2, fr, w3, tcol, deltas)
        attn_operands = (q, kt, v, cache, lam_vecs, subln_g)
        hyena_operands = (u, conv_w[layer], conv_b[layer][None, :], hy_bias[layer], filt,
                          fwd_b, inv_b)
        if tiles["fuse_mixers"]:
            o_attn, o_hy = _mixer_call(attn_operands, hyena_operands, batch=batch, seq=seq,
                                       lam_init=lam_init, tq=tiles["tq"], ct=tiles["ct"],
                                       tiles_per_stage=tiles["tiles_per_stage"])
        else:
            o_attn = _attn_call(*attn_operands, batch=batch, seq=seq, lam_init=lam_init,
                                tq=tiles["tq"], heads=tiles["attn_heads"])
            o_hy = _hyena_call(*hyena_operands, batch=batch, seq=seq, ct=tiles["ct"],
                               n_chain=tiles["hyena_seqs"])

        y = _post_call(x, o_attn, o_hy, mod, g1, g2, gf, w_in_b, post_w,
                       seq=seq, latent=latent, tm=tm)
        return y.reshape(batch, seq, D_MODEL), kf, vf

    x_ctx, ctx_outs = project(x_prompt, n_ctx, ctx_len, False)
    x_lat, lat_outs = project(x_sample, n_lat, lat_len, True, cast=post_w_f32)
    lat_outs, post_w = lat_outs[:-len(post_w_f32)], lat_outs[-len(post_w_f32):]
    y_prompt, kf, vf = mix(x_ctx, ctx_outs, post_w, n_ctx, ctx_len, False)
    y_sample, _, _ = mix(x_lat, lat_outs, post_w, n_lat, lat_len, True)
    new_cache_k = jnp.transpose(kf.reshape(n_ctx, depth, N_HEADS, 2, HEAD_DIM, ctx_len),
                                (0, 1, 5, 2, 3, 4))
    new_cache_v = vf.reshape(n_ctx, depth, ctx_len, N_HEADS, V_DIM)
    return (y_prompt, y_sample, new_cache_k, new_cache_v)
```

```python
import functools
import math

import numpy as np
import jax
import jax.numpy as jnp
from jax import lax
from jax.experimental import pallas as pl
from jax.experimental.pallas import tpu as pltpu

D_MODEL = 1024
GRID_W = 64
N_HEADS = 8
HEAD_DIM = 64
V_DIM = 2 * HEAD_DIM
D_HYENA = D_MODEL // 2
HYENA_ORDER = 2
FILTER_EMB = 33
FILTER_HIDDEN = 64
D_FF = 4 * D_MODEL
ROPE_THETA = 10000.0
EPS = 1e-6
LOG2_E = math.log2(math.e)
DECAY_TARGET = 1e-2
FAST_DECAY_PCT = 0.3
SLOW_DECAY_PCT = 1.5

Q_COLS = N_HEADS * 2 * HEAD_DIM
K_COLS = N_HEADS * 2 * HEAD_DIM
V_COLS = N_HEADS * V_DIM
HY_COLS = 3 * D_HYENA
GATE_COLS = 2 * D_MODEL
Q_OFF = 0
K_OFF = Q_OFF + Q_COLS
V_OFF = K_OFF + K_COLS
HY_OFF = V_OFF + V_COLS
GATE_OFF = HY_OFF + HY_COLS
IN_COLS = GATE_OFF + GATE_COLS
GATE_WINDOW = math.gcd(GATE_OFF, D_MODEL)

LANES = 128
MXU_WIDTH = 256
MOD_ROWS = 16
FILT_PAD = 128
VMEM_MIB = dict(mod=12, in_proj=44, attn_short=12, attn_long=24, hyena_short=20, hyena_long=52,
                post=48)

BF16 = jnp.bfloat16
F32 = jnp.float32


def _resident(shape, index=None):
    index = (0,) * len(shape) if index is None else index
    return pl.BlockSpec(shape, lambda *_: index, pipeline_mode=pl.Buffered(1))


def _params(call, *sem):
    return pltpu.CompilerParams(dimension_semantics=sem,
                                vmem_limit_bytes=VMEM_MIB[call] * 1024 * 1024)


def _rms(x, g):
    return x * lax.rsqrt(jnp.mean(x * x, axis=-1, keepdims=True) + EPS) * g


def _mod_kernel(c_ref, w_ref, b_ref, o_ref):
    c = c_ref[...]
    s = (c * jax.nn.sigmoid(c)).astype(BF16)
    o_ref[...] = jnp.dot(s, w_ref[...].astype(BF16), preferred_element_type=F32) + b_ref[...]


def _mod_call(cvec, w_ada, b_ada):
    tn = 1024
    n = w_ada.shape[1]
    return pl.pallas_call(
        _mod_kernel,
        grid=(n // tn,),
        in_specs=[pl.BlockSpec((MOD_ROWS, D_MODEL), lambda j: (0, 0)),
                  pl.BlockSpec((D_MODEL, tn), lambda j: (0, j)),
                  pl.BlockSpec((1, tn), lambda j: (0, j))],
        out_specs=pl.BlockSpec((MOD_ROWS, tn), lambda j: (0, j)),
        out_shape=jax.ShapeDtypeStruct((MOD_ROWS, n), F32),
        compiler_params=_params("mod", "arbitrary"),
        name="mod",
    )(cvec, w_ada, b_ada)


def _rope(x, cos, sin_lo, sin_hi):
    return (x * cos + pltpu.roll(x, 16, axis=1) * sin_hi
            + pltpu.roll(x, LANES - 16, axis=1) * sin_lo)


def _in_proj_kernel(*refs, latent, n_cast):
    refs = list(refs)
    take = lambda n: [refs.pop(0) for _ in range(n)]
    x_ref, mod_ref, g_ref, w_ref = take(4)
    if latent:
        cos_ref, slo_ref, shi_ref = take(3)
    cast_src = take(n_cast)
    q_ref, kt_ref, v_ref = take(3)
    if not latent:
        ktf_ref, vf_ref = take(2)
    (u_ref,) = take(1)
    cast_dst = take(n_cast)
    tm = x_ref.shape[0]
    x = x_ref[...]
    h = _rms(x, g_ref[...]) * (1.0 + mod_ref[1:2, :]) + mod_ref[0:1, :]
    hb = h.astype(BF16)

    def proj(off, width):
        return jnp.dot(hb, w_ref[:, off:off + width], preferred_element_type=F32)

    q = proj(Q_OFF, Q_COLS) * (HEAD_DIM ** -0.5 * LOG2_E)
    k = proj(K_OFF, K_COLS)
    if latent:
        cos, slo, shi = cos_ref[...], slo_ref[...], shi_ref[...]
        k_heads = []
        for hd in range(N_HEADS):
            sl = slice(hd * LANES, (hd + 1) * LANES)
            q_ref[:, sl] = _rope(q[:, sl], cos, slo, shi).astype(BF16)
            k_heads.append(_rope(k[:, sl], cos, slo, shi))
        kt_ref[...] = jnp.concatenate(k_heads, axis=1).T.astype(BF16)
    else:
        q_ref[...] = q.astype(BF16)
        seq = kt_ref.shape[1]
        for s in range(tm // seq):
            kt = k[s * seq:(s + 1) * seq, :].T
            rows = slice(s * K_COLS, (s + 1) * K_COLS)
            ktf_ref[rows, :] = kt
            kt_ref[rows, :] = kt.astype(BF16)
    v = proj(V_OFF, V_COLS)
    v_ref[...] = v.astype(BF16)
    if not latent:
        for hd in range(N_HEADS):
            vf_ref[pl.ds(hd, tm, stride=N_HEADS), :] = v[:, hd * LANES:(hd + 1) * LANES]
    u_ref[...] = proj(HY_OFF, HY_COLS)
    for src, dst in zip(cast_src, cast_dst):
        dst[...] = src[...].astype(BF16)


def _in_proj_call(x, mod, g1, w_in, rope, *, seq, latent, tm, cast=()):
    t = x.shape[0]
    n_steps = t // tm
    per_seq = seq // tm if latent else 1

    def mod_idx(i):
        return (1 + i // per_seq if latent else 0, 0, 0)

    row = lambda i: (i, 0)
    in_specs = [pl.BlockSpec((tm, D_MODEL), row),
                pl.BlockSpec((None, 6, D_MODEL), mod_idx),
                _resident((1, D_MODEL)),
                _resident((D_MODEL, GATE_OFF))]
    args = [x, mod, g1, w_in]
    tok = lambda width, dtype: ((t, width), dtype, pl.BlockSpec((tm, width), row))
    if latent:
        in_specs += [pl.BlockSpec((tm, LANES), lambda i: (i % per_seq, 0))] * 3
        args += list(rope)
        kt_spec = pl.BlockSpec((K_COLS, tm), lambda i: (i // per_seq, i % per_seq))
    else:
        kt_spec = pl.BlockSpec((tm // seq * K_COLS, seq), row)
    kt_shape = (t // seq * K_COLS, seq)
    outs = [tok(Q_COLS, BF16), (kt_shape, BF16, kt_spec), tok(V_COLS, BF16)]
    if not latent:
        outs += [(kt_shape, F32, kt_spec),
                 ((t * N_HEADS, LANES), F32, pl.BlockSpec((tm * N_HEADS, LANES), row))]
    outs += [tok(HY_COLS, F32)]
    for w in cast:
        slab = pl.BlockSpec((w.shape[0] // n_steps, w.shape[1]), row)
        in_specs.append(slab)
        args.append(w)
        outs.append((w.shape, BF16, slab))
    return pl.pallas_call(
        functools.partial(_in_proj_kernel, latent=latent, n_cast=len(cast)),
        grid=(n_steps,),
        in_specs=in_specs,
        out_specs=[spec for _, _, spec in outs],
        out_shape=[jax.ShapeDtypeStruct(s, d) for s, d, _ in outs],
        compiler_params=_params("in_proj", "arbitrary"),
        name="in_proj_lat" if latent else "in_proj_ctx",
    )(*args)


def _attn_kernel(*refs, n_cache, lam_init, tq):
    if n_cache:
        (q_ref, kt_ref, v_ref, ckt_ref, cv_ref, lam_ref, g_ref, o_ref, kt_s, v2_s) = refs
    else:
        (q_ref, kt_ref, v_ref, lam_ref, g_ref, o_ref, v2_s) = refs
    n_heads, n_keys, _ = v2_s.shape
    head0 = pl.program_id(1) * n_heads
    lv = lam_ref[...]
    lam = (jnp.exp(jnp.sum(lv[0:1] * lv[1:2], axis=-1, keepdims=True))
           - jnp.exp(jnp.sum(lv[2:3] * lv[3:4], axis=-1, keepdims=True)) + lam_init)
    gain = g_ref[...] * (1.0 - lam_init)
    first_map = lax.broadcasted_iota(jnp.int32, (1, LANES), 1) < HEAD_DIM
    zero = jnp.zeros((tq, LANES), BF16)

    def operands(hd):
        cols = slice(hd * LANES, (hd + 1) * LANES)
        v2_s[hd, :, LANES:] = jnp.ones((n_keys, LANES), BF16)
        if n_cache:
            kt_s[cols, 0:n_cache] = ckt_ref[cols, :].astype(BF16)
            kt_s[cols, n_cache:] = kt_ref[cols, :]
            cache_rows = pl.ds(head0 + hd, n_cache, stride=N_HEADS)
            v2_s[hd, 0:n_cache, 0:LANES] = cv_ref[cache_rows, :].astype(BF16)
            v2_s[hd, n_cache:, 0:LANES] = v_ref[:, cols]
            return kt_s[cols, :], v2_s[hd]
        v2_s[hd, :, 0:LANES] = v_ref[:, cols]
        return kt_ref[cols, :], v2_s[hd]

    def scores(hd, i, kt):
        q = q_ref[i * tq:(i + 1) * tq, hd * LANES:(hd + 1) * LANES]
        q2 = jnp.concatenate([jnp.where(first_map, q, zero), jnp.where(first_map, zero, q)], axis=0)
        s = jnp.dot(q2, kt, preferred_element_type=F32)
        return s, jnp.max(s, axis=-1, keepdims=True)

    def finish(hd, i, v2, s, m):
        p = jnp.exp2(s - m).astype(BF16)
        pv = jnp.dot(p, v2, preferred_element_type=F32)
        pv = pv[:, :LANES] / pv[:, LANES:]
        o = pv[:tq] - lam * pv[tq:]
        o_ref[i * tq:(i + 1) * tq, hd * LANES:(hd + 1) * LANES] = (_rms(o, 1.0) * gain).astype(BF16)

    n_tiles = q_ref.shape[0] // tq
    pending = None
    for hd in range(n_heads):
        kt, v2 = operands(hd)
        for i in range(n_tiles):
            cur = (hd, i, v2) + scores(hd, i, kt)
            if pending is not None:
                finish(*pending)
            pending = cur
    finish(*pending)


def _attn_call(q, kt, v, cache, lam_vecs, subln_g, *, batch, seq, lam_init, tq, heads):
    const = lambda b, h: (0, 0)
    width = heads * LANES
    n_groups = N_HEADS // heads
    in_specs = [pl.BlockSpec((seq, width), lambda b, h: (b, h)),
                pl.BlockSpec((width, seq), lambda b, h: (b * n_groups + h, 0)),
                pl.BlockSpec((seq, width), lambda b, h: (b, h))]
    args = [q, kt, v]
    n_cache = 0
    if cache is not None:
        ckt, cv = cache
        n_cache = ckt.shape[1]
        in_specs += [pl.BlockSpec((width, n_cache), lambda b, h: (b * n_groups + h, 0)),
                     pl.BlockSpec((n_cache * N_HEADS, LANES), lambda b, h: (b, 0))]
        args += [ckt, cv]
    n_keys = n_cache + seq
    scratch = [pltpu.VMEM((heads, n_keys, 2 * LANES), BF16)]
    if n_cache:
        scratch = [pltpu.VMEM((width, n_keys), BF16)] + scratch
    in_specs += [pl.BlockSpec(lam_vecs.shape, const), pl.BlockSpec((1, LANES), const)]
    args += [lam_vecs, subln_g]
    return pl.pallas_call(
        functools.partial(_attn_kernel, n_cache=n_cache, lam_init=lam_init, tq=tq),
        grid=(batch, n_groups),
        in_specs=in_specs,
        out_specs=pl.BlockSpec((seq, width), lambda b, h: (b, h)),
        out_shape=jax.ShapeDtypeStruct((batch * seq, N_HEADS * V_DIM), BF16),
        scratch_shapes=scratch,
        compiler_params=_params("attn_long" if n_cache else "attn_short", "arbitrary", "arbitrary"),
        name="attn_lat" if cache is not None else "attn_ctx",
    )(*args)


def _dft_tables(seq):
    n = 2 * seq
    f = np.arange(seq, dtype=np.int64)[:, None]
    t = np.arange(seq, dtype=np.int64)[None, :]
    ang = 2.0 * np.pi * ((f * t) % n).astype(np.float64) / n
    alt = np.where(np.arange(seq) % 2 == 0, 1.0, -1.0)
    cos, msin = np.cos(ang), -np.sin(ang)
    msin[0, :] = alt
    fwd = np.concatenate([cos, msin], axis=0)
    wgt = np.where(f == 0, 1.0, 2.0) / n
    inv_c = (cos * wgt).T
    inv_s = (msin * wgt).T
    inv = np.concatenate([inv_c, inv_s], axis=1)
    return fwd.astype(np.float32), inv.astype(np.float32)


def _filter_hidden(emb_ref, w1_ref, b1_ref, w2_ref, b2_ref, fr_ref):
    hp = functools.partial(jnp.dot, preferred_element_type=F32, precision=lax.Precision.HIGHEST)
    fr = fr_ref[...]
    h = jnp.sin(fr * (hp(emb_ref[...], w1_ref[...]) + b1_ref[...]))
    return jnp.sin(fr * (hp(h, w2_ref[...]) + b2_ref[...]))


def _filter_planes(h, w3f_ref, w3b_ref, decay, dft_ref, out_ref):
    seq = h.shape[0]
    fwd = jnp.dot(h, w3f_ref[...].astype(BF16), preferred_element_type=F32) * decay
    bwd = jnp.dot(h, w3b_ref[...].astype(BF16), preferred_element_type=F32) * decay
    hsum, hdif = fwd + bwd, fwd - bwd
    re = jnp.dot(dft_ref[0:seq, :], hsum.astype(BF16), preferred_element_type=F32)
    im = jnp.dot(dft_ref[seq:, :], hdif.astype(BF16), preferred_element_type=F32)
    row = lax.broadcasted_iota(jnp.int32, (seq, 1), 0)
    alt = jnp.where(row % 2 == 0, 1.0, -1.0)
    nyq = jnp.sum(hsum * alt, axis=0, keepdims=True)
    out_ref[0] = re
    out_ref[1] = jnp.where(row == 0, 0.0, im)
    out_ref[2] = jnp.where(row == 0, nyq, re)


def _hyena_kernel(uv_ref, u1_ref, u2_ref, wv_ref, w1_ref, w2_ref, bv_ref, b1_ref, b2_ref, hb_ref,
                  emb_ref, fw1_ref, fb1_ref, fw2_ref, fb2_ref, fr_ref,
                  w3f0_ref, w3b0_ref, w3f1_ref, w3b1_ref, t_ref, dl_ref, fwd_s, inv_s,
                  o_ref, hid_s, hf_s):
    seq = emb_ref.shape[0]
    new_tile = pl.program_id(1) == 0

    @pl.when(jnp.logical_and(pl.program_id(0) == 0, new_tile))
    def _():
        hid_s[...] = _filter_hidden(emb_ref, fw1_ref, fb1_ref, fw2_ref, fb2_ref, fr_ref)

    @pl.when(new_tile)
    def _():
        h = hid_s[...].astype(BF16)
        decay = jnp.exp(-t_ref[...] * jnp.abs(dl_ref[...]))
        _filter_planes(h, w3f0_ref, w3b0_ref, decay, fwd_s, hf_s.at[0])
        _filter_planes(h, w3f1_ref, w3b1_ref, decay, fwd_s, hf_s.at[1])

    chains = [slice(c * seq, (c + 1) * seq) for c in range(uv_ref.shape[0] // seq)]
    row = lax.broadcasted_iota(jnp.int32, (seq, 1), 0)
    first, last = row == 0, row == seq - 1

    def short_conv(u_ref, w_ref, b_ref):
        outs = []
        for rows in chains:
            u = u_ref[rows, :]
            prev = jnp.where(first, 0.0, pltpu.roll(u, 1, axis=0))
            nxt = jnp.where(last, 0.0, pltpu.roll(u, seq - 1, axis=0))
            outs.append(prev * w_ref[0:1, :] + u * w_ref[1:2, :] + nxt * w_ref[2:3, :] + b_ref[...])
        return outs

    def long_conv(us, order):
        h_ref = hf_s.at[order]
        bias = hb_ref[order:order + 1, :]
        specs = [jnp.dot(fwd_s[...], u.astype(BF16), preferred_element_type=F32) for u in us]
        ys = []
        for spec in specs:
            ure, uim = spec[:seq], spec[seq:]
            yre = ure * h_ref[0] - uim * h_ref[1]
            yim = ure * h_ref[1] + uim * h_ref[2]
            ys.append(jnp.concatenate([yre, yim], axis=0).astype(BF16))
        return [jnp.dot(inv_s[...], y, preferred_element_type=F32) + u * bias
                for y, u in zip(ys, us)]

    v = short_conv(uv_ref, wv_ref, bv_ref)
    x1 = short_conv(u1_ref, w1_ref, b1_ref)
    z = [a * b for a, b in zip(x1, long_conv(v, 0))]
    x2 = short_conv(u2_ref, w2_ref, b2_ref)
    for rows, a, b in zip(chains, x2, long_conv(z, 1)):
        o_ref[rows, :] = (a * b).astype(BF16)


def _hyena_call(u, conv_w, conv_b, hy_bias, filt, fwd, inv, *, batch, seq, ct, n_chain):
    emb, w1, b1, w2, b2, fr, w3, tcol, deltas = filt
    nct = D_HYENA // ct
    rows = n_chain * seq
    col = lambda part: (lambda c, b: (b, part * nct + c))
    wcol = lambda part: (lambda c, b: (0, part * nct + c))
    small = lambda shape: pl.BlockSpec(shape, lambda c, b: (0, 0))
    w3col = lambda direction, order: (lambda c, b: (0, (direction * HYENA_ORDER + order) * nct + c))
    in_specs = ([pl.BlockSpec((rows, ct), col(p)) for p in range(3)]
                + [pl.BlockSpec((3, ct), wcol(p)) for p in range(3)]
                + [pl.BlockSpec((1, ct), wcol(p)) for p in range(3)]
                + [pl.BlockSpec((HYENA_ORDER, ct), lambda c, b: (0, c)),
                   small((seq, FILT_PAD)), small((FILT_PAD, FILT_PAD)), small((1, FILT_PAD)),
                   small((FILT_PAD, FILT_PAD)), small((1, FILT_PAD)), small((1, FILT_PAD))]
                + [pl.BlockSpec((FILT_PAD, ct), w3col(direction, order))
                   for order in range(HYENA_ORDER) for direction in range(2)]
                + [small((seq, 1)), pl.BlockSpec((1, ct), lambda c, b: (0, c)),
                   _resident((2 * seq, seq)), _resident((seq, 2 * seq))])
    return pl.pallas_call(
        _hyena_kernel,
        grid=(nct, batch // n_chain),
        in_specs=in_specs,
        out_specs=pl.BlockSpec((rows, ct), lambda c, b: (b, c)),
        out_shape=jax.ShapeDtypeStruct((batch * seq, D_HYENA), BF16),
        scratch_shapes=[pltpu.VMEM((seq, FILT_PAD), F32),
                        pltpu.VMEM((HYENA_ORDER, 3, seq, ct), F32)],
        compiler_params=_params("hyena_short" if seq <= 256 else "hyena_long",
                                "arbitrary", "arbitrary"),
        name=f"hyena_{seq}",
    )(u, u, u, conv_w, conv_w, conv_w, conv_b, conv_b, conv_b, hy_bias,
      emb, w1, b1, w2, b2, fr, w3, w3, w3, w3, tcol, deltas, fwd, inv)


FF_CHUNK = D_MODEL
N_FF_CHUNKS = D_FF // FF_CHUNK


def _post_kernel(x_ref, oa_ref, oh_ref, mod_ref, g1_ref, g2_ref, gf_ref,
                 wga0_ref, wga1_ref, wgh0_ref, wgh1_ref,
                 wa_ref, wh_ref, wo_ref, wu_ref, wd_ref, y_ref):
    d = functools.partial(jnp.dot, preferred_element_type=F32)
    x = x_ref[...]
    h1 = (_rms(x, g1_ref[...]) * (1.0 + mod_ref[1:2, :]) + mod_ref[0:1, :]).astype(BF16)

    def gate(*w_refs):
        return jnp.concatenate([jax.nn.sigmoid(d(h1, w[...])) for w in w_refs], axis=1)

    merged = (gate(wga0_ref, wga1_ref) * d(oa_ref[...], wa_ref[...])
              + gate(wgh0_ref, wgh1_ref) * d(oh_ref[...], wh_ref[...]))
    x = x + mod_ref[2:3, :] * d(merged.astype(BF16), wo_ref[...])
    h2 = (_rms(x, g2_ref[...]) * (1.0 + mod_ref[4:5, :]) + mod_ref[3:4, :]).astype(BF16)
    acc = jnp.zeros_like(x)
    for c in range(N_FF_CHUNKS):
        sl = slice(c * FF_CHUNK, (c + 1) * FF_CHUNK)
        up = jnp.maximum(d(h2, wu_ref[:, sl]), 0.0)
        acc = acc + d((up * up).astype(BF16), wd_ref[sl, :])
    x = x + mod_ref[5:6, :] * acc
    y_ref[...] = _rms(x, gf_ref[...])


def _post_call(x, o_attn, o_hy, mod, g1, g2, gf, w_in, weights, *, seq, latent, tm):
    t = x.shape[0]
    gate_windows = [_resident((D_MODEL, GATE_WINDOW), (0, GATE_OFF // GATE_WINDOW + n))
                    for n in range(GATE_COLS // GATE_WINDOW)]
    per_seq = seq // tm if latent else 1
    row = lambda i: (i, 0)

    def mod_idx(i):
        return (1 + i // per_seq if latent else 0, 0, 0)

    return pl.pallas_call(
        _post_kernel,
        grid=(t // tm,),
        in_specs=[pl.BlockSpec((tm, D_MODEL), row),
                  pl.BlockSpec((tm, N_HEADS * V_DIM), row),
                  pl.BlockSpec((tm, D_HYENA), row),
                  pl.BlockSpec((None, 6, D_MODEL), mod_idx),
                  _resident((1, D_MODEL)), _resident((1, D_MODEL)), _resident((1, D_MODEL))]
                 + gate_windows + [_resident(w.shape) for w in weights],
        out_specs=pl.BlockSpec((tm, D_MODEL), row),
        out_shape=jax.ShapeDtypeStruct((t, D_MODEL), F32),
        compiler_params=_params("post", "arbitrary"),
        name="post_lat" if latent else "post_ctx",
    )(x, o_attn, o_hy, mod, g1, g2, gf, *[w_in] * len(gate_windows), *weights)


def _rope_tables(seq):
    half = HEAD_DIM // 2
    n = half // 2
    inv = ROPE_THETA ** (-np.arange(n, dtype=np.float64) / n)
    pos = np.arange(seq)
    ang_row = (pos // GRID_W).astype(np.float64)[:, None] * inv[None, :]
    ang_col = (pos % GRID_W).astype(np.float64)[:, None] * inv[None, :]
    zeros = np.zeros_like(ang_row)

    def per_map(a_row, a_col, lo, hi):
        return np.concatenate([lo(a_row), hi(a_row), lo(a_col), hi(a_col)], axis=-1)

    cos = per_map(ang_row, ang_col, np.cos, np.cos)
    sin_lo = per_map(ang_row, ang_col, lambda a: -np.sin(a), lambda a: zeros)
    sin_hi = per_map(ang_row, ang_col, lambda a: zeros, np.sin)
    return tuple(jnp.asarray(np.concatenate([tab, tab], axis=-1).astype(np.float32))
                 for tab in (cos, sin_lo, sin_hi))


def _filter_embedding(seq):
    bands = (FILTER_EMB - 1) // 2
    t = np.linspace(0.0, 1.0, seq)[:, None]
    wpos = 2.0 * np.pi * np.arange(seq, dtype=np.float64)[:, None] / seq
    f = np.linspace(1e-4, bands - 1, bands)[None, :]
    emb = np.concatenate([t, np.cos(f * wpos), -np.sin(f * wpos)], axis=-1)
    emb = np.pad(emb, ((0, 0), (0, FILT_PAD - FILTER_EMB)))
    return jnp.asarray(emb.astype(np.float32)), jnp.asarray(t.astype(np.float32))


def _tiles(seq):
    short = seq <= 256
    return dict(tm=512, tq=128, attn_heads=N_HEADS if short else 4, ct=MXU_WIDTH,
                hyena_seqs=4 if short else 2)


def _pad_to(x, rows, cols):
    return jnp.pad(x, ((0, rows - x.shape[0]), (0, cols - x.shape[1])))


def kernel(x_prompt, x_sample, cache_k, cache_v, c, c_ctx, w_ada, b_ada, norm1_g, norm2_g, w_in,
           lam_q1, lam_k1, lam_q2, lam_k2, attn_subln_g, conv_w, conv_b, filt_w1, filt_b1,
           filt_w2, filt_b2, filt_w3, filt_freq, hy_bias, w_br_attn, w_br_hy, w_out, w_up,
           w_down, final_g):
    depth = w_in.shape[0]
    assert depth == 1, "single trunk layer"
    layer = 0
    lam_init = 0.8 - 0.6 * math.exp(-0.3 * layer)
    n_ctx, ctx_len, _ = x_prompt.shape
    n_lat, lat_len, _ = x_sample.shape
    past = cache_k.shape[2]

    cvec = jnp.concatenate([c_ctx[None, :], c], axis=0)
    cvec = jnp.pad(cvec, ((0, MOD_ROWS - cvec.shape[0]), (0, 0)))
    mod = _mod_call(cvec, w_ada[layer], b_ada[layer][None, :]).reshape(MOD_ROWS, 6, D_MODEL)

    w_in_b = w_in[layer].astype(BF16)
    post_w_f32 = tuple(w[layer] for w in (w_br_attn, w_br_hy, w_out, w_up, w_down))
    g1, g2, gf = norm1_g[layer][None, :], norm2_g[layer][None, :], final_g[None, :]
    lam_vecs = jnp.stack([lam_q1[layer], lam_k1[layer], lam_q2[layer], lam_k2[layer]])
    subln_g = attn_subln_g[layer][None, :]

    w1 = _pad_to(filt_w1[layer], FILT_PAD, FILT_PAD)
    w2 = _pad_to(filt_w2[layer], FILT_PAD, FILT_PAD)
    w3 = _pad_to(filt_w3[layer], FILT_PAD, filt_w3.shape[2])
    b1 = _pad_to(filt_b1[layer][None, :], 1, FILT_PAD)
    b2 = _pad_to(filt_b2[layer][None, :], 1, FILT_PAD)
    fr = _pad_to(filt_freq[layer][None, :], 1, FILT_PAD)
    deltas = jnp.asarray(np.linspace(math.log(DECAY_TARGET) / FAST_DECAY_PCT,
                                     math.log(DECAY_TARGET) / SLOW_DECAY_PCT,
                                     D_HYENA)[None, :].astype(np.float32))

    def project(x3, batch, seq, latent, cast=()):
        x = x3.reshape(batch * seq, D_MODEL)
        rope = _rope_tables(seq) if latent else None
        return x, _in_proj_call(x, mod, g1, w_in_b, rope, seq=seq, latent=latent,
                                tm=_tiles(seq)["tm"], cast=cast)

    def mix(x, outs, post_w, batch, seq, latent):
        tiles = _tiles(seq)
        tm = tiles["tm"]
        if latent:
            q, kt, v, u = outs
            ckt = jnp.transpose(cache_k[:, layer], (0, 2, 3, 4, 1)).reshape(batch * K_COLS, past)
            cache = (ckt, cache_v[:, layer].reshape(batch * past * N_HEADS, V_DIM))
            kf = vf = None
        else:
            q, kt, v, kf, vf, u = outs
            cache = None
        o_attn = _attn_call(q, kt, v, cache, lam_vecs, subln_g, batch=batch, seq=seq,
                            lam_init=lam_init, tq=tiles["tq"], heads=tiles["attn_heads"])

        fwd_np, inv_np = _dft_tables(seq)
        fwd_b = jnp.asarray(fwd_np).astype(BF16)
        inv_b = jnp.asarray(inv_np).astype(BF16)
        emb, tcol = _filter_embedding(seq)
        filt = (emb, w1, b1, w2, b2, fr, w3, tcol, deltas)
        o_hy = _hyena_call(u, conv_w[layer], conv_b[layer][None, :], hy_bias[layer], filt,
                           fwd_b, inv_b, batch=batch, seq=seq, ct=tiles["ct"],
                           n_chain=tiles["hyena_seqs"])

        y = _post_call(x, o_attn, o_hy, mod, g1, g2, gf, w_in_b, post_w,
                       seq=seq, latent=latent, tm=tm)
        return y.reshape(batch, seq, D_MODEL), kf, vf

    x_ctx, ctx_outs = project(x_prompt, n_ctx, ctx_len, False)
    x_lat, lat_outs = project(x_sample, n_lat, lat_len, True, cast=post_w_f32)
    lat_outs, post_w = lat_outs[:-len(post_w_f32)], lat_outs[-len(post_w_f32):]
    y_prompt, kf, vf = mix(x_ctx, ctx_outs, post_w, n_ctx, ctx_len, False)
    y_sample, _, _ = mix(x_lat, lat_outs, post_w, n_lat, lat_len, True)
    new_cache_k = jnp.transpose(kf.reshape(n_ctx, depth, N_HEADS, 2, HEAD_DIM, ctx_len),
                                (0, 1, 5, 2, 3, 4))
    new_cache_v = vf.reshape(n_ctx, depth, ctx_len, N_HEADS, V_DIM)
    return (y_prompt, y_sample, new_cache_k, new_cache_v)
```

```python
import functools
import math

import numpy as np
import jax
import jax.numpy as jnp
from jax import lax
from jax.experimental import pallas as pl
from jax.experimental.pallas import tpu as pltpu

D_MODEL = 1024
GRID_W = 64
N_HEADS = 8
HEAD_DIM = 64
V_DIM = 2 * HEAD_DIM
D_HYENA = D_MODEL // 2
HYENA_ORDER = 2
FILTER_EMB = 33
FILTER_HIDDEN = 64
D_FF = 4 * D_MODEL
ROPE_THETA = 10000.0
EPS = 1e-6
LOG2_E = math.log2(math.e)
DECAY_TARGET = 1e-2
FAST_DECAY_PCT = 0.3
SLOW_DECAY_PCT = 1.5

Q_COLS = N_HEADS * 2 * HEAD_DIM
K_COLS = N_HEADS * 2 * HEAD_DIM
V_COLS = N_HEADS * V_DIM
HY_COLS = 3 * D_HYENA
GATE_COLS = 2 * D_MODEL
Q_OFF = 0
K_OFF = Q_OFF + Q_COLS
V_OFF = K_OFF + K_COLS
HY_OFF = V_OFF + V_COLS
GATE_OFF = HY_OFF + HY_COLS
IN_COLS = GATE_OFF + GATE_COLS
GATE_WINDOW = math.gcd(GATE_OFF, D_MODEL)

LANES = 128
MXU_WIDTH = 256
MOD_ROWS = 16
FILT_PAD = 128
VMEM_LIMIT = 56 * 1024 * 1024

BF16 = jnp.bfloat16
F32 = jnp.float32


def _resident(shape, index=None):
    index = (0,) * len(shape) if index is None else index
    return pl.BlockSpec(shape, lambda *_: index, pipeline_mode=pl.Buffered(1))


def _params(*sem):
    return pltpu.CompilerParams(dimension_semantics=sem, vmem_limit_bytes=VMEM_LIMIT)


def _rms(x, g):
    return x * lax.rsqrt(jnp.mean(x * x, axis=-1, keepdims=True) + EPS) * g


def _mod_kernel(c_ref, w_ref, b_ref, o_ref):
    c = c_ref[...]
    s = (c * jax.nn.sigmoid(c)).astype(BF16)
    o_ref[...] = jnp.dot(s, w_ref[...].astype(BF16), preferred_element_type=F32) + b_ref[...]


def _mod_call(cvec, w_ada, b_ada):
    tn = 1024
    n = w_ada.shape[1]
    return pl.pallas_call(
        _mod_kernel,
        grid=(n // tn,),
        in_specs=[pl.BlockSpec((MOD_ROWS, D_MODEL), lambda j: (0, 0)),
                  pl.BlockSpec((D_MODEL, tn), lambda j: (0, j)),
                  pl.BlockSpec((1, tn), lambda j: (0, j))],
        out_specs=pl.BlockSpec((MOD_ROWS, tn), lambda j: (0, j)),
        out_shape=jax.ShapeDtypeStruct((MOD_ROWS, n), F32),
        compiler_params=_params("arbitrary"),
        name="mod",
    )(cvec, w_ada, b_ada)


def _rope(x, cos, sin_lo, sin_hi):
    return (x * cos + pltpu.roll(x, 16, axis=1) * sin_hi
            + pltpu.roll(x, LANES - 16, axis=1) * sin_lo)


def _in_proj_kernel(*refs, latent, n_cast):
    refs = list(refs)
    take = lambda n: [refs.pop(0) for _ in range(n)]
    x_ref, mod_ref, g_ref, w_ref = take(4)
    if latent:
        cos_ref, slo_ref, shi_ref = take(3)
    cast_src = take(n_cast)
    q_ref, kt_ref, v_ref = take(3)
    if not latent:
        ktf_ref, vf_ref = take(2)
    (u_ref,) = take(1)
    cast_dst = take(n_cast)
    tm = x_ref.shape[0]
    x = x_ref[...]
    h = _rms(x, g_ref[...]) * (1.0 + mod_ref[1:2, :]) + mod_ref[0:1, :]
    hb = h.astype(BF16)

    def proj(off, width):
        return jnp.dot(hb, w_ref[:, off:off + width], preferred_element_type=F32)

    q = proj(Q_OFF, Q_COLS) * (HEAD_DIM ** -0.5 * LOG2_E)
    k = proj(K_OFF, K_COLS)
    if latent:
        cos, slo, shi = cos_ref[...], slo_ref[...], shi_ref[...]
        k_heads = []
        for hd in range(N_HEADS):
            sl = slice(hd * LANES, (hd + 1) * LANES)
            q_ref[:, sl] = _rope(q[:, sl], cos, slo, shi).astype(BF16)
            k_heads.append(_rope(k[:, sl], cos, slo, shi))
        kt_ref[...] = jnp.concatenate(k_heads, axis=1).T.astype(BF16)
    else:
        q_ref[...] = q.astype(BF16)
        seq = kt_ref.shape[1]
        for s in range(tm // seq):
            kt = k[s * seq:(s + 1) * seq, :].T
            rows = slice(s * K_COLS, (s + 1) * K_COLS)
            ktf_ref[rows, :] = kt
            kt_ref[rows, :] = kt.astype(BF16)
    v = proj(V_OFF, V_COLS)
    v_ref[...] = v.astype(BF16)
    if not latent:
        for hd in range(N_HEADS):
            vf_ref[pl.ds(hd, tm, stride=N_HEADS), :] = v[:, hd * LANES:(hd + 1) * LANES]
    u_ref[...] = proj(HY_OFF, HY_COLS)
    for src, dst in zip(cast_src, cast_dst):
        dst[...] = src[...].astype(BF16)


def _in_proj_call(x, mod, g1, w_in, rope, *, seq, latent, tm, cast=()):
    t = x.shape[0]
    n_steps = t // tm
    per_seq = seq // tm if latent else 1

    def mod_idx(i):
        return (1 + i // per_seq if latent else 0, 0, 0)

    row = lambda i: (i, 0)
    in_specs = [pl.BlockSpec((tm, D_MODEL), row),
                pl.BlockSpec((None, 6, D_MODEL), mod_idx),
                _resident((1, D_MODEL)),
                _resident((D_MODEL, GATE_OFF))]
    args = [x, mod, g1, w_in]
    tok = lambda width, dtype: ((t, width), dtype, pl.BlockSpec((tm, width), row))
    if latent:
        in_specs += [pl.BlockSpec((tm, LANES), lambda i: (i % per_seq, 0))] * 3
        args += list(rope)
        kt_spec = pl.BlockSpec((K_COLS, tm), lambda i: (i // per_seq, i % per_seq))
    else:
        kt_spec = pl.BlockSpec((tm // seq * K_COLS, seq), row)
    kt_shape = (t // seq * K_COLS, seq)
    outs = [tok(Q_COLS, BF16), (kt_shape, BF16, kt_spec), tok(V_COLS, BF16)]
    if not latent:
        outs += [(kt_shape, F32, kt_spec),
                 ((t * N_HEADS, LANES), F32, pl.BlockSpec((tm * N_HEADS, LANES), row))]
    outs += [tok(HY_COLS, F32)]
    for w in cast:
        slab = pl.BlockSpec((w.shape[0] // n_steps, w.shape[1]), row)
        in_specs.append(slab)
        args.append(w)
        outs.append((w.shape, BF16, slab))
    return pl.pallas_call(
        functools.partial(_in_proj_kernel, latent=latent, n_cast=len(cast)),
        grid=(n_steps,),
        in_specs=in_specs,
        out_specs=[spec for _, _, spec in outs],
        out_shape=[jax.ShapeDtypeStruct(s, d) for s, d, _ in outs],
        compiler_params=_params("arbitrary"),
        name="in_proj_lat" if latent else "in_proj_ctx",
    )(*args)


def _attn_kernel(*refs, n_cache, lam_init, tq):
    if n_cache:
        (q_ref, kt_ref, v_ref, ckt_ref, cv_ref, lam_ref, g_ref, o_ref, kt_s, v2_s) = refs
    else:
        (q_ref, kt_ref, v_ref, lam_ref, g_ref, o_ref, v2_s) = refs
    n_heads, n_keys, _ = v2_s.shape
    head0 = pl.program_id(1) * n_heads
    lv = lam_ref[...]
    lam = (jnp.exp(jnp.sum(lv[0:1] * lv[1:2], axis=-1, keepdims=True))
           - jnp.exp(jnp.sum(lv[2:3] * lv[3:4], axis=-1, keepdims=True)) + lam_init)
    gain = g_ref[...] * (1.0 - lam_init)
    first_map = lax.broadcasted_iota(jnp.int32, (1, LANES), 1) < HEAD_DIM
    zero = jnp.zeros((tq, LANES), BF16)

    def operands(hd):
        cols = slice(hd * LANES, (hd + 1) * LANES)
        v2_s[hd, :, LANES:] = jnp.ones((n_keys, LANES), BF16)
        if n_cache:
            kt_s[cols, 0:n_cache] = ckt_ref[cols, :].astype(BF16)
            kt_s[cols, n_cache:] = kt_ref[cols, :]
            cache_rows = pl.ds(head0 + hd, n_cache, stride=N_HEADS)
            v2_s[hd, 0:n_cache, 0:LANES] = cv_ref[cache_rows, :].astype(BF16)
            v2_s[hd, n_cache:, 0:LANES] = v_ref[:, cols]
            return kt_s[cols, :], v2_s[hd]
        v2_s[hd, :, 0:LANES] = v_ref[:, cols]
        return kt_ref[cols, :], v2_s[hd]

    def scores(hd, i, kt):
        q = q_ref[i * tq:(i + 1) * tq, hd * LANES:(hd + 1) * LANES]
        q2 = jnp.concatenate([jnp.where(first_map, q, zero), jnp.where(first_map, zero, q)], axis=0)
        s = jnp.dot(q2, kt, preferred_element_type=F32)
        return s, jnp.max(s, axis=-1, keepdims=True)

    def finish(hd, i, v2, s, m):
        p = jnp.exp2(s - m).astype(BF16)
        pv = jnp.dot(p, v2, preferred_element_type=F32)
        pv = pv[:, :LANES] / pv[:, LANES:]
        o = pv[:tq] - lam * pv[tq:]
        o_ref[i * tq:(i + 1) * tq, hd * LANES:(hd + 1) * LANES] = (_rms(o, 1.0) * gain).astype(BF16)

    n_tiles = q_ref.shape[0] // tq
    pending = None
    for hd in range(n_heads):
        kt, v2 = operands(hd)
        for i in range(n_tiles):
            cur = (hd, i, v2) + scores(hd, i, kt)
            if pending is not None:
                finish(*pending)
            pending = cur
    finish(*pending)


def _attn_call(q, kt, v, cache, lam_vecs, subln_g, *, batch, seq, lam_init, tq, heads):
    const = lambda b, h: (0, 0)
    width = heads * LANES
    n_groups = N_HEADS // heads
    in_specs = [pl.BlockSpec((seq, width), lambda b, h: (b, h)),
                pl.BlockSpec((width, seq), lambda b, h: (b * n_groups + h, 0)),
                pl.BlockSpec((seq, width), lambda b, h: (b, h))]
    args = [q, kt, v]
    n_cache = 0
    if cache is not None:
        ckt, cv = cache
        n_cache = ckt.shape[1]
        in_specs += [pl.BlockSpec((width, n_cache), lambda b, h: (b * n_groups + h, 0)),
                     pl.BlockSpec((n_cache * N_HEADS, LANES), lambda b, h: (b, 0))]
        args += [ckt, cv]
    n_keys = n_cache + seq
    scratch = [pltpu.VMEM((heads, n_keys, 2 * LANES), BF16)]
    if n_cache:
        scratch = [pltpu.VMEM((width, n_keys), BF16)] + scratch
    in_specs += [pl.BlockSpec(lam_vecs.shape, const), pl.BlockSpec((1, LANES), const)]
    args += [lam_vecs, subln_g]
    return pl.pallas_call(
        functools.partial(_attn_kernel, n_cache=n_cache, lam_init=lam_init, tq=tq),
        grid=(batch, n_groups),
        in_specs=in_specs,
        out_specs=pl.BlockSpec((seq, width), lambda b, h: (b, h)),
        out_shape=jax.ShapeDtypeStruct((batch * seq, N_HEADS * V_DIM), BF16),
        scratch_shapes=scratch,
        compiler_params=_params("arbitrary", "arbitrary"),
        name="attn_lat" if cache is not None else "attn_ctx",
    )(*args)


def _dft_tables(seq):
    n = 2 * seq
    f = np.arange(seq, dtype=np.int64)[:, None]
    t = np.arange(seq, dtype=np.int64)[None, :]
    ang = 2.0 * np.pi * ((f * t) % n).astype(np.float64) / n
    alt = np.where(np.arange(seq) % 2 == 0, 1.0, -1.0)
    cos, msin = np.cos(ang), -np.sin(ang)
    msin[0, :] = alt
    fwd = np.concatenate([cos, msin], axis=0)
    wgt = np.where(f == 0, 1.0, 2.0) / n
    inv_c = (cos * wgt).T
    inv_s = (msin * wgt).T
    inv = np.concatenate([inv_c, inv_s], axis=1)
    return fwd.astype(np.float32), inv.astype(np.float32)


def _filter_hidden(emb_ref, w1_ref, b1_ref, w2_ref, b2_ref, fr_ref):
    hp = functools.partial(jnp.dot, preferred_element_type=F32, precision=lax.Precision.HIGHEST)
    fr = fr_ref[...]
    h = jnp.sin(fr * (hp(emb_ref[...], w1_ref[...]) + b1_ref[...]))
    return jnp.sin(fr * (hp(h, w2_ref[...]) + b2_ref[...]))


def _filter_planes(h, w3f_ref, w3b_ref, decay, dft_ref, out_ref):
    seq = h.shape[0]
    fwd = jnp.dot(h, w3f_ref[...].astype(BF16), preferred_element_type=F32) * decay
    bwd = jnp.dot(h, w3b_ref[...].astype(BF16), preferred_element_type=F32) * decay
    hsum, hdif = fwd + bwd, fwd - bwd
    re = jnp.dot(dft_ref[0:seq, :], hsum.astype(BF16), preferred_element_type=F32)
    im = jnp.dot(dft_ref[seq:, :], hdif.astype(BF16), preferred_element_type=F32)
    row = lax.broadcasted_iota(jnp.int32, (seq, 1), 0)
    alt = jnp.where(row % 2 == 0, 1.0, -1.0)
    nyq = jnp.sum(hsum * alt, axis=0, keepdims=True)
    out_ref[0] = re
    out_ref[1] = jnp.where(row == 0, 0.0, im)
    out_ref[2] = jnp.where(row == 0, nyq, re)


def _hyena_kernel(uv_ref, u1_ref, u2_ref, wv_ref, w1_ref, w2_ref, bv_ref, b1_ref, b2_ref, hb_ref,
                  emb_ref, fw1_ref, fb1_ref, fw2_ref, fb2_ref, fr_ref,
                  w3f0_ref, w3b0_ref, w3f1_ref, w3b1_ref, t_ref, dl_ref, fwd_s, inv_s,
                  o_ref, hid_s, hf_s):
    seq = emb_ref.shape[0]
    new_tile = pl.program_id(1) == 0

    @pl.when(jnp.logical_and(pl.program_id(0) == 0, new_tile))
    def _():
        hid_s[...] = _filter_hidden(emb_ref, fw1_ref, fb1_ref, fw2_ref, fb2_ref, fr_ref)

    @pl.when(new_tile)
    def _():
        h = hid_s[...].astype(BF16)
        decay = jnp.exp(-t_ref[...] * jnp.abs(dl_ref[...]))
        _filter_planes(h, w3f0_ref, w3b0_ref, decay, fwd_s, hf_s.at[0])
        _filter_planes(h, w3f1_ref, w3b1_ref, decay, fwd_s, hf_s.at[1])

    chains = [slice(c * seq, (c + 1) * seq) for c in range(uv_ref.shape[0] // seq)]
    row = lax.broadcasted_iota(jnp.int32, (seq, 1), 0)
    first, last = row == 0, row == seq - 1

    def short_conv(u_ref, w_ref, b_ref):
        outs = []
        for rows in chains:
            u = u_ref[rows, :]
            prev = jnp.where(first, 0.0, pltpu.roll(u, 1, axis=0))
            nxt = jnp.where(last, 0.0, pltpu.roll(u, seq - 1, axis=0))
            outs.append(prev * w_ref[0:1, :] + u * w_ref[1:2, :] + nxt * w_ref[2:3, :] + b_ref[...])
        return outs

    def long_conv(us, order):
        h_ref = hf_s.at[order]
        bias = hb_ref[order:order + 1, :]
        specs = [jnp.dot(fwd_s[...], u.astype(BF16), preferred_element_type=F32) for u in us]
        ys = []
        for spec in specs:
            ure, uim = spec[:seq], spec[seq:]
            yre = ure * h_ref[0] - uim * h_ref[1]
            yim = ure * h_ref[1] + uim * h_ref[2]
            ys.append(jnp.concatenate([yre, yim], axis=0).astype(BF16))
        return [jnp.dot(inv_s[...], y, preferred_element_type=F32) + u * bias
                for y, u in zip(ys, us)]

    v = short_conv(uv_ref, wv_ref, bv_ref)
    x1 = short_conv(u1_ref, w1_ref, b1_ref)
    z = [a * b for a, b in zip(x1, long_conv(v, 0))]
    x2 = short_conv(u2_ref, w2_ref, b2_ref)
    for rows, a, b in zip(chains, x2, long_conv(z, 1)):
        o_ref[rows, :] = (a * b).astype(BF16)


def _hyena_call(u, conv_w, conv_b, hy_bias, filt, fwd, inv, *, batch, seq, ct, n_chain):
    emb, w1, b1, w2, b2, fr, w3, tcol, deltas = filt
    nct = D_HYENA // ct
    rows = n_chain * seq
    col = lambda part: (lambda c, b: (b, part * nct + c))
    wcol = lambda part: (lambda c, b: (0, part * nct + c))
    small = lambda shape: pl.BlockSpec(shape, lambda c, b: (0, 0))
    w3col = lambda direction, order: (lambda c, b: (0, (direction * HYENA_ORDER + order) * nct + c))
    in_specs = ([pl.BlockSpec((rows, ct), col(p)) for p in range(3)]
                + [pl.BlockSpec((3, ct), wcol(p)) for p in range(3)]
                + [pl.BlockSpec((1, ct), wcol(p)) for p in range(3)]
                + [pl.BlockSpec((HYENA_ORDER, ct), lambda c, b: (0, c)),
                   small((seq, FILT_PAD)), small((FILT_PAD, FILT_PAD)), small((1, FILT_PAD)),
                   small((FILT_PAD, FILT_PAD)), small((1, FILT_PAD)), small((1, FILT_PAD))]
                + [pl.BlockSpec((FILT_PAD, ct), w3col(direction, order))
                   for order in range(HYENA_ORDER) for direction in range(2)]
                + [small((seq, 1)), pl.BlockSpec((1, ct), lambda c, b: (0, c)),
                   _resident((2 * seq, seq)), _resident((seq, 2 * seq))])
    return pl.pallas_call(
        _hyena_kernel,
        grid=(nct, batch // n_chain),
        in_specs=in_specs,
        out_specs=pl.BlockSpec((rows, ct), lambda c, b: (b, c)),
        out_shape=jax.ShapeDtypeStruct((batch * seq, D_HYENA), BF16),
        scratch_shapes=[pltpu.VMEM((seq, FILT_PAD), F32),
                        pltpu.VMEM((HYENA_ORDER, 3, seq, ct), F32)],
        compiler_params=_params("arbitrary", "arbitrary"),
        name=f"hyena_{seq}",
    )(u, u, u, conv_w, conv_w, conv_w, conv_b, conv_b, conv_b, hy_bias,
      emb, w1, b1, w2, b2, fr, w3, w3, w3, w3, tcol, deltas, fwd, inv)


FF_CHUNK = D_MODEL
N_FF_CHUNKS = D_FF // FF_CHUNK


def _post_kernel(xc_ref, xl_ref, oac_ref, oal_ref, ohc_ref, ohl_ref, mod_ref, g1_ref, g2_ref, gf_ref,
                 wga0_ref, wga1_ref, wgh0_ref, wgh1_ref,
                 wa_ref, wh_ref, wo_ref, wu_ref, wd_ref, yc_ref, yl_ref, *, n_ctx_steps):
    d = functools.partial(jnp.dot, preferred_element_type=F32)
    is_ctx = pl.program_id(0) < n_ctx_steps
    pick = lambda c_ref, l_ref: jnp.where(is_ctx, c_ref[...], l_ref[...])
    x = pick(xc_ref, xl_ref)
    h1 = (_rms(x, g1_ref[...]) * (1.0 + mod_ref[1:2, :]) + mod_ref[0:1, :]).astype(BF16)

    def gate(*w_refs):
        return jnp.concatenate([jax.nn.sigmoid(d(h1, w[...])) for w in w_refs], axis=1)

    merged = (gate(wga0_ref, wga1_ref) * d(pick(oac_ref, oal_ref), wa_ref[...])
              + gate(wgh0_ref, wgh1_ref) * d(pick(ohc_ref, ohl_ref), wh_ref[...]))
    x = x + mod_ref[2:3, :] * d(merged.astype(BF16), wo_ref[...])
    h2 = (_rms(x, g2_ref[...]) * (1.0 + mod_ref[4:5, :]) + mod_ref[3:4, :]).astype(BF16)
    acc = jnp.zeros_like(x)
    for c in range(N_FF_CHUNKS):
        sl = slice(c * FF_CHUNK, (c + 1) * FF_CHUNK)
        up = jnp.maximum(d(h2, wu_ref[:, sl]), 0.0)
        acc = acc + d((up * up).astype(BF16), wd_ref[sl, :])
    y = _rms(x + mod_ref[5:6, :] * acc, gf_ref[...])

    @pl.when(is_ctx)
    def _():
        yc_ref[...] = y

    @pl.when(jnp.logical_not(is_ctx))
    def _():
        yl_ref[...] = y


def _post_call(ctx, lat, mod, g1, g2, gf, w_in, weights, *, lat_seq, tm):
    n_ctx, n_lat = ctx[0].shape[0] // tm, lat[0].shape[0] // tm
    per_seq = lat_seq // tm
    gate_windows = [_resident((D_MODEL, GATE_WINDOW), (0, GATE_OFF // GATE_WINDOW + n))
                    for n in range(GATE_COLS // GATE_WINDOW)]
    ctx_row = lambda i: (jnp.minimum(i, n_ctx - 1), 0)
    lat_row = lambda i: (jnp.maximum(i - n_ctx, 0), 0)
    mod_idx = lambda i: (jnp.where(i < n_ctx, 0, 1 + (i - n_ctx) // per_seq), 0, 0)
    widths = (D_MODEL, N_HEADS * V_DIM, D_HYENA)
    in_specs = [pl.BlockSpec((tm, w), r) for w in widths for r in (ctx_row, lat_row)]
    args = [a for pair in zip(ctx, lat) for a in pair]
    return pl.pallas_call(
        functools.partial(_post_kernel, n_ctx_steps=n_ctx),
        grid=(n_ctx + n_lat,),
        in_specs=in_specs
                 + [pl.BlockSpec((None, 6, D_MODEL), mod_idx),
                    _resident((1, D_MODEL)), _resident((1, D_MODEL)), _resident((1, D_MODEL))]
                 + gate_windows + [_resident(w.shape) for w in weights],
        out_specs=[pl.BlockSpec((tm, D_MODEL), ctx_row), pl.BlockSpec((tm, D_MODEL), lat_row)],
        out_shape=[jax.ShapeDtypeStruct((n_ctx * tm, D_MODEL), F32),
                   jax.ShapeDtypeStruct((n_lat * tm, D_MODEL), F32)],
        compiler_params=_params("arbitrary"),
        name="post",
    )(*args, mod, g1, g2, gf, *[w_in] * len(gate_windows), *weights)


def _rope_tables(seq):
    half = HEAD_DIM // 2
    n = half // 2
    inv = ROPE_THETA ** (-np.arange(n, dtype=np.float64) / n)
    pos = np.arange(seq)
    ang_row = (pos // GRID_W).astype(np.float64)[:, None] * inv[None, :]
    ang_col = (pos % GRID_W).astype(np.float64)[:, None] * inv[None, :]
    zeros = np.zeros_like(ang_row)

    def per_map(a_row, a_col, lo, hi):
        return np.concatenate([lo(a_row), hi(a_row), lo(a_col), hi(a_col)], axis=-1)

    cos = per_map(ang_row, ang_col, np.cos, np.cos)
    sin_lo = per_map(ang_row, ang_col, lambda a: -np.sin(a), lambda a: zeros)
    sin_hi = per_map(ang_row, ang_col, lambda a: zeros, np.sin)
    return tuple(jnp.asarray(np.concatenate([tab, tab], axis=-1).astype(np.float32))
                 for tab in (cos, sin_lo, sin_hi))


def _filter_embedding(seq):
    bands = (FILTER_EMB - 1) // 2
    t = np.linspace(0.0, 1.0, seq)[:, None]
    wpos = 2.0 * np.pi * np.arange(seq, dtype=np.float64)[:, None] / seq
    f = np.linspace(1e-4, bands - 1, bands)[None, :]
    emb = np.concatenate([t, np.cos(f * wpos), -np.sin(f * wpos)], axis=-1)
    emb = np.pad(emb, ((0, 0), (0, FILT_PAD - FILTER_EMB)))
    return jnp.asarray(emb.astype(np.float32)), jnp.asarray(t.astype(np.float32))


def _tiles(seq):
    short = seq <= 256
    return dict(tm=512, tq=128, attn_heads=N_HEADS if short else 4, ct=MXU_WIDTH,
                hyena_seqs=4 if short else 2)


def _pad_to(x, rows, cols):
    return jnp.pad(x, ((0, rows - x.shape[0]), (0, cols - x.shape[1])))


def kernel(x_prompt, x_sample, cache_k, cache_v, c, c_ctx, w_ada, b_ada, norm1_g, norm2_g, w_in,
           lam_q1, lam_k1, lam_q2, lam_k2, attn_subln_g, conv_w, conv_b, filt_w1, filt_b1,
           filt_w2, filt_b2, filt_w3, filt_freq, hy_bias, w_br_attn, w_br_hy, w_out, w_up,
           w_down, final_g):
    depth = w_in.shape[0]
    assert depth == 1, "single trunk layer"
    layer = 0
    lam_init = 0.8 - 0.6 * math.exp(-0.3 * layer)
    n_ctx, ctx_len, _ = x_prompt.shape
    n_lat, lat_len, _ = x_sample.shape
    past = cache_k.shape[2]

    cvec = jnp.concatenate([c_ctx[None, :], c], axis=0)
    cvec = jnp.pad(cvec, ((0, MOD_ROWS - cvec.shape[0]), (0, 0)))
    mod = _mod_call(cvec, w_ada[layer], b_ada[layer][None, :]).reshape(MOD_ROWS, 6, D_MODEL)

    w_in_b = w_in[layer].astype(BF16)
    post_w_f32 = tuple(w[layer] for w in (w_br_attn, w_br_hy, w_out, w_up, w_down))
    g1, g2, gf = norm1_g[layer][None, :], norm2_g[layer][None, :], final_g[None, :]
    lam_vecs = jnp.stack([lam_q1[layer], lam_k1[layer], lam_q2[layer], lam_k2[layer]])
    subln_g = attn_subln_g[layer][None, :]

    w1 = _pad_to(filt_w1[layer], FILT_PAD, FILT_PAD)
    w2 = _pad_to(filt_w2[layer], FILT_PAD, FILT_PAD)
    w3 = _pad_to(filt_w3[layer], FILT_PAD, filt_w3.shape[2])
    b1 = _pad_to(filt_b1[layer][None, :], 1, FILT_PAD)
    b2 = _pad_to(filt_b2[layer][None, :], 1, FILT_PAD)
    fr = _pad_to(filt_freq[layer][None, :], 1, FILT_PAD)
    deltas = jnp.asarray(np.linspace(math.log(DECAY_TARGET) / FAST_DECAY_PCT,
                                     math.log(DECAY_TARGET) / SLOW_DECAY_PCT,
                                     D_HYENA)[None, :].astype(np.float32))

    def project(x3, batch, seq, latent, cast=()):
        x = x3.reshape(batch * seq, D_MODEL)
        rope = _rope_tables(seq) if latent else None
        return x, _in_proj_call(x, mod, g1, w_in_b, rope, seq=seq, latent=latent,
                                tm=_tiles(seq)["tm"], cast=cast)

    def mix(x, outs, batch, seq, latent):
        tiles = _tiles(seq)
        if latent:
            q, kt, v, u = outs
            ckt = jnp.transpose(cache_k[:, layer], (0, 2, 3, 4, 1)).reshape(batch * K_COLS, past)
            cache = (ckt, cache_v[:, layer].reshape(batch * past * N_HEADS, V_DIM))
            kf = vf = None
        else:
            q, kt, v, kf, vf, u = outs
            cache = None
        o_attn = _attn_call(q, kt, v, cache, lam_vecs, subln_g, batch=batch, seq=seq,
                            lam_init=lam_init, tq=tiles["tq"], heads=tiles["attn_heads"])

        fwd_np, inv_np = _dft_tables(seq)
        fwd_b = jnp.asarray(fwd_np).astype(BF16)
        inv_b = jnp.asarray(inv_np).astype(BF16)
        emb, tcol = _filter_embedding(seq)
        filt = (emb, w1, b1, w2, b2, fr, w3, tcol, deltas)
        o_hy = _hyena_call(u, conv_w[layer], conv_b[layer][None, :], hy_bias[layer], filt,
                           fwd_b, inv_b, batch=batch, seq=seq, ct=tiles["ct"],
                           n_chain=tiles["hyena_seqs"])

        return (x, o_attn, o_hy), kf, vf

    x_ctx, ctx_outs = project(x_prompt, n_ctx, ctx_len, False)
    x_lat, lat_outs = project(x_sample, n_lat, lat_len, True, cast=post_w_f32)
    lat_outs, post_w = lat_outs[:-len(post_w_f32)], lat_outs[-len(post_w_f32):]
    ctx_mixed, kf, vf = mix(x_ctx, ctx_outs, n_ctx, ctx_len, False)
    lat_mixed, _, _ = mix(x_lat, lat_outs, n_lat, lat_len, True)
    y_prompt, y_sample = _post_call(ctx_mixed, lat_mixed, mod, g1, g2, gf, w_in_b, post_w,
                                    lat_seq=lat_len, tm=_tiles(lat_len)["tm"])
    y_prompt = y_prompt.reshape(n_ctx, ctx_len, D_MODEL)
    y_sample = y_sample.reshape(n_lat, lat_len, D_MODEL)
    new_cache_k = jnp.transpose(kf.reshape(n_ctx, depth, N_HEADS, 2, HEAD_DIM, ctx_len),
                                (0, 1, 5, 2, 3, 4))
    new_cache_v = vf.reshape(n_ctx, depth, ctx_len, N_HEADS, V_DIM)
    return (y_prompt, y_sample, new_cache_k, new_cache_v)
```

```python
import functools
import math

import numpy as np
import jax
import jax.numpy as jnp
from jax import lax
from jax.experimental import pallas as pl
from jax.experimental.pallas import tpu as pltpu

D_MODEL = 1024
GRID_W = 64
N_HEADS = 8
HEAD_DIM = 64
V_DIM = 2 * HEAD_DIM
D_HYENA = D_MODEL // 2
HYENA_ORDER = 2
FILTER_EMB = 33
FILTER_HIDDEN = 64
D_FF = 4 * D_MODEL
ROPE_THETA = 10000.0
EPS = 1e-6
LOG2_E = math.log2(math.e)
DECAY_TARGET = 1e-2
FAST_DECAY_PCT = 0.3
SLOW_DECAY_PCT = 1.5

Q_COLS = N_HEADS * 2 * HEAD_DIM
K_COLS = N_HEADS * 2 * HEAD_DIM
V_COLS = N_HEADS * V_DIM
HY_COLS = 3 * D_HYENA
GATE_COLS = 2 * D_MODEL
Q_OFF = 0
K_OFF = Q_OFF + Q_COLS
V_OFF = K_OFF + K_COLS
HY_OFF = V_OFF + V_COLS
GATE_OFF = HY_OFF + HY_COLS
IN_COLS = GATE_OFF + GATE_COLS
GATE_WINDOW = math.gcd(GATE_OFF, D_MODEL)

LANES = 128
MXU_WIDTH = 256
MOD_ROWS = 16
FILT_PAD = 128
VMEM_LIMIT = 56 * 1024 * 1024

BF16 = jnp.bfloat16
F32 = jnp.float32


def _resident(shape, index=None):
    index = (0,) * len(shape) if index is None else index
    return pl.BlockSpec(shape, lambda *_: index, pipeline_mode=pl.Buffered(1))


def _params(*sem):
    return pltpu.CompilerParams(dimension_semantics=sem, vmem_limit_bytes=VMEM_LIMIT)


def _rms(x, g):
    return x * lax.rsqrt(jnp.mean(x * x, axis=-1, keepdims=True) + EPS) * g


def _mod_kernel(c_ref, w_ref, b_ref, o_ref):
    c = c_ref[...]
    s = (c * jax.nn.sigmoid(c)).astype(BF16)
    o_ref[...] = jnp.dot(s, w_ref[...].astype(BF16), preferred_element_type=F32) + b_ref[...]


def _mod_call(cvec, w_ada, b_ada):
    tn = 2048
    n = w_ada.shape[1]
    return pl.pallas_call(
        _mod_kernel,
        grid=(n // tn,),
        in_specs=[pl.BlockSpec((MOD_ROWS, D_MODEL), lambda j: (0, 0)),
                  pl.BlockSpec((D_MODEL, tn), lambda j: (0, j)),
                  pl.BlockSpec((1, tn), lambda j: (0, j))],
        out_specs=pl.BlockSpec((MOD_ROWS, tn), lambda j: (0, j)),
        out_shape=jax.ShapeDtypeStruct((MOD_ROWS, n), F32),
        compiler_params=_params("arbitrary"),
        name="mod",
    )(cvec, w_ada, b_ada)


def _rope(x, cos, sin_lo, sin_hi):
    return (x * cos + pltpu.roll(x, 16, axis=1) * sin_hi
            + pltpu.roll(x, LANES - 16, axis=1) * sin_lo)


def _in_proj_kernel(*refs, latent, n_cast):
    refs = list(refs)
    take = lambda n: [refs.pop(0) for _ in range(n)]
    x_ref, mod_ref, g_ref, w_ref = take(4)
    if latent:
        cos_ref, slo_ref, shi_ref = take(3)
    cast_src = take(n_cast)
    q_ref, kt_ref, v_ref = take(3)
    if not latent:
        ktf_ref, vf_ref = take(2)
    (u_ref,) = take(1)
    cast_dst = take(n_cast)
    tm = x_ref.shape[0]
    x = x_ref[...]
    h = _rms(x, g_ref[...]) * (1.0 + mod_ref[1:2, :]) + mod_ref[0:1, :]
    hb = h.astype(BF16)

    def proj(off, width):
        return jnp.dot(hb, w_ref[:, off:off + width], preferred_element_type=F32)

    q = proj(Q_OFF, Q_COLS) * (HEAD_DIM ** -0.5 * LOG2_E)
    k = proj(K_OFF, K_COLS)
    if latent:
        cos, slo, shi = cos_ref[...], slo_ref[...], shi_ref[...]
        k_heads = []
        for hd in range(N_HEADS):
            sl = slice(hd * LANES, (hd + 1) * LANES)
            q_ref[:, sl] = _rope(q[:, sl], cos, slo, shi).astype(BF16)
            k_heads.append(_rope(k[:, sl], cos, slo, shi))
        kt_ref[...] = jnp.concatenate(k_heads, axis=1).T.astype(BF16)
    else:
        q_ref[...] = q.astype(BF16)
        seq = kt_ref.shape[1]
        for s in range(tm // seq):
            kt = k[s * seq:(s + 1) * seq, :].T
            rows = slice(s * K_COLS, (s + 1) * K_COLS)
            ktf_ref[rows, :] = kt
            kt_ref[rows, :] = kt.astype(BF16)
    v = proj(V_OFF, V_COLS)
    v_ref[...] = v.astype(BF16)
    if not latent:
        for hd in range(N_HEADS):
            vf_ref[pl.ds(hd, tm, stride=N_HEADS), :] = v[:, hd * LANES:(hd + 1) * LANES]
    u_ref[...] = proj(HY_OFF, HY_COLS)
    for src, dst in zip(cast_src, cast_dst):
        dst[...] = src[...].astype(BF16)


def _in_proj_call(x, mod, g1, w_in, rope, *, seq, latent, tm, cast=()):
    t = x.shape[0]
    n_steps = t // tm
    per_seq = seq // tm if latent else 1

    def mod_idx(i):
        return (1 + i // per_seq if latent else 0, 0, 0)

    row = lambda i: (i, 0)
    in_specs = [pl.BlockSpec((tm, D_MODEL), row),
                pl.BlockSpec((None, 6, D_MODEL), mod_idx),
                _resident((1, D_MODEL)),
                _resident((D_MODEL, GATE_OFF))]
    args = [x, mod, g1, w_in]
    tok = lambda width, dtype: ((t, width), dtype, pl.BlockSpec((tm, width), row))
    if latent:
        in_specs += [pl.BlockSpec((tm, LANES), lambda i: (i % per_seq, 0))] * 3
        args += list(rope)
        kt_spec = pl.BlockSpec((K_COLS, tm), lambda i: (i // per_seq, i % per_seq))
    else:
        kt_spec = pl.BlockSpec((tm // seq * K_COLS, seq), row)
    kt_shape = (t // seq * K_COLS, seq)
    outs = [tok(Q_COLS, BF16), (kt_shape, BF16, kt_spec), tok(V_COLS, BF16)]
    if not latent:
        outs += [(kt_shape, F32, kt_spec),
                 ((t * N_HEADS, LANES), F32, pl.BlockSpec((tm * N_HEADS, LANES), row))]
    outs += [tok(HY_COLS, F32)]
    for w in cast:
        slab = pl.BlockSpec((w.shape[0] // n_steps, w.shape[1]), row)
        in_specs.append(slab)
        args.append(w)
        outs.append((w.shape, BF16, slab))
    return pl.pallas_call(
        functools.partial(_in_proj_kernel, latent=latent, n_cast=len(cast)),
        grid=(n_steps,),
        in_specs=in_specs,
        out_specs=[spec for _, _, spec in outs],
        out_shape=[jax.ShapeDtypeStruct(s, d) for s, d, _ in outs],
        compiler_params=_params("arbitrary"),
        name="in_proj_lat" if latent else "in_proj_ctx",
    )(*args)


def _attn_kernel(*refs, n_cache, lam_init, tq):
    if n_cache:
        (q_ref, kt_ref, v_ref, ckt_ref, cv_ref, lam_ref, g_ref, o_ref, kt_s, v2_s) = refs
    else:
        (q_ref, kt_ref, v_ref, lam_ref, g_ref, o_ref, v2_s) = refs
    n_heads, n_keys, _ = v2_s.shape
    head0 = pl.program_id(1) * n_heads
    lv = lam_ref[...]
    lam = (jnp.exp(jnp.sum(lv[0:1] * lv[1:2], axis=-1, keepdims=True))
           - jnp.exp(jnp.sum(lv[2:3] * lv[3:4], axis=-1, keepdims=True)) + lam_init)
    gain = g_ref[...] * (1.0 - lam_init)
    first_map = lax.broadcasted_iota(jnp.int32, (1, LANES), 1) < HEAD_DIM
    zero = jnp.zeros((tq, LANES), BF16)

    def operands(hd):
        cols = slice(hd * LANES, (hd + 1) * LANES)
        v2_s[hd, :, LANES:] = jnp.ones((n_keys, LANES), BF16)
        if n_cache:
            kt_s[cols, 0:n_cache] = ckt_ref[cols, :].astype(BF16)
            kt_s[cols, n_cache:] = kt_ref[cols, :]
            cache_rows = pl.ds(head0 + hd, n_cache, stride=N_HEADS)
            v2_s[hd, 0:n_cache, 0:LANES] = cv_ref[cache_rows, :].astype(BF16)
            v2_s[hd, n_cache:, 0:LANES] = v_ref[:, cols]
            return kt_s[cols, :], v2_s[hd]
        v2_s[hd, :, 0:LANES] = v_ref[:, cols]
        return kt_ref[cols, :], v2_s[hd]

    def scores(hd, i, kt):
        q = q_ref[i * tq:(i + 1) * tq, hd * LANES:(hd + 1) * LANES]
        q2 = jnp.concatenate([jnp.where(first_map, q, zero), jnp.where(first_map, zero, q)], axis=0)
        s = jnp.dot(q2, kt, preferred_element_type=F32)
        return s, jnp.max(s, axis=-1, keepdims=True)

    def finish(hd, i, v2, s, m):
        p = jnp.exp2(s - m).astype(BF16)
        pv = jnp.dot(p, v2, preferred_element_type=F32)
        pv = pv[:, :LANES] / pv[:, LANES:]
        o = pv[:tq] - lam * pv[tq:]
        o_ref[i * tq:(i + 1) * tq, hd * LANES:(hd + 1) * LANES] = (_rms(o, 1.0) * gain).astype(BF16)

    n_tiles = q_ref.shape[0] // tq
    pending = None
    for hd in range(n_heads):
        kt, v2 = operands(hd)
        for i in range(n_tiles):
            cur = (hd, i, v2) + scores(hd, i, kt)
            if pending is not None:
                finish(*pending)
            pending = cur
    finish(*pending)


def _attn_call(q, kt, v, cache, lam_vecs, subln_g, *, batch, seq, lam_init, tq, heads):
    const = lambda b, h: (0, 0)
    width = heads * LANES
    n_groups = N_HEADS // heads
    in_specs = [pl.BlockSpec((seq, width), lambda b, h: (b, h)),
                pl.BlockSpec((width, seq), lambda b, h: (b * n_groups + h, 0)),
                pl.BlockSpec((seq, width), lambda b, h: (b, h))]
    args = [q, kt, v]
    n_cache = 0
    if cache is not None:
        ckt, cv = cache
        n_cache = ckt.shape[1]
        in_specs += [pl.BlockSpec((width, n_cache), lambda b, h: (b * n_groups + h, 0)),
                     pl.BlockSpec((n_cache * N_HEADS, LANES), lambda b, h: (b, 0))]
        args += [ckt, cv]
    n_keys = n_cache + seq
    scratch = [pltpu.VMEM((heads, n_keys, 2 * LANES), BF16)]
    if n_cache:
        scratch = [pltpu.VMEM((width, n_keys), BF16)] + scratch
    in_specs += [pl.BlockSpec(lam_vecs.shape, const), pl.BlockSpec((1, LANES), const)]
    args += [lam_vecs, subln_g]
    return pl.pallas_call(
        functools.partial(_attn_kernel, n_cache=n_cache, lam_init=lam_init, tq=tq),
        grid=(batch, n_groups),
        in_specs=in_specs,
        out_specs=pl.BlockSpec((seq, width), lambda b, h: (b, h)),
        out_shape=jax.ShapeDtypeStruct((batch * seq, N_HEADS * V_DIM), BF16),
        scratch_shapes=scratch,
        compiler_params=_params("arbitrary", "arbitrary"),
        name="attn_lat" if cache is not None else "attn_ctx",
    )(*args)


def _dft_tables(seq):
    n = 2 * seq
    f = np.arange(seq, dtype=np.int64)[:, None]
    t = np.arange(seq, dtype=np.int64)[None, :]
    ang = 2.0 * np.pi * ((f * t) % n).astype(np.float64) / n
    alt = np.where(np.arange(seq) % 2 == 0, 1.0, -1.0)
    cos, msin = np.cos(ang), -np.sin(ang)
    msin[0, :] = alt
    fwd = np.concatenate([cos, msin], axis=0)
    wgt = np.where(f == 0, 1.0, 2.0) / n
    inv_c = (cos * wgt).T
    inv_s = (msin * wgt).T
    inv = np.concatenate([inv_c, inv_s], axis=1)
    return fwd.astype(np.float32), inv.astype(np.float32)


def _filter_hidden(emb_ref, w1_ref, b1_ref, w2_ref, b2_ref, fr_ref):
    hp = functools.partial(jnp.dot, preferred_element_type=F32, precision=lax.Precision.HIGHEST)
    fr = fr_ref[...]
    h = jnp.sin(fr * (hp(emb_ref[...], w1_ref[...]) + b1_ref[...]))
    return jnp.sin(fr * (hp(h, w2_ref[...]) + b2_ref[...]))


def _filter_planes(h, w3f_ref, w3b_ref, decay, dft_ref, out_ref):
    seq = h.shape[0]
    fwd = jnp.dot(h, w3f_ref[...].astype(BF16), preferred_element_type=F32) * decay
    bwd = jnp.dot(h, w3b_ref[...].astype(BF16), preferred_element_type=F32) * decay
    hsum, hdif = fwd + bwd, fwd - bwd
    re = jnp.dot(dft_ref[0:seq, :], hsum.astype(BF16), preferred_element_type=F32)
    im = jnp.dot(dft_ref[seq:, :], hdif.astype(BF16), preferred_element_type=F32)
    row = lax.broadcasted_iota(jnp.int32, (seq, 1), 0)
    alt = jnp.where(row % 2 == 0, 1.0, -1.0)
    nyq = jnp.sum(hsum * alt, axis=0, keepdims=True)
    out_ref[0] = re
    out_ref[1] = jnp.where(row == 0, 0.0, im)
    out_ref[2] = jnp.where(row == 0, nyq, re)


def _hyena_kernel(uv_ref, u1_ref, u2_ref, wv_ref, w1_ref, w2_ref, bv_ref, b1_ref, b2_ref, hb_ref,
                  emb_ref, fw1_ref, fb1_ref, fw2_ref, fb2_ref, fr_ref,
                  w3f0_ref, w3b0_ref, w3f1_ref, w3b1_ref, t_ref, dl_ref, fwd_s, inv_s,
                  o_ref, hid_s, hf_s):
    seq = emb_ref.shape[0]
    new_tile = pl.program_id(1) == 0

    @pl.when(jnp.logical_and(pl.program_id(0) == 0, new_tile))
    def _():
        hid_s[...] = _filter_hidden(emb_ref, fw1_ref, fb1_ref, fw2_ref, fb2_ref, fr_ref)

    @pl.when(new_tile)
    def _():
        h = hid_s[...].astype(BF16)
        decay = jnp.exp(-t_ref[...] * jnp.abs(dl_ref[...]))
        _filter_planes(h, w3f0_ref, w3b0_ref, decay, fwd_s, hf_s.at[0])
        _filter_planes(h, w3f1_ref, w3b1_ref, decay, fwd_s, hf_s.at[1])

    chains = [slice(c * seq, (c + 1) * seq) for c in range(uv_ref.shape[0] // seq)]
    row = lax.broadcasted_iota(jnp.int32, (seq, 1), 0)
    first, last = row == 0, row == seq - 1

    def short_conv(u_ref, w_ref, b_ref):
        outs = []
        for rows in chains:
            u = u_ref[rows, :]
            prev = jnp.where(first, 0.0, pltpu.roll(u, 1, axis=0))
            nxt = jnp.where(last, 0.0, pltpu.roll(u, seq - 1, axis=0))
            outs.append(prev * w_ref[0:1, :] + u * w_ref[1:2, :] + nxt * w_ref[2:3, :] + b_ref[...])
        return outs

    def long_conv(us, order):
        h_ref = hf_s.at[order]
        bias = hb_ref[order:order + 1, :]
        specs = [jnp.dot(fwd_s[...], u.astype(BF16), preferred_element_type=F32) for u in us]
        ys = []
        for spec in specs:
            ure, uim = spec[:seq], spec[seq:]
            yre = ure * h_ref[0] - uim * h_ref[1]
            yim = ure * h_ref[1] + uim * h_ref[2]
            ys.append(jnp.concatenate([yre, yim], axis=0).astype(BF16))
        return [jnp.dot(inv_s[...], y, preferred_element_type=F32) + u * bias
                for y, u in zip(ys, us)]

    v = short_conv(uv_ref, wv_ref, bv_ref)
    x1 = short_conv(u1_ref, w1_ref, b1_ref)
    z = [a * b for a, b in zip(x1, long_conv(v, 0))]
    x2 = short_conv(u2_ref, w2_ref, b2_ref)
    for rows, a, b in zip(chains, x2, long_conv(z, 1)):
        o_ref[rows, :] = (a * b).astype(BF16)


def _hyena_call(u, conv_w, conv_b, hy_bias, filt, fwd, inv, *, batch, seq, ct, n_chain):
    emb, w1, b1, w2, b2, fr, w3, tcol, deltas = filt
    nct = D_HYENA // ct
    rows = n_chain * seq
    col = lambda part: (lambda c, b: (b, part * nct + c))
    wcol = lambda part: (lambda c, b: (0, part * nct + c))
    small = lambda shape: pl.BlockSpec(shape, lambda c, b: (0, 0))
    w3col = lambda direction, order: (lambda c, b: (0, (direction * HYENA_ORDER + order) * nct + c))
    in_specs = ([pl.BlockSpec((rows, ct), col(p)) for p in range(3)]
                + [pl.BlockSpec((3, ct), wcol(p)) for p in range(3)]
                + [pl.BlockSpec((1, ct), wcol(p)) for p in range(3)]
                + [pl.BlockSpec((HYENA_ORDER, ct), lambda c, b: (0, c)),
                   small((seq, FILT_PAD)), small((FILT_PAD, FILT_PAD)), small((1, FILT_PAD)),
                   small((FILT_PAD, FILT_PAD)), small((1, FILT_PAD)), small((1, FILT_PAD))]
                + [pl.BlockSpec((FILT_PAD, ct), w3col(direction, order))
                   for order in range(HYENA_ORDER) for direction in range(2)]
                + [small((seq, 1)), pl.BlockSpec((1, ct), lambda c, b: (0, c)),
                   _resident((2 * seq, seq)), _resident((seq, 2 * seq))])
    return pl.pallas_call(
        _hyena_kernel,
        grid=(nct, batch // n_chain),
        in_specs=in_specs,
        out_specs=pl.BlockSpec((rows, ct), lambda c, b: (b, c)),
        out_shape=jax.ShapeDtypeStruct((batch * seq, D_HYENA), BF16),
        scratch_shapes=[pltpu.VMEM((seq, FILT_PAD), F32),
                        pltpu.VMEM((HYENA_ORDER, 3, seq, ct), F32)],
        compiler_params=_params("arbitrary", "arbitrary"),
        name=f"hyena_{seq}",
    )(u, u, u, conv_w, conv_w, conv_w, conv_b, conv_b, conv_b, hy_bias,
      emb, w1, b1, w2, b2, fr, w3, w3, w3, w3, tcol, deltas, fwd, inv)


FF_CHUNK = D_MODEL
N_FF_CHUNKS = D_FF // FF_CHUNK


def _post_kernel(xc_ref, xl_ref, oac_ref, oal_ref, ohc_ref, ohl_ref, mod_ref, g1_ref, g2_ref, gf_ref,
                 wga0_ref, wga1_ref, wgh0_ref, wgh1_ref,
                 wa_ref, wh_ref, wo_ref, wu_ref, wd_ref, yc_ref, yl_ref, *, n_ctx_steps):
    d = functools.partial(jnp.dot, preferred_element_type=F32)
    is_ctx = pl.program_id(0) < n_ctx_steps
    pick = lambda c_ref, l_ref: jnp.where(is_ctx, c_ref[...], l_ref[...])
    x = pick(xc_ref, xl_ref)
    h1 = (_rms(x, g1_ref[...]) * (1.0 + mod_ref[1:2, :]) + mod_ref[0:1, :]).astype(BF16)

    def gate(*w_refs):
        return jnp.concatenate([jax.nn.sigmoid(d(h1, w[...])) for w in w_refs], axis=1)

    merged = (gate(wga0_ref, wga1_ref) * d(pick(oac_ref, oal_ref), wa_ref[...])
              + gate(wgh0_ref, wgh1_ref) * d(pick(ohc_ref, ohl_ref), wh_ref[...]))
    x = x + mod_ref[2:3, :] * d(merged.astype(BF16), wo_ref[...])
    h2 = (_rms(x, g2_ref[...]) * (1.0 + mod_ref[4:5, :]) + mod_ref[3:4, :]).astype(BF16)
    acc = jnp.zeros_like(x)
    for c in range(N_FF_CHUNKS):
        sl = slice(c * FF_CHUNK, (c + 1) * FF_CHUNK)
        up = jnp.maximum(d(h2, wu_ref[:, sl]), 0.0)
        acc = acc + d((up * up).astype(BF16), wd_ref[sl, :])
    y = _rms(x + mod_ref[5:6, :] * acc, gf_ref[...])

    @pl.when(is_ctx)
    def _():
        yc_ref[...] = y

    @pl.when(jnp.logical_not(is_ctx))
    def _():
        yl_ref[...] = y


def _post_call(ctx, lat, mod, g1, g2, gf, w_in, weights, *, lat_seq, tm):
    n_ctx, n_lat = ctx[0].shape[0] // tm, lat[0].shape[0] // tm
    per_seq = lat_seq // tm
    gate_windows = [_resident((D_MODEL, GATE_WINDOW), (0, GATE_OFF // GATE_WINDOW + n))
                    for n in range(GATE_COLS // GATE_WINDOW)]
    ctx_row = lambda i: (jnp.minimum(i, n_ctx - 1), 0)
    lat_row = lambda i: (jnp.maximum(i - n_ctx, 0), 0)
    mod_idx = lambda i: (jnp.where(i < n_ctx, 0, 1 + (i - n_ctx) // per_seq), 0, 0)
    widths = (D_MODEL, N_HEADS * V_DIM, D_HYENA)
    in_specs = [pl.BlockSpec((tm, w), r) for w in widths for r in (ctx_row, lat_row)]
    args = [a for pair in zip(ctx, lat) for a in pair]
    return pl.pallas_call(
        functools.partial(_post_kernel, n_ctx_steps=n_ctx),
        grid=(n_ctx + n_lat,),
        in_specs=in_specs
                 + [pl.BlockSpec((None, 6, D_MODEL), mod_idx),
                    _resident((1, D_MODEL)), _resident((1, D_MODEL)), _resident((1, D_MODEL))]
                 + gate_windows + [_resident(w.shape) for w in weights],
        out_specs=[pl.BlockSpec((tm, D_MODEL), ctx_row), pl.BlockSpec((tm, D_MODEL), lat_row)],
        out_shape=[jax.ShapeDtypeStruct((n_ctx * tm, D_MODEL), F32),
                   jax.ShapeDtypeStruct((n_lat * tm, D_MODEL), F32)],
        compiler_params=_params("arbitrary"),
        name="post",
    )(*args, mod, g1, g2, gf, *[w_in] * len(gate_windows), *weights)


def _rope_tables(seq):
    half = HEAD_DIM // 2
    n = half // 2
    inv = ROPE_THETA ** (-np.arange(n, dtype=np.float64) / n)
    pos = np.arange(seq)
    ang_row = (pos // GRID_W).astype(np.float64)[:, None] * inv[None, :]
    ang_col = (pos % GRID_W).astype(np.float64)[:, None] * inv[None, :]
    zeros = np.zeros_like(ang_row)

    def per_map(a_row, a_col, lo, hi):
        return np.concatenate([lo(a_row), hi(a_row), lo(a_col), hi(a_col)], axis=-1)

    cos = per_map(ang_row, ang_col, np.cos, np.cos)
    sin_lo = per_map(ang_row, ang_col, lambda a: -np.sin(a), lambda a: zeros)
    sin_hi = per_map(ang_row, ang_col, lambda a: zeros, np.sin)
    return tuple(jnp.asarray(np.concatenate([tab, tab], axis=-1).astype(np.float32))
                 for tab in (cos, sin_lo, sin_hi))


def _filter_embedding(seq):
    bands = (FILTER_EMB - 1) // 2
    t = np.linspace(0.0, 1.0, seq)[:, None]
    wpos = 2.0 * np.pi * np.arange(seq, dtype=np.float64)[:, None] / seq
    f = np.linspace(1e-4, bands - 1, bands)[None, :]
    emb = np.concatenate([t, np.cos(f * wpos), -np.sin(f * wpos)], axis=-1)
    emb = np.pad(emb, ((0, 0), (0, FILT_PAD - FILTER_EMB)))
    return jnp.asarray(emb.astype(np.float32)), jnp.asarray(t.astype(np.float32))


def _tiles(seq):
    short = seq <= 256
    return dict(tm=512, tq=128, attn_heads=N_HEADS, ct=MXU_WIDTH,
                hyena_seqs=4 if short else 2)


def _pad_to(x, rows, cols):
    return jnp.pad(x, ((0, rows - x.shape[0]), (0, cols - x.shape[1])))


def kernel(x_prompt, x_sample, cache_k, cache_v, c, c_ctx, w_ada, b_ada, norm1_g, norm2_g, w_in,
           lam_q1, lam_k1, lam_q2, lam_k2, attn_subln_g, conv_w, conv_b, filt_w1, filt_b1,
           filt_w2, filt_b2, filt_w3, filt_freq, hy_bias, w_br_attn, w_br_hy, w_out, w_up,
           w_down, final_g):
    depth = w_in.shape[0]
    assert depth == 1, "single trunk layer"
    layer = 0
    lam_init = 0.8 - 0.6 * math.exp(-0.3 * layer)
    n_ctx, ctx_len, _ = x_prompt.shape
    n_lat, lat_len, _ = x_sample.shape
    past = cache_k.shape[2]

    cvec = jnp.concatenate([c_ctx[None, :], c], axis=0)
    cvec = jnp.pad(cvec, ((0, MOD_ROWS - cvec.shape[0]), (0, 0)))
    mod = _mod_call(cvec, w_ada[layer], b_ada[layer][None, :]).reshape(MOD_ROWS, 6, D_MODEL)

    w_in_b = w_in[layer].astype(BF16)
    post_w_f32 = tuple(w[layer] for w in (w_br_attn, w_br_hy, w_out, w_up, w_down))
    g1, g2, gf = norm1_g[layer][None, :], norm2_g[layer][None, :], final_g[None, :]
    lam_vecs = jnp.stack([lam_q1[layer], lam_k1[layer], lam_q2[layer], lam_k2[layer]])
    subln_g = attn_subln_g[layer][None, :]

    w1 = _pad_to(filt_w1[layer], FILT_PAD, FILT_PAD)
    w2 = _pad_to(filt_w2[layer], FILT_PAD, FILT_PAD)
    w3 = _pad_to(filt_w3[layer], FILT_PAD, filt_w3.shape[2])
    b1 = _pad_to(filt_b1[layer][None, :], 1, FILT_PAD)
    b2 = _pad_to(filt_b2[layer][None, :], 1, FILT_PAD)
    fr = _pad_to(filt_freq[layer][None, :], 1, FILT_PAD)
    deltas = jnp.asarray(np.linspace(math.log(DECAY_TARGET) / FAST_DECAY_PCT,
                                     math.log(DECAY_TARGET) / SLOW_DECAY_PCT,
                                     D_HYENA)[None, :].astype(np.float32))

    def project(x3, batch, seq, latent, cast=()):
        x = x3.reshape(batch * seq, D_MODEL)
        rope = _rope_tables(seq) if latent else None
        return x, _in_proj_call(x, mod, g1, w_in_b, rope, seq=seq, latent=latent,
                                tm=_tiles(seq)["tm"], cast=cast)

    def mix(x, outs, batch, seq, latent):
        tiles = _tiles(seq)
        if latent:
            q, kt, v, u = outs
            ckt = jnp.transpose(cache_k[:, layer], (0, 2, 3, 4, 1)).reshape(batch * K_COLS, past)
            cache = (ckt, cache_v[:, layer].reshape(batch * past * N_HEADS, V_DIM))
            kf = vf = None
        else:
            q, kt, v, kf, vf, u = outs
            cache = None
        o_attn = _attn_call(q, kt, v, cache, lam_vecs, subln_g, batch=batch, seq=seq,
                            lam_init=lam_init, tq=tiles["tq"], heads=tiles["attn_heads"])

        fwd_np, inv_np = _dft_tables(seq)
        fwd_b = jnp.asarray(fwd_np).astype(BF16)
        inv_b = jnp.asarray(inv_np).astype(BF16)
        emb, tcol = _filter_embedding(seq)
        filt = (emb, w1, b1, w2, b2, fr, w3, tcol, deltas)
        o_hy = _hyena_call(u, conv_w[layer], conv_b[layer][None, :], hy_bias[layer], filt,
                           fwd_b, inv_b, batch=batch, seq=seq, ct=tiles["ct"],
                           n_chain=tiles["hyena_seqs"])

        return (x, o_attn, o_hy), kf, vf

    x_ctx, ctx_outs = project(x_prompt, n_ctx, ctx_len, False)
    x_lat, lat_outs = project(x_sample, n_lat, lat_len, True, cast=post_w_f32)
    lat_outs, post_w = lat_outs[:-len(post_w_f32)], lat_outs[-len(post_w_f32):]
    ctx_mixed, kf, vf = mix(x_ctx, ctx_outs, n_ctx, ctx_len, False)
    lat_mixed, _, _ = mix(x_lat, lat_outs, n_lat, lat_len, True)
    y_prompt, y_sample = _post_call(ctx_mixed, lat_mixed, mod, g1, g2, gf, w_in_b, post_w,
                                    lat_seq=lat_len, tm=_tiles(lat_len)["tm"])
    y_prompt = y_prompt.reshape(n_ctx, ctx_len, D_MODEL)
    y_sample = y_sample.reshape(n_lat, lat_len, D_MODEL)
    new_cache_k = jnp.transpose(kf.reshape(n_ctx, depth, N_HEADS, 2, HEAD_DIM, ctx_len),
                                (0, 1, 5, 2, 3, 4))
    new_cache_v = vf.reshape(n_ctx, depth, ctx_len, N_HEADS, V_DIM)
    return (y_prompt, y_sample, new_cache_k, new_cache_v)
```

```python
import functools
import math

import numpy as np
import jax
import jax.numpy as jnp
from jax import lax
from jax.experimental import pallas as pl
from jax.experimental.pallas import tpu as pltpu

D_MODEL = 1024
GRID_W = 64
N_HEADS = 8
HEAD_DIM = 64
V_DIM = 2 * HEAD_DIM
D_HYENA = D_MODEL // 2
HYENA_ORDER = 2
FILTER_EMB = 33
FILTER_HIDDEN = 64
D_FF = 4 * D_MODEL
ROPE_THETA = 10000.0
EPS = 1e-6
LOG2_E = math.log2(math.e)
DECAY_TARGET = 1e-2
FAST_DECAY_PCT = 0.3
SLOW_DECAY_PCT = 1.5

Q_COLS = N_HEADS * 2 * HEAD_DIM
K_COLS = N_HEADS * 2 * HEAD_DIM
V_COLS = N_HEADS * V_DIM
HY_COLS = 3 * D_HYENA
GATE_COLS = 2 * D_MODEL
Q_OFF = 0
K_OFF = Q_OFF + Q_COLS
V_OFF = K_OFF + K_COLS
HY_OFF = V_OFF + V_COLS
GATE_OFF = HY_OFF + HY_COLS
IN_COLS = GATE_OFF + GATE_COLS
GATE_WINDOW = math.gcd(GATE_OFF, D_MODEL)

LANES = 128
SUBLANES = 8
MXU_WIDTH = 256
MOD_ROWS = 16
FILT_PAD = 128
VMEM_LIMIT = 56 * 1024 * 1024

BF16 = jnp.bfloat16
F32 = jnp.float32


def _resident(shape, index=None):
    index = (0,) * len(shape) if index is None else index
    return pl.BlockSpec(shape, lambda *_: index, pipeline_mode=pl.Buffered(1))


def _params(*sem):
    return pltpu.CompilerParams(dimension_semantics=sem, vmem_limit_bytes=VMEM_LIMIT)


def _rms(x, g):
    return x * lax.rsqrt(jnp.mean(x * x, axis=-1, keepdims=True) + EPS) * g


def _mod_kernel(c_ref, w_ref, b_ref, o_ref):
    c = c_ref[...]
    s = (c * jax.nn.sigmoid(c)).astype(BF16)
    chunk = jnp.dot(s, w_ref[...].astype(BF16), preferred_element_type=F32) + b_ref[...]
    for j in range(o_ref.shape[1]):
        @pl.when(pl.program_id(0) == j)
        def _():
            o_ref[:, j, :] = chunk


def _mod_call(cvec, w_ada, b_ada):
    n_chunks = w_ada.shape[1] // D_MODEL
    return pl.pallas_call(
        _mod_kernel,
        grid=(n_chunks,),
        in_specs=[pl.BlockSpec((MOD_ROWS, D_MODEL), lambda j: (0, 0)),
                  pl.BlockSpec((D_MODEL, D_MODEL), lambda j: (0, j)),
                  pl.BlockSpec((1, D_MODEL), lambda j: (0, j))],
        out_specs=pl.BlockSpec((MOD_ROWS, n_chunks, D_MODEL), lambda j: (0, 0, 0)),
        out_shape=jax.ShapeDtypeStruct((MOD_ROWS, n_chunks, D_MODEL), F32),
        compiler_params=_params("arbitrary"),
        name="mod",
    )(cvec, w_ada, b_ada)


def _rope(x, cos, sin_lo, sin_hi):
    return (x * cos + pltpu.roll(x, 16, axis=1) * sin_hi
            + pltpu.roll(x, LANES - 16, axis=1) * sin_lo)


def _in_proj_kernel(*refs, latent, n_cast):
    refs = list(refs)
    take = lambda n: [refs.pop(0) for _ in range(n)]
    x_ref, mod_ref, g_ref, w_ref = take(4)
    if latent:
        cos_ref, slo_ref, shi_ref = take(3)
    cast_src = take(n_cast)
    q_ref, kt_ref, v_ref = take(3)
    if not latent:
        ktf_ref, vf_ref = take(2)
    (u_ref,) = take(1)
    cast_dst = take(n_cast)
    tm = x_ref.shape[0]
    x = x_ref[...]
    h = _rms(x, g_ref[...]) * (1.0 + mod_ref[1:2, :]) + mod_ref[0:1, :]
    hb = h.astype(BF16)

    def proj(off, width):
        return jnp.dot(hb, w_ref[:, off:off + width], preferred_element_type=F32)

    q = proj(Q_OFF, Q_COLS) * (HEAD_DIM ** -0.5 * LOG2_E)
    k = proj(K_OFF, K_COLS)
    if latent:
        cos, slo, shi = cos_ref[...], slo_ref[...], shi_ref[...]
        k_heads = []
        for hd in range(N_HEADS):
            sl = slice(hd * LANES, (hd + 1) * LANES)
            q_ref[:, sl] = _rope(q[:, sl], cos, slo, shi).astype(BF16)
            k_heads.append(_rope(k[:, sl], cos, slo, shi))
        kt_ref[...] = jnp.concatenate(k_heads, axis=1).T.astype(BF16)
    else:
        q_ref[...] = q.astype(BF16)
        seq = kt_ref.shape[1]
        for s in range(tm // seq):
            kt = k[s * seq:(s + 1) * seq, :].T
            rows = slice(s * K_COLS, (s + 1) * K_COLS)
            ktf_ref[rows, :] = kt
            kt_ref[rows, :] = kt.astype(BF16)
    v = proj(V_OFF, V_COLS)
    v_ref[...] = v.astype(BF16)
    if not latent:
        for hd in range(N_HEADS):
            vf_ref[pl.ds(hd, tm, stride=N_HEADS), :] = v[:, hd * LANES:(hd + 1) * LANES]
    u_ref[...] = proj(HY_OFF, HY_COLS)
    for src, dst in zip(cast_src, cast_dst):
        dst[...] = src[...].astype(BF16)


def _in_proj_call(x, mod, g1, w_in, rope, *, seq, latent, tm, cast=()):
    t = x.shape[0]
    n_steps = t // tm
    per_seq = seq // tm if latent else 1

    def mod_idx(i):
        return (1 + i // per_seq if latent else 0, 0, 0)

    row = lambda i: (i, 0)
    in_specs = [pl.BlockSpec((tm, D_MODEL), row),
                pl.BlockSpec((None, 6, D_MODEL), mod_idx),
                _resident((1, D_MODEL)),
                _resident((D_MODEL, GATE_OFF))]
    args = [x, mod, g1, w_in]
    tok = lambda width, dtype: ((t, width), dtype, pl.BlockSpec((tm, width), row))
    if latent:
        in_specs += [pl.BlockSpec((tm, LANES), lambda i: (i % per_seq, 0))] * 3
        args += list(rope)
        kt_spec = pl.BlockSpec((K_COLS, tm), lambda i: (i // per_seq, i % per_seq))
    else:
        kt_spec = pl.BlockSpec((tm // seq * K_COLS, seq), row)
    kt_shape = (t // seq * K_COLS, seq)
    outs = [tok(Q_COLS, BF16), (kt_shape, BF16, kt_spec), tok(V_COLS, BF16)]
    if not latent:
        outs += [(kt_shape, F32, kt_spec),
                 ((t * N_HEADS, LANES), F32, pl.BlockSpec((tm * N_HEADS, LANES), row))]
    outs += [tok(HY_COLS, F32)]
    for w in cast:
        slab = pl.BlockSpec((w.shape[0] // n_steps, w.shape[1]), row)
        in_specs.append(slab)
        args.append(w)
        outs.append((w.shape, BF16, slab))
    return pl.pallas_call(
        functools.partial(_in_proj_kernel, latent=latent, n_cast=len(cast)),
        grid=(n_steps,),
        in_specs=in_specs,
        out_specs=[spec for _, _, spec in outs],
        out_shape=[jax.ShapeDtypeStruct(s, d) for s, d, _ in outs],
        compiler_params=_params("arbitrary"),
        name="in_proj_lat" if latent else "in_proj_ctx",
    )(*args)


def _attn_kernel(*refs, n_cache, lam_init, tq):
    if n_cache:
        (q_ref, kt_ref, v_ref, ckt_ref, cv_ref, lam_ref, g_ref, o_ref, kt_s, v2_s) = refs
    else:
        (q_ref, kt_ref, v_ref, lam_ref, g_ref, o_ref, v2_s) = refs
    n_heads, n_keys, _ = v2_s.shape
    head0 = pl.program_id(1) * n_heads
    lv = lam_ref[...]
    lam = (jnp.exp(jnp.sum(lv[0:1] * lv[1:2], axis=-1, keepdims=True))
           - jnp.exp(jnp.sum(lv[2:3] * lv[3:4], axis=-1, keepdims=True)) + lam_init)
    gain = g_ref[...] * (1.0 - lam_init)
    first_map = lax.broadcasted_iota(jnp.int32, (1, LANES), 1) < HEAD_DIM
    zero = jnp.zeros((tq, LANES), BF16)

    def operands(hd):
        cols = slice(hd * LANES, (hd + 1) * LANES)
        v2_s[hd, :, LANES:] = jnp.ones((n_keys, LANES), BF16)
        if n_cache:
            kt_s[cols, 0:n_cache] = ckt_ref[cols, :].astype(BF16)
            kt_s[cols, n_cache:] = kt_ref[cols, :]
            cache_rows = pl.ds(head0 + hd, n_cache, stride=N_HEADS)
            v2_s[hd, 0:n_cache, 0:LANES] = cv_ref[cache_rows, :].astype(BF16)
            v2_s[hd, n_cache:, 0:LANES] = v_ref[:, cols]
            return kt_s[cols, :], v2_s[hd]
        v2_s[hd, :, 0:LANES] = v_ref[:, cols]
        return kt_ref[cols, :], v2_s[hd]

    def scores(hd, i, kt):
        q = q_ref[i * tq:(i + 1) * tq, hd * LANES:(hd + 1) * LANES]
        q2 = jnp.concatenate([jnp.where(first_map, q, zero), jnp.where(first_map, zero, q)], axis=0)
        s = jnp.dot(q2, kt, preferred_element_type=F32)
        return s, jnp.max(s, axis=-1, keepdims=True)

    def finish(hd, i, v2, s, m):
        p = jnp.exp2(s - m).astype(BF16)
        pv = jnp.dot(p, v2, preferred_element_type=F32)
        pv = pv[:, :LANES] / pv[:, LANES:]
        o = pv[:tq] - lam * pv[tq:]
        o_ref[i * tq:(i + 1) * tq, hd * LANES:(hd + 1) * LANES] = (_rms(o, 1.0) * gain).astype(BF16)

    n_tiles = q_ref.shape[0] // tq
    pending = None
    for hd in range(n_heads):
        kt, v2 = operands(hd)
        for i in range(n_tiles):
            cur = (hd, i, v2) + scores(hd, i, kt)
            if pending is not None:
                finish(*pending)
            pending = cur
    finish(*pending)


def _attn_call(q, kt, v, cache, lam_vecs, subln_g, *, batch, seq, lam_init, tq, heads):
    const = lambda b, h: (0, 0)
    width = heads * LANES
    n_groups = N_HEADS // heads
    in_specs = [pl.BlockSpec((seq, width), lambda b, h: (b, h)),
                pl.BlockSpec((width, seq), lambda b, h: (b * n_groups + h, 0)),
                pl.BlockSpec((seq, width), lambda b, h: (b, h))]
    args = [q, kt, v]
    n_cache = 0
    if cache is not None:
        ckt, cv = cache
        n_cache = ckt.shape[1]
        in_specs += [pl.BlockSpec((width, n_cache), lambda b, h: (b * n_groups + h, 0)),
                     pl.BlockSpec((n_cache * N_HEADS, LANES), lambda b, h: (b, 0))]
        args += [ckt, cv]
    n_keys = n_cache + seq
    scratch = [pltpu.VMEM((heads, n_keys, 2 * LANES), BF16)]
    if n_cache:
        scratch = [pltpu.VMEM((width, n_keys), BF16)] + scratch
    in_specs += [pl.BlockSpec(lam_vecs.shape, const), pl.BlockSpec((1, LANES), const)]
    args += [lam_vecs, subln_g]
    return pl.pallas_call(
        functools.partial(_attn_kernel, n_cache=n_cache, lam_init=lam_init, tq=tq),
        grid=(batch, n_groups),
        in_specs=in_specs,
        out_specs=pl.BlockSpec((seq, width), lambda b, h: (b, h)),
        out_shape=jax.ShapeDtypeStruct((batch * seq, N_HEADS * V_DIM), BF16),
        scratch_shapes=scratch,
        compiler_params=_params("arbitrary", "arbitrary"),
        name="attn_lat" if cache is not None else "attn_ctx",
    )(*args)


def _dft_tables(seq):
    n = 2 * seq
    f = np.arange(seq, dtype=np.int64)[:, None]
    t = np.arange(seq, dtype=np.int64)[None, :]
    ang = 2.0 * np.pi * ((f * t) % n).astype(np.float64) / n
    alt = np.where(np.arange(seq) % 2 == 0, 1.0, -1.0)
    cos, msin = np.cos(ang), -np.sin(ang)
    msin[0, :] = alt
    fwd = np.concatenate([cos, msin], axis=0)
    wgt = np.where(f == 0, 1.0, 2.0) / n
    inv_c = (cos * wgt).T
    inv_s = (msin * wgt).T
    inv = np.concatenate([inv_c, inv_s], axis=1)
    return fwd.astype(np.float32), inv.astype(np.float32)


def _filter_hidden(emb_ref, w1_ref, b1_ref, w2_ref, b2_ref, fr_ref):
    hp = functools.partial(jnp.dot, preferred_element_type=F32, precision=lax.Precision.HIGHEST)
    fr = fr_ref[0:1, :]
    h = jnp.sin(fr * (hp(emb_ref[...], w1_ref[...]) + b1_ref[0:1, :]))
    return jnp.sin(fr * (hp(h, w2_ref[...]) + b2_ref[0:1, :]))


def _filter_planes(h, w3f_ref, w3b_ref, decay, dft_ref, out_ref):
    seq = h.shape[0]
    fwd = jnp.dot(h, w3f_ref[...].astype(BF16), preferred_element_type=F32) * decay
    bwd = jnp.dot(h, w3b_ref[...].astype(BF16), preferred_element_type=F32) * decay
    hsum, hdif = fwd + bwd, fwd - bwd
    re = jnp.dot(dft_ref[0:seq, :], hsum.astype(BF16), preferred_element_type=F32)
    im = jnp.dot(dft_ref[seq:, :], hdif.astype(BF16), preferred_element_type=F32)
    row = lax.broadcasted_iota(jnp.int32, (seq, 1), 0)
    alt = jnp.where(row % 2 == 0, 1.0, -1.0)
    nyq = jnp.sum(hsum * alt, axis=0, keepdims=True)
    out_ref[0] = re
    out_ref[1] = jnp.where(row == 0, 0.0, im)
    out_ref[2] = jnp.where(row == 0, nyq, re)


def _hyena_kernel(uv_ref, u1_ref, u2_ref, wv_ref, w1_ref, w2_ref, bv_ref, b1_ref, b2_ref, hb_ref,
                  emb_ref, fw1_ref, fb1_ref, fw2_ref, fb2_ref, fr_ref,
                  w3f0_ref, w3b0_ref, w3f1_ref, w3b1_ref, t_ref, dl_ref, fwd_s, inv_s,
                  o_ref, hid_s, hf_s):
    seq = emb_ref.shape[0]
    new_tile = pl.program_id(1) == 0

    @pl.when(jnp.logical_and(pl.program_id(0) == 0, new_tile))
    def _():
        hid_s[...] = _filter_hidden(emb_ref, fw1_ref, fb1_ref, fw2_ref, fb2_ref, fr_ref)

    @pl.when(new_tile)
    def _():
        h = hid_s[...].astype(BF16)
        decay = jnp.exp(-t_ref[...] * jnp.abs(dl_ref[...]))
        _filter_planes(h, w3f0_ref, w3b0_ref, decay, fwd_s, hf_s.at[0])
        _filter_planes(h, w3f1_ref, w3b1_ref, decay, fwd_s, hf_s.at[1])

    chains = [slice(c * seq, (c + 1) * seq) for c in range(uv_ref.shape[0] // seq)]
    row = lax.broadcasted_iota(jnp.int32, (seq, 1), 0)
    first, last = row == 0, row == seq - 1

    def short_conv(u_ref, w_ref, b_ref):
        outs = []
        for rows in chains:
            u = u_ref[rows, :]
            prev = jnp.where(first, 0.0, pltpu.roll(u, 1, axis=0))
            nxt = jnp.where(last, 0.0, pltpu.roll(u, seq - 1, axis=0))
            outs.append(prev * w_ref[0:1, :] + u * w_ref[1:2, :] + nxt * w_ref[2:3, :] + b_ref[...])
        return outs

    def long_conv(us, order):
        h_ref = hf_s.at[order]
        bias = hb_ref[order:order + 1, :]
        specs = [jnp.dot(fwd_s[...], u.astype(BF16), preferred_element_type=F32) for u in us]
        ys = []
        for spec in specs:
            ure, uim = spec[:seq], spec[seq:]
            yre = ure * h_ref[0] - uim * h_ref[1]
            yim = ure * h_ref[1] + uim * h_ref[2]
            ys.append(jnp.concatenate([yre, yim], axis=0).astype(BF16))
        return [jnp.dot(inv_s[...], y, preferred_element_type=F32) + u * bias
                for y, u in zip(ys, us)]

    v = short_conv(uv_ref, wv_ref, bv_ref)
    x1 = short_conv(u1_ref, w1_ref, b1_ref)
    z = [a * b for a, b in zip(x1, long_conv(v, 0))]
    x2 = short_conv(u2_ref, w2_ref, b2_ref)
    for rows, a, b in zip(chains, x2, long_conv(z, 1)):
        o_ref[rows, :] = (a * b).astype(BF16)


def _hyena_call(u, conv_w, conv_b, hy_bias, filt, fwd, inv, *, batch, seq, ct, n_chain):
    emb, filt_pack, w3, tcol, deltas = filt
    nct = D_HYENA // ct
    first_vec = 2 * FILT_PAD // SUBLANES
    mat = lambda n: pl.BlockSpec((FILT_PAD, FILT_PAD), lambda c, b: (n, 0))
    vec = lambda n: pl.BlockSpec((SUBLANES, FILT_PAD), lambda c, b: (first_vec + n, 0))
    pack_specs = [mat(0), vec(0), mat(1), vec(1), vec(2)]
    rows = n_chain * seq
    col = lambda part: (lambda c, b: (b, part * nct + c))
    wcol = lambda part: (lambda c, b: (0, part * nct + c))
    small = lambda shape: pl.BlockSpec(shape, lambda c, b: (0, 0))
    w3col = lambda direction, order: (lambda c, b: (0, (direction * HYENA_ORDER + order) * nct + c))
    in_specs = ([pl.BlockSpec((rows, ct), col(p)) for p in range(3)]
                + [pl.BlockSpec((3, ct), wcol(p)) for p in range(3)]
                + [pl.BlockSpec((1, ct), wcol(p)) for p in range(3)]
                + [pl.BlockSpec((HYENA_ORDER, ct), lambda c, b: (0, c)),
                   small((seq, FILT_PAD))] + pack_specs
                + [pl.BlockSpec((FILT_PAD, ct), w3col(direction, order))
                   for order in range(HYENA_ORDER) for direction in range(2)]
                + [small((seq, 1)), pl.BlockSpec((1, ct), lambda c, b: (0, c)),
                   _resident((2 * seq, seq)), _resident((seq, 2 * seq))])
    return pl.pallas_call(
        _hyena_kernel,
        grid=(nct, batch // n_chain),
        in_specs=in_specs,
        out_specs=pl.BlockSpec((rows, ct), lambda c, b: (b, c)),
        out_shape=jax.ShapeDtypeStruct((batch * seq, D_HYENA), BF16),
        scratch_shapes=[pltpu.VMEM((seq, FILT_PAD), F32),
                        pltpu.VMEM((HYENA_ORDER, 3, seq, ct), F32)],
        compiler_params=_params("arbitrary", "arbitrary"),
        name=f"hyena_{seq}",
    )(u, u, u, conv_w, conv_w, conv_w, conv_b, conv_b, conv_b, hy_bias,
      emb, *[filt_pack] * len(pack_specs), w3, w3, w3, w3, tcol, deltas, fwd, inv)


FF_CHUNK = D_MODEL
N_FF_CHUNKS = D_FF // FF_CHUNK


def _post_kernel(xc_ref, xl_ref, oac_ref, oal_ref, ohc_ref, ohl_ref, mod_ref, g1_ref, g2_ref, gf_ref,
                 wga0_ref, wga1_ref, wgh0_ref, wgh1_ref,
                 wa_ref, wh_ref, wo_ref, wu_ref, wd_ref, yc_ref, yl_ref, *, n_ctx_steps):
    d = functools.partial(jnp.dot, preferred_element_type=F32)
    is_ctx = pl.program_id(0) < n_ctx_steps
    pick = lambda c_ref, l_ref: jnp.where(is_ctx, c_ref[...], l_ref[...])
    x = pick(xc_ref, xl_ref)
    h1 = (_rms(x, g1_ref[...]) * (1.0 + mod_ref[1:2, :]) + mod_ref[0:1, :]).astype(BF16)

    def gate(*w_refs):
        return jnp.concatenate([jax.nn.sigmoid(d(h1, w[...])) for w in w_refs], axis=1)

    merged = (gate(wga0_ref, wga1_ref) * d(pick(oac_ref, oal_ref), wa_ref[...])
              + gate(wgh0_ref, wgh1_ref) * d(pick(ohc_ref, ohl_ref), wh_ref[...]))
    x = x + mod_ref[2:3, :] * d(merged.astype(BF16), wo_ref[...])
    h2 = (_rms(x, g2_ref[...]) * (1.0 + mod_ref[4:5, :]) + mod_ref[3:4, :]).astype(BF16)
    acc = jnp.zeros_like(x)
    for c in range(N_FF_CHUNKS):
        sl = slice(c * FF_CHUNK, (c + 1) * FF_CHUNK)
        up = jnp.maximum(d(h2, wu_ref[:, sl]), 0.0)
        acc = acc + d((up * up).astype(BF16), wd_ref[sl, :])
    y = _rms(x + mod_ref[5:6, :] * acc, gf_ref[...])

    @pl.when(is_ctx)
    def _():
        yc_ref[...] = y

    @pl.when(jnp.logical_not(is_ctx))
    def _():
        yl_ref[...] = y


def _post_call(ctx, lat, mod, g1, g2, gf, w_in, weights, *, lat_seq, tm):
    n_ctx, n_lat = ctx[0].shape[0] // tm, lat[0].shape[0] // tm
    per_seq = lat_seq // tm
    gate_windows = [_resident((D_MODEL, GATE_WINDOW), (0, GATE_OFF // GATE_WINDOW + n))
                    for n in range(GATE_COLS // GATE_WINDOW)]
    ctx_row = lambda i: (jnp.minimum(i, n_ctx - 1), 0)
    lat_row = lambda i: (jnp.maximum(i - n_ctx, 0), 0)
    mod_idx = lambda i: (jnp.where(i < n_ctx, 0, 1 + (i - n_ctx) // per_seq), 0, 0)
    widths = (D_MODEL, N_HEADS * V_DIM, D_HYENA)
    in_specs = [pl.BlockSpec((tm, w), r) for w in widths for r in (ctx_row, lat_row)]
    args = [a for pair in zip(ctx, lat) for a in pair]
    return pl.pallas_call(
        functools.partial(_post_kernel, n_ctx_steps=n_ctx),
        grid=(n_ctx + n_lat,),
        in_specs=in_specs
                 + [pl.BlockSpec((None, 6, D_MODEL), mod_idx),
                    _resident((1, D_MODEL)), _resident((1, D_MODEL)), _resident((1, D_MODEL))]
                 + gate_windows + [_resident(w.shape) for w in weights],
        out_specs=[pl.BlockSpec((tm, D_MODEL), ctx_row), pl.BlockSpec((tm, D_MODEL), lat_row)],
        out_shape=[jax.ShapeDtypeStruct((n_ctx * tm, D_MODEL), F32),
                   jax.ShapeDtypeStruct((n_lat * tm, D_MODEL), F32)],
        compiler_params=_params("arbitrary"),
        name="post",
    )(*args, mod, g1, g2, gf, *[w_in] * len(gate_windows), *weights)


def _rope_tables(seq):
    half = HEAD_DIM // 2
    n = half // 2
    inv = ROPE_THETA ** (-np.arange(n, dtype=np.float64) / n)
    pos = np.arange(seq)
    ang_row = (pos // GRID_W).astype(np.float64)[:, None] * inv[None, :]
    ang_col = (pos % GRID_W).astype(np.float64)[:, None] * inv[None, :]
    zeros = np.zeros_like(ang_row)

    def per_map(a_row, a_col, lo, hi):
        return np.concatenate([lo(a_row), hi(a_row), lo(a_col), hi(a_col)], axis=-1)

    cos = per_map(ang_row, ang_col, np.cos, np.cos)
    sin_lo = per_map(ang_row, ang_col, lambda a: -np.sin(a), lambda a: zeros)
    sin_hi = per_map(ang_row, ang_col, lambda a: zeros, np.sin)
    return tuple(jnp.asarray(np.concatenate([tab, tab], axis=-1).astype(np.float32))
                 for tab in (cos, sin_lo, sin_hi))


def _filter_embedding(seq):
    bands = (FILTER_EMB - 1) // 2
    t = np.linspace(0.0, 1.0, seq)[:, None]
    wpos = 2.0 * np.pi * np.arange(seq, dtype=np.float64)[:, None] / seq
    f = np.linspace(1e-4, bands - 1, bands)[None, :]
    emb = np.concatenate([t, np.cos(f * wpos), -np.sin(f * wpos)], axis=-1)
    emb = np.pad(emb, ((0, 0), (0, FILT_PAD - FILTER_EMB)))
    return jnp.asarray(emb.astype(np.float32)), jnp.asarray(t.astype(np.float32))


def _tiles(seq):
    short = seq <= 256
    return dict(tm=512, tq=128, attn_heads=N_HEADS if short else 4, ct=MXU_WIDTH,
                hyena_seqs=4 if short else 2)


def _pad_to(x, rows, cols):
    return jnp.pad(x, ((0, rows - x.shape[0]), (0, cols - x.shape[1])))


def kernel(x_prompt, x_sample, cache_k, cache_v, c, c_ctx, w_ada, b_ada, norm1_g, norm2_g, w_in,
           lam_q1, lam_k1, lam_q2, lam_k2, attn_subln_g, conv_w, conv_b, filt_w1, filt_b1,
           filt_w2, filt_b2, filt_w3, filt_freq, hy_bias, w_br_attn, w_br_hy, w_out, w_up,
           w_down, final_g):
    depth = w_in.shape[0]
    assert depth == 1, "single trunk layer"
    layer = 0
    lam_init = 0.8 - 0.6 * math.exp(-0.3 * layer)
    n_ctx, ctx_len, _ = x_prompt.shape
    n_lat, lat_len, _ = x_sample.shape
    past = cache_k.shape[2]

    cvec = jnp.concatenate([c_ctx[None, :], c], axis=0)
    cvec = jnp.pad(cvec, ((0, MOD_ROWS - cvec.shape[0]), (0, 0)))
    mod = _mod_call(cvec, w_ada[layer], b_ada[layer][None, :])

    w_in_b = w_in[layer].astype(BF16)
    post_w_f32 = tuple(w[layer] for w in (w_br_attn, w_br_hy, w_out, w_up, w_down))
    g1, g2, gf = norm1_g[layer][None, :], norm2_g[layer][None, :], final_g[None, :]
    lam_vecs = jnp.stack([lam_q1[layer], lam_k1[layer], lam_q2[layer], lam_k2[layer]])
    subln_g = attn_subln_g[layer][None, :]

    filt_pack = jnp.concatenate(
        [_pad_to(filt_w1[layer], FILT_PAD, FILT_PAD), _pad_to(filt_w2[layer], FILT_PAD, FILT_PAD)]
        + [_pad_to(vec[layer][None, :], SUBLANES, FILT_PAD)
           for vec in (filt_b1, filt_b2, filt_freq)], axis=0)
    w3 = _pad_to(filt_w3[layer], FILT_PAD, filt_w3.shape[2])
    deltas = jnp.asarray(np.linspace(math.log(DECAY_TARGET) / FAST_DECAY_PCT,
                                     math.log(DECAY_TARGET) / SLOW_DECAY_PCT,
                                     D_HYENA)[None, :].astype(np.float32))

    def project(x3, batch, seq, latent, cast=()):
        x = x3.reshape(batch * seq, D_MODEL)
        rope = _rope_tables(seq) if latent else None
        return x, _in_proj_call(x, mod, g1, w_in_b, rope, seq=seq, latent=latent,
                                tm=_tiles(seq)["tm"], cast=cast)

    def mix(x, outs, batch, seq, latent):
        tiles = _tiles(seq)
        if latent:
            q, kt, v, u = outs
            ckt = jnp.transpose(cache_k[:, layer], (0, 2, 3, 4, 1)).reshape(batch * K_COLS, past)
            cache = (ckt, cache_v[:, layer].reshape(batch * past * N_HEADS, V_DIM))
            kf = vf = None
        else:
            q, kt, v, kf, vf, u = outs
            cache = None
        o_attn = _attn_call(q, kt, v, cache, lam_vecs, subln_g, batch=batch, seq=seq,
                            lam_init=lam_init, tq=tiles["tq"], heads=tiles["attn_heads"])

        fwd_np, inv_np = _dft_tables(seq)
        fwd_b = jnp.asarray(fwd_np).astype(BF16)
        inv_b = jnp.asarray(inv_np).astype(BF16)
        emb, tcol = _filter_embedding(seq)
        filt = (emb, filt_pack, w3, tcol, deltas)
        o_hy = _hyena_call(u, conv_w[layer], conv_b[layer][None, :], hy_bias[layer], filt,
                           fwd_b, inv_b, batch=batch, seq=seq, ct=tiles["ct"],
                           n_chain=tiles["hyena_seqs"])

        return (x, o_attn, o_hy), kf, vf

    x_ctx, ctx_outs = project(x_prompt, n_ctx, ctx_len, False)
    x_lat, lat_outs = project(x_sample, n_lat, lat_len, True, cast=post_w_f32)
    lat_outs, post_w = lat_outs[:-len(post_w_f32)], lat_outs[-len(post_w_f32):]
    ctx_mixed, kf, vf = mix(x_ctx, ctx_outs, n_ctx, ctx_len, False)
    lat_mixed, _, _ = mix(x_lat, lat_outs, n_lat, lat_len, True)
    y_prompt, y_sample = _post_call(ctx_mixed, lat_mixed, mod, g1, g2, gf, w_in_b, post_w,
                                    lat_seq=lat_len, tm=_tiles(lat_len)["tm"])
    y_prompt = y_prompt.reshape(n_ctx, ctx_len, D_MODEL)
    y_sample = y_sample.reshape(n_lat, lat_len, D_MODEL)
    new_cache_k = jnp.transpose(kf.reshape(n_ctx, depth, N_HEADS, 2, HEAD_DIM, ctx_len),
                                (0, 1, 5, 2, 3, 4))
    new_cache_v = vf.reshape(n_ctx, depth, ctx_len, N_HEADS, V_DIM)
    return (y_prompt, y_sample, new_cache_k, new_cache_v)
```

```python
import functools
import math

import numpy as np
import jax
import jax.numpy as jnp
from jax import lax
from jax.experimental import pallas as pl
from jax.experimental.pallas import tpu as pltpu

D_MODEL = 1024
GRID_W = 64
N_HEADS = 8
HEAD_DIM = 64
V_DIM = 2 * HEAD_DIM
D_HYENA = D_MODEL // 2
HYENA_ORDER = 2
FILTER_EMB = 33
FILTER_HIDDEN = 64
D_FF = 4 * D_MODEL
ROPE_THETA = 10000.0
EPS = 1e-6
LOG2_E = math.log2(math.e)
DECAY_TARGET = 1e-2
FAST_DECAY_PCT = 0.3
SLOW_DECAY_PCT = 1.5

Q_COLS = N_HEADS * 2 * HEAD_DIM
K_COLS = N_HEADS * 2 * HEAD_DIM
V_COLS = N_HEADS * V_DIM
HY_COLS = 3 * D_HYENA
GATE_COLS = 2 * D_MODEL
Q_OFF = 0
K_OFF = Q_OFF + Q_COLS
V_OFF = K_OFF + K_COLS
HY_OFF = V_OFF + V_COLS
GATE_OFF = HY_OFF + HY_COLS
IN_COLS = GATE_OFF + GATE_COLS
GATE_WINDOW = math.gcd(GATE_OFF, D_MODEL)

LANES = 128
SUBLANES = 8
MXU_WIDTH = 256
MOD_CTX_ROWS = 8
FILT_PAD = 128
VMEM_LIMIT = 56 * 1024 * 1024

BF16 = jnp.bfloat16
F32 = jnp.float32


def _resident(shape, index=None):
    index = (0,) * len(shape) if index is None else index
    return pl.BlockSpec(shape, lambda *_: index, pipeline_mode=pl.Buffered(1))


def _params(*sem):
    return pltpu.CompilerParams(dimension_semantics=sem, vmem_limit_bytes=VMEM_LIMIT)


def _rms(x, g):
    return x * lax.rsqrt(jnp.mean(x * x, axis=-1, keepdims=True) + EPS) * g


def _mod_kernel(cctx_ref, c_ref, w_ref, b_ref, o_ref):
    c = jnp.concatenate([jnp.broadcast_to(cctx_ref[...], (MOD_CTX_ROWS, D_MODEL)), c_ref[...]],
                        axis=0)
    s = (c * jax.nn.sigmoid(c)).astype(BF16)
    chunk = jnp.dot(s, w_ref[...].astype(BF16), preferred_element_type=F32) + b_ref[...]
    for j in range(o_ref.shape[1]):
        @pl.when(pl.program_id(0) == j)
        def _():
            o_ref[:, j, :] = chunk


def _mod_call(c_ctx, c, w_ada, b_ada):
    n_chunks = w_ada.shape[1] // D_MODEL
    rows = MOD_CTX_ROWS + c.shape[0]
    return pl.pallas_call(
        _mod_kernel,
        grid=(n_chunks,),
        in_specs=[pl.BlockSpec(c_ctx.shape, lambda j: (0, 0)),
                  pl.BlockSpec(c.shape, lambda j: (0, 0)),
                  pl.BlockSpec((D_MODEL, D_MODEL), lambda j: (0, j)),
                  pl.BlockSpec((1, D_MODEL), lambda j: (0, j))],
        out_specs=pl.BlockSpec((rows, n_chunks, D_MODEL), lambda j: (0, 0, 0)),
        out_shape=jax.ShapeDtypeStruct((rows, n_chunks, D_MODEL), F32),
        compiler_params=_params("arbitrary"),
        name="mod",
    )(c_ctx, c, w_ada, b_ada)


def _rope(x, cos, sin_lo, sin_hi):
    return (x * cos + pltpu.roll(x, 16, axis=1) * sin_hi
            + pltpu.roll(x, LANES - 16, axis=1) * sin_lo)


def _in_proj_kernel(*refs, latent, n_cast):
    refs = list(refs)
    take = lambda n: [refs.pop(0) for _ in range(n)]
    x_ref, mod_ref, g_ref, w_ref = take(4)
    if latent:
        cos_ref, slo_ref, shi_ref = take(3)
    cast_src = take(n_cast)
    q_ref, kt_ref, v_ref = take(3)
    if not latent:
        ktf_ref, vf_ref = take(2)
    (u_ref,) = take(1)
    cast_dst = take(n_cast)
    tm = x_ref.shape[0]
    x = x_ref[...]
    h = _rms(x, g_ref[...]) * (1.0 + mod_ref[1:2, :]) + mod_ref[0:1, :]
    hb = h.astype(BF16)

    def proj(off, width):
        return jnp.dot(hb, w_ref[:, off:off + width], preferred_element_type=F32)

    q = proj(Q_OFF, Q_COLS) * (HEAD_DIM ** -0.5 * LOG2_E)
    k = proj(K_OFF, K_COLS)
    if latent:
        cos, slo, shi = cos_ref[...], slo_ref[...], shi_ref[...]
        k_heads = []
        for hd in range(N_HEADS):
            sl = slice(hd * LANES, (hd + 1) * LANES)
            q_ref[:, sl] = _rope(q[:, sl], cos, slo, shi).astype(BF16)
            k_heads.append(_rope(k[:, sl], cos, slo, shi))
        kt_ref[...] = jnp.concatenate(k_heads, axis=1).T.astype(BF16)
    else:
        q_ref[...] = q.astype(BF16)
        seq = kt_ref.shape[1]
        for s in range(tm // seq):
            kt = k[s * seq:(s + 1) * seq, :].T
            rows = slice(s * K_COLS, (s + 1) * K_COLS)
            ktf_ref[rows, :] = kt
            kt_ref[rows, :] = kt.astype(BF16)
    v = proj(V_OFF, V_COLS)
    v_ref[...] = v.astype(BF16)
    if not latent:
        for hd in range(N_HEADS):
            vf_ref[pl.ds(hd, tm, stride=N_HEADS), :] = v[:, hd * LANES:(hd + 1) * LANES]
    u_ref[...] = proj(HY_OFF, HY_COLS)
    for src, dst in zip(cast_src, cast_dst):
        dst[...] = src[...].astype(BF16)


def _in_proj_call(x, mod, g1, w_in, rope, *, seq, latent, tm, cast=()):
    t = x.shape[0]
    n_steps = t // tm
    per_seq = seq // tm if latent else 1

    def mod_idx(i):
        return (MOD_CTX_ROWS + i // per_seq if latent else 0, 0, 0)

    row = lambda i: (i, 0)
    in_specs = [pl.BlockSpec((tm, D_MODEL), row),
                pl.BlockSpec((None, 6, D_MODEL), mod_idx),
                _resident((1, D_MODEL)),
                _resident((D_MODEL, GATE_OFF))]
    args = [x, mod, g1, w_in]
    tok = lambda width, dtype: ((t, width), dtype, pl.BlockSpec((tm, width), row))
    if latent:
        in_specs += [pl.BlockSpec((tm, LANES), lambda i: (i % per_seq, 0))] * 3
        args += list(rope)
        kt_spec = pl.BlockSpec((K_COLS, tm), lambda i: (i // per_seq, i % per_seq))
    else:
        kt_spec = pl.BlockSpec((tm // seq * K_COLS, seq), row)
    kt_shape = (t // seq * K_COLS, seq)
    outs = [tok(Q_COLS, BF16), (kt_shape, BF16, kt_spec), tok(V_COLS, BF16)]
    if not latent:
        outs += [(kt_shape, F32, kt_spec),
                 ((t * N_HEADS, LANES), F32, pl.BlockSpec((tm * N_HEADS, LANES), row))]
    outs += [tok(HY_COLS, F32)]
    for w in cast:
        slab = pl.BlockSpec((w.shape[0] // n_steps, w.shape[1]), row)
        in_specs.append(slab)
        args.append(w)
        outs.append((w.shape, BF16, slab))
    return pl.pallas_call(
        functools.partial(_in_proj_kernel, latent=latent, n_cast=len(cast)),
        grid=(n_steps,),
        in_specs=in_specs,
        out_specs=[spec for _, _, spec in outs],
        out_shape=[jax.ShapeDtypeStruct(s, d) for s, d, _ in outs],
        compiler_params=_params("arbitrary"),
        name="in_proj_lat" if latent else "in_proj_ctx",
    )(*args)


def _attn_kernel(*refs, n_cache, lam_init, tq):
    if n_cache:
        (q_ref, kt_ref, v_ref, ckt_ref, cv_ref, lam_ref, g_ref, o_ref, kt_s, v2_s) = refs
    else:
        (q_ref, kt_ref, v_ref, lam_ref, g_ref, o_ref, v2_s) = refs
    n_heads, n_keys, _ = v2_s.shape
    head0 = pl.program_id(1) * n_heads
    lv = lam_ref[...]
    lam = (jnp.exp(jnp.sum(lv[0:1] * lv[1:2], axis=-1, keepdims=True))
           - jnp.exp(jnp.sum(lv[2:3] * lv[3:4], axis=-1, keepdims=True)) + lam_init)
    gain = g_ref[...] * (1.0 - lam_init)
    first_map = lax.broadcasted_iota(jnp.int32, (1, LANES), 1) < HEAD_DIM
    zero = jnp.zeros((tq, LANES), BF16)

    def operands(hd):
        cols = slice(hd * LANES, (hd + 1) * LANES)
        v2_s[hd, :, LANES:] = jnp.ones((n_keys, LANES), BF16)
        if n_cache:
            kt_s[cols, 0:n_cache] = ckt_ref[cols, :].astype(BF16)
            kt_s[cols, n_cache:] = kt_ref[cols, :]
            cache_rows = pl.ds(head0 + hd, n_cache, stride=N_HEADS)
            v2_s[hd, 0:n_cache, 0:LANES] = cv_ref[cache_rows, :].astype(BF16)
            v2_s[hd, n_cache:, 0:LANES] = v_ref[:, cols]
            return kt_s[cols, :], v2_s[hd]
        v2_s[hd, :, 0:LANES] = v_ref[:, cols]
        return kt_ref[cols, :], v2_s[hd]

    def scores(hd, i, kt):
        q = q_ref[i * tq:(i + 1) * tq, hd * LANES:(hd + 1) * LANES]
        q2 = jnp.concatenate([jnp.where(first_map, q, zero), jnp.where(first_map, zero, q)], axis=0)
        s = jnp.dot(q2, kt, preferred_element_type=F32)
        return s, jnp.max(s, axis=-1, keepdims=True)

    def finish(hd, i, v2, s, m):
        p = jnp.exp2(s - m).astype(BF16)
        pv = jnp.dot(p, v2, preferred_element_type=F32)
        pv = pv[:, :LANES] / pv[:, LANES:]
        o = pv[:tq] - lam * pv[tq:]
        o_ref[i * tq:(i + 1) * tq, hd * LANES:(hd + 1) * LANES] = (_rms(o, 1.0) * gain).astype(BF16)

    n_tiles = q_ref.shape[0] // tq
    pending = None
    for hd in range(n_heads):
        kt, v2 = operands(hd)
        for i in range(n_tiles):
            cur = (hd, i, v2) + scores(hd, i, kt)
            if pending is not None:
                finish(*pending)
            pending = cur
    finish(*pending)


def _attn_call(q, kt, v, cache, lam_vecs, subln_g, *, batch, seq, lam_init, tq, heads):
    const = lambda b, h: (0, 0)
    width = heads * LANES
    n_groups = N_HEADS // heads
    in_specs = [pl.BlockSpec((seq, width), lambda b, h: (b, h)),
                pl.BlockSpec((width, seq), lambda b, h: (b * n_groups + h, 0)),
                pl.BlockSpec((seq, width), lambda b, h: (b, h))]
    args = [q, kt, v]
    n_cache = 0
    if cache is not None:
        ckt, cv = cache
        n_cache = ckt.shape[1]
        in_specs += [pl.BlockSpec((width, n_cache), lambda b, h: (b * n_groups + h, 0)),
                     pl.BlockSpec((n_cache * N_HEADS, LANES), lambda b, h: (b, 0))]
        args += [ckt, cv]
    n_keys = n_cache + seq
    scratch = [pltpu.VMEM((heads, n_keys, 2 * LANES), BF16)]
    if n_cache:
        scratch = [pltpu.VMEM((width, n_keys), BF16)] + scratch
    in_specs += [pl.BlockSpec(lam_vecs.shape, const), pl.BlockSpec((1, LANES), const)]
    args += [lam_vecs, subln_g]
    return pl.pallas_call(
        functools.partial(_attn_kernel, n_cache=n_cache, lam_init=lam_init, tq=tq),
        grid=(batch, n_groups),
        in_specs=in_specs,
        out_specs=pl.BlockSpec((seq, width), lambda b, h: (b, h)),
        out_shape=jax.ShapeDtypeStruct((batch * seq, N_HEADS * V_DIM), BF16),
        scratch_shapes=scratch,
        compiler_params=_params("arbitrary", "arbitrary"),
        name="attn_lat" if cache is not None else "attn_ctx",
    )(*args)


def _dft_tables(seq):
    n = 2 * seq
    f = np.arange(seq, dtype=np.int64)[:, None]
    t = np.arange(seq, dtype=np.int64)[None, :]
    ang = 2.0 * np.pi * ((f * t) % n).astype(np.float64) / n
    alt = np.where(np.arange(seq) % 2 == 0, 1.0, -1.0)
    cos, msin = np.cos(ang), -np.sin(ang)
    msin[0, :] = alt
    fwd = np.concatenate([cos, msin], axis=0)
    wgt = np.where(f == 0, 1.0, 2.0) / n
    inv_c = (cos * wgt).T
    inv_s = (msin * wgt).T
    inv = np.concatenate([inv_c, inv_s], axis=1)
    return fwd.astype(np.float32), inv.astype(np.float32)


def _filter_hidden(emb_ref, w1_ref, b1_ref, w2_ref, b2_ref, fr_ref):
    hp = functools.partial(jnp.dot, preferred_element_type=F32, precision=lax.Precision.HIGHEST)
    fr = fr_ref[0:1, :]
    h = jnp.sin(fr * (hp(emb_ref[...], w1_ref[...]) + b1_ref[0:1, :]))
    return jnp.sin(fr * (hp(h, w2_ref[...]) + b2_ref[0:1, :]))


def _filter_planes(h, w3f_ref, w3b_ref, decay, dft_ref, out_ref):
    seq = h.shape[0]
    h = h[:, :w3f_ref.shape[0]]
    fwd = jnp.dot(h, w3f_ref[...].astype(BF16), preferred_element_type=F32) * decay
    bwd = jnp.dot(h, w3b_ref[...].astype(BF16), preferred_element_type=F32) * decay
    hsum, hdif = fwd + bwd, fwd - bwd
    re = jnp.dot(dft_ref[0:seq, :], hsum.astype(BF16), preferred_element_type=F32)
    im = jnp.dot(dft_ref[seq:, :], hdif.astype(BF16), preferred_element_type=F32)
    row = lax.broadcasted_iota(jnp.int32, (seq, 1), 0)
    alt = jnp.where(row % 2 == 0, 1.0, -1.0)
    nyq = jnp.sum(hsum * alt, axis=0, keepdims=True)
    out_ref[0] = re
    out_ref[1] = jnp.where(row == 0, 0.0, im)
    out_ref[2] = jnp.where(row == 0, nyq, re)


def _hyena_kernel(uv_ref, u1_ref, u2_ref, wv_ref, w1_ref, w2_ref, bv_ref, b1_ref, b2_ref, hb_ref,
                  emb_ref, fw1_ref, fb1_ref, fw2_ref, fb2_ref, fr_ref,
                  w3f0_ref, w3b0_ref, w3f1_ref, w3b1_ref, t_ref, dl_ref, fwd_s, inv_s,
                  o_ref, hid_s, hf_s):
    seq = emb_ref.shape[0]
    new_tile = pl.program_id(1) == 0

    @pl.when(jnp.logical_and(pl.program_id(0) == 0, new_tile))
    def _():
        hid_s[...] = _filter_hidden(emb_ref, fw1_ref, fb1_ref, fw2_ref, fb2_ref, fr_ref)

    @pl.when(new_tile)
    def _():
        h = hid_s[...].astype(BF16)
        decay = jnp.exp(-t_ref[...] * jnp.abs(dl_ref[...]))
        _filter_planes(h, w3f0_ref, w3b0_ref, decay, fwd_s, hf_s.at[0])
        _filter_planes(h, w3f1_ref, w3b1_ref, decay, fwd_s, hf_s.at[1])

    chains = [slice(c * seq, (c + 1) * seq) for c in range(uv_ref.shape[0] // seq)]
    row = lax.broadcasted_iota(jnp.int32, (seq, 1), 0)
    first, last = row == 0, row == seq - 1

    def short_conv(u_ref, w_ref, b_ref):
        outs = []
        for rows in chains:
            u = u_ref[rows, :]
            prev = jnp.where(first, 0.0, pltpu.roll(u, 1, axis=0))
            nxt = jnp.where(last, 0.0, pltpu.roll(u, seq - 1, axis=0))
            outs.append(prev * w_ref[0:1, :] + u * w_ref[1:2, :] + nxt * w_ref[2:3, :] + b_ref[...])
        return outs

    def long_conv(us, order):
        h_ref = hf_s.at[order]
        bias = hb_ref[order:order + 1, :]
        specs = [jnp.dot(fwd_s[...], u.astype(BF16), preferred_element_type=F32) for u in us]
        ys = []
        for spec in specs:
            ure, uim = spec[:seq], spec[seq:]
            yre = ure * h_ref[0] - uim * h_ref[1]
            yim = ure * h_ref[1] + uim * h_ref[2]
            ys.append(jnp.concatenate([yre, yim], axis=0).astype(BF16))
        return [jnp.dot(inv_s[...], y, preferred_element_type=F32) + u * bias
                for y, u in zip(ys, us)]

    v = short_conv(uv_ref, wv_ref, bv_ref)
    x1 = short_conv(u1_ref, w1_ref, b1_ref)
    z = [a * b for a, b in zip(x1, long_conv(v, 0))]
    x2 = short_conv(u2_ref, w2_ref, b2_ref)
    for rows, a, b in zip(chains, x2, long_conv(z, 1)):
        o_ref[rows, :] = (a * b).astype(BF16)


def _hyena_call(u, conv_w, conv_b, hy_bias, filt, fwd, inv, *, batch, seq, ct, n_chain):
    emb, filt_pack, w3, tcol, deltas = filt
    nct = D_HYENA // ct
    first_vec = 2 * FILT_PAD // SUBLANES
    mat = lambda n: pl.BlockSpec((FILT_PAD, FILT_PAD), lambda c, b: (n, 0))
    vec = lambda n: pl.BlockSpec((SUBLANES, FILT_PAD), lambda c, b: (first_vec + n, 0))
    pack_specs = [mat(0), vec(0), mat(1), vec(1), vec(2)]
    rows = n_chain * seq
    col = lambda part: (lambda c, b: (b, part * nct + c))
    wcol = lambda part: (lambda c, b: (0, part * nct + c))
    small = lambda shape: pl.BlockSpec(shape, lambda c, b: (0, 0))
    w3col = lambda direction, order: (lambda c, b: (0, (direction * HYENA_ORDER + order) * nct + c))
    in_specs = ([pl.BlockSpec((rows, ct), col(p)) for p in range(3)]
                + [pl.BlockSpec((3, ct), wcol(p)) for p in range(3)]
                + [pl.BlockSpec((1, ct), wcol(p)) for p in range(3)]
                + [pl.BlockSpec((HYENA_ORDER, ct), lambda c, b: (0, c)),
                   small((seq, FILT_PAD))] + pack_specs
                + [pl.BlockSpec((FILTER_HIDDEN, ct), w3col(direction, order))
                   for order in range(HYENA_ORDER) for direction in range(2)]
                + [small((seq, 1)), pl.BlockSpec((1, ct), lambda c, b: (0, c)),
                   _resident((2 * seq, seq)), _resident((seq, 2 * seq))])
    return pl.pallas_call(
        _hyena_kernel,
        grid=(nct, batch // n_chain),
        in_specs=in_specs,
        out_specs=pl.BlockSpec((rows, ct), lambda c, b: (b, c)),
        out_shape=jax.ShapeDtypeStruct((batch * seq, D_HYENA), BF16),
        scratch_shapes=[pltpu.VMEM((seq, FILT_PAD), F32),
                        pltpu.VMEM((HYENA_ORDER, 3, seq, ct), F32)],
        compiler_params=_params("arbitrary", "arbitrary"),
        name=f"hyena_{seq}",
    )(u, u, u, conv_w, conv_w, conv_w, conv_b, conv_b, conv_b, hy_bias,
      emb, *[filt_pack] * len(pack_specs), w3, w3, w3, w3, tcol, deltas, fwd, inv)


FF_CHUNK = D_MODEL
N_FF_CHUNKS = D_FF // FF_CHUNK


def _post_kernel(xc_ref, xl_ref, oac_ref, oal_ref, ohc_ref, ohl_ref, mod_ref, g1_ref, g2_ref, gf_ref,
                 wga0_ref, wga1_ref, wgh0_ref, wgh1_ref,
                 wa_ref, wh_ref, wo_ref, wu_ref, wd_ref, yc_ref, yl_ref, *, n_ctx_steps):
    d = functools.partial(jnp.dot, preferred_element_type=F32)
    is_ctx = pl.program_id(0) < n_ctx_steps
    pick = lambda c_ref, l_ref: jnp.where(is_ctx, c_ref[...], l_ref[...])
    x = pick(xc_ref, xl_ref)
    h1 = (_rms(x, g1_ref[...]) * (1.0 + mod_ref[1:2, :]) + mod_ref[0:1, :]).astype(BF16)

    def gate(*w_refs):
        return jnp.concatenate([jax.nn.sigmoid(d(h1, w[...])) for w in w_refs], axis=1)

    merged = (gate(wga0_ref, wga1_ref) * d(pick(oac_ref, oal_ref), wa_ref[...])
              + gate(wgh0_ref, wgh1_ref) * d(pick(ohc_ref, ohl_ref), wh_ref[...]))
    x = x + mod_ref[2:3, :] * d(merged.astype(BF16), wo_ref[...])
    h2 = (_rms(x, g2_ref[...]) * (1.0 + mod_ref[4:5, :]) + mod_ref[3:4, :]).astype(BF16)
    acc = jnp.zeros_like(x)
    for c in range(N_FF_CHUNKS):
        sl = slice(c * FF_CHUNK, (c + 1) * FF_CHUNK)
        up = jnp.maximum(d(h2, wu_ref[:, sl]), 0.0)
        acc = acc + d((up * up).astype(BF16), wd_ref[sl, :])
    y = _rms(x + mod_ref[5:6, :] * acc, gf_ref[...])

    @pl.when(is_ctx)
    def _():
        yc_ref[...] = y

    @pl.when(jnp.logical_not(is_ctx))
    def _():
        yl_ref[...] = y


def _post_call(ctx, lat, mod, g1, g2, gf, w_in, weights, *, lat_seq, tm):
    n_ctx, n_lat = ctx[0].shape[0] // tm, lat[0].shape[0] // tm
    per_seq = lat_seq // tm
    gate_windows = [_resident((D_MODEL, GATE_WINDOW), (0, GATE_OFF // GATE_WINDOW + n))
                    for n in range(GATE_COLS // GATE_WINDOW)]
    ctx_row = lambda i: (jnp.minimum(i, n_ctx - 1), 0)
    lat_row = lambda i: (jnp.maximum(i - n_ctx, 0), 0)
    mod_idx = lambda i: (jnp.where(i < n_ctx, 0, MOD_CTX_ROWS + (i - n_ctx) // per_seq), 0, 0)
    widths = (D_MODEL, N_HEADS * V_DIM, D_HYENA)
    in_specs = [pl.BlockSpec((tm, w), r) for w in widths for r in (ctx_row, lat_row)]
    args = [a for pair in zip(ctx, lat) for a in pair]
    return pl.pallas_call(
        functools.partial(_post_kernel, n_ctx_steps=n_ctx),
        grid=(n_ctx + n_lat,),
        in_specs=in_specs
                 + [pl.BlockSpec((None, 6, D_MODEL), mod_idx),
                    _resident((1, D_MODEL)), _resident((1, D_MODEL)), _resident((1, D_MODEL))]
                 + gate_windows + [_resident(w.shape) for w in weights],
        out_specs=[pl.BlockSpec((tm, D_MODEL), ctx_row), pl.BlockSpec((tm, D_MODEL), lat_row)],
        out_shape=[jax.ShapeDtypeStruct((n_ctx * tm, D_MODEL), F32),
                   jax.ShapeDtypeStruct((n_lat * tm, D_MODEL), F32)],
        compiler_params=_params("arbitrary"),
        name="post",
    )(*args, mod, g1, g2, gf, *[w_in] * len(gate_windows), *weights)


def _rope_tables(seq):
    half = HEAD_DIM // 2
    n = half // 2
    inv = ROPE_THETA ** (-np.arange(n, dtype=np.float64) / n)
    pos = np.arange(seq)
    ang_row = (pos // GRID_W).astype(np.float64)[:, None] * inv[None, :]
    ang_col = (pos % GRID_W).astype(np.float64)[:, None] * inv[None, :]
    zeros = np.zeros_like(ang_row)

    def per_map(a_row, a_col, lo, hi):
        return np.concatenate([lo(a_row), hi(a_row), lo(a_col), hi(a_col)], axis=-1)

    cos = per_map(ang_row, ang_col, np.cos, np.cos)
    sin_lo = per_map(ang_row, ang_col, lambda a: -np.sin(a), lambda a: zeros)
    sin_hi = per_map(ang_row, ang_col, lambda a: zeros, np.sin)
    return tuple(jnp.asarray(np.concatenate([tab, tab], axis=-1).astype(np.float32))
                 for tab in (cos, sin_lo, sin_hi))


def _filter_embedding(seq):
    bands = (FILTER_EMB - 1) // 2
    t = np.linspace(0.0, 1.0, seq)[:, None]
    wpos = 2.0 * np.pi * np.arange(seq, dtype=np.float64)[:, None] / seq
    f = np.linspace(1e-4, bands - 1, bands)[None, :]
    emb = np.concatenate([t, np.cos(f * wpos), -np.sin(f * wpos)], axis=-1)
    emb = np.pad(emb, ((0, 0), (0, FILT_PAD - FILTER_EMB)))
    return jnp.asarray(emb.astype(np.float32)), jnp.asarray(t.astype(np.float32))


def _tiles(seq):
    short = seq <= 256
    return dict(tm=512, tq=128, attn_heads=N_HEADS if short else 4, ct=MXU_WIDTH,
                hyena_seqs=4 if short else 2)


def _pad_to(x, rows, cols):
    return jnp.pad(x, ((0, rows - x.shape[0]), (0, cols - x.shape[1])))


def kernel(x_prompt, x_sample, cache_k, cache_v, c, c_ctx, w_ada, b_ada, norm1_g, norm2_g, w_in,
           lam_q1, lam_k1, lam_q2, lam_k2, attn_subln_g, conv_w, conv_b, filt_w1, filt_b1,
           filt_w2, filt_b2, filt_w3, filt_freq, hy_bias, w_br_attn, w_br_hy, w_out, w_up,
           w_down, final_g):
    depth = w_in.shape[0]
    assert depth == 1, "single trunk layer"
    layer = 0
    lam_init = 0.8 - 0.6 * math.exp(-0.3 * layer)
    n_ctx, ctx_len, _ = x_prompt.shape
    n_lat, lat_len, _ = x_sample.shape
    past = cache_k.shape[2]

    mod = _mod_call(c_ctx[None, :], c, w_ada[layer], b_ada[layer][None, :])

    w_in_b = w_in[layer].astype(BF16)
    post_w_f32 = tuple(w[layer] for w in (w_br_attn, w_br_hy, w_out, w_up, w_down))
    g1, g2, gf = norm1_g[layer][None, :], norm2_g[layer][None, :], final_g[None, :]
    lam_vecs = jnp.stack([lam_q1[layer], lam_k1[layer], lam_q2[layer], lam_k2[layer]])
    subln_g = attn_subln_g[layer][None, :]

    filt_pack = jnp.concatenate(
        [_pad_to(filt_w1[layer], FILT_PAD, FILT_PAD), _pad_to(filt_w2[layer], FILT_PAD, FILT_PAD)]
        + [_pad_to(vec[layer][None, :], SUBLANES, FILT_PAD)
           for vec in (filt_b1, filt_b2, filt_freq)], axis=0)
    w3 = filt_w3[layer]
    deltas = jnp.asarray(np.linspace(math.log(DECAY_TARGET) / FAST_DECAY_PCT,
                                     math.log(DECAY_TARGET) / SLOW_DECAY_PCT,
                                     D_HYENA)[None, :].astype(np.float32))

    def project(x3, batch, seq, latent, cast=()):
        x = x3.reshape(batch * seq, D_MODEL)
        rope = _rope_tables(seq) if latent else None
        return x, _in_proj_call(x, mod, g1, w_in_b, rope, seq=seq, latent=latent,
                                tm=_tiles(seq)["tm"], cast=cast)

    def mix(x, outs, batch, seq, latent):
        tiles = _tiles(seq)
        if latent:
            q, kt, v, u = outs
            ckt = jnp.transpose(cache_k[:, layer], (0, 2, 3, 4, 1)).reshape(batch * K_COLS, past)
            cache = (ckt, cache_v[:, layer].reshape(batch * past * N_HEADS, V_DIM))
            kf = vf = None
        else:
            q, kt, v, kf, vf, u = outs
            cache = None
        o_attn = _attn_call(q, kt, v, cache, lam_vecs, subln_g, batch=batch, seq=seq,
                            lam_init=lam_init, tq=tiles["tq"], heads=tiles["attn_heads"])

        fwd_np, inv_np = _dft_tables(seq)
        fwd_b = jnp.asarray(fwd_np).astype(BF16)
        inv_b = jnp.asarray(inv_np).astype(BF16)
        emb, tcol = _filter_embedding(seq)
        filt = (emb, filt_pack, w3, tcol, deltas)
        o_hy = _hyena_call(u, conv_w[layer], conv_b[layer][None, :], hy_bias[layer], filt,
                           fwd_b, inv_b, batch=batch, seq=seq, ct=tiles["ct"],
                           n_chain=tiles["hyena_seqs"])

        return (x, o_attn, o_hy), kf, vf

    x_ctx, ctx_outs = project(x_prompt, n_ctx, ctx_len, False)
    x_lat, lat_outs = project(x_sample, n_lat, lat_len, True, cast=post_w_f32)
    lat_outs, post_w = lat_outs[:-len(post_w_f32)], lat_outs[-len(post_w_f32):]
    ctx_mixed, kf, vf = mix(x_ctx, ctx_outs, n_ctx, ctx_len, False)
    lat_mixed, _, _ = mix(x_lat, lat_outs, n_lat, lat_len, True)
    y_prompt, y_sample = _post_call(ctx_mixed, lat_mixed, mod, g1, g2, gf, w_in_b, post_w,
                                    lat_seq=lat_len, tm=_tiles(lat_len)["tm"])
    y_prompt = y_prompt.reshape(n_ctx, ctx_len, D_MODEL)
    y_sample = y_sample.reshape(n_lat, lat_len, D_MODEL)
    new_cache_k = jnp.transpose(kf.reshape(n_ctx, depth, N_HEADS, 2, HEAD_DIM, ctx_len),
                                (0, 1, 5, 2, 3, 4))
    new_cache_v = vf.reshape(n_ctx, depth, ctx_len, N_HEADS, V_DIM)
    return (y_prompt, y_sample, new_cache_k, new_cache_v)
```

```python
import functools
import math

import numpy as np
import jax
import jax.numpy as jnp
from jax import lax
from jax.experimental import pallas as pl
from jax.experimental.pallas import tpu as pltpu

D_MODEL = 1024
GRID_W = 64
N_HEADS = 8
HEAD_DIM = 64
V_DIM = 2 * HEAD_DIM
D_HYENA = D_MODEL // 2
HYENA_ORDER = 2
FILTER_EMB = 33
FILTER_HIDDEN = 64
D_FF = 4 * D_MODEL
ROPE_THETA = 10000.0
EPS = 1e-6
LOG2_E = math.log2(math.e)
DECAY_TARGET = 1e-2
FAST_DECAY_PCT = 0.3
SLOW_DECAY_PCT = 1.5

Q_COLS = N_HEADS * 2 * HEAD_DIM
K_COLS = N_HEADS * 2 * HEAD_DIM
V_COLS = N_HEADS * V_DIM
HY_COLS = 3 * D_HYENA
GATE_COLS = 2 * D_MODEL
Q_OFF = 0
K_OFF = Q_OFF + Q_COLS
V_OFF = K_OFF + K_COLS
HY_OFF = V_OFF + V_COLS
GATE_OFF = HY_OFF + HY_COLS
IN_COLS = GATE_OFF + GATE_COLS
GATE_WINDOW = math.gcd(GATE_OFF, D_MODEL)

LANES = 128
SUBLANES = 8
MXU_WIDTH = 256
MOD_CTX_ROWS = 8
FILT_PAD = 128
VMEM_LIMIT = 56 * 1024 * 1024

BF16 = jnp.bfloat16
F32 = jnp.float32


def _resident(shape, index=None):
    index = (0,) * len(shape) if index is None else index
    return pl.BlockSpec(shape, lambda *_: index, pipeline_mode=pl.Buffered(1))


def _params(*sem):
    return pltpu.CompilerParams(dimension_semantics=sem, vmem_limit_bytes=VMEM_LIMIT)


def _rms(x, g):
    return x * lax.rsqrt(jnp.mean(x * x, axis=-1, keepdims=True) + EPS) * g


def _mod_kernel(cctx_ref, c_ref, w_ref, b_ref, o_ref):
    c = jnp.concatenate([jnp.broadcast_to(cctx_ref[...], (MOD_CTX_ROWS, D_MODEL)), c_ref[...]],
                        axis=0)
    s = (c * jax.nn.sigmoid(c)).astype(BF16)
    chunk = jnp.dot(s, w_ref[...].astype(BF16), preferred_element_type=F32) + b_ref[...]
    for j in range(o_ref.shape[1]):
        @pl.when(pl.program_id(0) == j)
        def _():
            o_ref[:, j, :] = chunk


def _mod_call(c_ctx, c, w_ada, b_ada):
    n_chunks = w_ada.shape[1] // D_MODEL
    rows = MOD_CTX_ROWS + c.shape[0]
    return pl.pallas_call(
        _mod_kernel,
        grid=(n_chunks,),
        in_specs=[pl.BlockSpec(c_ctx.shape, lambda j: (0, 0)),
                  pl.BlockSpec(c.shape, lambda j: (0, 0)),
                  pl.BlockSpec((D_MODEL, D_MODEL), lambda j: (0, j)),
                  pl.BlockSpec((1, D_MODEL), lambda j: (0, j))],
        out_specs=pl.BlockSpec((rows, n_chunks, D_MODEL), lambda j: (0, 0, 0)),
        out_shape=jax.ShapeDtypeStruct((rows, n_chunks, D_MODEL), F32),
        compiler_params=_params("arbitrary"),
        name="mod",
    )(c_ctx, c, w_ada, b_ada)


ROPE_PAIR = HEAD_DIM // 4


def _rope(x, cos, sin_lo, sin_hi):
    return (x * cos + pltpu.roll(x, ROPE_PAIR, axis=1) * sin_hi
            + pltpu.roll(x, LANES - ROPE_PAIR, axis=1) * sin_lo)


def _in_proj_kernel(*refs, latent, n_cast):
    refs = list(refs)
    take = lambda n: [refs.pop(0) for _ in range(n)]
    x_ref, mod_ref, g_ref, w_ref = take(4)
    if latent:
        cos_ref, slo_ref, shi_ref = take(3)
    cast_src = take(n_cast)
    q_ref, kt_ref, v_ref = take(3)
    if not latent:
        ktf_ref, vf_ref = take(2)
    (u_ref,) = take(1)
    cast_dst = take(n_cast)
    tm = x_ref.shape[0]
    x = x_ref[...]
    h = _rms(x, g_ref[...]) * (1.0 + mod_ref[1:2, :]) + mod_ref[0:1, :]
    hb = h.astype(BF16)

    def proj(off, width):
        return jnp.dot(hb, w_ref[:, off:off + width], preferred_element_type=F32)

    q = proj(Q_OFF, Q_COLS) * (HEAD_DIM ** -0.5 * LOG2_E)
    k = proj(K_OFF, K_COLS)
    if latent:
        cos, slo, shi = cos_ref[...], slo_ref[...], shi_ref[...]
        k_heads = []
        for hd in range(N_HEADS):
            sl = slice(hd * LANES, (hd + 1) * LANES)
            q_ref[:, sl] = _rope(q[:, sl], cos, slo, shi).astype(BF16)
            k_heads.append(_rope(k[:, sl], cos, slo, shi))
        kt_ref[...] = jnp.concatenate(k_heads, axis=1).T.astype(BF16)
    else:
        q_ref[...] = q.astype(BF16)
        seq = kt_ref.shape[1]
        for s in range(tm // seq):
            kt = k[s * seq:(s + 1) * seq, :].T
            rows = slice(s * K_COLS, (s + 1) * K_COLS)
            ktf_ref[rows, :] = kt
            kt_ref[rows, :] = kt.astype(BF16)
    v = proj(V_OFF, V_COLS)
    v_ref[...] = v.astype(BF16)
    if not latent:
        for hd in range(N_HEADS):
            vf_ref[pl.ds(hd, tm, stride=N_HEADS), :] = v[:, hd * LANES:(hd + 1) * LANES]
    u_ref[...] = proj(HY_OFF, HY_COLS)
    for src, dst in zip(cast_src, cast_dst):
        dst[...] = src[...].astype(BF16)


def _in_proj_call(x, mod, g1, w_in, rope, *, seq, latent, tm, cast=()):
    t = x.shape[0]
    n_steps = t // tm
    per_seq = seq // tm if latent else 1

    def mod_idx(i):
        return (MOD_CTX_ROWS + i // per_seq if latent else 0, 0, 0)

    row = lambda i: (i, 0)
    in_specs = [pl.BlockSpec((tm, D_MODEL), row),
                pl.BlockSpec((None, 6, D_MODEL), mod_idx),
                _resident((1, D_MODEL)),
                _resident((D_MODEL, GATE_OFF))]
    args = [x, mod, g1, w_in]
    tok = lambda width, dtype: ((t, width), dtype, pl.BlockSpec((tm, width), row))
    if latent:
        in_specs += [pl.BlockSpec((tm, LANES), lambda i: (i % per_seq, 0))] * 3
        args += list(rope)
        kt_spec = pl.BlockSpec((K_COLS, tm), lambda i: (i // per_seq, i % per_seq))
    else:
        kt_spec = pl.BlockSpec((tm // seq * K_COLS, seq), row)
    kt_shape = (t // seq * K_COLS, seq)
    outs = [tok(Q_COLS, BF16), (kt_shape, BF16, kt_spec), tok(V_COLS, BF16)]
    if not latent:
        outs += [(kt_shape, F32, kt_spec),
                 ((t * N_HEADS, LANES), F32, pl.BlockSpec((tm * N_HEADS, LANES), row))]
    outs += [tok(HY_COLS, F32)]
    for w in cast:
        slab = pl.BlockSpec((w.shape[0] // n_steps, w.shape[1]), row)
        in_specs.append(slab)
        args.append(w)
        outs.append((w.shape, BF16, slab))
    return pl.pallas_call(
        functools.partial(_in_proj_kernel, latent=latent, n_cast=len(cast)),
        grid=(n_steps,),
        in_specs=in_specs,
        out_specs=[spec for _, _, spec in outs],
        out_shape=[jax.ShapeDtypeStruct(s, d) for s, d, _ in outs],
        compiler_params=_params("arbitrary"),
        name="in_proj_lat" if latent else "in_proj_ctx",
    )(*args)


def _attn_kernel(*refs, n_cache, lam_init, tq):
    if n_cache:
        (q_ref, kt_ref, v_ref, ckt_ref, cv_ref, lam_ref, g_ref, o_ref, kt_s, v2_s) = refs
    else:
        (q_ref, kt_ref, v_ref, lam_ref, g_ref, o_ref, v2_s) = refs
    n_heads, n_keys, _ = v2_s.shape
    head0 = pl.program_id(1) * n_heads
    lv = lam_ref[...]
    lam = (jnp.exp(jnp.sum(lv[0:1] * lv[1:2], axis=-1, keepdims=True))
           - jnp.exp(jnp.sum(lv[2:3] * lv[3:4], axis=-1, keepdims=True)) + lam_init)
    gain = g_ref[...] * (1.0 - lam_init)
    first_map = lax.broadcasted_iota(jnp.int32, (1, LANES), 1) < HEAD_DIM
    zero = jnp.zeros((tq, LANES), BF16)

    def operands(hd):
        cols = slice(hd * LANES, (hd + 1) * LANES)
        v2_s[hd, :, LANES:] = jnp.ones((n_keys, LANES), BF16)
        if n_cache:
            kt_s[cols, 0:n_cache] = ckt_ref[cols, :].astype(BF16)
            kt_s[cols, n_cache:] = kt_ref[cols, :]
            cache_rows = pl.ds(head0 + hd, n_cache, stride=N_HEADS)
            v2_s[hd, 0:n_cache, 0:LANES] = cv_ref[cache_rows, :].astype(BF16)
            v2_s[hd, n_cache:, 0:LANES] = v_ref[:, cols]
            return kt_s[cols, :], v2_s[hd]
        v2_s[hd, :, 0:LANES] = v_ref[:, cols]
        return kt_ref[cols, :], v2_s[hd]

    def scores(hd, i, kt):
        q = q_ref[i * tq:(i + 1) * tq, hd * LANES:(hd + 1) * LANES]
        q2 = jnp.concatenate([jnp.where(first_map, q, zero), jnp.where(first_map, zero, q)], axis=0)
        s = jnp.dot(q2, kt, preferred_element_type=F32)
        return s, jnp.max(s, axis=-1, keepdims=True)

    def finish(hd, i, v2, s, m):
        p = jnp.exp2(s - m).astype(BF16)
        pv = jnp.dot(p, v2, preferred_element_type=F32)
        pv = pv[:, :LANES] / pv[:, LANES:]
        o = pv[:tq] - lam * pv[tq:]
        o_ref[i * tq:(i + 1) * tq, hd * LANES:(hd + 1) * LANES] = (_rms(o, 1.0) * gain).astype(BF16)

    n_tiles = q_ref.shape[0] // tq
    pending = None
    for hd in range(n_heads):
        kt, v2 = operands(hd)
        for i in range(n_tiles):
            cur = (hd, i, v2) + scores(hd, i, kt)
            if pending is not None:
                finish(*pending)
            pending = cur
    finish(*pending)


def _attn_call(q, kt, v, cache, lam_vecs, subln_g, *, batch, seq, lam_init, tq, heads):
    const = lambda b, h: (0, 0)
    width = heads * LANES
    n_groups = N_HEADS // heads
    in_specs = [pl.BlockSpec((seq, width), lambda b, h: (b, h)),
                pl.BlockSpec((width, seq), lambda b, h: (b * n_groups + h, 0)),
                pl.BlockSpec((seq, width), lambda b, h: (b, h))]
    args = [q, kt, v]
    n_cache = 0
    if cache is not None:
        ckt, cv = cache
        n_cache = ckt.shape[1]
        in_specs += [pl.BlockSpec((width, n_cache), lambda b, h: (b * n_groups + h, 0)),
                     pl.BlockSpec((n_cache * N_HEADS, LANES), lambda b, h: (b, 0))]
        args += [ckt, cv]
    n_keys = n_cache + seq
    scratch = [pltpu.VMEM((heads, n_keys, 2 * LANES), BF16)]
    if n_cache:
        scratch = [pltpu.VMEM((width, n_keys), BF16)] + scratch
    in_specs += [pl.BlockSpec(lam_vecs.shape, const), pl.BlockSpec((1, LANES), const)]
    args += [lam_vecs, subln_g]
    return pl.pallas_call(
        functools.partial(_attn_kernel, n_cache=n_cache, lam_init=lam_init, tq=tq),
        grid=(batch, n_groups),
        in_specs=in_specs,
        out_specs=pl.BlockSpec((seq, width), lambda b, h: (b, h)),
        out_shape=jax.ShapeDtypeStruct((batch * seq, N_HEADS * V_DIM), BF16),
        scratch_shapes=scratch,
        compiler_params=_params("arbitrary", "arbitrary"),
        name="attn_lat" if cache is not None else "attn_ctx",
    )(*args)


def _dft_tables(seq):
    n = 2 * seq
    f = np.arange(seq, dtype=np.int64)[:, None]
    t = np.arange(seq, dtype=np.int64)[None, :]
    ang = 2.0 * np.pi * ((f * t) % n).astype(np.float64) / n
    alt = np.where(np.arange(seq) % 2 == 0, 1.0, -1.0)
    cos, msin = np.cos(ang), -np.sin(ang)
    msin[0, :] = alt
    fwd = np.concatenate([cos, msin], axis=0)
    wgt = np.where(f == 0, 1.0, 2.0) / n
    inv_c = (cos * wgt).T
    inv_s = (msin * wgt).T
    inv = np.concatenate([inv_c, inv_s], axis=1)
    return fwd.astype(np.float32), inv.astype(np.float32)


def _filter_hidden(emb_ref, w1_ref, b1_ref, w2_ref, b2_ref, fr_ref):
    hp = functools.partial(jnp.dot, preferred_element_type=F32, precision=lax.Precision.HIGHEST)
    fr = fr_ref[0:1, :]
    h = jnp.sin(fr * (hp(emb_ref[...], w1_ref[...]) + b1_ref[0:1, :]))
    return jnp.sin(fr * (hp(h, w2_ref[...]) + b2_ref[0:1, :]))


def _filter_planes(h, w3f_ref, w3b_ref, decay, dft_ref, out_ref):
    seq = h.shape[0]
    h = h[:, :w3f_ref.shape[0]]
    fwd = jnp.dot(h, w3f_ref[...].astype(BF16), preferred_element_type=F32) * decay
    bwd = jnp.dot(h, w3b_ref[...].astype(BF16), preferred_element_type=F32) * decay
    hsum, hdif = fwd + bwd, fwd - bwd
    re = jnp.dot(dft_ref[0:seq, :], hsum.astype(BF16), preferred_element_type=F32)
    im = jnp.dot(dft_ref[seq:, :], hdif.astype(BF16), preferred_element_type=F32)
    row = lax.broadcasted_iota(jnp.int32, (seq, 1), 0)
    alt = jnp.where(row % 2 == 0, 1.0, -1.0)
    nyq = jnp.sum(hsum * alt, axis=0, keepdims=True)
    out_ref[0] = re
    out_ref[1] = jnp.where(row == 0, 0.0, im)
    out_ref[2] = jnp.where(row == 0, nyq, re)


def _hyena_kernel(uv_ref, u1_ref, u2_ref, wv_ref, w1_ref, w2_ref, bv_ref, b1_ref, b2_ref, hb_ref,
                  emb_ref, fw1_ref, fb1_ref, fw2_ref, fb2_ref, fr_ref,
                  w3f0_ref, w3b0_ref, w3f1_ref, w3b1_ref, t_ref, dl_ref, fwd_s, inv_s,
                  o_ref, hid_s, hf_s):
    seq = emb_ref.shape[0]
    new_tile = pl.program_id(1) == 0

    @pl.when(jnp.logical_and(pl.program_id(0) == 0, new_tile))
    def _():
        hid_s[...] = _filter_hidden(emb_ref, fw1_ref, fb1_ref, fw2_ref, fb2_ref, fr_ref)

    @pl.when(new_tile)
    def _():
        h = hid_s[...].astype(BF16)
        decay = jnp.exp(-t_ref[...] * jnp.abs(dl_ref[...]))
        _filter_planes(h, w3f0_ref, w3b0_ref, decay, fwd_s, hf_s.at[0])
        _filter_planes(h, w3f1_ref, w3b1_ref, decay, fwd_s, hf_s.at[1])

    chains = [slice(c * seq, (c + 1) * seq) for c in range(uv_ref.shape[0] // seq)]
    row = lax.broadcasted_iota(jnp.int32, (seq, 1), 0)
    first, last = row == 0, row == seq - 1

    def short_conv(u_ref, w_ref, b_ref):
        outs = []
        for rows in chains:
            u = u_ref[rows, :]
            prev = jnp.where(first, 0.0, pltpu.roll(u, 1, axis=0))
            nxt = jnp.where(last, 0.0, pltpu.roll(u, seq - 1, axis=0))
            outs.append(prev * w_ref[0:1, :] + u * w_ref[1:2, :] + nxt * w_ref[2:3, :] + b_ref[...])
        return outs

    def long_conv(us, order):
        h_ref = hf_s.at[order]
        bias = hb_ref[order:order + 1, :]
        specs = [jnp.dot(fwd_s[...], u.astype(BF16), preferred_element_type=F32) for u in us]
        ys = []
        for spec in specs:
            ure, uim = spec[:seq], spec[seq:]
            yre = ure * h_ref[0] - uim * h_ref[1]
            yim = ure * h_ref[1] + uim * h_ref[2]
            ys.append(jnp.concatenate([yre, yim], axis=0).astype(BF16))
        return [jnp.dot(inv_s[...], y, preferred_element_type=F32) + u * bias
                for y, u in zip(ys, us)]

    v = short_conv(uv_ref, wv_ref, bv_ref)
    x1 = short_conv(u1_ref, w1_ref, b1_ref)
    z = [a * b for a, b in zip(x1, long_conv(v, 0))]
    x2 = short_conv(u2_ref, w2_ref, b2_ref)
    for rows, a, b in zip(chains, x2, long_conv(z, 1)):
        o_ref[rows, :] = (a * b).astype(BF16)


def _hyena_call(u, conv_w, conv_b, hy_bias, filt, fwd, inv, *, batch, seq, ct, n_chain):
    emb, filt_pack, w3, tcol, deltas = filt
    nct = D_HYENA // ct
    first_vec = 2 * FILT_PAD // SUBLANES
    mat = lambda n: pl.BlockSpec((FILT_PAD, FILT_PAD), lambda c, b: (n, 0))
    vec = lambda n: pl.BlockSpec((SUBLANES, FILT_PAD), lambda c, b: (first_vec + n, 0))
    pack_specs = [mat(0), vec(0), mat(1), vec(1), vec(2)]
    rows = n_chain * seq
    col = lambda part: (lambda c, b: (b, part * nct + c))
    wcol = lambda part: (lambda c, b: (0, part * nct + c))
    small = lambda shape: pl.BlockSpec(shape, lambda c, b: (0, 0))
    w3col = lambda direction, order: (lambda c, b: (0, (direction * HYENA_ORDER + order) * nct + c))
    in_specs = ([pl.BlockSpec((rows, ct), col(p)) for p in range(3)]
                + [pl.BlockSpec((3, ct), wcol(p)) for p in range(3)]
                + [pl.BlockSpec((1, ct), wcol(p)) for p in range(3)]
                + [pl.BlockSpec((HYENA_ORDER, ct), lambda c, b: (0, c)),
                   small((seq, FILT_PAD))] + pack_specs
                + [pl.BlockSpec((FILTER_HIDDEN, ct), w3col(direction, order))
                   for order in range(HYENA_ORDER) for direction in range(2)]
                + [small((seq, 1)), pl.BlockSpec((1, ct), lambda c, b: (0, c)),
                   _resident((2 * seq, seq)), _resident((seq, 2 * seq))])
    return pl.pallas_call(
        _hyena_kernel,
        grid=(nct, batch // n_chain),
        in_specs=in_specs,
        out_specs=pl.BlockSpec((rows, ct), lambda c, b: (b, c)),
        out_shape=jax.ShapeDtypeStruct((batch * seq, D_HYENA), BF16),
        scratch_shapes=[pltpu.VMEM((seq, FILT_PAD), F32),
                        pltpu.VMEM((HYENA_ORDER, 3, seq, ct), F32)],
        compiler_params=_params("arbitrary", "arbitrary"),
        name=f"hyena_{seq}",
    )(u, u, u, conv_w, conv_w, conv_w, conv_b, conv_b, conv_b, hy_bias,
      emb, *[filt_pack] * len(pack_specs), w3, w3, w3, w3, tcol, deltas, fwd, inv)


FF_CHUNK = D_MODEL
N_FF_CHUNKS = D_FF // FF_CHUNK


def _post_kernel(xc_ref, xl_ref, oac_ref, oal_ref, ohc_ref, ohl_ref, mod_ref, g1_ref, g2_ref, gf_ref,
                 wga0_ref, wga1_ref, wgh0_ref, wgh1_ref,
                 wa_ref, wh_ref, wo_ref, wu_ref, wd_ref, yc_ref, yl_ref, *, n_ctx_steps):
    d = functools.partial(jnp.dot, preferred_element_type=F32)
    is_ctx = pl.program_id(0) < n_ctx_steps
    pick = lambda c_ref, l_ref: jnp.where(is_ctx, c_ref[...], l_ref[...])
    x = pick(xc_ref, xl_ref)
    h1 = (_rms(x, g1_ref[...]) * (1.0 + mod_ref[1:2, :]) + mod_ref[0:1, :]).astype(BF16)

    def gate(*w_refs):
        return jnp.concatenate([jax.nn.sigmoid(d(h1, w[...])) for w in w_refs], axis=1)

    merged = (gate(wga0_ref, wga1_ref) * d(pick(oac_ref, oal_ref), wa_ref[...])
              + gate(wgh0_ref, wgh1_ref) * d(pick(ohc_ref, ohl_ref), wh_ref[...]))
    x = x + mod_ref[2:3, :] * d(merged.astype(BF16), wo_ref[...])
    h2 = (_rms(x, g2_ref[...]) * (1.0 + mod_ref[4:5, :]) + mod_ref[3:4, :]).astype(BF16)
    acc = jnp.zeros_like(x)
    for c in range(N_FF_CHUNKS):
        sl = slice(c * FF_CHUNK, (c + 1) * FF_CHUNK)
        up = jnp.maximum(d(h2, wu_ref[:, sl]), 0.0)
        acc = acc + d((up * up).astype(BF16), wd_ref[sl, :])
    y = _rms(x + mod_ref[5:6, :] * acc, gf_ref[...])

    @pl.when(is_ctx)
    def _():
        yc_ref[...] = y

    @pl.when(jnp.logical_not(is_ctx))
    def _():
        yl_ref[...] = y


def _post_call(ctx, lat, mod, g1, g2, gf, w_in, weights, *, lat_seq, tm):
    n_ctx, n_lat = ctx[0].shape[0] // tm, lat[0].shape[0] // tm
    per_seq = lat_seq // tm
    gate_windows = [_resident((D_MODEL, GATE_WINDOW), (0, GATE_OFF // GATE_WINDOW + n))
                    for n in range(GATE_COLS // GATE_WINDOW)]
    ctx_row = lambda i: (jnp.minimum(i, n_ctx - 1), 0)
    lat_row = lambda i: (jnp.maximum(i - n_ctx, 0), 0)
    mod_idx = lambda i: (jnp.where(i < n_ctx, 0, MOD_CTX_ROWS + (i - n_ctx) // per_seq), 0, 0)
    widths = (D_MODEL, N_HEADS * V_DIM, D_HYENA)
    in_specs = [pl.BlockSpec((tm, w), r) for w in widths for r in (ctx_row, lat_row)]
    args = [a for pair in zip(ctx, lat) for a in pair]
    return pl.pallas_call(
        functools.partial(_post_kernel, n_ctx_steps=n_ctx),
        grid=(n_ctx + n_lat,),
        in_specs=in_specs
                 + [pl.BlockSpec((None, 6, D_MODEL), mod_idx),
                    _resident((1, D_MODEL)), _resident((1, D_MODEL)), _resident((1, D_MODEL))]
                 + gate_windows + [_resident(w.shape) for w in weights],
        out_specs=[pl.BlockSpec((tm, D_MODEL), ctx_row), pl.BlockSpec((tm, D_MODEL), lat_row)],
        out_shape=[jax.ShapeDtypeStruct((n_ctx * tm, D_MODEL), F32),
                   jax.ShapeDtypeStruct((n_lat * tm, D_MODEL), F32)],
        compiler_params=_params("arbitrary"),
        name="post",
    )(*args, mod, g1, g2, gf, *[w_in] * len(gate_windows), *weights)


def _rope_tables(seq):
    half = HEAD_DIM // 2
    n = half // 2
    inv = ROPE_THETA ** (-np.arange(n, dtype=np.float64) / n)
    pos = np.arange(seq)
    ang_row = (pos // GRID_W).astype(np.float64)[:, None] * inv[None, :]
    ang_col = (pos % GRID_W).astype(np.float64)[:, None] * inv[None, :]
    zeros = np.zeros_like(ang_row)

    def per_map(a_row, a_col, lo, hi):
        return np.concatenate([lo(a_row), hi(a_row), lo(a_col), hi(a_col)], axis=-1)

    cos = per_map(ang_row, ang_col, np.cos, np.cos)
    sin_lo = per_map(ang_row, ang_col, lambda a: -np.sin(a), lambda a: zeros)
    sin_hi = per_map(ang_row, ang_col, lambda a: zeros, np.sin)
    return tuple(jnp.asarray(np.concatenate([tab, tab], axis=-1).astype(np.float32))
                 for tab in (cos, sin_lo, sin_hi))


def _filter_embedding(seq):
    bands = (FILTER_EMB - 1) // 2
    t = np.linspace(0.0, 1.0, seq)[:, None]
    wpos = 2.0 * np.pi * np.arange(seq, dtype=np.float64)[:, None] / seq
    f = np.linspace(1e-4, bands - 1, bands)[None, :]
    emb = np.concatenate([t, np.cos(f * wpos), -np.sin(f * wpos)], axis=-1)
    emb = np.pad(emb, ((0, 0), (0, FILT_PAD - FILTER_EMB)))
    return jnp.asarray(emb.astype(np.float32)), jnp.asarray(t.astype(np.float32))


def _tiles(seq):
    short = seq <= 256
    return dict(tm=512, tq=128, attn_heads=N_HEADS if short else 4, ct=MXU_WIDTH,
                hyena_seqs=4 if short else 2)


def _pad_to(x, rows, cols):
    return jnp.pad(x, ((0, rows - x.shape[0]), (0, cols - x.shape[1])))


def kernel(x_prompt, x_sample, cache_k, cache_v, c, c_ctx, w_ada, b_ada, norm1_g, norm2_g, w_in,
           lam_q1, lam_k1, lam_q2, lam_k2, attn_subln_g, conv_w, conv_b, filt_w1, filt_b1,
           filt_w2, filt_b2, filt_w3, filt_freq, hy_bias, w_br_attn, w_br_hy, w_out, w_up,
           w_down, final_g):
    depth = w_in.shape[0]
    assert depth == 1, "single trunk layer"
    layer = 0
    lam_init = 0.8 - 0.6 * math.exp(-0.3 * layer)
    n_ctx, ctx_len, _ = x_prompt.shape
    n_lat, lat_len, _ = x_sample.shape
    past = cache_k.shape[2]
    assert w_in.shape[1:] == (D_MODEL, IN_COLS) and x_prompt.shape[2] == x_sample.shape[2] == D_MODEL
    assert cache_k.shape == (n_lat, depth, past, N_HEADS, 2, HEAD_DIM)
    assert cache_v.shape == (n_lat, depth, past, N_HEADS, V_DIM)
    assert lat_len % GRID_W == 0 and c.shape == (n_lat, D_MODEL)

    mod = _mod_call(c_ctx[None, :], c, w_ada[layer], b_ada[layer][None, :])

    w_in_b = w_in[layer].astype(BF16)
    post_w_f32 = tuple(w[layer] for w in (w_br_attn, w_br_hy, w_out, w_up, w_down))
    g1, g2, gf = norm1_g[layer][None, :], norm2_g[layer][None, :], final_g[None, :]
    lam_vecs = jnp.stack([lam_q1[layer], lam_k1[layer], lam_q2[layer], lam_k2[layer]])
    subln_g = attn_subln_g[layer][None, :]

    filt_pack = jnp.concatenate(
        [_pad_to(filt_w1[layer], FILT_PAD, FILT_PAD), _pad_to(filt_w2[layer], FILT_PAD, FILT_PAD)]
        + [_pad_to(vec[layer][None, :], SUBLANES, FILT_PAD)
           for vec in (filt_b1, filt_b2, filt_freq)], axis=0)
    w3 = filt_w3[layer]
    deltas = jnp.asarray(np.linspace(math.log(DECAY_TARGET) / FAST_DECAY_PCT,
                                     math.log(DECAY_TARGET) / SLOW_DECAY_PCT,
                                     D_HYENA)[None, :].astype(np.float32))

    def project(x3, batch, seq, latent, cast=()):
        x = x3.reshape(batch * seq, D_MODEL)
        rope = _rope_tables(seq) if latent else None
        return x, _in_proj_call(x, mod, g1, w_in_b, rope, seq=seq, latent=latent,
                                tm=_tiles(seq)["tm"], cast=cast)

    def mix(x, outs, batch, seq, latent):
        tiles = _tiles(seq)
        if latent:
            q, kt, v, u = outs
            ckt = jnp.transpose(cache_k[:, layer], (0, 2, 3, 4, 1)).reshape(batch * K_COLS, past)
            cache = (ckt, cache_v[:, layer].reshape(batch * past * N_HEADS, V_DIM))
            kf = vf = None
        else:
            q, kt, v, kf, vf, u = outs
            cache = None
        o_attn = _attn_call(q, kt, v, cache, lam_vecs, subln_g, batch=batch, seq=seq,
                            lam_init=lam_init, tq=tiles["tq"], heads=tiles["attn_heads"])

        fwd_np, inv_np = _dft_tables(seq)
        fwd_b = jnp.asarray(fwd_np).astype(BF16)
        inv_b = jnp.asarray(inv_np).astype(BF16)
        emb, tcol = _filter_embedding(seq)
        filt = (emb, filt_pack, w3, tcol, deltas)
        o_hy = _hyena_call(u, conv_w[layer], conv_b[layer][None, :], hy_bias[layer], filt,
                           fwd_b, inv_b, batch=batch, seq=seq, ct=tiles["ct"],
                           n_chain=tiles["hyena_seqs"])

        return (x, o_attn, o_hy), kf, vf

    x_ctx, ctx_outs = project(x_prompt, n_ctx, ctx_len, False)
    x_lat, lat_outs = project(x_sample, n_lat, lat_len, True, cast=post_w_f32)
    lat_outs, post_w = lat_outs[:-len(post_w_f32)], lat_outs[-len(post_w_f32):]
    ctx_mixed, kf, vf = mix(x_ctx, ctx_outs, n_ctx, ctx_len, False)
    lat_mixed, _, _ = mix(x_lat, lat_outs, n_lat, lat_len, True)
    y_prompt, y_sample = _post_call(ctx_mixed, lat_mixed, mod, g1, g2, gf, w_in_b, post_w,
                                    lat_seq=lat_len, tm=_tiles(lat_len)["tm"])
    y_prompt = y_prompt.reshape(n_ctx, ctx_len, D_MODEL)
    y_sample = y_sample.reshape(n_lat, lat_len, D_MODEL)
    new_cache_k = jnp.transpose(kf.reshape(n_ctx, depth, N_HEADS, 2, HEAD_DIM, ctx_len),
                                (0, 1, 5, 2, 3, 4))
    new_cache_v = vf.reshape(n_ctx, depth, ctx_len, N_HEADS, V_DIM)
    return (y_prompt, y_sample, new_cache_k, new_cache_v)
```

```python
import functools
import math

import numpy as np
import jax
import jax.numpy as jnp
from jax import lax
from jax.experimental import pallas as pl
from jax.experimental.pallas import tpu as pltpu

D_MODEL = 1024
GRID_W = 64
N_HEADS = 8
HEAD_DIM = 64
V_DIM = 2 * HEAD_DIM
D_HYENA = D_MODEL // 2
HYENA_ORDER = 2
FILTER_EMB = 33
FILTER_HIDDEN = 64
D_FF = 4 * D_MODEL
ROPE_THETA = 10000.0
EPS = 1e-6
LOG2_E = math.log2(math.e)
DECAY_TARGET = 1e-2
FAST_DECAY_PCT = 0.3
SLOW_DECAY_PCT = 1.5

Q_COLS = N_HEADS * 2 * HEAD_DIM
K_COLS = N_HEADS * 2 * HEAD_DIM
V_COLS = N_HEADS * V_DIM
HY_COLS = 3 * D_HYENA
GATE_COLS = 2 * D_MODEL
Q_OFF = 0
K_OFF = Q_OFF + Q_COLS
V_OFF = K_OFF + K_COLS
HY_OFF = V_OFF + V_COLS
GATE_OFF = HY_OFF + HY_COLS
IN_COLS = GATE_OFF + GATE_COLS
GATE_WINDOW = math.gcd(GATE_OFF, D_MODEL)

LANES = 128
SUBLANES = 8
MXU_WIDTH = 256
MOD_CTX_ROWS = 8
FILT_PAD = 128
VMEM_LIMIT = 56 * 1024 * 1024

BF16 = jnp.bfloat16
F32 = jnp.float32


def _resident(shape, index=None):
    index = (0,) * len(shape) if index is None else index
    return pl.BlockSpec(shape, lambda *_: index, pipeline_mode=pl.Buffered(1))


def _params(*sem):
    return pltpu.CompilerParams(dimension_semantics=sem, vmem_limit_bytes=VMEM_LIMIT)


def _rms(x, g):
    return x * lax.rsqrt(jnp.mean(x * x, axis=-1, keepdims=True) + EPS) * g


def _mod_kernel(cctx_ref, c_ref, w_ref, b_ref, o_ref):
    c = jnp.concatenate([jnp.broadcast_to(cctx_ref[...], (MOD_CTX_ROWS, D_MODEL)), c_ref[...]],
                        axis=0)
    s = (c * jax.nn.sigmoid(c)).astype(BF16)
    chunk = jnp.dot(s, w_ref[...].astype(BF16), preferred_element_type=F32) + b_ref[...]
    for j in range(o_ref.shape[1]):
        @pl.when(pl.program_id(0) == j)
        def _():
            o_ref[:, j, :] = chunk


def _mod_call(c_ctx, c, w_ada, b_ada):
    n_chunks = w_ada.shape[1] // D_MODEL
    rows = MOD_CTX_ROWS + c.shape[0]
    return pl.pallas_call(
        _mod_kernel,
        grid=(n_chunks,),
        in_specs=[pl.BlockSpec(c_ctx.shape, lambda j: (0, 0)),
                  pl.BlockSpec(c.shape, lambda j: (0, 0)),
                  pl.BlockSpec((D_MODEL, D_MODEL), lambda j: (0, j)),
                  pl.BlockSpec((1, D_MODEL), lambda j: (0, j))],
        out_specs=pl.BlockSpec((rows, n_chunks, D_MODEL), lambda j: (0, 0, 0)),
        out_shape=jax.ShapeDtypeStruct((rows, n_chunks, D_MODEL), F32),
        compiler_params=_params("arbitrary"),
        name="mod",
    )(c_ctx, c, w_ada, b_ada)


ROPE_PAIR = HEAD_DIM // 4


def _rope(x, cos, sin_lo, sin_hi):
    return (x * cos + pltpu.roll(x, ROPE_PAIR, axis=1) * sin_hi
            + pltpu.roll(x, LANES - ROPE_PAIR, axis=1) * sin_lo)


def _in_proj_kernel(*refs, latent, n_cast):
    refs = list(refs)
    take = lambda n: [refs.pop(0) for _ in range(n)]
    x_ref, mod_ref, g_ref, w_ref = take(4)
    if latent:
        cos_ref, slo_ref, shi_ref = take(3)
    cast_src = take(n_cast)
    q_ref, kt_ref, v_ref = take(3)
    if not latent:
        ktf_ref, vf_ref = take(2)
    (u_ref,) = take(1)
    cast_dst = take(n_cast)
    tm = x_ref.shape[0]
    x = x_ref[...]
    h = _rms(x, g_ref[...]) * (1.0 + mod_ref[1:2, :]) + mod_ref[0:1, :]
    hb = h.astype(BF16)

    def proj(off, width):
        return jnp.dot(hb, w_ref[:, off:off + width], preferred_element_type=F32)

    q = proj(Q_OFF, Q_COLS) * (HEAD_DIM ** -0.5 * LOG2_E)
    k = proj(K_OFF, K_COLS)
    if latent:
        cos, slo, shi = cos_ref[...], slo_ref[...], shi_ref[...]
        k_heads = []
        for hd in range(N_HEADS):
            sl = slice(hd * LANES, (hd + 1) * LANES)
            q_ref[:, sl] = _rope(q[:, sl], cos, slo, shi).astype(BF16)
            k_heads.append(_rope(k[:, sl], cos, slo, shi))
        kt_ref[...] = jnp.concatenate(k_heads, axis=1).T.astype(BF16)
    else:
        q_ref[...] = q.astype(BF16)
        seq = kt_ref.shape[1]
        for s in range(tm // seq):
            kt = k[s * seq:(s + 1) * seq, :].T
            rows = slice(s * K_COLS, (s + 1) * K_COLS)
            ktf_ref[rows, :] = kt
            kt_ref[rows, :] = kt.astype(BF16)
    v = proj(V_OFF, V_COLS)
    v_ref[...] = v.astype(BF16)
    if not latent:
        for hd in range(N_HEADS):
            vf_ref[pl.ds(hd, tm, stride=N_HEADS), :] = v[:, hd * LANES:(hd + 1) * LANES]
    u_ref[...] = proj(HY_OFF, HY_COLS).astype(BF16)
    for src, dst in zip(cast_src, cast_dst):
        dst[...] = src[...].astype(BF16)


def _in_proj_call(x, mod, g1, w_in, rope, *, seq, latent, tm, cast=()):
    t = x.shape[0]
    n_steps = t // tm
    per_seq = seq // tm if latent else 1

    def mod_idx(i):
        return (MOD_CTX_ROWS + i // per_seq if latent else 0, 0, 0)

    row = lambda i: (i, 0)
    in_specs = [pl.BlockSpec((tm, D_MODEL), row),
                pl.BlockSpec((None, 6, D_MODEL), mod_idx),
                _resident((1, D_MODEL)),
                _resident((D_MODEL, GATE_OFF))]
    args = [x, mod, g1, w_in]
    tok = lambda width, dtype: ((t, width), dtype, pl.BlockSpec((tm, width), row))
    if latent:
        in_specs += [pl.BlockSpec((tm, LANES), lambda i: (i % per_seq, 0))] * 3
        args += list(rope)
        kt_spec = pl.BlockSpec((K_COLS, tm), lambda i: (i // per_seq, i % per_seq))
    else:
        kt_spec = pl.BlockSpec((tm // seq * K_COLS, seq), row)
    kt_shape = (t // seq * K_COLS, seq)
    outs = [tok(Q_COLS, BF16), (kt_shape, BF16, kt_spec), tok(V_COLS, BF16)]
    if not latent:
        outs += [(kt_shape, F32, kt_spec),
                 ((t * N_HEADS, LANES), F32, pl.BlockSpec((tm * N_HEADS, LANES), row))]
    outs += [tok(HY_COLS, BF16)]
    for w in cast:
        slab = pl.BlockSpec((w.shape[0] // n_steps, w.shape[1]), row)
        in_specs.append(slab)
        args.append(w)
        outs.append((w.shape, BF16, slab))
    return pl.pallas_call(
        functools.partial(_in_proj_kernel, latent=latent, n_cast=len(cast)),
        grid=(n_steps,),
        in_specs=in_specs,
        out_specs=[spec for _, _, spec in outs],
        out_shape=[jax.ShapeDtypeStruct(s, d) for s, d, _ in outs],
        compiler_params=_params("arbitrary"),
        name="in_proj_lat" if latent else "in_proj_ctx",
    )(*args)


def _attn_kernel(*refs, n_cache, lam_init, tq):
    if n_cache:
        (q_ref, kt_ref, v_ref, ckt_ref, cv_ref, lam_ref, g_ref, o_ref, kt_s, v2_s) = refs
    else:
        (q_ref, kt_ref, v_ref, lam_ref, g_ref, o_ref, v2_s) = refs
    n_heads, n_keys, _ = v2_s.shape
    head0 = pl.program_id(1) * n_heads
    lv = lam_ref[...]
    lam = (jnp.exp(jnp.sum(lv[0:1] * lv[1:2], axis=-1, keepdims=True))
           - jnp.exp(jnp.sum(lv[2:3] * lv[3:4], axis=-1, keepdims=True)) + lam_init)
    gain = g_ref[...] * (1.0 - lam_init)
    first_map = lax.broadcasted_iota(jnp.int32, (1, LANES), 1) < HEAD_DIM
    zero = jnp.zeros((tq, LANES), BF16)

    def operands(hd):
        cols = slice(hd * LANES, (hd + 1) * LANES)
        v2_s[hd, :, LANES:] = jnp.ones((n_keys, LANES), BF16)
        if n_cache:
            kt_s[cols, 0:n_cache] = ckt_ref[cols, :].astype(BF16)
            kt_s[cols, n_cache:] = kt_ref[cols, :]
            cache_rows = pl.ds(head0 + hd, n_cache, stride=N_HEADS)
            v2_s[hd, 0:n_cache, 0:LANES] = cv_ref[cache_rows, :].astype(BF16)
            v2_s[hd, n_cache:, 0:LANES] = v_ref[:, cols]
            return kt_s[cols, :], v2_s[hd]
        v2_s[hd, :, 0:LANES] = v_ref[:, cols]
        return kt_ref[cols, :], v2_s[hd]

    def scores(hd, i, kt):
        q = q_ref[i * tq:(i + 1) * tq, hd * LANES:(hd + 1) * LANES]
        q2 = jnp.concatenate([jnp.where(first_map, q, zero), jnp.where(first_map, zero, q)], axis=0)
        s = jnp.dot(q2, kt, preferred_element_type=F32)
        return s, jnp.max(s, axis=-1, keepdims=True)

    def finish(hd, i, v2, s, m):
        p = jnp.exp2(s - m).astype(BF16)
        pv = jnp.dot(p, v2, preferred_element_type=F32)
        pv = pv[:, :LANES] / pv[:, LANES:]
        o = pv[:tq] - lam * pv[tq:]
        o_ref[i * tq:(i + 1) * tq, hd * LANES:(hd + 1) * LANES] = (_rms(o, 1.0) * gain).astype(BF16)

    n_tiles = q_ref.shape[0] // tq
    pending = None
    for hd in range(n_heads):
        kt, v2 = operands(hd)
        for i in range(n_tiles):
            cur = (hd, i, v2) + scores(hd, i, kt)
            if pending is not None:
                finish(*pending)
            pending = cur
    finish(*pending)


def _attn_call(q, kt, v, cache, lam_vecs, subln_g, *, batch, seq, lam_init, tq, heads):
    const = lambda b, h: (0, 0)
    width = heads * LANES
    n_groups = N_HEADS // heads
    in_specs = [pl.BlockSpec((seq, width), lambda b, h: (b, h)),
                pl.BlockSpec((width, seq), lambda b, h: (b * n_groups + h, 0)),
                pl.BlockSpec((seq, width), lambda b, h: (b, h))]
    args = [q, kt, v]
    n_cache = 0
    if cache is not None:
        ckt, cv = cache
        n_cache = ckt.shape[1]
        in_specs += [pl.BlockSpec((width, n_cache), lambda b, h: (b * n_groups + h, 0)),
                     pl.BlockSpec((n_cache * N_HEADS, LANES), lambda b, h: (b, 0))]
        args += [ckt, cv]
    n_keys = n_cache + seq
    scratch = [pltpu.VMEM((heads, n_keys, 2 * LANES), BF16)]
    if n_cache:
        scratch = [pltpu.VMEM((width, n_keys), BF16)] + scratch
    in_specs += [pl.BlockSpec(lam_vecs.shape, const), pl.BlockSpec((1, LANES), const)]
    args += [lam_vecs, subln_g]
    return pl.pallas_call(
        functools.partial(_attn_kernel, n_cache=n_cache, lam_init=lam_init, tq=tq),
        grid=(batch, n_groups),
        in_specs=in_specs,
        out_specs=pl.BlockSpec((seq, width), lambda b, h: (b, h)),
        out_shape=jax.ShapeDtypeStruct((batch * seq, N_HEADS * V_DIM), BF16),
        scratch_shapes=scratch,
        compiler_params=_params("arbitrary", "arbitrary"),
        name="attn_lat" if cache is not None else "attn_ctx",
    )(*args)


def _dft_tables(seq):
    n = 2 * seq
    f = np.arange(seq, dtype=np.int64)[:, None]
    t = np.arange(seq, dtype=np.int64)[None, :]
    ang = 2.0 * np.pi * ((f * t) % n).astype(np.float64) / n
    alt = np.where(np.arange(seq) % 2 == 0, 1.0, -1.0)
    cos, msin = np.cos(ang), -np.sin(ang)
    msin[0, :] = alt
    fwd = np.concatenate([cos, msin], axis=0)
    wgt = np.where(f == 0, 1.0, 2.0) / n
    inv_c = (cos * wgt).T
    inv_s = (msin * wgt).T
    inv = np.concatenate([inv_c, inv_s], axis=1)
    return fwd.astype(np.float32), inv.astype(np.float32)


def _filter_hidden(emb_ref, w1_ref, b1_ref, w2_ref, b2_ref, fr_ref):
    hp = functools.partial(jnp.dot, preferred_element_type=F32, precision=lax.Precision.HIGHEST)
    fr = fr_ref[0:1, :]
    h = jnp.sin(fr * (hp(emb_ref[...], w1_ref[...]) + b1_ref[0:1, :]))
    return jnp.sin(fr * (hp(h, w2_ref[...]) + b2_ref[0:1, :]))


def _filter_planes(h, w3f_ref, w3b_ref, decay, dft_ref, out_ref):
    seq = h.shape[0]
    h = h[:, :w3f_ref.shape[0]]
    fwd = jnp.dot(h, w3f_ref[...].astype(BF16), preferred_element_type=F32) * decay
    bwd = jnp.dot(h, w3b_ref[...].astype(BF16), preferred_element_type=F32) * decay
    hsum, hdif = fwd + bwd, fwd - bwd
    re = jnp.dot(dft_ref[0:seq, :], hsum.astype(BF16), preferred_element_type=F32)
    im = jnp.dot(dft_ref[seq:, :], hdif.astype(BF16), preferred_element_type=F32)
    row = lax.broadcasted_iota(jnp.int32, (seq, 1), 0)
    alt = jnp.where(row % 2 == 0, 1.0, -1.0)
    nyq = jnp.sum(hsum * alt, axis=0, keepdims=True)
    out_ref[0] = re
    out_ref[1] = jnp.where(row == 0, 0.0, im)
    out_ref[2] = jnp.where(row == 0, nyq, re)


def _hyena_kernel(uv_ref, u1_ref, u2_ref, wv_ref, w1_ref, w2_ref, bv_ref, b1_ref, b2_ref, hb_ref,
                  emb_ref, fw1_ref, fb1_ref, fw2_ref, fb2_ref, fr_ref,
                  w3f0_ref, w3b0_ref, w3f1_ref, w3b1_ref, t_ref, dl_ref, fwd_s, inv_s,
                  o_ref, hid_s, hf_s):
    seq = emb_ref.shape[0]
    new_tile = pl.program_id(1) == 0

    @pl.when(jnp.logical_and(pl.program_id(0) == 0, new_tile))
    def _():
        hid_s[...] = _filter_hidden(emb_ref, fw1_ref, fb1_ref, fw2_ref, fb2_ref, fr_ref)

    @pl.when(new_tile)
    def _():
        h = hid_s[...].astype(BF16)
        decay = jnp.exp(-t_ref[...] * jnp.abs(dl_ref[...]))
        _filter_planes(h, w3f0_ref, w3b0_ref, decay, fwd_s, hf_s.at[0])
        _filter_planes(h, w3f1_ref, w3b1_ref, decay, fwd_s, hf_s.at[1])

    chains = [slice(c * seq, (c + 1) * seq) for c in range(uv_ref.shape[0] // seq)]
    row = lax.broadcasted_iota(jnp.int32, (seq, 1), 0)
    first, last = row == 0, row == seq - 1

    def short_conv(u_ref, w_ref, b_ref):
        outs = []
        for rows in chains:
            u = u_ref[rows, :].astype(F32)
            prev = jnp.where(first, 0.0, pltpu.roll(u, 1, axis=0))
            nxt = jnp.where(last, 0.0, pltpu.roll(u, seq - 1, axis=0))
            outs.append(prev * w_ref[0:1, :] + u * w_ref[1:2, :] + nxt * w_ref[2:3, :] + b_ref[...])
        return outs

    def long_conv(us, order):
        h_ref = hf_s.at[order]
        bias = hb_ref[order:order + 1, :]
        specs = [jnp.dot(fwd_s[...], u.astype(BF16), preferred_element_type=F32) for u in us]
        ys = []
        for spec in specs:
            ure, uim = spec[:seq], spec[seq:]
            yre = ure * h_ref[0] - uim * h_ref[1]
            yim = ure * h_ref[1] + uim * h_ref[2]
            ys.append(jnp.concatenate([yre, yim], axis=0).astype(BF16))
        return [jnp.dot(inv_s[...], y, preferred_element_type=F32) + u * bias
                for y, u in zip(ys, us)]

    v = short_conv(uv_ref, wv_ref, bv_ref)
    x1 = short_conv(u1_ref, w1_ref, b1_ref)
    z = [a * b for a, b in zip(x1, long_conv(v, 0))]
    x2 = short_conv(u2_ref, w2_ref, b2_ref)
    for rows, a, b in zip(chains, x2, long_conv(z, 1)):
        o_ref[rows, :] = (a * b).astype(BF16)


def _hyena_call(u, conv_w, conv_b, hy_bias, filt, fwd, inv, *, batch, seq, ct, n_chain):
    emb, filt_pack, w3, tcol, deltas = filt
    nct = D_HYENA // ct
    first_vec = 2 * FILT_PAD // SUBLANES
    mat = lambda n: pl.BlockSpec((FILT_PAD, FILT_PAD), lambda c, b: (n, 0))
    vec = lambda n: pl.BlockSpec((SUBLANES, FILT_PAD), lambda c, b: (first_vec + n, 0))
    pack_specs = [mat(0), vec(0), mat(1), vec(1), vec(2)]
    rows = n_chain * seq
    col = lambda part: (lambda c, b: (b, part * nct + c))
    wcol = lambda part: (lambda c, b: (0, part * nct + c))
    small = lambda shape: pl.BlockSpec(shape, lambda c, b: (0, 0))
    w3col = lambda direction, order: (lambda c, b: (0, (direction * HYENA_ORDER + order) * nct + c))
    in_specs = ([pl.BlockSpec((rows, ct), col(p)) for p in range(3)]
                + [pl.BlockSpec((3, ct), wcol(p)) for p in range(3)]
                + [pl.BlockSpec((1, ct), wcol(p)) for p in range(3)]
                + [pl.BlockSpec((HYENA_ORDER, ct), lambda c, b: (0, c)),
                   small((seq, FILT_PAD))] + pack_specs
                + [pl.BlockSpec((FILTER_HIDDEN, ct), w3col(direction, order))
                   for order in range(HYENA_ORDER) for direction in range(2)]
                + [small((seq, 1)), pl.BlockSpec((1, ct), lambda c, b: (0, c)),
                   _resident((2 * seq, seq)), _resident((seq, 2 * seq))])
    return pl.pallas_call(
        _hyena_kernel,
        grid=(nct, batch // n_chain),
        in_specs=in_specs,
        out_specs=pl.BlockSpec((rows, ct), lambda c, b: (b, c)),
        out_shape=jax.ShapeDtypeStruct((batch * seq, D_HYENA), BF16),
        scratch_shapes=[pltpu.VMEM((seq, FILT_PAD), F32),
                        pltpu.VMEM((HYENA_ORDER, 3, seq, ct), F32)],
        compiler_params=_params("arbitrary", "arbitrary"),
        name=f"hyena_{seq}",
    )(u, u, u, conv_w, conv_w, conv_w, conv_b, conv_b, conv_b, hy_bias,
      emb, *[filt_pack] * len(pack_specs), w3, w3, w3, w3, tcol, deltas, fwd, inv)


FF_CHUNK = D_MODEL
N_FF_CHUNKS = D_FF // FF_CHUNK


def _post_kernel(xc_ref, xl_ref, oac_ref, oal_ref, ohc_ref, ohl_ref, mod_ref, g1_ref, g2_ref, gf_ref,
                 wga0_ref, wga1_ref, wgh0_ref, wgh1_ref,
                 wa_ref, wh_ref, wo_ref, wu_ref, wd_ref, yc_ref, yl_ref, *, n_ctx_steps):
    d = functools.partial(jnp.dot, preferred_element_type=F32)
    is_ctx = pl.program_id(0) < n_ctx_steps
    pick = lambda c_ref, l_ref: jnp.where(is_ctx, c_ref[...], l_ref[...])
    x = pick(xc_ref, xl_ref)
    h1 = (_rms(x, g1_ref[...]) * (1.0 + mod_ref[1:2, :]) + mod_ref[0:1, :]).astype(BF16)

    def gate(*w_refs):
        return jnp.concatenate([jax.nn.sigmoid(d(h1, w[...])) for w in w_refs], axis=1)

    merged = (gate(wga0_ref, wga1_ref) * d(pick(oac_ref, oal_ref), wa_ref[...])
              + gate(wgh0_ref, wgh1_ref) * d(pick(ohc_ref, ohl_ref), wh_ref[...]))
    x = x + mod_ref[2:3, :] * d(merged.astype(BF16), wo_ref[...])
    h2 = (_rms(x, g2_ref[...]) * (1.0 + mod_ref[4:5, :]) + mod_ref[3:4, :]).astype(BF16)
    acc = jnp.zeros_like(x)
    for c in range(N_FF_CHUNKS):
        sl = slice(c * FF_CHUNK, (c + 1) * FF_CHUNK)
        up = jnp.maximum(d(h2, wu_ref[:, sl]), 0.0)
        acc = acc + d((up * up).astype(BF16), wd_ref[sl, :])
    y = _rms(x + mod_ref[5:6, :] * acc, gf_ref[...])

    @pl.when(is_ctx)
    def _():
        yc_ref[...] = y

    @pl.when(jnp.logical_not(is_ctx))
    def _():
        yl_ref[...] = y


def _post_call(ctx, lat, mod, g1, g2, gf, w_in, weights, *, lat_seq, tm):
    n_ctx, n_lat = ctx[0].shape[0] // tm, lat[0].shape[0] // tm
    per_seq = lat_seq // tm
    gate_windows = [_resident((D_MODEL, GATE_WINDOW), (0, GATE_OFF // GATE_WINDOW + n))
                    for n in range(GATE_COLS // GATE_WINDOW)]
    ctx_row = lambda i: (jnp.minimum(i, n_ctx - 1), 0)
    lat_row = lambda i: (jnp.maximum(i - n_ctx, 0), 0)
    mod_idx = lambda i: (jnp.where(i < n_ctx, 0, MOD_CTX_ROWS + (i - n_ctx) // per_seq), 0, 0)
    widths = (D_MODEL, N_HEADS * V_DIM, D_HYENA)
    in_specs = [pl.BlockSpec((tm, w), r) for w in widths for r in (ctx_row, lat_row)]
    args = [a for pair in zip(ctx, lat) for a in pair]
    return pl.pallas_call(
        functools.partial(_post_kernel, n_ctx_steps=n_ctx),
        grid=(n_ctx + n_lat,),
        in_specs=in_specs
                 + [pl.BlockSpec((None, 6, D_MODEL), mod_idx),
                    _resident((1, D_MODEL)), _resident((1, D_MODEL)), _resident((1, D_MODEL))]
                 + gate_windows + [_resident(w.shape) for w in weights],
        out_specs=[pl.BlockSpec((tm, D_MODEL), ctx_row), pl.BlockSpec((tm, D_MODEL), lat_row)],
        out_shape=[jax.ShapeDtypeStruct((n_ctx * tm, D_MODEL), F32),
                   jax.ShapeDtypeStruct((n_lat * tm, D_MODEL), F32)],
        compiler_params=_params("arbitrary"),
        name="post",
    )(*args, mod, g1, g2, gf, *[w_in] * len(gate_windows), *weights)


def _rope_tables(seq):
    half = HEAD_DIM // 2
    n = half // 2
    inv = ROPE_THETA ** (-np.arange(n, dtype=np.float64) / n)
    pos = np.arange(seq)
    ang_row = (pos // GRID_W).astype(np.float64)[:, None] * inv[None, :]
    ang_col = (pos % GRID_W).astype(np.float64)[:, None] * inv[None, :]
    zeros = np.zeros_like(ang_row)

    def per_map(a_row, a_col, lo, hi):
        return np.concatenate([lo(a_row), hi(a_row), lo(a_col), hi(a_col)], axis=-1)

    cos = per_map(ang_row, ang_col, np.cos, np.cos)
    sin_lo = per_map(ang_row, ang_col, lambda a: -np.sin(a), lambda a: zeros)
    sin_hi = per_map(ang_row, ang_col, lambda a: zeros, np.sin)
    return tuple(jnp.asarray(np.concatenate([tab, tab], axis=-1).astype(np.float32))
                 for tab in (cos, sin_lo, sin_hi))


def _filter_embedding(seq):
    bands = (FILTER_EMB - 1) // 2
    t = np.linspace(0.0, 1.0, seq)[:, None]
    wpos = 2.0 * np.pi * np.arange(seq, dtype=np.float64)[:, None] / seq
    f = np.linspace(1e-4, bands - 1, bands)[None, :]
    emb = np.concatenate([t, np.cos(f * wpos), -np.sin(f * wpos)], axis=-1)
    emb = np.pad(emb, ((0, 0), (0, FILT_PAD - FILTER_EMB)))
    return jnp.asarray(emb.astype(np.float32)), jnp.asarray(t.astype(np.float32))


def _tiles(seq):
    short = seq <= 256
    return dict(tm=512, tq=128, attn_heads=N_HEADS if short else 4, ct=MXU_WIDTH,
                hyena_seqs=4 if short else 2)


def _pad_to(x, rows, cols):
    return jnp.pad(x, ((0, rows - x.shape[0]), (0, cols - x.shape[1])))


def kernel(x_prompt, x_sample, cache_k, cache_v, c, c_ctx, w_ada, b_ada, norm1_g, norm2_g, w_in,
           lam_q1, lam_k1, lam_q2, lam_k2, attn_subln_g, conv_w, conv_b, filt_w1, filt_b1,
           filt_w2, filt_b2, filt_w3, filt_freq, hy_bias, w_br_attn, w_br_hy, w_out, w_up,
           w_down, final_g):
    depth = w_in.shape[0]
    assert depth == 1, "single trunk layer"
    layer = 0
    lam_init = 0.8 - 0.6 * math.exp(-0.3 * layer)
    n_ctx, ctx_len, _ = x_prompt.shape
    n_lat, lat_len, _ = x_sample.shape
    past = cache_k.shape[2]
    assert w_in.shape[1:] == (D_MODEL, IN_COLS) and x_prompt.shape[2] == x_sample.shape[2] == D_MODEL
    assert cache_k.shape == (n_lat, depth, past, N_HEADS, 2, HEAD_DIM)
    assert cache_v.shape == (n_lat, depth, past, N_HEADS, V_DIM)
    assert lat_len % GRID_W == 0 and c.shape == (n_lat, D_MODEL)

    mod = _mod_call(c_ctx[None, :], c, w_ada[layer], b_ada[layer][None, :])

    w_in_b = w_in[layer].astype(BF16)
    post_w_f32 = tuple(w[layer] for w in (w_br_attn, w_br_hy, w_out, w_up, w_down))
    g1, g2, gf = norm1_g[layer][None, :], norm2_g[layer][None, :], final_g[None, :]
    lam_vecs = jnp.stack([lam_q1[layer], lam_k1[layer], lam_q2[layer], lam_k2[layer]])
    subln_g = attn_subln_g[layer][None, :]

    filt_pack = jnp.concatenate(
        [_pad_to(filt_w1[layer], FILT_PAD, FILT_PAD), _pad_to(filt_w2[layer], FILT_PAD, FILT_PAD)]
        + [_pad_to(vec[layer][None, :], SUBLANES, FILT_PAD)
           for vec in (filt_b1, filt_b2, filt_freq)], axis=0)
    w3 = filt_w3[layer]
    deltas = jnp.asarray(np.linspace(math.log(DECAY_TARGET) / FAST_DECAY_PCT,
                                     math.log(DECAY_TARGET) / SLOW_DECAY_PCT,
                                     D_HYENA)[None, :].astype(np.float32))

    def project(x3, batch, seq, latent, cast=()):
        x = x3.reshape(batch * seq, D_MODEL)
        rope = _rope_tables(seq) if latent else None
        return x, _in_proj_call(x, mod, g1, w_in_b, rope, seq=seq, latent=latent,
                                tm=_tiles(seq)["tm"], cast=cast)

    def mix(x, outs, batch, seq, latent):
        tiles = _tiles(seq)
        if latent:
            q, kt, v, u = outs
            ckt = jnp.transpose(cache_k[:, layer], (0, 2, 3, 4, 1)).reshape(batch * K_COLS, past)
            cache = (ckt, cache_v[:, layer].reshape(batch * past * N_HEADS, V_DIM))
            kf = vf = None
        else:
            q, kt, v, kf, vf, u = outs
            cache = None
        o_attn = _attn_call(q, kt, v, cache, lam_vecs, subln_g, batch=batch, seq=seq,
                            lam_init=lam_init, tq=tiles["tq"], heads=tiles["attn_heads"])

        fwd_np, inv_np = _dft_tables(seq)
        fwd_b = jnp.asarray(fwd_np).astype(BF16)
        inv_b = jnp.asarray(inv_np).astype(BF16)
        emb, tcol = _filter_embedding(seq)
        filt = (emb, filt_pack, w3, tcol, deltas)
        o_hy = _hyena_call(u, conv_w[layer], conv_b[layer][None, :], hy_bias[layer], filt,
                           fwd_b, inv_b, batch=batch, seq=seq, ct=tiles["ct"],
                           n_chain=tiles["hyena_seqs"])

        return (x, o_attn, o_hy), kf, vf

    x_ctx, ctx_outs = project(x_prompt, n_ctx, ctx_len, False)
    x_lat, lat_outs = project(x_sample, n_lat, lat_len, True, cast=post_w_f32)
    lat_outs, post_w = lat_outs[:-len(post_w_f32)], lat_outs[-len(post_w_f32):]
    ctx_mixed, kf, vf = mix(x_ctx, ctx_outs, n_ctx, ctx_len, False)
    lat_mixed, _, _ = mix(x_lat, lat_outs, n_lat, lat_len, True)
    y_prompt, y_sample = _post_call(ctx_mixed, lat_mixed, mod, g1, g2, gf, w_in_b, post_w,
                                    lat_seq=lat_len, tm=_tiles(lat_len)["tm"])
    y_prompt = y_prompt.reshape(n_ctx, ctx_len, D_MODEL)
    y_sample = y_sample.reshape(n_lat, lat_len, D_MODEL)
    new_cache_k = jnp.transpose(kf.reshape(n_ctx, depth, N_HEADS, 2, HEAD_DIM, ctx_len),
                                (0, 1, 5, 2, 3, 4))
    new_cache_v = vf.reshape(n_ctx, depth, ctx_len, N_HEADS, V_DIM)
    return (y_prompt, y_sample, new_cache_k, new_cache_v)
```

```python
import functools
import math

import numpy as np
import jax
import jax.numpy as jnp
from jax import lax
from jax.experimental import pallas as pl
from jax.experimental.pallas import tpu as pltpu

D_MODEL = 1024
GRID_W = 64
N_HEADS = 8
HEAD_DIM = 64
V_DIM = 2 * HEAD_DIM
D_HYENA = D_MODEL // 2
HYENA_ORDER = 2
FILTER_EMB = 33
FILTER_HIDDEN = 64
D_FF = 4 * D_MODEL
ROPE_THETA = 10000.0
EPS = 1e-6
LOG2_E = math.log2(math.e)
DECAY_TARGET = 1e-2
FAST_DECAY_PCT = 0.3
SLOW_DECAY_PCT = 1.5

Q_COLS = N_HEADS * 2 * HEAD_DIM
K_COLS = N_HEADS * 2 * HEAD_DIM
V_COLS = N_HEADS * V_DIM
HY_COLS = 3 * D_HYENA
GATE_COLS = 2 * D_MODEL
Q_OFF = 0
K_OFF = Q_OFF + Q_COLS
V_OFF = K_OFF + K_COLS
HY_OFF = V_OFF + V_COLS
GATE_OFF = HY_OFF + HY_COLS
IN_COLS = GATE_OFF + GATE_COLS
GATE_WINDOW = math.gcd(GATE_OFF, D_MODEL)

LANES = 128
SUBLANES = 8
MXU_WIDTH = 256
MOD_CTX_ROWS = 8
FILT_PAD = 128
VMEM_LIMIT = 56 * 1024 * 1024

BF16 = jnp.bfloat16
F32 = jnp.float32


def _resident(shape, index=None):
    index = (0,) * len(shape) if index is None else index
    return pl.BlockSpec(shape, lambda *_: index, pipeline_mode=pl.Buffered(1))


def _params(*sem):
    return pltpu.CompilerParams(dimension_semantics=sem, vmem_limit_bytes=VMEM_LIMIT)


def _rms(x, g):
    return x * lax.rsqrt(jnp.mean(x * x, axis=-1, keepdims=True) + EPS) * g


def _mod_kernel(cctx_ref, c_ref, w_ref, b_ref, o_ref):
    c = jnp.concatenate([jnp.broadcast_to(cctx_ref[...], (MOD_CTX_ROWS, D_MODEL)), c_ref[...]],
                        axis=0)
    s = (c * jax.nn.sigmoid(c)).astype(BF16)
    chunk = jnp.dot(s, w_ref[...].astype(BF16), preferred_element_type=F32) + b_ref[...]
    for j in range(o_ref.shape[1]):
        @pl.when(pl.program_id(0) == j)
        def _():
            o_ref[:, j, :] = chunk


def _mod_call(c_ctx, c, w_ada, b_ada):
    n_chunks = w_ada.shape[1] // D_MODEL
    rows = MOD_CTX_ROWS + c.shape[0]
    return pl.pallas_call(
        _mod_kernel,
        grid=(n_chunks,),
        in_specs=[pl.BlockSpec(c_ctx.shape, lambda j: (0, 0)),
                  pl.BlockSpec(c.shape, lambda j: (0, 0)),
                  pl.BlockSpec((D_MODEL, D_MODEL), lambda j: (0, j)),
                  pl.BlockSpec((1, D_MODEL), lambda j: (0, j))],
        out_specs=pl.BlockSpec((rows, n_chunks, D_MODEL), lambda j: (0, 0, 0)),
        out_shape=jax.ShapeDtypeStruct((rows, n_chunks, D_MODEL), F32),
        compiler_params=_params("arbitrary"),
        name="mod",
    )(c_ctx, c, w_ada, b_ada)


ROPE_PAIR = HEAD_DIM // 4


def _rope(x, cos, sin_lo, sin_hi):
    return (x * cos + pltpu.roll(x, ROPE_PAIR, axis=1) * sin_hi
            + pltpu.roll(x, LANES - ROPE_PAIR, axis=1) * sin_lo)


def _in_proj_kernel(*refs, latent, n_cast):
    refs = list(refs)
    take = lambda n: [refs.pop(0) for _ in range(n)]
    x_ref, mod_ref, g_ref, w_ref = take(4)
    if latent:
        cos_ref, slo_ref, shi_ref = take(3)
    cast_src = take(n_cast)
    q_ref, kt_ref, v_ref = take(3)
    if not latent:
        ktf_ref, vf_ref = take(2)
    (u_ref,) = take(1)
    cast_dst = take(n_cast)
    tm = x_ref.shape[0]
    x = x_ref[...]
    h = _rms(x, g_ref[...]) * (1.0 + mod_ref[1:2, :]) + mod_ref[0:1, :]
    hb = h.astype(BF16)

    def proj(off, width):
        return jnp.dot(hb, w_ref[:, off:off + width], preferred_element_type=F32)

    q = proj(Q_OFF, Q_COLS) * (HEAD_DIM ** -0.5 * LOG2_E)
    k = proj(K_OFF, K_COLS)
    if latent:
        cos, slo, shi = cos_ref[...], slo_ref[...], shi_ref[...]
        k_heads = []
        for hd in range(N_HEADS):
            sl = slice(hd * LANES, (hd + 1) * LANES)
            q_ref[:, sl] = _rope(q[:, sl], cos, slo, shi).astype(BF16)
            k_heads.append(_rope(k[:, sl], cos, slo, shi))
        kt_ref[...] = jnp.concatenate(k_heads, axis=1).T.astype(BF16)
    else:
        q_ref[...] = q.astype(BF16)
        seq = kt_ref.shape[1]
        for s in range(tm // seq):
            kt = k[s * seq:(s + 1) * seq, :].T
            rows = slice(s * K_COLS, (s + 1) * K_COLS)
            ktf_ref[rows, :] = kt
            kt_ref[rows, :] = kt.astype(BF16)
    v = proj(V_OFF, V_COLS)
    v_ref[...] = v.astype(BF16)
    if not latent:
        for hd in range(N_HEADS):
            vf_ref[pl.ds(hd, tm, stride=N_HEADS), :] = v[:, hd * LANES:(hd + 1) * LANES]
    u_ref[...] = proj(HY_OFF, HY_COLS)
    for src, dst in zip(cast_src, cast_dst):
        dst[...] = src[...].astype(BF16)


def _in_proj_call(x, mod, g1, w_in, rope, *, seq, latent, tm, cast=()):
    t = x.shape[0]
    n_steps = t // tm
    per_seq = seq // tm if latent else 1

    def mod_idx(i):
        return (MOD_CTX_ROWS + i // per_seq if latent else 0, 0, 0)

    row = lambda i: (i, 0)
    in_specs = [pl.BlockSpec((tm, D_MODEL), row),
                pl.BlockSpec((None, 6, D_MODEL), mod_idx),
                _resident((1, D_MODEL)),
                _resident((D_MODEL, GATE_OFF))]
    args = [x, mod, g1, w_in]
    tok = lambda width, dtype: ((t, width), dtype, pl.BlockSpec((tm, width), row))
    if latent:
        in_specs += [pl.BlockSpec((tm, LANES), lambda i: (i % per_seq, 0))] * 3
        args += list(rope)
        kt_spec = pl.BlockSpec((K_COLS, tm), lambda i: (i // per_seq, i % per_seq))
    else:
        kt_spec = pl.BlockSpec((tm // seq * K_COLS, seq), row)
    kt_shape = (t // seq * K_COLS, seq)
    outs = [tok(Q_COLS, BF16), (kt_shape, BF16, kt_spec), tok(V_COLS, BF16)]
    if not latent:
        outs += [(kt_shape, F32, kt_spec),
                 ((t * N_HEADS, LANES), F32, pl.BlockSpec((tm * N_HEADS, LANES), row))]
    outs += [tok(HY_COLS, F32)]
    for w in cast:
        slab = pl.BlockSpec((w.shape[0] // n_steps, w.shape[1]), row)
        in_specs.append(slab)
        args.append(w)
        outs.append((w.shape, BF16, slab))
    return pl.pallas_call(
        functools.partial(_in_proj_kernel, latent=latent, n_cast=len(cast)),
        grid=(n_steps,),
        in_specs=in_specs,
        out_specs=[spec for _, _, spec in outs],
        out_shape=[jax.ShapeDtypeStruct(s, d) for s, d, _ in outs],
        compiler_params=_params("arbitrary"),
        name="in_proj_lat" if latent else "in_proj_ctx",
    )(*args)


def _attn_kernel(*refs, n_cache, lam_init, tq):
    if n_cache:
        (q_ref, kt_ref, v_ref, ckt_ref, cv_ref, lam_ref, g_ref, o_ref, kt_s, v2_s) = refs
    else:
        (q_ref, kt_ref, v_ref, lam_ref, g_ref, o_ref, v2_s) = refs
    n_heads, n_keys, _ = v2_s.shape
    head0 = pl.program_id(1) * n_heads
    lv = lam_ref[...]
    lam = (jnp.exp(jnp.sum(lv[0:1] * lv[1:2], axis=-1, keepdims=True))
           - jnp.exp(jnp.sum(lv[2:3] * lv[3:4], axis=-1, keepdims=True)) + lam_init)
    gain = g_ref[...] * (1.0 - lam_init)
    first_map = lax.broadcasted_iota(jnp.int32, (1, LANES), 1) < HEAD_DIM
    zero = jnp.zeros((tq, LANES), BF16)

    def operands(hd):
        cols = slice(hd * LANES, (hd + 1) * LANES)
        v2_s[hd, :, LANES:] = jnp.ones((n_keys, LANES), BF16)
        if n_cache:
            kt_s[cols, 0:n_cache] = ckt_ref[cols, :].astype(BF16)
            kt_s[cols, n_cache:] = kt_ref[cols, :]
            cache_rows = pl.ds(head0 + hd, n_cache, stride=N_HEADS)
            v2_s[hd, 0:n_cache, 0:LANES] = cv_ref[cache_rows, :].astype(BF16)
            v2_s[hd, n_cache:, 0:LANES] = v_ref[:, cols]
            return kt_s[cols, :], v2_s[hd]
        v2_s[hd, :, 0:LANES] = v_ref[:, cols]
        return kt_ref[cols, :], v2_s[hd]

    def scores(hd, i, kt):
        q = q_ref[i * tq:(i + 1) * tq, hd * LANES:(hd + 1) * LANES]
        q2 = jnp.concatenate([jnp.where(first_map, q, zero), jnp.where(first_map, zero, q)], axis=0)
        s = jnp.dot(q2, kt, preferred_element_type=F32)
        return s, jnp.max(s, axis=-1, keepdims=True)

    def finish(hd, i, v2, s, m):
        p = jnp.exp2(s - m).astype(BF16)
        pv = jnp.dot(p, v2, preferred_element_type=F32)
        pv = pv[:, :LANES] / pv[:, LANES:]
        o = pv[:tq] - lam * pv[tq:]
        o_ref[i * tq:(i + 1) * tq, hd * LANES:(hd + 1) * LANES] = (_rms(o, 1.0) * gain).astype(BF16)

    n_tiles = q_ref.shape[0] // tq
    pending = None
    for hd in range(n_heads):
        kt, v2 = operands(hd)
        for i in range(n_tiles):
            cur = (hd, i, v2) + scores(hd, i, kt)
            if pending is not None:
                finish(*pending)
            pending = cur
    finish(*pending)


def _attn_call(q, kt, v, cache, lam_vecs, subln_g, *, batch, seq, lam_init, tq, heads):
    const = lambda b, h: (0, 0)
    width = heads * LANES
    n_groups = N_HEADS // heads
    in_specs = [pl.BlockSpec((seq, width), lambda b, h: (b, h)),
                pl.BlockSpec((width, seq), lambda b, h: (b * n_groups + h, 0)),
                pl.BlockSpec((seq, width), lambda b, h: (b, h))]
    args = [q, kt, v]
    n_cache = 0
    if cache is not None:
        ckt, cv = cache
        n_cache = ckt.shape[1]
        in_specs += [pl.BlockSpec((width, n_cache), lambda b, h: (b * n_groups + h, 0)),
                     pl.BlockSpec((n_cache * N_HEADS, LANES), lambda b, h: (b, 0))]
        args += [ckt, cv]
    n_keys = n_cache + seq
    scratch = [pltpu.VMEM((heads, n_keys, 2 * LANES), BF16)]
    if n_cache:
        scratch = [pltpu.VMEM((width, n_keys), BF16)] + scratch
    in_specs += [pl.BlockSpec(lam_vecs.shape, const), pl.BlockSpec((1, LANES), const)]
    args += [lam_vecs, subln_g]
    return pl.pallas_call(
        functools.partial(_attn_kernel, n_cache=n_cache, lam_init=lam_init, tq=tq),
        grid=(batch, n_groups),
        in_specs=in_specs,
        out_specs=pl.BlockSpec((seq, width), lambda b, h: (b, h)),
        out_shape=jax.ShapeDtypeStruct((batch * seq, N_HEADS * V_DIM), BF16),
        scratch_shapes=scratch,
        compiler_params=_params("arbitrary", "arbitrary"),
        name="attn_lat" if cache is not None else "attn_ctx",
    )(*args)


def _dft_tables(seq):
    n = 2 * seq
    f = np.arange(seq, dtype=np.int64)[:, None]
    t = np.arange(seq, dtype=np.int64)[None, :]
    ang = 2.0 * np.pi * ((f * t) % n).astype(np.float64) / n
    alt = np.where(np.arange(seq) % 2 == 0, 1.0, -1.0)
    cos, msin = np.cos(ang), -np.sin(ang)
    msin[0, :] = alt
    fwd = np.concatenate([cos, msin], axis=0)
    wgt = np.where(f == 0, 1.0, 2.0) / n
    inv_c = (cos * wgt).T
    inv_s = (msin * wgt).T
    inv = np.concatenate([inv_c, inv_s], axis=1)
    return fwd.astype(np.float32), inv.astype(np.float32)


def _filter_hidden(emb_ref, w1_ref, b1_ref, w2_ref, b2_ref, fr_ref):
    hp = functools.partial(jnp.dot, preferred_element_type=F32, precision=lax.Precision.HIGHEST)
    fr = fr_ref[0:1, :]
    h = jnp.sin(fr * (hp(emb_ref[...], w1_ref[...]) + b1_ref[0:1, :]))
    return jnp.sin(fr * (hp(h, w2_ref[...]) + b2_ref[0:1, :]))


def _filter_planes(h, w3f_ref, w3b_ref, decay, dft_ref, out_ref):
    seq = h.shape[0]
    h = h[:, :w3f_ref.shape[0]]
    fwd = jnp.dot(h, w3f_ref[...].astype(BF16), preferred_element_type=F32) * decay
    bwd = jnp.dot(h, w3b_ref[...].astype(BF16), preferred_element_type=F32) * decay
    hsum, hdif = fwd + bwd, fwd - bwd
    re = jnp.dot(dft_ref[0:seq, :], hsum.astype(BF16), preferred_element_type=F32)
    im = jnp.dot(dft_ref[seq:, :], hdif.astype(BF16), preferred_element_type=F32)
    row = lax.broadcasted_iota(jnp.int32, (seq, 1), 0)
    alt = jnp.where(row % 2 == 0, 1.0, -1.0)
    nyq = jnp.sum(hsum * alt, axis=0, keepdims=True)
    out_ref[0] = re
    out_ref[1] = jnp.where(row == 0, 0.0, im)
    out_ref[2] = jnp.where(row == 0, nyq, re)


def _hyena_kernel(uv_ref, u1_ref, u2_ref, wv_ref, w1_ref, w2_ref, bv_ref, b1_ref, b2_ref, hb_ref,
                  emb_ref, fw1_ref, fb1_ref, fw2_ref, fb2_ref, fr_ref,
                  w3f0_ref, w3b0_ref, w3f1_ref, w3b1_ref, t_ref, dl_ref, fwd_s, inv_s,
                  o_ref, hid_s, hf_s):
    seq = emb_ref.shape[0]
    new_tile = pl.program_id(1) == 0

    @pl.when(jnp.logical_and(pl.program_id(0) == 0, new_tile))
    def _():
        hid_s[...] = _filter_hidden(emb_ref, fw1_ref, fb1_ref, fw2_ref, fb2_ref, fr_ref)

    @pl.when(new_tile)
    def _():
        h = hid_s[...].astype(BF16)
        decay = jnp.exp(-t_ref[...] * jnp.abs(dl_ref[...]))
        _filter_planes(h, w3f0_ref, w3b0_ref, decay, fwd_s, hf_s.at[0])
        _filter_planes(h, w3f1_ref, w3b1_ref, decay, fwd_s, hf_s.at[1])

    chains = [slice(c * seq, (c + 1) * seq) for c in range(uv_ref.shape[0] // seq)]
    row = lax.broadcasted_iota(jnp.int32, (seq, 1), 0)
    first, last = row == 0, row == seq - 1

    def short_conv(u_ref, w_ref, b_ref):
        outs = []
        for rows in chains:
            u = u_ref[rows, :]
            prev = jnp.where(first, 0.0, pltpu.roll(u, 1, axis=0))
            nxt = jnp.where(last, 0.0, pltpu.roll(u, seq - 1, axis=0))
            outs.append(prev * w_ref[0:1, :] + u * w_ref[1:2, :] + nxt * w_ref[2:3, :] + b_ref[...])
        return outs

    def long_conv(us, order):
        h_ref = hf_s.at[order]
        bias = hb_ref[order:order + 1, :]
        specs = [jnp.dot(fwd_s[...], u.astype(BF16), preferred_element_type=F32) for u in us]
        ys = []
        for spec in specs:
            ure, uim = spec[:seq], spec[seq:]
            yre = ure * h_ref[0] - uim * h_ref[1]
            yim = ure * h_ref[1] + uim * h_ref[2]
            ys.append(jnp.concatenate([yre, yim], axis=0).astype(BF16))
        return [jnp.dot(inv_s[...], y, preferred_element_type=F32) + u * bias
                for y, u in zip(ys, us)]

    v = short_conv(uv_ref, wv_ref, bv_ref)
    x1 = short_conv(u1_ref, w1_ref, b1_ref)
    z = [a * b for a, b in zip(x1, long_conv(v, 0))]
    x2 = short_conv(u2_ref, w2_ref, b2_ref)
    for rows, a, b in zip(chains, x2, long_conv(z, 1)):
        o_ref[rows, :] = (a * b).astype(BF16)


def _hyena_call(u, conv_w, conv_b, hy_bias, filt, fwd, inv, *, batch, seq, ct, n_chain, layer):
    emb, filt_pack, w3, tcol, deltas = filt
    nct = D_HYENA // ct
    first_vec = 2 * FILT_PAD // SUBLANES
    mat = lambda n: pl.BlockSpec((FILT_PAD, FILT_PAD), lambda c, b: (n, 0))
    vec = lambda n: pl.BlockSpec((SUBLANES, FILT_PAD), lambda c, b: (first_vec + n, 0))
    pack_specs = [mat(0), vec(0), mat(1), vec(1), vec(2)]
    rows = n_chain * seq
    col = lambda part: (lambda c, b: (b, part * nct + c))
    wcol = lambda part: (lambda c, b: (0, part * nct + c))
    small = lambda shape: pl.BlockSpec(shape, lambda c, b: (0, 0))
    w3col = lambda direction, order: (lambda c, b: (0, (direction * HYENA_ORDER + order) * nct + c))
    in_specs = ([pl.BlockSpec((rows, ct), col(p)) for p in range(3)]
                + [pl.BlockSpec((None, 3, ct), lambda c, b, p=p: (layer, 0, p * nct + c))
                   for p in range(3)]
                + [pl.BlockSpec((1, ct), wcol(p)) for p in range(3)]
                + [pl.BlockSpec((HYENA_ORDER, ct), lambda c, b: (0, c)),
                   small((seq, FILT_PAD))] + pack_specs
                + [pl.BlockSpec((FILTER_HIDDEN, ct), w3col(direction, order))
                   for order in range(HYENA_ORDER) for direction in range(2)]
                + [small((seq, 1)), pl.BlockSpec((1, ct), lambda c, b: (0, c)),
                   _resident((2 * seq, seq)), _resident((seq, 2 * seq))])
    return pl.pallas_call(
        _hyena_kernel,
        grid=(nct, batch // n_chain),
        in_specs=in_specs,
        out_specs=pl.BlockSpec((rows, ct), lambda c, b: (b, c)),
        out_shape=jax.ShapeDtypeStruct((batch * seq, D_HYENA), BF16),
        scratch_shapes=[pltpu.VMEM((seq, FILT_PAD), F32),
                        pltpu.VMEM((HYENA_ORDER, 3, seq, ct), F32)],
        compiler_params=_params("arbitrary", "arbitrary"),
        name=f"hyena_{seq}",
    )(u, u, u, conv_w, conv_w, conv_w, conv_b, conv_b, conv_b, hy_bias,
      emb, *[filt_pack] * len(pack_specs), w3, w3, w3, w3, tcol, deltas, fwd, inv)


FF_CHUNK = D_MODEL
N_FF_CHUNKS = D_FF // FF_CHUNK


def _post_kernel(xc_ref, xl_ref, oac_ref, oal_ref, ohc_ref, ohl_ref, mod_ref, g1_ref, g2_ref, gf_ref,
                 wga0_ref, wga1_ref, wgh0_ref, wgh1_ref,
                 wa_ref, wh_ref, wo_ref, wu_ref, wd_ref, yc_ref, yl_ref, *, n_ctx_steps):
    d = functools.partial(jnp.dot, preferred_element_type=F32)
    is_ctx = pl.program_id(0) < n_ctx_steps
    pick = lambda c_ref, l_ref: jnp.where(is_ctx, c_ref[...], l_ref[...])
    x = pick(xc_ref, xl_ref)
    h1 = (_rms(x, g1_ref[...]) * (1.0 + mod_ref[1:2, :]) + mod_ref[0:1, :]).astype(BF16)

    def gate(*w_refs):
        return jnp.concatenate([jax.nn.sigmoid(d(h1, w[...])) for w in w_refs], axis=1)

    merged = (gate(wga0_ref, wga1_ref) * d(pick(oac_ref, oal_ref), wa_ref[...])
              + gate(wgh0_ref, wgh1_ref) * d(pick(ohc_ref, ohl_ref), wh_ref[...]))
    x = x + mod_ref[2:3, :] * d(merged.astype(BF16), wo_ref[...])
    h2 = (_rms(x, g2_ref[...]) * (1.0 + mod_ref[4:5, :]) + mod_ref[3:4, :]).astype(BF16)
    acc = jnp.zeros_like(x)
    for c in range(N_FF_CHUNKS):
        sl = slice(c * FF_CHUNK, (c + 1) * FF_CHUNK)
        up = jnp.maximum(d(h2, wu_ref[:, sl]), 0.0)
        acc = acc + d((up * up).astype(BF16), wd_ref[sl, :])
    y = _rms(x + mod_ref[5:6, :] * acc, gf_ref[...])

    @pl.when(is_ctx)
    def _():
        yc_ref[...] = y

    @pl.when(jnp.logical_not(is_ctx))
    def _():
        yl_ref[...] = y


def _post_call(ctx, lat, mod, g1, g2, gf, w_in, weights, *, lat_seq, tm):
    n_ctx, n_lat = ctx[0].shape[0] // tm, lat[0].shape[0] // tm
    per_seq = lat_seq // tm
    gate_windows = [_resident((D_MODEL, GATE_WINDOW), (0, GATE_OFF // GATE_WINDOW + n))
                    for n in range(GATE_COLS // GATE_WINDOW)]
    ctx_row = lambda i: (jnp.minimum(i, n_ctx - 1), 0)
    lat_row = lambda i: (jnp.maximum(i - n_ctx, 0), 0)
    mod_idx = lambda i: (jnp.where(i < n_ctx, 0, MOD_CTX_ROWS + (i - n_ctx) // per_seq), 0, 0)
    widths = (D_MODEL, N_HEADS * V_DIM, D_HYENA)
    in_specs = [pl.BlockSpec((tm, w), r) for w in widths for r in (ctx_row, lat_row)]
    args = [a for pair in zip(ctx, lat) for a in pair]
    return pl.pallas_call(
        functools.partial(_post_kernel, n_ctx_steps=n_ctx),
        grid=(n_ctx + n_lat,),
        in_specs=in_specs
                 + [pl.BlockSpec((None, 6, D_MODEL), mod_idx),
                    _resident((1, D_MODEL)), _resident((1, D_MODEL)), _resident((1, D_MODEL))]
                 + gate_windows + [_resident(w.shape) for w in weights],
        out_specs=[pl.BlockSpec((tm, D_MODEL), ctx_row), pl.BlockSpec((tm, D_MODEL), lat_row)],
        out_shape=[jax.ShapeDtypeStruct((n_ctx * tm, D_MODEL), F32),
                   jax.ShapeDtypeStruct((n_lat * tm, D_MODEL), F32)],
        compiler_params=_params("arbitrary"),
        name="post",
    )(*args, mod, g1, g2, gf, *[w_in] * len(gate_windows), *weights)


def _rope_tables(seq):
    half = HEAD_DIM // 2
    n = half // 2
    inv = ROPE_THETA ** (-np.arange(n, dtype=np.float64) / n)
    pos = np.arange(seq)
    ang_row = (pos // GRID_W).astype(np.float64)[:, None] * inv[None, :]
    ang_col = (pos % GRID_W).astype(np.float64)[:, None] * inv[None, :]
    zeros = np.zeros_like(ang_row)

    def per_map(a_row, a_col, lo, hi):
        return np.concatenate([lo(a_row), hi(a_row), lo(a_col), hi(a_col)], axis=-1)

    cos = per_map(ang_row, ang_col, np.cos, np.cos)
    sin_lo = per_map(ang_row, ang_col, lambda a: -np.sin(a), lambda a: zeros)
    sin_hi = per_map(ang_row, ang_col, lambda a: zeros, np.sin)
    return tuple(jnp.asarray(np.concatenate([tab, tab], axis=-1).astype(np.float32))
                 for tab in (cos, sin_lo, sin_hi))


def _filter_embedding(seq):
    bands = (FILTER_EMB - 1) // 2
    t = np.linspace(0.0, 1.0, seq)[:, None]
    wpos = 2.0 * np.pi * np.arange(seq, dtype=np.float64)[:, None] / seq
    f = np.linspace(1e-4, bands - 1, bands)[None, :]
    emb = np.concatenate([t, np.cos(f * wpos), -np.sin(f * wpos)], axis=-1)
    emb = np.pad(emb, ((0, 0), (0, FILT_PAD - FILTER_EMB)))
    return jnp.asarray(emb.astype(np.float32)), jnp.asarray(t.astype(np.float32))


def _tiles(seq):
    short = seq <= 256
    return dict(tm=512, tq=128, attn_heads=N_HEADS if short else 4, ct=MXU_WIDTH,
                hyena_seqs=4 if short else 2)


def _pad_to(x, rows, cols):
    return jnp.pad(x, ((0, rows - x.shape[0]), (0, cols - x.shape[1])))


def kernel(x_prompt, x_sample, cache_k, cache_v, c, c_ctx, w_ada, b_ada, norm1_g, norm2_g, w_in,
           lam_q1, lam_k1, lam_q2, lam_k2, attn_subln_g, conv_w, conv_b, filt_w1, filt_b1,
           filt_w2, filt_b2, filt_w3, filt_freq, hy_bias, w_br_attn, w_br_hy, w_out, w_up,
           w_down, final_g):
    depth = w_in.shape[0]
    assert depth == 1, "single trunk layer"
    layer = 0
    lam_init = 0.8 - 0.6 * math.exp(-0.3 * layer)
    n_ctx, ctx_len, _ = x_prompt.shape
    n_lat, lat_len, _ = x_sample.shape
    past = cache_k.shape[2]
    assert w_in.shape[1:] == (D_MODEL, IN_COLS) and x_prompt.shape[2] == x_sample.shape[2] == D_MODEL
    assert cache_k.shape == (n_lat, depth, past, N_HEADS, 2, HEAD_DIM)
    assert cache_v.shape == (n_lat, depth, past, N_HEADS, V_DIM)
    assert lat_len % GRID_W == 0 and c.shape == (n_lat, D_MODEL)

    mod = _mod_call(c_ctx[None, :], c, w_ada[layer], b_ada[layer][None, :])

    w_in_b = w_in[layer].astype(BF16)
    post_w_f32 = tuple(w[layer] for w in (w_br_attn, w_br_hy, w_out, w_up, w_down))
    g1, g2, gf = norm1_g[layer][None, :], norm2_g[layer][None, :], final_g[None, :]
    lam_vecs = jnp.stack([lam_q1[layer], lam_k1[layer], lam_q2[layer], lam_k2[layer]])
    subln_g = attn_subln_g[layer][None, :]

    filt_pack = jnp.concatenate(
        [_pad_to(filt_w1[layer], FILT_PAD, FILT_PAD), _pad_to(filt_w2[layer], FILT_PAD, FILT_PAD)]
        + [_pad_to(vec[layer][None, :], SUBLANES, FILT_PAD)
           for vec in (filt_b1, filt_b2, filt_freq)], axis=0)
    w3 = filt_w3[layer]
    deltas = jnp.asarray(np.linspace(math.log(DECAY_TARGET) / FAST_DECAY_PCT,
                                     math.log(DECAY_TARGET) / SLOW_DECAY_PCT,
                                     D_HYENA)[None, :].astype(np.float32))

    def project(x3, batch, seq, latent, cast=()):
        x = x3.reshape(batch * seq, D_MODEL)
        rope = _rope_tables(seq) if latent else None
        return x, _in_proj_call(x, mod, g1, w_in_b, rope, seq=seq, latent=latent,
                                tm=_tiles(seq)["tm"], cast=cast)

    def mix(x, outs, batch, seq, latent):
        tiles = _tiles(seq)
        if latent:
            q, kt, v, u = outs
            ckt = jnp.transpose(cache_k[:, layer], (0, 2, 3, 4, 1)).reshape(batch * K_COLS, past)
            cache = (ckt, cache_v[:, layer].reshape(batch * past * N_HEADS, V_DIM))
            kf = vf = None
        else:
            q, kt, v, kf, vf, u = outs
            cache = None
        o_attn = _attn_call(q, kt, v, cache, lam_vecs, subln_g, batch=batch, seq=seq,
                            lam_init=lam_init, tq=tiles["tq"], heads=tiles["attn_heads"])

        fwd_np, inv_np = _dft_tables(seq)
        fwd_b = jnp.asarray(fwd_np).astype(BF16)
        inv_b = jnp.asarray(inv_np).astype(BF16)
        emb, tcol = _filter_embedding(seq)
        filt = (emb, filt_pack, w3, tcol, deltas)
        o_hy = _hyena_call(u, conv_w, conv_b[layer][None, :], hy_bias[layer], filt,
                           fwd_b, inv_b, batch=batch, seq=seq, ct=tiles["ct"],
                           n_chain=tiles["hyena_seqs"], layer=layer)

        return (x, o_attn, o_hy), kf, vf

    x_ctx, ctx_outs = project(x_prompt, n_ctx, ctx_len, False)
    x_lat, lat_outs = project(x_sample, n_lat, lat_len, True, cast=post_w_f32)
    lat_outs, post_w = lat_outs[:-len(post_w_f32)], lat_outs[-len(post_w_f32):]
    ctx_mixed, kf, vf = mix(x_ctx, ctx_outs, n_ctx, ctx_len, False)
    lat_mixed, _, _ = mix(x_lat, lat_outs, n_lat, lat_len, True)
    y_prompt, y_sample = _post_call(ctx_mixed, lat_mixed, mod, g1, g2, gf, w_in_b, post_w,
                                    lat_seq=lat_len, tm=_tiles(lat_len)["tm"])
    y_prompt = y_prompt.reshape(n_ctx, ctx_len, D_MODEL)
    y_sample = y_sample.reshape(n_lat, lat_len, D_MODEL)
    new_cache_k = jnp.transpose(kf.reshape(n_ctx, depth, N_HEADS, 2, HEAD_DIM, ctx_len),
                                (0, 1, 5, 2, 3, 4))
    new_cache_v = vf.reshape(n_ctx, depth, ctx_len, N_HEADS, V_DIM)
    return (y_prompt, y_sample, new_cache_k, new_cache_v)
```
